```python
import math
import jax, jax.numpy as jnp
from jax import lax
import numpy as np

D_MODEL = 1024
BATCH = 8
SEQ = 2048
DEPTH = 1
DEC_BATCH = 128
DEC_SEQ = 1
PAST_LEN = 16384
PAGE_SIZE = 128

CONV_DIM = D_MODEL
CONV_W = 3
HEAD_DIM = 64
N_HEADS = D_MODEL // HEAD_DIM
N_KV = max(N_HEADS // 8, 1)
GROUP = N_HEADS // N_KV
WINDOW = 128
ATTN_BLOCK = 128
ATTN_SCALE = HEAD_DIM ** -0.5
N_BUCKETS = 32
MAX_DISTANCE = 128
NEG_INF = -1e30
N_EXPERTS = 32
TOP_K = 4
D_EXPERT = D_MODEL
SWIGLU_ALPHA = 1.702
SWIGLU_LIMIT = 7.0
MOE_BLOCK = 128
EPS = 1e-5
IN_SPLITS = (CONV_DIM, CONV_DIM, CONV_DIM, N_HEADS * HEAD_DIM, N_KV * HEAD_DIM, N_KV * HEAD_DIM, D_MODEL, D_MODEL)
IN_DIM = sum(IN_SPLITS)

kernel_name = 'hybrid_shortconv_swa_moe_step'


def rmsnorm(x, w):
    xf = x.astype(jnp.float32)
    y = xf * lax.rsqrt(jnp.mean(xf * xf, axis=-1, keepdims=True) + EPS) * w.astype(jnp.float32)
    return y.astype(x.dtype)


def t5_bucket(rel):
    n = jnp.maximum(rel, 0)
    max_exact = N_BUCKETS // 2
    nf = jnp.maximum(n, 1).astype(jnp.float32)
    large = max_exact + (jnp.log(nf / max_exact) / math.log(MAX_DISTANCE / max_exact) * (N_BUCKETS - max_exact)).astype(jnp.int32)
    large = jnp.minimum(large, N_BUCKETS - 1)
    return jnp.where(n < max_exact, n, large)


def causal_short_conv(u, prefix, w):
    T = u.shape[1]
    full = jnp.concatenate([prefix.astype(u.dtype), u], axis=1)
    out = full[:, 0:T] * w[0]
    for j in range(1, CONV_W):
        out = out + full[:, j:j + T] * w[j]
    return out, full[:, -(CONV_W - 1):]


def attn_core(qb, kk, vv, rel, valid, rel_bias, sinks):
    Q, S = rel.shape
    s = jnp.einsum('bnqkgd,bnskd->bnkgqs', qb, kk).astype(jnp.float32) * ATTN_SCALE
    bias = rel_bias[t5_bucket(rel)].astype(jnp.float32)
    s = s + bias.transpose(2, 0, 1).reshape(N_KV, GROUP, Q, S)
    s = jnp.where(valid[None, :, None, None], s, NEG_INF)
    sink = jnp.broadcast_to(sinks.astype(jnp.float32).reshape(N_KV, GROUP, 1, 1), s.shape[:-1] + (1,))
    p = jax.nn.softmax(jnp.concatenate([s, sink], axis=-1), axis=-1)[..., :-1]
    return jnp.einsum('bnkgqs,bnskd->bnqkgd', p.astype(vv.dtype), vv)


def window_attention_prompt(q, k, v, rel_bias, sinks):
    b, T = q.shape[:2]
    nb = T // ATTN_BLOCK
    qb = q.reshape(b, nb, ATTN_BLOCK, N_KV, GROUP, HEAD_DIM)
    kb = k.reshape(b, nb, ATTN_BLOCK, N_KV, HEAD_DIM)
    vb = v.reshape(b, nb, ATTN_BLOCK, N_KV, HEAD_DIM)
    pad = ((0, 0), (1, 0), (0, 0), (0, 0), (0, 0))
    kk = jnp.concatenate([jnp.pad(kb, pad)[:, :-1], kb], axis=2)
    vv = jnp.concatenate([jnp.pad(vb, pad)[:, :-1], vb], axis=2)
    rel = jnp.arange(ATTN_BLOCK)[:, None] + ATTN_BLOCK - jnp.arange(2 * ATTN_BLOCK)[None, :]
    k_pos = (jnp.arange(nb) * ATTN_BLOCK)[:, None] - ATTN_BLOCK + jnp.arange(2 * ATTN_BLOCK)[None, :]
    valid = ((rel >= 0) & (rel < WINDOW))[None] & (k_pos >= 0)[:, None, :]
    out = attn_core(qb, kk, vv, rel, valid, rel_bias, sinks)
    w = min(WINDOW, T)
    return out.reshape(b, T, N_HEADS * HEAD_DIM), k[:, -w:], v[:, -w:]


def window_attention_sample(q, k, v, k_past, v_past, rel_bias, sinks):
    b, T = q.shape[:2]
    W = k_past.shape[1]
    kk = jnp.concatenate([k_past.astype(k.dtype), k], axis=1)
    vv = jnp.concatenate([v_past.astype(v.dtype), v], axis=1)
    rel = jnp.arange(T)[:, None] + W - jnp.arange(W + T)[None, :]
    valid = ((rel >= 0) & (rel < WINDOW))[None]
    out = attn_core(q.reshape(b, 1, T, N_KV, GROUP, HEAD_DIM), kk[:, None], vv[:, None], rel, valid, rel_bias, sinks)
    return out.reshape(b, T, N_HEADS * HEAD_DIM), kk[:, -W:], vv[:, -W:]


def moe_ffn(h, w_router, b_router, w_up, b_up, w_down, b_down):
    T, D = h.shape
    logits = jnp.einsum('td,de->te', h, w_router).astype(jnp.float32) + b_router.astype(jnp.float32)
    top_vals, top_idx = lax.top_k(logits, TOP_K)
    gates = jax.nn.softmax(top_vals, axis=-1)
    TK = T * TOP_K
    n_blocks = -(-TK // MOE_BLOCK) + N_EXPERTS
    n_rows = n_blocks * MOE_BLOCK
    flat_e = top_idx.reshape(TK)
    flat_tok = jnp.repeat(jnp.arange(T, dtype=jnp.int32), TOP_K)
    flat_g = gates.reshape(TK)
    order = jnp.argsort(flat_e)
    sorted_e = flat_e[order]
    counts = jnp.bincount(flat_e, length=N_EXPERTS)
    padded = (counts + MOE_BLOCK - 1) // MOE_BLOCK * MOE_BLOCK
    pad_end = jnp.cumsum(padded)
    pad_start = pad_end - padded
    start = jnp.cumsum(counts) - counts
    dest = pad_start[sorted_e] + jnp.arange(TK) - start[sorted_e]
    row_tok = jnp.full((n_rows,), T, jnp.int32).at[dest].set(flat_tok[order])
    row_gate = jnp.zeros((n_rows,), jnp.float32).at[dest].set(flat_g[order])
    block_expert = jnp.minimum(jnp.searchsorted(pad_end, jnp.arange(n_blocks) * MOE_BLOCK, side='right'), N_EXPERTS - 1)
    xs = jnp.concatenate([h, jnp.zeros((1, D), h.dtype)], axis=0)[row_tok].reshape(n_blocks, MOE_BLOCK, D)

    def expert_block(args):
        xb, e = args
        u = xb @ w_up[e] + b_up[e]
        glu = jnp.minimum(u[:, 0::2], SWIGLU_LIMIT)
        lin = jnp.clip(u[:, 1::2], -SWIGLU_LIMIT, SWIGLU_LIMIT)
        a = glu * jax.nn.sigmoid(SWIGLU_ALPHA * glu) * (lin + 1)
        return a @ w_down[e] + b_down[e]

    ys = lax.map(expert_block, (xs, block_expert))
    y = jax.ops.segment_sum(ys.reshape(n_rows, D).astype(jnp.float32) * row_gate[:, None], row_tok, num_segments=T + 1)[:T]
    return y.astype(h.dtype)


def trunk_layer(x, conv_prefix, k_past, v_past, rel_bias, attn_norm_w, w_in, conv_w, q_norm_w, k_norm_w, sinks,
                w_out, ffn_norm_w, w_router, b_router, w_up, b_up, w_down, b_down):
    b, T, _ = x.shape
    h = rmsnorm(x, attn_norm_w)
    proj = jnp.einsum('btd,de->bte', h, w_in)
    xin, bg, cg, q, k, v, g_conv, g_attn = jnp.split(proj, list(np.cumsum(IN_SPLITS)[:-1]), axis=-1)
    conv_out, conv_state = causal_short_conv(cg * xin, conv_prefix, conv_w)
    y_conv = bg * conv_out
    q = rmsnorm(q.reshape(b, T, N_HEADS, HEAD_DIM), q_norm_w)
    k = rmsnorm(k.reshape(b, T, N_KV, HEAD_DIM), k_norm_w)
    v = v.reshape(b, T, N_KV, HEAD_DIM)
    if k_past is None:
        y_attn, k_new, v_new = window_attention_prompt(q, k, v, rel_bias, sinks)
    else:
        y_attn, k_new, v_new = window_attention_sample(q, k, v, k_past, v_past, rel_bias, sinks)
    merged = jax.nn.sigmoid(g_conv) * y_conv + jax.nn.sigmoid(g_attn) * y_attn
    x = x + jnp.einsum('bte,ed->btd', merged, w_out)
    h2 = rmsnorm(x, ffn_norm_w)
    x = x + moe_ffn(h2.reshape(b * T, D_MODEL), w_router, b_router, w_up, b_up, w_down, b_down).reshape(b, T, D_MODEL)
    return x, conv_state, k_new, v_new


def setup_inputs(seed: int = 0) -> dict:
    key = jax.random.key(seed)
    ks = jax.random.split(key, 24)

    def nrm(k, shape, s):
        return jax.random.normal(k, shape, jnp.float32) * s

    win_buf = min(WINDOW, PAST_LEN)
    return {
        'x_prompt': nrm(ks[0], (BATCH, SEQ, D_MODEL), 1.0),
        'x_sample': nrm(ks[1], (DEC_BATCH, DEC_SEQ, D_MODEL), 1.0),
        'state_conv': nrm(ks[2], (DEPTH, DEC_BATCH, CONV_W - 1, CONV_DIM), 1.0),
        'cache_k_win': nrm(ks[3], (DEPTH, DEC_BATCH, win_buf, N_KV, HEAD_DIM), 1.0),
        'cache_v_win': nrm(ks[4], (DEPTH, DEC_BATCH, win_buf, N_KV, HEAD_DIM), 1.0),
        'rel_bias': nrm(ks[5], (N_BUCKETS, N_HEADS), 0.5),
        'attn_norm_w': 1.0 + nrm(ks[6], (DEPTH, D_MODEL), 0.01),
        'w_in': nrm(ks[7], (DEPTH, D_MODEL, IN_DIM), D_MODEL ** -0.5),
        'conv_w': nrm(ks[8], (DEPTH, CONV_W, CONV_DIM), CONV_W ** -0.5),
        'q_norm_w': 1.0 + nrm(ks[9], (DEPTH, HEAD_DIM), 0.01),
        'k_norm_w': 1.0 + nrm(ks[10], (DEPTH, HEAD_DIM), 0.01),
        'sinks': nrm(ks[11], (DEPTH, N_HEADS), 0.5),
        'w_out': nrm(ks[12], (DEPTH, D_MODEL, D_MODEL), D_MODEL ** -0.5),
        'ffn_norm_w': 1.0 + nrm(ks[13], (DEPTH, D_MODEL), 0.01),
        'w_router': nrm(ks[14], (DEPTH, D_MODEL, N_EXPERTS), D_MODEL ** -0.5),
        'b_router': nrm(ks[15], (DEPTH, N_EXPERTS), 0.01),
        'w_up': nrm(ks[16], (DEPTH, N_EXPERTS, D_MODEL, 2 * D_EXPERT), D_MODEL ** -0.5),
        'b_up': nrm(ks[17], (DEPTH, N_EXPERTS, 2 * D_EXPERT), 0.01),
        'w_down': nrm(ks[18], (DEPTH, N_EXPERTS, D_EXPERT, D_MODEL), D_EXPERT ** -0.5),
        'b_down': nrm(ks[19], (DEPTH, N_EXPERTS, D_MODEL), 0.01),
    }


def reference(x_prompt, x_sample, state_conv, cache_k_win, cache_v_win, rel_bias, attn_norm_w, w_in, conv_w,
              q_norm_w, k_norm_w, sinks, w_out, ffn_norm_w, w_router, b_router, w_up, b_up, w_down, b_down):
    xp, xs = x_prompt, x_sample
    p_conv, p_k, p_v, s_conv, s_k, s_v = [], [], [], [], [], []
    for l in range(DEPTH):
        lw = (attn_norm_w[l], w_in[l], conv_w[l], q_norm_w[l], k_norm_w[l], sinks[l], w_out[l],
              ffn_norm_w[l], w_router[l], b_router[l], w_up[l], b_up[l], w_down[l], b_down[l])
        zero_prefix = jnp.zeros((xp.shape[0], CONV_W - 1, CONV_DIM), xp.dtype)
        xp, c1, k1, v1 = trunk_layer(xp, zero_prefix, None, None, rel_bias, *lw)
        xs, c2, k2, v2 = trunk_layer(xs, state_conv[l], cache_k_win[l], cache_v_win[l], rel_bias, *lw)
        p_conv.append(c1); p_k.append(k1); p_v.append(v1)
        s_conv.append(c2); s_k.append(k2); s_v.append(v2)
    return (xp, xs, jnp.stack(p_conv), jnp.stack(p_k), jnp.stack(p_v), jnp.stack(s_conv), jnp.stack(s_k), jnp.stack(s_v))
```

```python
import functools
import math

import numpy as np
import jax
import jax.numpy as jnp
from jax import lax
from jax.experimental import pallas as pl
from jax.experimental.pallas import tpu as pltpu

F32 = jnp.float32
BF16 = jnp.bfloat16

D_MODEL = 1024
HEAD_DIM = 64
N_HEADS = 16
N_KV = 2
GROUP = N_HEADS // N_KV
WINDOW = 128
ATTN_BLOCK = 128
N_BUCKETS = 32
MAX_DISTANCE = 128
NEG_INF = -1e30
N_EXPERTS = 32
TOP_K = 4
SWIGLU_ALPHA = 1.702
SWIGLU_LIMIT = 7.0
EPS = 1e-5
ATTN_SCALE = HEAD_DIM ** -0.5

OFF_XIN, OFF_BG, OFF_CG, OFF_Q = 0, 1024, 2048, 3072
OFF_K, OFF_V, OFF_GC, OFF_GA = 4096, 4224, 4352, 5376
IN_DIM = 6400

LANES = 128
MIXER_TILE = 512
EXPERT_BLOCK = 256
VMEM_LIMIT = 56 * 1024 * 1024


def _t5_bucket_np(rel):
    n = np.maximum(rel, 0)
    max_exact = N_BUCKETS // 2
    nf = np.maximum(n, 1).astype(np.float64)
    large = max_exact + (np.log(nf / max_exact) / math.log(MAX_DISTANCE / max_exact)
                         * (N_BUCKETS - max_exact)).astype(np.int32)
    large = np.minimum(large, N_BUCKETS - 1)
    return np.where(n < max_exact, n, large).astype(np.int32)


def _rms(x, w):
    return x * lax.rsqrt(jnp.mean(x * x, axis=-1, keepdims=True) + EPS) * w


def _lo_half():
    return lax.broadcasted_iota(jnp.int32, (1, LANES), 1) < HEAD_DIM


def _pair_norm(t, w128):
    lo = _lo_half()
    sq = t * t
    s_lo = jnp.sum(jnp.where(lo, sq, 0.0), axis=-1, keepdims=True)
    s_hi = jnp.sum(jnp.where(lo, 0.0, sq), axis=-1, keepdims=True)
    r = jnp.where(lo, lax.rsqrt(s_lo * (1.0 / HEAD_DIM) + EPS), lax.rsqrt(s_hi * (1.0 / HEAD_DIM) + EPS))
    return t * r * w128


def _top4_gates(logits):
    rows = logits.shape[0]
    lane = lax.broadcasted_iota(jnp.int32, (rows, N_EXPERTS), 1).astype(F32)
    vals, idxs = [], []
    l = logits
    for _ in range(TOP_K):
        m = jnp.max(l, axis=-1, keepdims=True)
        idx = jnp.min(jnp.where(l == m, lane, float(N_EXPERTS)), axis=-1, keepdims=True)
        vals.append(m)
        idxs.append(idx)
        l = jnp.where(lane == idx, -jnp.inf, l)
    lane4 = lax.broadcasted_iota(jnp.int32, (rows, TOP_K), 1)

    def pack(cols):
        return jnp.where(lane4 == 0, cols[0], jnp.where(lane4 == 1, cols[1], jnp.where(lane4 == 2, cols[2], cols[3])))

    v = pack(vals)
    e = jnp.exp(v - vals[0])
    gates = e / jnp.sum(e, axis=-1, keepdims=True)
    return pack(idxs).astype(jnp.int32), gates


def _epilogue(x, merged, wout_ref, fnw_ref, wr_ref, br_ref, x1_ref, h2_ref, idx_ref, gate_ref):
    x1 = x + jnp.dot(merged.astype(BF16), wout_ref[...], preferred_element_type=F32)
    x1_ref[...] = x1
    h2 = _rms(x1, fnw_ref[...])
    h2_ref[...] = h2.astype(BF16)
    logits = jnp.dot(h2, wr_ref[...], precision=lax.Precision.HIGHEST, preferred_element_type=F32) + br_ref[...]
    idx, gates = _top4_gates(logits)
    idx_ref[...] = idx
    gate_ref[...] = gates


def _mixer_prompt_kernel(sinks_ref, x_ref, anw_ref, win_ref, convw_ref, qnw_ref, knw_ref, bias_ref, wout_ref,
                         fnw_ref, wr_ref, br_ref,
                         x1_ref, h2_ref, idx_ref, gate_ref, kout_ref, vout_ref, cout_ref,
                         ubuf, q_s, kd_s, vd_s, ya_s):
    tm = x_ref.shape[0]
    nblk = tm // ATTN_BLOCK
    j = pl.program_id(1)
    first_tile = j == 0
    lo = _lo_half()

    x = x_ref[...]
    h = _rms(x, anw_ref[...]).astype(BF16)

    def proj(off, n):
        return jnp.dot(h, win_ref[:, off:off + n], preferred_element_type=F32)

    @pl.when(first_tile)
    def _():
        ubuf[0:8, :] = jnp.zeros((8, D_MODEL), F32)
        kd_s[0:ATTN_BLOCK, :] = jnp.zeros((ATTN_BLOCK, 2 * LANES), BF16)
        vd_s[0:ATTN_BLOCK, :] = jnp.zeros((ATTN_BLOCK, 2 * LANES), BF16)

    u = proj(OFF_CG, D_MODEL) * proj(OFF_XIN, D_MODEL)
    ubuf[8:tm + 8, :] = u
    cw = convw_ref[...]
    conv = ubuf[6:tm + 6, :] * cw[0:1, :] + ubuf[7:tm + 7, :] * cw[1:2, :] + u * cw[2:3, :]
    merged = jax.nn.sigmoid(proj(OFF_GC, D_MODEL)) * (proj(OFF_BG, D_MODEL) * conv)
    tail = ubuf[tm + 6:tm + 8, :]
    cout_ref[0] = tail
    ubuf[6:8, :] = tail

    q = proj(OFF_Q, D_MODEL)
    kv = proj(OFF_K, 2 * LANES)
    k = _pair_norm(kv[:, :LANES], knw_ref[...])
    v = kv[:, LANES:]
    qnw = qnw_ref[...]
    for p in range(N_HEADS // 2):
        sl = slice(p * LANES, (p + 1) * LANES)
        q_s[:, sl] = (_pair_norm(q[:, sl], qnw) * ATTN_SCALE).astype(BF16)
    k_sw = pltpu.roll(k, HEAD_DIM, axis=1)
    v_sw = pltpu.roll(v, HEAD_DIM, axis=1)
    kd_s[ATTN_BLOCK:tm + ATTN_BLOCK, 0:LANES] = jnp.where(lo, k, k_sw).astype(BF16)
    kd_s[ATTN_BLOCK:tm + ATTN_BLOCK, LANES:2 * LANES] = jnp.where(lo, k_sw, k).astype(BF16)
    vd_s[ATTN_BLOCK:tm + ATTN_BLOCK, 0:LANES] = jnp.where(lo, v, v_sw).astype(BF16)
    vd_s[ATTN_BLOCK:tm + ATTN_BLOCK, LANES:2 * LANES] = jnp.where(lo, v_sw, v).astype(BF16)

    @pl.when(j == pl.num_programs(1) - 1)
    def _():
        kout_ref[0] = k[tm - ATTN_BLOCK:, :]
        vout_ref[0] = v[tm - ATTN_BLOCK:, :]

    sink_cols = [
        jnp.concatenate([jnp.full((ATTN_BLOCK, 1), sinks_ref[g * GROUP + t], F32) for t in range(GROUP)], axis=0)
        for g in range(N_KV)
    ]
    prev_cols = lax.broadcasted_iota(jnp.int32, (1, 2 * ATTN_BLOCK), 1) < ATTN_BLOCK

    def attn_block(r0, mask_prev):
        kk = kd_s[pl.ds(r0, 2 * ATTN_BLOCK), :]
        vv = vd_s[pl.ds(r0, 2 * ATTN_BLOCK), :]
        qb = q_s[pl.ds(r0, ATTN_BLOCK), :]
        for g in range(N_KV):
            parts = []
            for t in range(GROUP):
                hd = g * GROUP + t
                slab = qb[:, (hd // 2) * LANES:(hd // 2 + 1) * LANES]
                keep = lo if hd % 2 == 0 else jnp.logical_not(lo)
                parts.append(jnp.where(keep, slab, jnp.zeros_like(slab)))
            lhs = jnp.concatenate(parts, axis=0)
            s = lax.dot_general(lhs, kk[:, g * LANES:(g + 1) * LANES], (((1,), (1,)), ((), ())),
                                preferred_element_type=F32)
            s = s + bias_ref[g * GROUP * ATTN_BLOCK:(g + 1) * GROUP * ATTN_BLOCK, :]
            if mask_prev:
                s = jnp.where(jnp.logical_and(prev_cols, first_tile), NEG_INF, s)
            m = jnp.maximum(jnp.max(s, axis=-1, keepdims=True), sink_cols[g])
            pr = jnp.exp(s - m)
            den = jnp.sum(pr, axis=-1, keepdims=True) + jnp.exp(sink_cols[g] - m)
            o = jnp.dot(pr.astype(BF16), vv[:, g * LANES:(g + 1) * LANES], preferred_element_type=F32) / den
            for i in range(GROUP // 2):
                pair = g * (GROUP // 2) + i
                even = o[(2 * i) * ATTN_BLOCK:(2 * i + 1) * ATTN_BLOCK, :]
                odd = o[(2 * i + 1) * ATTN_BLOCK:(2 * i + 2) * ATTN_BLOCK, :]
                ya_s[pl.ds(r0, ATTN_BLOCK), pair * LANES:(pair + 1) * LANES] = jnp.where(lo, even, odd)

    attn_block(0, True)

    def loop_body(blk, carry):
        attn_block(pl.multiple_of(blk * ATTN_BLOCK, ATTN_BLOCK), False)
        return carry

    lax.fori_loop(1, nblk, loop_body, 0)

    kd_s[0:ATTN_BLOCK, :] = kd_s[tm:tm + ATTN_BLOCK, :]
    vd_s[0:ATTN_BLOCK, :] = vd_s[tm:tm + ATTN_BLOCK, :]

    merged = merged + jax.nn.sigmoid(proj(OFF_GA, D_MODEL)) * ya_s[...]
    _epilogue(x, merged, wout_ref, fnw_ref, wr_ref, br_ref, x1_ref, h2_ref, idx_ref, gate_ref)


def _const_spec(shape):
    nd = len(shape)
    return pl.BlockSpec(shape, lambda *_: (0,) * nd, pipeline_mode=pl.Buffered(1))


def _mixer_prompt(x, sinks, anw, win_bf, convw, qnw, knw, bias_tab, wout_bf, fnw, wr, br):
    batch, seq, _ = x.shape
    tm = MIXER_TILE
    nj = seq // tm
    tokens = batch * seq
    x2 = x.reshape(tokens, D_MODEL)
    tok_spec = lambda width: pl.BlockSpec((tm, width), lambda b, j: (b * nj + j, 0))
    per_batch = lambda rows, width: pl.BlockSpec((1, rows, width), lambda b, j: (b, 0, 0))
    in_specs = [
        pl.BlockSpec(memory_space=pltpu.SMEM),
        tok_spec(D_MODEL),
        _const_spec((1, D_MODEL)),
        _const_spec((D_MODEL, IN_DIM)),
        _const_spec((3, D_MODEL)),
        _const_spec((1, LANES)),
        _const_spec((1, LANES)),
        _const_spec((N_HEADS * ATTN_BLOCK, 2 * ATTN_BLOCK)),
        _const_spec((D_MODEL, D_MODEL)),
        _const_spec((1, D_MODEL)),
        _const_spec((D_MODEL, N_EXPERTS)),
        _const_spec((1, N_EXPERTS)),
    ]
    out_shape = (
        jax.ShapeDtypeStruct((tokens, D_MODEL), F32),
        jax.ShapeDtypeStruct((tokens, D_MODEL), BF16),
        jax.ShapeDtypeStruct((tokens, TOP_K), jnp.int32),
        jax.ShapeDtypeStruct((tokens, TOP_K), F32),
        jax.ShapeDtypeStruct((batch, ATTN_BLOCK, LANES), F32),
        jax.ShapeDtypeStruct((batch, ATTN_BLOCK, LANES), F32),
        jax.ShapeDtypeStruct((batch, 2, D_MODEL), F32),
    )
    out_specs = (
        tok_spec(D_MODEL), tok_spec(D_MODEL), tok_spec(TOP_K), tok_spec(TOP_K),
        per_batch(ATTN_BLOCK, LANES), per_batch(ATTN_BLOCK, LANES), per_batch(2, D_MODEL),
    )
    scratch = [
        pltpu.VMEM((tm + 8, D_MODEL), F32),
        pltpu.VMEM((tm, D_MODEL), BF16),
        pltpu.VMEM((tm + ATTN_BLOCK, 2 * LANES), BF16),
        pltpu.VMEM((tm + ATTN_BLOCK, 2 * LANES), BF16),
        pltpu.VMEM((tm, D_MODEL), F32),
    ]
    return pl.pallas_call(
        _mixer_prompt_kernel,
        grid=(batch, nj),
        in_specs=in_specs,
        out_specs=out_specs,
        out_shape=out_shape,
        scratch_shapes=scratch,
        compiler_params=pltpu.CompilerParams(
            dimension_semantics=("arbitrary", "arbitrary"), vmem_limit_bytes=VMEM_LIMIT),
        name="mixer_prompt",
    )(sinks, x2, anw, win_bf, convw, qnw, knw, bias_tab, wout_bf, fnw, wr, br)


def _mixer_sample_kernel(x_ref, p0_ref, p1_ref, kp_ref, vp_ref, sink_ref, anw_ref, win_ref, convw_ref, qnw_ref,
                         knw_ref, bias_ref, wout_ref, fnw_ref, wr_ref, br_ref,
                         x1_ref, h2_ref, idx_ref, gate_ref, knew_ref, vnew_ref, unew_ref,
                         qh_s, o_s, kn_s, vn_s):
    nb = x_ref.shape[0]
    lo = _lo_half()
    x = x_ref[...]
    h = _rms(x, anw_ref[...]).astype(BF16)

    def proj(off, n):
        return jnp.dot(h, win_ref[:, off:off + n], preferred_element_type=F32)

    u = proj(OFF_CG, D_MODEL) * proj(OFF_XIN, D_MODEL)
    unew_ref[...] = u
    cw = convw_ref[...]
    conv = p0_ref[...] * cw[0:1, :] + p1_ref[...] * cw[1:2, :] + u * cw[2:3, :]
    merged = jax.nn.sigmoid(proj(OFF_GC, D_MODEL)) * (proj(OFF_BG, D_MODEL) * conv)

    q = proj(OFF_Q, D_MODEL)
    kv = proj(OFF_K, 2 * LANES)
    k = _pair_norm(kv[:, :LANES], knw_ref[...])
    v = kv[:, LANES:]
    knew_ref[...] = k
    vnew_ref[...] = v
    kn_s[...] = k
    vn_s[...] = v

    qnw = qnw_ref[...]
    for hd in range(N_HEADS):
        pair, half, grp = hd // 2, hd % 2, hd // GROUP
        slab = _pair_norm(q[:, pair * LANES:(pair + 1) * LANES], qnw) * ATTN_SCALE
        slab = jnp.where(lo if half == 0 else jnp.logical_not(lo), slab, 0.0)
        if half != grp:
            slab = pltpu.roll(slab, HEAD_DIM, axis=1)
        qh_s[hd * nb:(hd + 1) * nb, :] = slab

    sink = sink_ref[...]
    bias = bias_ref[...]
    row0 = lax.broadcasted_iota(jnp.int32, (WINDOW, 1), 0) == 0

    def token(b, carry):
        qb = qh_s[pl.ds(b, N_HEADS, stride=nb), :]
        kb = jnp.where(row0, kn_s[pl.ds(b, 1), :], kp_ref[b])
        vb = jnp.where(row0, vn_s[pl.ds(b, 1), :], vp_ref[b])
        s = lax.dot_general(qb.astype(BF16), kb.astype(BF16), (((1,), (1,)), ((), ())),
                            preferred_element_type=F32) + bias
        m = jnp.maximum(jnp.max(s, axis=-1, keepdims=True), sink)
        pr = jnp.exp(s - m)
        den = jnp.sum(pr, axis=-1, keepdims=True) + jnp.exp(sink - m)
        o = jnp.dot(pr.astype(BF16), vb.astype(BF16), preferred_element_type=F32) / den
        o_s[pl.ds(b, N_HEADS, stride=nb), :] = o
        return carry

    lax.fori_loop(0, nb, token, 0)

    cols = []
    for pair in range(N_HEADS // 2):
        halves = []
        for half in range(2):
            hd = 2 * pair + half
            slab = o_s[hd * nb:(hd + 1) * nb, :]
            if half != hd // GROUP:
                slab = pltpu.roll(slab, HEAD_DIM, axis=1)
            halves.append(slab)
        cols.append(jnp.where(lo, halves[0], halves[1]))
    y_attn = jnp.concatenate(cols, axis=1)

    merged = merged + jax.nn.sigmoid(proj(OFF_GA, D_MODEL)) * y_attn
    _epilogue(x, merged, wout_ref, fnw_ref, wr_ref, br_ref, x1_ref, h2_ref, idx_ref, gate_ref)


def _mixer_sample(x, p0, p1, k_past, v_past, sink_col, anw, win_bf, convw, qnw, knw, bias_s, wout_bf, fnw, wr, br):
    nb = x.shape[0]
    args = (x, p0, p1, k_past, v_past, sink_col, anw, win_bf, convw, qnw, knw, bias_s, wout_bf, fnw, wr, br)
    out_shape = (
        jax.ShapeDtypeStruct((nb, D_MODEL), F32),
        jax.ShapeDtypeStruct((nb, D_MODEL), BF16),
        jax.ShapeDtypeStruct((nb, TOP_K), jnp.int32),
        jax.ShapeDtypeStruct((nb, TOP_K), F32),
        jax.ShapeDtypeStruct((nb, LANES), F32),
        jax.ShapeDtypeStruct((nb, LANES), F32),
        jax.ShapeDtypeStruct((nb, D_MODEL), F32),
    )
    scratch = [
        pltpu.VMEM((N_HEADS * nb, LANES), F32),
        pltpu.VMEM((N_HEADS * nb, LANES), F32),
        pltpu.VMEM((nb, LANES), F32),
        pltpu.VMEM((nb, LANES), F32),
    ]
    return pl.pallas_call(
        _mixer_sample_kernel,
        grid=(1,),
        in_specs=[_const_spec(a.shape) for a in args],
        out_specs=tuple(pl.BlockSpec(s.shape, lambda i: (0, 0)) for s in out_shape),
        out_shape=out_shape,
        scratch_shapes=scratch,
        compiler_params=pltpu.CompilerParams(dimension_semantics=("arbitrary",), vmem_limit_bytes=VMEM_LIMIT),
        name="mixer_sample",
    )(*args)


def _expert_kernel(be_ref, nact_ref, xs_ref, gate_ref, wup_ref, bup_ref, wdn_ref, bdn_ref, out_ref,
                   wup_s, wdn_s, perm_s):
    i = pl.program_id(0)
    e = be_ref[i]
    e_prev = be_ref[jnp.maximum(i - 1, 0)]

    @pl.when(jnp.logical_or(i == 0, e != e_prev))
    def _():
        wup_s[...] = wup_ref[0].astype(BF16)
        half = LANES // 2
        for cs in range(D_MODEL // LANES):
            cols = slice(cs * LANES, (cs + 1) * LANES)
            for c in range(D_MODEL // LANES):
                for par in range(2):
                    src = c * LANES + par * half
                    perm_s[cs, pl.ds(c * LANES + par, half, stride=2), :] = wdn_ref[0, src:src + half, cols]
            wdn_s[:, cols] = perm_s[cs].astype(BF16)

    active = i < nact_ref[0]

    @pl.when(active)
    def _():
        u = jnp.dot(xs_ref[...], wup_s[...], preferred_element_type=F32) + bup_ref[0]
        even = (lax.broadcasted_iota(jnp.int32, (1, LANES), 1) & 1) == 0
        cols = []
        for c in range(D_MODEL // LANES):
            c0 = u[:, (2 * c) * LANES:(2 * c + 1) * LANES]
            c1 = u[:, (2 * c + 1) * LANES:(2 * c + 2) * LANES]
            glu = jnp.where(even, c0, pltpu.roll(c1, 1, axis=1))
            lin = jnp.where(even, pltpu.roll(c0, LANES - 1, axis=1), c1)
            glu = jnp.minimum(glu, SWIGLU_LIMIT)
            lin = jnp.clip(lin, -SWIGLU_LIMIT, SWIGLU_LIMIT)
            cols.append(glu * jax.nn.sigmoid(SWIGLU_ALPHA * glu) * (lin + 1.0))
        a = jnp.concatenate(cols, axis=1).astype(BF16)
        y = jnp.dot(a, wdn_s[...], preferred_element_type=F32) + bdn_ref[0]
        out_ref[...] = y * gate_ref[...]

    @pl.when(jnp.logical_not(active))
    def _():
        out_ref[...] = jnp.zeros(out_ref.shape, F32)


def _experts(block_expert, n_active, xs, row_gate, w_up, b_up, w_down, b_down):
    n_rows = xs.shape[0]
    bm = EXPERT_BLOCK
    n_blocks = n_rows // bm
    grid_spec = pltpu.PrefetchScalarGridSpec(
        num_scalar_prefetch=2,
        grid=(n_blocks,),
        in_specs=[
            pl.BlockSpec((bm, D_MODEL), lambda i, be, na: (i, 0)),
            pl.BlockSpec((bm, 1), lambda i, be, na: (i, 0)),
            pl.BlockSpec((1, D_MODEL, 2 * D_MODEL), lambda i, be, na: (be[i], 0, 0)),
            pl.BlockSpec((1, 1, 2 * D_MODEL), lambda i, be, na: (be[i], 0, 0)),
            pl.BlockSpec((1, D_MODEL, D_MODEL), lambda i, be, na: (be[i], 0, 0)),
            pl.BlockSpec((1, 1, D_MODEL), lambda i, be, na: (be[i], 0, 0)),
        ],
        out_specs=pl.BlockSpec((bm, D_MODEL), lambda i, be, na: (i, 0)),
        scratch_shapes=[
            pltpu.VMEM((D_MODEL, 2 * D_MODEL), BF16),
            pltpu.VMEM((D_MODEL, D_MODEL), BF16),
            pltpu.VMEM((D_MODEL // LANES, D_MODEL, LANES), F32),
        ],
    )
    return pl.pallas_call(
        _expert_kernel,
        grid_spec=grid_spec,
        out_shape=jax.ShapeDtypeStruct((n_rows, D_MODEL), F32),
        compiler_params=pltpu.CompilerParams(dimension_semantics=("arbitrary",), vmem_limit_bytes=VMEM_LIMIT),
        name="experts",
    )(block_expert, n_active, xs, row_gate, w_up, b_up.reshape(N_EXPERTS, 1, 2 * D_MODEL),
      w_down, b_down.reshape(N_EXPERTS, 1, D_MODEL))


def _bias_tables(rel_bias):
    qi = np.arange(ATTN_BLOCK)[:, None]
    kj = np.arange(2 * ATTN_BLOCK)[None, :]
    rel = qi + ATTN_BLOCK - kj
    valid = (rel >= 0) & (rel < WINDOW)
    tab = rel_bias[_t5_bucket_np(rel)]
    tab = jnp.where(valid[:, :, None], tab, NEG_INF).transpose(2, 0, 1)
    prompt_tab = tab.reshape(N_HEADS * ATTN_BLOCK, 2 * ATTN_BLOCK)
    slot_rel = np.where(np.arange(WINDOW) == 0, 0, WINDOW - np.arange(WINDOW))
    sample_tab = rel_bias[_t5_bucket_np(slot_rel)].T
    return prompt_tab, sample_tab


def _route(idx, gates, bm):
    t = idx.shape[0]
    tk = t * TOP_K
    n_blocks = -(-tk // bm) + N_EXPERTS
    n_rows = n_blocks * bm
    flat_e = idx.reshape(tk)
    flat_tok = jnp.repeat(jnp.arange(t, dtype=jnp.int32), TOP_K)
    flat_g = gates.reshape(tk)
    order = jnp.argsort(flat_e)
    sorted_e = flat_e[order]
    counts = jnp.bincount(flat_e, length=N_EXPERTS)
    padded = (counts + bm - 1) // bm * bm
    pad_end = jnp.cumsum(padded)
    pad_start = pad_end - padded
    start = jnp.cumsum(counts) - counts
    dest = pad_start[sorted_e] + jnp.arange(tk) - start[sorted_e]
    row_tok = jnp.full((n_rows,), t, jnp.int32).at[dest].set(flat_tok[order])
    row_gate = jnp.zeros((n_rows,), F32).at[dest].set(flat_g[order])
    block_expert = jnp.minimum(jnp.searchsorted(pad_end, jnp.arange(n_blocks) * bm, side='right'),
                               N_EXPERTS - 1).astype(jnp.int32)
    n_active = (pad_end[-1] // bm).astype(jnp.int32).reshape(1)
    return row_tok, row_gate, block_expert, n_active


def kernel(x_prompt, x_sample, state_conv, cache_k_win, cache_v_win, rel_bias, attn_norm_w, w_in, conv_w,
           q_norm_w, k_norm_w, sinks, w_out, ffn_norm_w, w_router, b_router, w_up, b_up, w_down, b_down):
    batch, seq, _ = x_prompt.shape
    nb = x_sample.shape[0]
    anw = attn_norm_w[0].reshape(1, D_MODEL)
    fnw = ffn_norm_w[0].reshape(1, D_MODEL)
    win_bf = w_in[0].astype(BF16)
    wout_bf = w_out[0].astype(BF16)
    qnw = jnp.tile(q_norm_w[0], 2).reshape(1, LANES)
    knw = jnp.tile(k_norm_w[0], 2).reshape(1, LANES)
    br = b_router[0].reshape(1, N_EXPERTS)
    prompt_tab, sample_tab = _bias_tables(rel_bias)

    x1p, h2p, idxp, gatep, kp, vp, cp = _mixer_prompt(
        x_prompt, sinks[0], anw, win_bf, conv_w[0], qnw, knw, prompt_tab, wout_bf, fnw, w_router[0], br)

    k_past = cache_k_win[0].reshape(nb, WINDOW, LANES)
    v_past = cache_v_win[0].reshape(nb, WINDOW, LANES)
    x1s, h2s, idxs, gates, knew, vnew, unew = _mixer_sample(
        x_sample.reshape(nb, D_MODEL), state_conv[0, :, 0, :], state_conv[0, :, 1, :], k_past, v_past,
        sinks[0].reshape(N_HEADS, 1), anw, win_bf, conv_w[0], qnw, knw, sample_tab, wout_bf, fnw, w_router[0], br)

    x1 = jnp.concatenate([x1p, x1s], axis=0)
    h2 = jnp.concatenate([h2p, h2s], axis=0)
    idx = jnp.concatenate([idxp, idxs], axis=0)
    gate = jnp.concatenate([gatep, gates], axis=0)
    t_all = x1.shape[0]
    row_tok, row_gate, block_expert, n_active = _route(idx, gate, EXPERT_BLOCK)
    xs = jnp.concatenate([h2, jnp.zeros((1, D_MODEL), BF16)], axis=0)[row_tok]
    ys = _experts(block_expert, n_active, xs, row_gate.reshape(-1, 1), w_up[0], b_up[0], w_down[0], b_down[0])
    y = jax.ops.segment_sum(ys, row_tok, num_segments=t_all + 1)[:t_all]
    out = x1 + y

    n_prompt = batch * seq
    y_prompt = out[:n_prompt].reshape(batch, seq, D_MODEL)
    y_sample = out[n_prompt:].reshape(nb, 1, D_MODEL)
    new_k_sample = jnp.concatenate([k_past[:, 1:], knew[:, None, :]], axis=1)
    new_v_sample = jnp.concatenate([v_past[:, 1:], vnew[:, None, :]], axis=1)
    new_conv_sample = jnp.stack([state_conv[0, :, 1, :], unew], axis=1)
    return (
        y_prompt,
        y_sample,
        cp[None],
        kp.reshape(1, batch, ATTN_BLOCK, N_KV, HEAD_DIM),
        vp.reshape(1, batch, ATTN_BLOCK, N_KV, HEAD_DIM),
        new_conv_sample[None],
        new_k_sample.reshape(1, nb, WINDOW, N_KV, HEAD_DIM),
        new_v_sample.reshape(1, nb, WINDOW, N_KV, HEAD_DIM),
    )
```

```python
import math

import numpy as np
import jax
import jax.numpy as jnp
from jax import lax
from jax.experimental import pallas as pl
from jax.experimental.pallas import tpu as pltpu

F32 = jnp.float32
BF16 = jnp.bfloat16
I32 = jnp.int32

D_MODEL = 1024
HEAD_DIM = 64
N_HEADS = 16
N_KV = 2
GROUP = N_HEADS // N_KV
WINDOW = 128
ATTN_BLOCK = 128
N_BUCKETS = 32
MAX_DISTANCE = 128
NEG_INF = -1e30
N_EXPERTS = 32
TOP_K = 4
SWIGLU_ALPHA = 1.702
SWIGLU_LIMIT = 7.0
EPS = 1e-5
ATTN_SCALE = HEAD_DIM ** -0.5

OFF_XIN, OFF_BG, OFF_CG, OFF_Q = 0, 1024, 2048, 3072
OFF_K, OFF_V, OFF_GC, OFF_GA = 4096, 4224, 4352, 5376
IN_DIM = 6400

LANES = 128
MIXER_TILE = 512
EXPERT_BLOCK = 256
CHUNK = 16
CHUNKS_PER_BLOCK = EXPERT_BLOCK // CHUNK
XS_WIDTH = D_MODEL + LANES
VMEM_LIMIT = 60 * 1024 * 1024


def _local_rows(tm):
    need = tm * TOP_K + N_EXPERTS * (CHUNK - 1)
    return -(-need // 512) * 512


def _t5_bucket_np(rel):
    n = np.maximum(rel, 0)
    max_exact = N_BUCKETS // 2
    nf = np.maximum(n, 1).astype(np.float64)
    large = max_exact + (np.log(nf / max_exact) / math.log(MAX_DISTANCE / max_exact)
                         * (N_BUCKETS - max_exact)).astype(np.int32)
    large = np.minimum(large, N_BUCKETS - 1)
    return np.where(n < max_exact, n, large).astype(np.int32)


def _rms(x, w):
    return x * lax.rsqrt(jnp.mean(x * x, axis=-1, keepdims=True) + EPS) * w


def _lane_iota(rows=1):
    return lax.broadcasted_iota(I32, (rows, LANES), 1)


def _lo_half():
    return _lane_iota() < HEAD_DIM


def _pair_norm(t, w128):
    lo = _lo_half()
    sq = t * t
    s_lo = jnp.sum(jnp.where(lo, sq, 0.0), axis=-1, keepdims=True)
    s_hi = jnp.sum(jnp.where(lo, 0.0, sq), axis=-1, keepdims=True)
    r = jnp.where(lo, lax.rsqrt(s_lo * (1.0 / HEAD_DIM) + EPS), lax.rsqrt(s_hi * (1.0 / HEAD_DIM) + EPS))
    return t * r * w128


def _top4_gates(logits):
    rows = logits.shape[0]
    lane = lax.broadcasted_iota(I32, (rows, N_EXPERTS), 1).astype(F32)
    vals, idxs = [], []
    l = logits
    for _ in range(TOP_K):
        m = jnp.max(l, axis=-1, keepdims=True)
        idx = jnp.min(jnp.where(l == m, lane, float(N_EXPERTS)), axis=-1, keepdims=True)
        vals.append(m)
        idxs.append(idx)
        l = jnp.where(lane == idx, -jnp.inf, l)
    es = [jnp.exp(v - vals[0]) for v in vals]
    den = es[0] + es[1] + es[2] + es[3]
    return idxs, [e / den for e in es]


def _cols_to_lanes(cols, rows, fill=0.0):
    lane = _lane_iota(rows)
    out = jnp.full((rows, LANES), fill, F32)
    for k, c in enumerate(cols):
        out = jnp.where(lane == k, c, out)
    return out


def _dispatch(h2, idxs, gates, xs_ref, lp_ref, stat_ref):
    tm = h2.shape[0]
    lrows = xs_ref.shape[0]
    lane = _lane_iota(tm)
    lane_f = lane.astype(F32)
    member = jnp.zeros((tm, LANES), F32)
    for k in range(TOP_K):
        member = member + jnp.where(lane_f == idxs[k], 1.0, 0.0)
    cnt = jnp.sum(member, axis=0, keepdims=True)
    seg = jnp.floor((cnt + (CHUNK - 1)) * (1.0 / CHUNK)) * CHUNK
    lane1 = _lane_iota()
    incl = seg
    for s in (1, 2, 4, 8, 16):
        incl = incl + jnp.where(lane1 >= s, pltpu.roll(incl, s, axis=1), 0.0)
    starts = incl - seg
    row8 = lax.broadcasted_iota(I32, (8, LANES), 0)
    stat_ref[0] = jnp.where(row8 == 0, seg, jnp.where(row8 == 1, starts, 0.0))

    tri = (lax.broadcasted_iota(I32, (tm, tm), 1) < lax.broadcasted_iota(I32, (tm, tm), 0)).astype(BF16)
    rank = jnp.dot(tri, member.astype(BF16), preferred_element_type=F32)
    pos = starts + rank
    lps = [jnp.sum(jnp.where(lane_f == idxs[k], pos, 0.0), axis=-1, keepdims=True) for k in range(TOP_K)]
    lane4 = lax.broadcasted_iota(I32, (tm, TOP_K), 1)
    lp_ref[...] = jnp.where(lane4 == 0, lps[0], jnp.where(lane4 == 1, lps[1],
                                                          jnp.where(lane4 == 2, lps[2], lps[3]))).astype(I32)
    lp_t = _cols_to_lanes(lps, tm, fill=-1.0).T

    g_hi = [g.astype(BF16).astype(F32) for g in gates]
    g_lo = [g - h for g, h in zip(gates, g_hi)]
    meta_in = _cols_to_lanes(list(idxs) + g_hi + g_lo, tm)
    rhs = jnp.concatenate([h2.astype(BF16), meta_in.astype(BF16)], axis=1)

    sub = 512
    starts32 = jnp.where(lane1 < N_EXPERTS, starts, 1e9)

    def rows_chunk(c, carry):
        r0 = pl.multiple_of(c * sub, sub)
        rid = (lax.broadcasted_iota(I32, (sub, 1), 0) + r0).astype(F32)
        rid_t = lax.broadcasted_iota(I32, (sub, tm), 0).astype(F32) + r0.astype(F32)
        perm = rid_t == lp_t[0:1, :]
        for k in range(1, TOP_K):
            perm = jnp.logical_or(perm, rid_t == lp_t[k:k + 1, :])
        full = jnp.dot(perm.astype(BF16), rhs, preferred_element_type=F32)
        got = full[:, D_MODEL:]
        e_row = jnp.sum(jnp.where(rid >= starts32, 1.0, 0.0), axis=-1, keepdims=True) - 1.0
        g_sum = pltpu.roll(got, LANES - TOP_K, axis=1) + pltpu.roll(got, LANES - 2 * TOP_K, axis=1)
        lane_s = _lane_iota(sub)
        pick = jnp.logical_and(lane_s < TOP_K, got == e_row)
        gate_row = jnp.sum(jnp.where(pick, g_sum, 0.0), axis=-1, keepdims=True)
        gr_hi = gate_row.astype(BF16).astype(F32)
        meta = jnp.where(lane_s == 0, gr_hi, jnp.where(lane_s == 1, gate_row - gr_hi, 0.0))
        xs_ref[pl.ds(r0, sub), 0:D_MODEL] = full[:, :D_MODEL].astype(BF16)
        xs_ref[pl.ds(r0, sub), D_MODEL:XS_WIDTH] = meta.astype(BF16)
        return carry

    lax.fori_loop(0, lrows // sub, rows_chunk, 0)


def _epilogue(x, merged, wout_ref, fnw_ref, wr_ref, br_ref, x1_ref, xs_ref, lp_ref, stat_ref):
    x1 = x + jnp.dot(merged.astype(BF16), wout_ref[...], preferred_element_type=F32)
    x1_ref[...] = x1
    h2 = _rms(x1, fnw_ref[...])
    logits = jnp.dot(h2.astype(BF16), wr_ref[...].astype(BF16), preferred_element_type=F32) + br_ref[...]
    idxs, gates = _top4_gates(logits)
    _dispatch(h2, idxs, gates, xs_ref, lp_ref, stat_ref)


def _mixer_prompt_kernel(sinks_ref, x_ref, anw_ref, win_ref, convw_ref, qnw_ref, knw_ref, bias_ref, wout_ref,
                         fnw_ref, wr_ref, br_ref,
                         x1_ref, xs_ref, lp_ref, stat_ref, kout_ref, vout_ref, cout_ref,
                         ubuf, q_s, kd_s, vd_s, ya_s):
    tm = x_ref.shape[0]
    nblk = tm // ATTN_BLOCK
    j = pl.program_id(1)
    first_tile = j == 0
    lo = _lo_half()

    x = x_ref[...]
    h = _rms(x, anw_ref[...]).astype(BF16)

    def proj(off, n):
        return jnp.dot(h, win_ref[:, off:off + n], preferred_element_type=F32)

    @pl.when(first_tile)
    def _():
        ubuf[0:8, :] = jnp.zeros((8, D_MODEL), F32)
        kd_s[0:ATTN_BLOCK, :] = jnp.zeros((ATTN_BLOCK, 2 * LANES), BF16)
        vd_s[0:ATTN_BLOCK, :] = jnp.zeros((ATTN_BLOCK, 2 * LANES), BF16)

    u = proj(OFF_CG, D_MODEL) * proj(OFF_XIN, D_MODEL)
    ubuf[8:tm + 8, :] = u
    cw = convw_ref[...]
    conv = ubuf[6:tm + 6, :] * cw[0:1, :] + ubuf[7:tm + 7, :] * cw[1:2, :] + u * cw[2:3, :]
    merged = jax.nn.sigmoid(proj(OFF_GC, D_MODEL)) * (proj(OFF_BG, D_MODEL) * conv)
    tail = ubuf[tm + 6:tm + 8, :]
    cout_ref[0] = tail
    ubuf[6:8, :] = tail

    q = proj(OFF_Q, D_MODEL)
    kv = proj(OFF_K, 2 * LANES)
    k = _pair_norm(kv[:, :LANES], knw_ref[...])
    v = kv[:, LANES:]
    qnw = qnw_ref[...]
    for p in range(N_HEADS // 2):
        sl = slice(p * LANES, (p + 1) * LANES)
        q_s[:, sl] = (_pair_norm(q[:, sl], qnw) * ATTN_SCALE).astype(BF16)
    k_sw = pltpu.roll(k, HEAD_DIM, axis=1)
    v_sw = pltpu.roll(v, HEAD_DIM, axis=1)
    kd_s[ATTN_BLOCK:tm + ATTN_BLOCK, 0:LANES] = jnp.where(lo, k, k_sw).astype(BF16)
    kd_s[ATTN_BLOCK:tm + ATTN_BLOCK, LANES:2 * LANES] = jnp.where(lo, k_sw, k).astype(BF16)
    vd_s[ATTN_BLOCK:tm + ATTN_BLOCK, 0:LANES] = jnp.where(lo, v, v_sw).astype(BF16)
    vd_s[ATTN_BLOCK:tm + ATTN_BLOCK, LANES:2 * LANES] = jnp.where(lo, v_sw, v).astype(BF16)

    @pl.when(j == pl.num_programs(1) - 1)
    def _():
        kout_ref[0] = k[tm - ATTN_BLOCK:, :]
        vout_ref[0] = v[tm - ATTN_BLOCK:, :]

    sink_cols = [
        jnp.concatenate([jnp.full((ATTN_BLOCK, 1), sinks_ref[g * GROUP + t], F32) for t in range(GROUP)], axis=0)
        for g in range(N_KV)
    ]
    prev_cols = lax.broadcasted_iota(I32, (1, 2 * ATTN_BLOCK), 1) < ATTN_BLOCK

    def attn_block(r0, mask_prev):
        kk = kd_s[pl.ds(r0, 2 * ATTN_BLOCK), :]
        vv = vd_s[pl.ds(r0, 2 * ATTN_BLOCK), :]
        qb = q_s[pl.ds(r0, ATTN_BLOCK), :]
        for g in range(N_KV):
            parts = []
            for t in range(GROUP):
                hd = g * GROUP + t
                slab = qb[:, (hd // 2) * LANES:(hd // 2 + 1) * LANES]
                keep = lo if hd % 2 == 0 else jnp.logical_not(lo)
                parts.append(jnp.where(keep, slab, jnp.zeros_like(slab)))
            lhs = jnp.concatenate(parts, axis=0)
            s = lax.dot_general(lhs, kk[:, g * LANES:(g + 1) * LANES], (((1,), (1,)), ((), ())),
                                preferred_element_type=F32)
            s = s + bias_ref[g * GROUP * ATTN_BLOCK:(g + 1) * GROUP * ATTN_BLOCK, :]
            if mask_prev:
                s = jnp.where(jnp.logical_and(prev_cols, first_tile), NEG_INF, s)
            m = jnp.maximum(jnp.max(s, axis=-1, keepdims=True), sink_cols[g])
            pr = jnp.exp(s - m)
            den = jnp.sum(pr, axis=-1, keepdims=True) + jnp.exp(sink_cols[g] - m)
            o = jnp.dot(pr.astype(BF16), vv[:, g * LANES:(g + 1) * LANES], preferred_element_type=F32) / den
            for i in range(GROUP // 2):
                pair = g * (GROUP // 2) + i
                even = o[(2 * i) * ATTN_BLOCK:(2 * i + 1) * ATTN_BLOCK, :]
                odd = o[(2 * i + 1) * ATTN_BLOCK:(2 * i + 2) * ATTN_BLOCK, :]
                ya_s[pl.ds(r0, ATTN_BLOCK), pair * LANES:(pair + 1) * LANES] = jnp.where(lo, even, odd)

    attn_block(0, True)

    def loop_body(blk, carry):
        attn_block(pl.multiple_of(blk * ATTN_BLOCK, ATTN_BLOCK), False)
        return carry

    lax.fori_loop(1, nblk, loop_body, 0)

    kd_s[0:ATTN_BLOCK, :] = kd_s[tm:tm + ATTN_BLOCK, :]
    vd_s[0:ATTN_BLOCK, :] = vd_s[tm:tm + ATTN_BLOCK, :]

    merged = merged + jax.nn.sigmoid(proj(OFF_GA, D_MODEL)) * ya_s[...]
    _epilogue(x, merged, wout_ref, fnw_ref, wr_ref, br_ref, x1_ref, xs_ref, lp_ref, stat_ref)


def _const_spec(shape):
    nd = len(shape)
    return pl.BlockSpec(shape, lambda *_: (0,) * nd, pipeline_mode=pl.Buffered(1))


def _mixer_prompt(x, xs_rows, sinks, anw, win_bf, convw, qnw, knw, bias_tab, wout_bf, fnw, wr, br):
    batch, seq, _ = x.shape
    tm = MIXER_TILE
    lrows = _local_rows(tm)
    nj = seq // tm
    tokens = batch * seq
    n_tiles = batch * nj
    x2 = x.reshape(tokens, D_MODEL)
    tok_spec = lambda width: pl.BlockSpec((tm, width), lambda b, j: (b * nj + j, 0))
    per_batch = lambda rows, width: pl.BlockSpec((1, rows, width), lambda b, j: (b, 0, 0))
    in_specs = [
        pl.BlockSpec(memory_space=pltpu.SMEM),
        tok_spec(D_MODEL),
        _const_spec((1, D_MODEL)),
        _const_spec((D_MODEL, IN_DIM)),
        _const_spec((3, D_MODEL)),
        _const_spec((1, LANES)),
        _const_spec((1, LANES)),
        _const_spec((N_HEADS * ATTN_BLOCK, 2 * ATTN_BLOCK)),
        _const_spec((D_MODEL, D_MODEL)),
        _const_spec((1, D_MODEL)),
        _const_spec((D_MODEL, N_EXPERTS)),
        _const_spec((1, N_EXPERTS)),
    ]
    out_shape = (
        jax.ShapeDtypeStruct((tokens, D_MODEL), F32),
        jax.ShapeDtypeStruct((xs_rows, XS_WIDTH), BF16),
        jax.ShapeDtypeStruct((tokens, TOP_K), I32),
        jax.ShapeDtypeStruct((n_tiles, 8, LANES), F32),
        jax.ShapeDtypeStruct((batch, ATTN_BLOCK, LANES), F32),
        jax.ShapeDtypeStruct((batch, ATTN_BLOCK, LANES), F32),
        jax.ShapeDtypeStruct((batch, 2, D_MODEL), F32),
    )
    out_specs = (
        tok_spec(D_MODEL),
        pl.BlockSpec((lrows, XS_WIDTH), lambda b, j: (b * nj + j, 0)),
        tok_spec(TOP_K),
        pl.BlockSpec((1, 8, LANES), lambda b, j: (b * nj + j, 0, 0)),
        per_batch(ATTN_BLOCK, LANES), per_batch(ATTN_BLOCK, LANES), per_batch(2, D_MODEL),
    )
    scratch = [
        pltpu.VMEM((tm + 8, D_MODEL), F32),
        pltpu.VMEM((tm, D_MODEL), BF16),
        pltpu.VMEM((tm + ATTN_BLOCK, 2 * LANES), BF16),
        pltpu.VMEM((tm + ATTN_BLOCK, 2 * LANES), BF16),
        pltpu.VMEM((tm, D_MODEL), F32),
    ]
    return pl.pallas_call(
        _mixer_prompt_kernel,
        grid=(batch, nj),
        in_specs=in_specs,
        out_specs=out_specs,
        out_shape=out_shape,
        scratch_shapes=scratch,
        compiler_params=pltpu.CompilerParams(
            dimension_semantics=("arbitrary", "arbitrary"), vmem_limit_bytes=VMEM_LIMIT),
        name="mixer_prompt",
    )(sinks, x2, anw, win_bf, convw, qnw, knw, bias_tab, wout_bf, fnw, wr, br)


def _mixer_sample_kernel(xs_in_ref, x_ref, p0_ref, p1_ref, kp_ref, vp_ref, sink_ref, anw_ref, win_ref, convw_ref,
                         qnw_ref, knw_ref, bias_ref, wout_ref, fnw_ref, wr_ref, br_ref,
                         x1_ref, xs_ref, lp_ref, stat_ref, knew_ref, vnew_ref, unew_ref,
                         qh_s, o_s, kn_s, vn_s):
    del xs_in_ref
    nb = x_ref.shape[0]
    lo = _lo_half()
    x = x_ref[...]
    h = _rms(x, anw_ref[...]).astype(BF16)

    def proj(off, n):
        return jnp.dot(h, win_ref[:, off:off + n], preferred_element_type=F32)

    u = proj(OFF_CG, D_MODEL) * proj(OFF_XIN, D_MODEL)
    unew_ref[...] = u
    cw = convw_ref[...]
    conv = p0_ref[...] * cw[0:1, :] + p1_ref[...] * cw[1:2, :] + u * cw[2:3, :]
    merged = jax.nn.sigmoid(proj(OFF_GC, D_MODEL)) * (proj(OFF_BG, D_MODEL) * conv)

    q = proj(OFF_Q, D_MODEL)
    kv = proj(OFF_K, 2 * LANES)
    k = _pair_norm(kv[:, :LANES], knw_ref[...])
    v = kv[:, LANES:]
    knew_ref[...] = k
    vnew_ref[...] = v
    kn_s[...] = k
    vn_s[...] = v

    qnw = qnw_ref[...]
    for hd in range(N_HEADS):
        pair, half, grp = hd // 2, hd % 2, hd // GROUP
        slab = _pair_norm(q[:, pair * LANES:(pair + 1) * LANES], qnw) * ATTN_SCALE
        slab = jnp.where(lo if half == 0 else jnp.logical_not(lo), slab, 0.0)
        if half != grp:
            slab = pltpu.roll(slab, HEAD_DIM, axis=1)
        qh_s[hd * nb:(hd + 1) * nb, :] = slab

    sink = sink_ref[...]
    bias = bias_ref[...]
    row0 = lax.broadcasted_iota(I32, (WINDOW, 1), 0) == 0

    def token(b, carry):
        qb = qh_s[pl.ds(b, N_HEADS, stride=nb), :]
        kb = jnp.where(row0, kn_s[pl.ds(b, 1), :], kp_ref[b])
        vb = jnp.where(row0, vn_s[pl.ds(b, 1), :], vp_ref[b])
        s = lax.dot_general(qb.astype(BF16), kb.astype(BF16), (((1,), (1,)), ((), ())),
                            preferred_element_type=F32) + bias
        m = jnp.maximum(jnp.max(s, axis=-1, keepdims=True), sink)
        pr = jnp.exp(s - m)
        den = jnp.sum(pr, axis=-1, keepdims=True) + jnp.exp(sink - m)
        o = jnp.dot(pr.astype(BF16), vb.astype(BF16), preferred_element_type=F32) / den
        o_s[pl.ds(b, N_HEADS, stride=nb), :] = o
        return carry

    lax.fori_loop(0, nb, token, 0)

    cols = []
    for pair in range(N_HEADS // 2):
        halves = []
        for half in range(2):
            hd = 2 * pair + half
            slab = o_s[hd * nb:(hd + 1) * nb, :]
            if half != hd // GROUP:
                slab = pltpu.roll(slab, HEAD_DIM, axis=1)
            halves.append(slab)
        cols.append(jnp.where(lo, halves[0], halves[1]))
    y_attn = jnp.concatenate(cols, axis=1)

    merged = merged + jax.nn.sigmoid(proj(OFF_GA, D_MODEL)) * y_attn
    _epilogue(x, merged, wout_ref, fnw_ref, wr_ref, br_ref, x1_ref, xs_ref, lp_ref, stat_ref)


def _mixer_sample(xs_big, xs_block, x, p0, p1, k_past, v_past, sink_col, anw, win_bf, convw, qnw, knw, bias_s,
                  wout_bf, fnw, wr, br):
    nb = x.shape[0]
    lrows = _local_rows(nb)
    args = (x, p0, p1, k_past, v_past, sink_col, anw, win_bf, convw, qnw, knw, bias_s, wout_bf, fnw, wr, br)
    out_shape = (
        jax.ShapeDtypeStruct((nb, D_MODEL), F32),
        jax.ShapeDtypeStruct(xs_big.shape, BF16),
        jax.ShapeDtypeStruct((nb, TOP_K), I32),
        jax.ShapeDtypeStruct((1, 8, LANES), F32),
        jax.ShapeDtypeStruct((nb, LANES), F32),
        jax.ShapeDtypeStruct((nb, LANES), F32),
        jax.ShapeDtypeStruct((nb, D_MODEL), F32),
    )
    full = lambda s: pl.BlockSpec(s.shape, lambda i: (0,) * len(s.shape))
    out_specs = (
        full(out_shape[0]),
        pl.BlockSpec((lrows, XS_WIDTH), lambda i: (xs_block, 0)),
        full(out_shape[2]), full(out_shape[3]), full(out_shape[4]), full(out_shape[5]), full(out_shape[6]),
    )
    scratch = [
        pltpu.VMEM((N_HEADS * nb, LANES), F32),
        pltpu.VMEM((N_HEADS * nb, LANES), F32),
        pltpu.VMEM((nb, LANES), F32),
        pltpu.VMEM((nb, LANES), F32),
    ]
    return pl.pallas_call(
        _mixer_sample_kernel,
        grid=(1,),
        in_specs=[pl.BlockSpec(memory_space=pl.ANY)] + [_const_spec(a.shape) for a in args],
        out_specs=out_specs,
        out_shape=out_shape,
        scratch_shapes=scratch,
        input_output_aliases={0: 1},
        compiler_params=pltpu.CompilerParams(dimension_semantics=("arbitrary",), vmem_limit_bytes=VMEM_LIMIT),
        name="mixer_sample",
    )(xs_big, *args)


def _expert_kernel(be_ref, nact_ref, src_ref, dst_ref, xs_hbm, wup_ref, bup_ref, wdn_ref, bdn_ref, ys_hbm,
                   wup_s, wdn_s, perm_s, xbuf, ybuf, in_sem, out_sem):
    i = pl.program_id(0)
    n_real = nact_ref[0]
    n_active = nact_ref[1]
    e = be_ref[i]
    e_prev = be_ref[jnp.maximum(i - 1, 0)]
    slot = lax.rem(i, 2)

    def gather_copy(blk, slt, c):
        row = pl.multiple_of(src_ref[blk * CHUNKS_PER_BLOCK + c] * CHUNK, CHUNK)
        return pltpu.make_async_copy(xs_hbm.at[pl.ds(row, CHUNK), :],
                                     xbuf.at[slt, pl.ds(c * CHUNK, CHUNK), :], in_sem.at[slt])

    def scatter_chunks(blk, slt, wait):
        for c in range(CHUNKS_PER_BLOCK):
            dst = dst_ref[blk * CHUNKS_PER_BLOCK + c]

            @pl.when(dst >= 0)
            def _():
                row = pl.multiple_of(dst * CHUNK, CHUNK)
                cp = pltpu.make_async_copy(ybuf.at[slt, pl.ds(c * CHUNK, CHUNK), :],
                                           ys_hbm.at[pl.ds(row, CHUNK), :], out_sem.at[slt])
                if wait:
                    cp.wait()
                else:
                    cp.start()

    @pl.when(jnp.logical_and(i == 0, n_real > 0))
    def _():
        for c in range(CHUNKS_PER_BLOCK):
            gather_copy(0, 0, c).start()

    @pl.when(jnp.logical_and(i < n_real, jnp.logical_or(i == 0, e != e_prev)))
    def _():
        wup_s[...] = wup_ref[0].astype(BF16)
        half = LANES // 2
        for cs in range(D_MODEL // LANES):
            cols = slice(cs * LANES, (cs + 1) * LANES)
            for c in range(D_MODEL // LANES):
                for par in range(2):
                    s0 = c * LANES + par * half
                    perm_s[cs, pl.ds(c * LANES + par, half, stride=2), :] = wdn_ref[0, s0:s0 + half, cols]
            wdn_s[:, cols] = perm_s[cs].astype(BF16)

    @pl.when(i + 1 < n_real)
    def _():
        for c in range(CHUNKS_PER_BLOCK):
            gather_copy(i + 1, 1 - slot, c).start()

    @pl.when(jnp.logical_and(i >= 2, i < n_active))
    def _():
        scatter_chunks(i - 2, slot, wait=True)

    @pl.when(jnp.logical_and(i >= n_real, i < n_active))
    def _():
        ybuf[slot] = jnp.zeros((EXPERT_BLOCK, D_MODEL), BF16)
        scatter_chunks(i, slot, wait=False)

    @pl.when(i < n_real)
    def _():
        for c in range(CHUNKS_PER_BLOCK):
            gather_copy(i, slot, c).wait()

        xb = xbuf[slot]
        meta = xb[:, D_MODEL:].astype(F32)
        gate = meta[:, 0:1] + meta[:, 1:2]
        u = jnp.dot(xb[:, :D_MODEL], wup_s[...], preferred_element_type=F32) + bup_ref[0]
        even = (_lane_iota() & 1) == 0
        cols = []
        for c in range(D_MODEL // LANES):
            c0 = u[:, (2 * c) * LANES:(2 * c + 1) * LANES]
            c1 = u[:, (2 * c + 1) * LANES:(2 * c + 2) * LANES]
            glu = jnp.where(even, c0, pltpu.roll(c1, 1, axis=1))
            lin = jnp.where(even, pltpu.roll(c0, LANES - 1, axis=1), c1)
            glu = jnp.minimum(glu, SWIGLU_LIMIT)
            lin = jnp.clip(lin, -SWIGLU_LIMIT, SWIGLU_LIMIT)
            cols.append(glu * jax.nn.sigmoid(SWIGLU_ALPHA * glu) * (lin + 1.0))
        a = jnp.concatenate(cols, axis=1).astype(BF16)
        y = jnp.dot(a, wdn_s[...], preferred_element_type=F32) + bdn_ref[0]
        ybuf[slot] = (y * gate).astype(BF16)
        scatter_chunks(i, slot, wait=False)

    @pl.when(i == n_active - 1)
    def _():
        @pl.when(i >= 1)
        def _():
            scatter_chunks(i - 1, 1 - slot, wait=True)

        scatter_chunks(i, slot, wait=True)


def _experts(block_expert, n_active, chunk_src, chunk_dst, xs_big, ys_rows, w_up, b_up, w_down, b_down):
    n_blocks = block_expert.shape[0]
    wspec = lambda shape: pl.BlockSpec(shape, lambda i, be, na, cs, cd: (jnp.minimum(be[i], N_EXPERTS - 1), 0, 0))
    grid_spec = pltpu.PrefetchScalarGridSpec(
        num_scalar_prefetch=4,
        grid=(n_blocks,),
        in_specs=[
            pl.BlockSpec(memory_space=pl.ANY),
            wspec((1, D_MODEL, 2 * D_MODEL)),
            wspec((1, 1, 2 * D_MODEL)),
            wspec((1, D_MODEL, D_MODEL)),
            wspec((1, 1, D_MODEL)),
        ],
        out_specs=pl.BlockSpec(memory_space=pl.ANY),
        scratch_shapes=[
            pltpu.VMEM((D_MODEL, 2 * D_MODEL), BF16),
            pltpu.VMEM((D_MODEL, D_MODEL), BF16),
            pltpu.VMEM((D_MODEL // LANES, D_MODEL, LANES), F32),
            pltpu.VMEM((2, EXPERT_BLOCK, XS_WIDTH), BF16),
            pltpu.VMEM((2, EXPERT_BLOCK, D_MODEL), BF16),
            pltpu.SemaphoreType.DMA((2,)),
            pltpu.SemaphoreType.DMA((2,)),
        ],
    )
    return pl.pallas_call(
        _expert_kernel,
        grid_spec=grid_spec,
        out_shape=jax.ShapeDtypeStruct((ys_rows, D_MODEL), BF16),
        compiler_params=pltpu.CompilerParams(dimension_semantics=("arbitrary",), vmem_limit_bytes=VMEM_LIMIT),
        name="experts",
    )(block_expert, n_active, chunk_src, chunk_dst, xs_big, w_up, b_up.reshape(N_EXPERTS, 1, 2 * D_MODEL),
      w_down, b_down.reshape(N_EXPERTS, 1, D_MODEL))


def _combine_kernel(lp_ref, ys_ref, x1_ref, out_ref):
    tm = x1_ref.shape[0]
    lrows = ys_ref.shape[0]
    lp = lp_ref[...].astype(F32)
    rid = lax.broadcasted_iota(I32, (tm, lrows), 1).astype(F32)
    sel = rid == lp[:, 0:1]
    for k in range(1, TOP_K):
        sel = jnp.logical_or(sel, rid == lp[:, k:k + 1])
    out_ref[...] = x1_ref[...] + jnp.dot(sel.astype(BF16), ys_ref[...], preferred_element_type=F32)


def _combine(lp, ys_big, x1, tm, first_block):
    tokens = x1.shape[0]
    lrows = _local_rows(tm)
    return pl.pallas_call(
        _combine_kernel,
        grid=(tokens // tm,),
        in_specs=[
            pl.BlockSpec((tm, TOP_K), lambda j: (j, 0)),
            pl.BlockSpec((lrows, D_MODEL), lambda j: (first_block + j, 0)),
            pl.BlockSpec((tm, D_MODEL), lambda j: (j, 0)),
        ],
        out_specs=pl.BlockSpec((tm, D_MODEL), lambda j: (j, 0)),
        out_shape=jax.ShapeDtypeStruct((tokens, D_MODEL), F32),
        compiler_params=pltpu.CompilerParams(dimension_semantics=("arbitrary",), vmem_limit_bytes=VMEM_LIMIT),
        name="combine",
    )(lp, ys_big, x1)


def _bias_tables(rel_bias):
    qi = np.arange(ATTN_BLOCK)[:, None]
    kj = np.arange(2 * ATTN_BLOCK)[None, :]
    rel = qi + ATTN_BLOCK - kj
    valid = (rel >= 0) & (rel < WINDOW)
    tab = rel_bias[_t5_bucket_np(rel)]
    tab = jnp.where(valid[:, :, None], tab, NEG_INF).transpose(2, 0, 1)
    prompt_tab = tab.reshape(N_HEADS * ATTN_BLOCK, 2 * ATTN_BLOCK)
    slot_rel = np.where(np.arange(WINDOW) == 0, 0, WINDOW - np.arange(WINDOW))
    sample_tab = rel_bias[_t5_bucket_np(slot_rel)].T
    return prompt_tab, sample_tab


def _chunk_tables(seg, starts, tile_base, tile_rows, n_blocks):
    n_tiles = seg.shape[0]
    n_seg = (N_EXPERTS + 1) * n_tiles
    used = jnp.sum(seg, axis=1)
    seg_e = jnp.concatenate([seg.T, (tile_rows - used)[None, :]], axis=0)
    src0 = jnp.concatenate([tile_base[None, :] + starts.T, (tile_base + used)[None, :]], axis=0).reshape(-1)
    total = jnp.sum(seg_e, axis=1)
    region = (total + EXPERT_BLOCK - 1) // EXPERT_BLOCK * EXPERT_BLOCK
    pad_end = jnp.cumsum(region)
    pad_start = pad_end - region
    g_start = (pad_start[:, None] + jnp.cumsum(seg_e, axis=1) - seg_e).reshape(-1)
    g_end = g_start + seg_e.reshape(-1)
    rows = jnp.arange(n_blocks * CHUNKS_PER_BLOCK, dtype=I32) * CHUNK
    sid = jnp.sum((g_end[None, :] <= rows[:, None]).astype(I32), axis=1)
    sid_c = jnp.minimum(sid, n_seg - 1)
    valid = jnp.logical_and(sid < n_seg, rows >= g_start[sid_c])
    src_row = src0[sid_c] + rows - g_start[sid_c]
    chunk_src = jnp.where(valid, src_row // CHUNK, 0).astype(I32)
    chunk_dst = jnp.where(valid, src_row // CHUNK, -1).astype(I32)
    blk_rows = jnp.arange(n_blocks, dtype=I32) * EXPERT_BLOCK
    block_region = jnp.minimum(jnp.sum((pad_end[None, :] <= blk_rows[:, None]).astype(I32), axis=1),
                               N_EXPERTS).astype(I32)
    counts = jnp.stack([pad_end[N_EXPERTS - 1], pad_end[N_EXPERTS]]).astype(I32) // EXPERT_BLOCK
    return block_region, counts, chunk_src, chunk_dst


def kernel(x_prompt, x_sample, state_conv, cache_k_win, cache_v_win, rel_bias, attn_norm_w, w_in, conv_w,
           q_norm_w, k_norm_w, sinks, w_out, ffn_norm_w, w_router, b_router, w_up, b_up, w_down, b_down):
    batch, seq, _ = x_prompt.shape
    nb = x_sample.shape[0]
    anw = attn_norm_w[0].reshape(1, D_MODEL)
    fnw = ffn_norm_w[0].reshape(1, D_MODEL)
    win_bf = w_in[0].astype(BF16)
    wout_bf = w_out[0].astype(BF16)
    qnw = jnp.tile(q_norm_w[0], 2).reshape(1, LANES)
    knw = jnp.tile(k_norm_w[0], 2).reshape(1, LANES)
    br = b_router[0].reshape(1, N_EXPERTS)
    prompt_tab, sample_tab = _bias_tables(rel_bias)

    n_tiles_p = batch * seq // MIXER_TILE
    lrows_p, lrows_s = _local_rows(MIXER_TILE), _local_rows(nb)
    sample_base = n_tiles_p * lrows_p
    assert sample_base % lrows_s == 0
    xs_rows = sample_base + lrows_s
    ys_rows = xs_rows

    x1p, xs_big, lpp, statp, kp, vp, cp = _mixer_prompt(
        x_prompt, xs_rows, sinks[0], anw, win_bf, conv_w[0], qnw, knw, prompt_tab, wout_bf, fnw, w_router[0], br)

    k_past = cache_k_win[0].reshape(nb, WINDOW, LANES)
    v_past = cache_v_win[0].reshape(nb, WINDOW, LANES)
    x1s, xs_big, lps, stats, knew, vnew, unew = _mixer_sample(
        xs_big, sample_base // lrows_s, x_sample.reshape(nb, D_MODEL), state_conv[0, :, 0, :], state_conv[0, :, 1, :],
        k_past, v_past, sinks[0].reshape(N_HEADS, 1), anw, win_bf, conv_w[0], qnw, knw, sample_tab, wout_bf, fnw,
        w_router[0], br)

    stat = jnp.concatenate([statp, stats], axis=0)
    seg = stat[:, 0, :N_EXPERTS].astype(I32)
    starts = stat[:, 1, :N_EXPERTS].astype(I32)
    tile_base = jnp.concatenate([jnp.arange(n_tiles_p, dtype=I32) * lrows_p, jnp.array([sample_base], I32)])
    tile_rows = jnp.array([lrows_p] * n_tiles_p + [lrows_s], I32)
    n_blocks = -(-(xs_rows + (N_EXPERTS + 1) * (EXPERT_BLOCK - 1)) // EXPERT_BLOCK)
    block_expert, n_active, chunk_src, chunk_dst = _chunk_tables(seg, starts, tile_base, tile_rows, n_blocks)

    ys_big = _experts(block_expert, n_active, chunk_src, chunk_dst, xs_big, ys_rows, w_up[0], b_up[0], w_down[0],
                      b_down[0])

    y_prompt = _combine(lpp, ys_big, x1p, MIXER_TILE, 0).reshape(batch, seq, D_MODEL)
    y_sample = _combine(lps, ys_big, x1s, nb, sample_base // lrows_s).reshape(nb, 1, D_MODEL)

    new_k_sample = jnp.concatenate([k_past[:, 1:], knew[:, None, :]], axis=1)
    new_v_sample = jnp.concatenate([v_past[:, 1:], vnew[:, None, :]], axis=1)
    new_conv_sample = jnp.stack([state_conv[0, :, 1, :], unew], axis=1)
    return (
        y_prompt,
        y_sample,
        cp[None],
        kp.reshape(1, batch, ATTN_BLOCK, N_KV, HEAD_DIM),
        vp.reshape(1, batch, ATTN_BLOCK, N_KV, HEAD_DIM),
        new_conv_sample[None],
        new_k_sample.reshape(1, nb, WINDOW, N_KV, HEAD_DIM),
        new_v_sample.reshape(1, nb, WINDOW, N_KV, HEAD_DIM),
    )
```

```python
import math

import numpy as np
import jax
import jax.numpy as jnp
from jax import lax
from jax.experimental import pallas as pl
from jax.experimental.pallas import tpu as pltpu

F32 = jnp.float32
BF16 = jnp.bfloat16
I32 = jnp.int32

D_MODEL = 1024
HEAD_DIM = 64
N_HEADS = 16
N_KV = 2
GROUP = N_HEADS // N_KV
WINDOW = 128
ATTN_BLOCK = 128
N_BUCKETS = 32
MAX_DISTANCE = 128
NEG_INF = -1e30
N_EXPERTS = 32
TOP_K = 4
SWIGLU_ALPHA = 1.702
SWIGLU_LIMIT = 7.0
EPS = 1e-5
ATTN_SCALE = HEAD_DIM ** -0.5

OFF_XIN, OFF_BG, OFF_CG, OFF_Q = 0, 1024, 2048, 3072
OFF_K, OFF_V, OFF_GC, OFF_GA = 4096, 4224, 4352, 5376
IN_DIM = 6400

LANES = 128
MIXER_TILE = 512
EXPERT_BLOCK = 256
CHUNK = 16
CHUNKS_PER_BLOCK = EXPERT_BLOCK // CHUNK
XS_WIDTH = D_MODEL + LANES
VMEM_LIMIT = 60 * 1024 * 1024


def _local_rows(tm):
    need = tm * TOP_K + N_EXPERTS * (CHUNK - 1)
    return -(-need // 512) * 512


def _t5_bucket_np(rel):
    n = np.maximum(rel, 0)
    max_exact = N_BUCKETS // 2
    nf = np.maximum(n, 1).astype(np.float64)
    large = max_exact + (np.log(nf / max_exact) / math.log(MAX_DISTANCE / max_exact)
                         * (N_BUCKETS - max_exact)).astype(np.int32)
    large = np.minimum(large, N_BUCKETS - 1)
    return np.where(n < max_exact, n, large).astype(np.int32)


def _rms(x, w):
    return x * lax.rsqrt(jnp.mean(x * x, axis=-1, keepdims=True) + EPS) * w


def _lane_iota(rows=1):
    return lax.broadcasted_iota(I32, (rows, LANES), 1)


def _lo_half():
    return _lane_iota() < HEAD_DIM


def _pair_norm(t, w128):
    lo = _lo_half()
    sq = t * t
    s_lo = jnp.sum(jnp.where(lo, sq, 0.0), axis=-1, keepdims=True)
    s_hi = jnp.sum(jnp.where(lo, 0.0, sq), axis=-1, keepdims=True)
    r = jnp.where(lo, lax.rsqrt(s_lo * (1.0 / HEAD_DIM) + EPS), lax.rsqrt(s_hi * (1.0 / HEAD_DIM) + EPS))
    return t * r * w128


def _top4_gates(logits):
    rows = logits.shape[0]
    lane = lax.broadcasted_iota(I32, (rows, N_EXPERTS), 1).astype(F32)
    vals, idxs = [], []
    l = logits
    for _ in range(TOP_K):
        m = jnp.max(l, axis=-1, keepdims=True)
        idx = jnp.min(jnp.where(l == m, lane, float(N_EXPERTS)), axis=-1, keepdims=True)
        vals.append(m)
        idxs.append(idx)
        l = jnp.where(lane == idx, -jnp.inf, l)
    es = [jnp.exp(v - vals[0]) for v in vals]
    den = es[0] + es[1] + es[2] + es[3]
    return idxs, [e / den for e in es]


def _cols_to_lanes(cols, rows, fill=0.0):
    lane = _lane_iota(rows)
    out = jnp.full((rows, LANES), fill, F32)
    for k, c in enumerate(cols):
        out = jnp.where(lane == k, c, out)
    return out


def _dispatch(h2, idxs, gates, xs_ref, lp_ref, stat_ref):
    tm = h2.shape[0]
    lrows = xs_ref.shape[0]
    lane = _lane_iota(tm)
    lane_f = lane.astype(F32)
    member = jnp.zeros((tm, LANES), F32)
    for k in range(TOP_K):
        member = member + jnp.where(lane_f == idxs[k], 1.0, 0.0)
    cnt = jnp.sum(member, axis=0, keepdims=True)
    seg = jnp.floor((cnt + (CHUNK - 1)) * (1.0 / CHUNK)) * CHUNK
    lane1 = _lane_iota()
    incl = seg
    for s in (1, 2, 4, 8, 16):
        incl = incl + jnp.where(lane1 >= s, pltpu.roll(incl, s, axis=1), 0.0)
    starts = incl - seg
    row8 = lax.broadcasted_iota(I32, (8, LANES), 0)
    stat_ref[0] = jnp.where(row8 == 0, seg, jnp.where(row8 == 1, starts, 0.0))

    tri = (lax.broadcasted_iota(I32, (tm, tm), 1) < lax.broadcasted_iota(I32, (tm, tm), 0)).astype(BF16)
    rank = jnp.dot(tri, member.astype(BF16), preferred_element_type=F32)
    pos = starts + rank
    lps = [jnp.sum(jnp.where(lane_f == idxs[k], pos, 0.0), axis=-1, keepdims=True) for k in range(TOP_K)]
    lane4 = lax.broadcasted_iota(I32, (tm, TOP_K), 1)
    lp_ref[...] = jnp.where(lane4 == 0, lps[0], jnp.where(lane4 == 1, lps[1],
                                                          jnp.where(lane4 == 2, lps[2], lps[3]))).astype(I32)
    lp_t = _cols_to_lanes(lps, tm, fill=-1.0).T

    g_hi = [g.astype(BF16).astype(F32) for g in gates]
    g_lo = [g - h for g, h in zip(gates, g_hi)]
    meta_in = _cols_to_lanes(list(idxs) + g_hi + g_lo, tm)
    rhs = jnp.concatenate([h2.astype(BF16), meta_in.astype(BF16)], axis=1)

    sub = 512
    starts32 = jnp.where(lane1 < N_EXPERTS, starts, 1e9)

    def rows_chunk(c, carry):
        r0 = pl.multiple_of(c * sub, sub)
        rid = (lax.broadcasted_iota(I32, (sub, 1), 0) + r0).astype(F32)
        rid_t = lax.broadcasted_iota(I32, (sub, tm), 0).astype(F32) + r0.astype(F32)
        perm = rid_t == lp_t[0:1, :]
        for k in range(1, TOP_K):
            perm = jnp.logical_or(perm, rid_t == lp_t[k:k + 1, :])
        full = jnp.dot(perm.astype(BF16), rhs, preferred_element_type=F32)
        got = full[:, D_MODEL:]
        e_row = jnp.sum(jnp.where(rid >= starts32, 1.0, 0.0), axis=-1, keepdims=True) - 1.0
        g_sum = pltpu.roll(got, LANES - TOP_K, axis=1) + pltpu.roll(got, LANES - 2 * TOP_K, axis=1)
        lane_s = _lane_iota(sub)
        pick = jnp.logical_and(lane_s < TOP_K, got == e_row)
        gate_row = jnp.sum(jnp.where(pick, g_sum, 0.0), axis=-1, keepdims=True)
        gr_hi = gate_row.astype(BF16).astype(F32)
        meta = jnp.where(lane_s == 0, gr_hi, jnp.where(lane_s == 1, gate_row - gr_hi, 0.0))
        xs_ref[pl.ds(r0, sub), 0:D_MODEL] = full[:, :D_MODEL].astype(BF16)
        xs_ref[pl.ds(r0, sub), D_MODEL:XS_WIDTH] = meta.astype(BF16)
        return carry

    lax.fori_loop(0, lrows // sub, rows_chunk, 0)


def _epilogue(x, merged, wout_ref, fnw_ref, wr_ref, br_ref, x1_ref, xs_ref, lp_ref, stat_ref):
    x1 = x + jnp.dot(merged.astype(BF16), wout_ref[...], preferred_element_type=F32)
    x1_ref[...] = x1
    h2 = _rms(x1, fnw_ref[...])
    logits = jnp.dot(h2.astype(BF16), wr_ref[...].astype(BF16), preferred_element_type=F32) + br_ref[...]
    idxs, gates = _top4_gates(logits)
    _dispatch(h2, idxs, gates, xs_ref, lp_ref, stat_ref)


def _mixer_prompt_kernel(sink_ref, x_ref, anw_ref, win_ref, convw_ref, qnw_ref, knw_ref, bias_ref, wout_ref,
                         fnw_ref, wr_ref, br_ref,
                         x1_ref, xs_ref, lp_ref, stat_ref, kout_ref, vout_ref, cout_ref,
                         ubuf, q_s, kd_s, vt_s, ya_s, st_s, pt_s):
    tm = x_ref.shape[0]
    nblk = tm // ATTN_BLOCK
    j = pl.program_id(1)
    first_tile = j == 0
    lo = _lo_half()

    x = x_ref[...]
    h = _rms(x, anw_ref[...]).astype(BF16)

    def proj(off, n):
        return jnp.dot(h, win_ref[:, off:off + n], preferred_element_type=F32)

    @pl.when(first_tile)
    def _():
        ubuf[0:8, :] = jnp.zeros((8, D_MODEL), F32)
        kd_s[0:ATTN_BLOCK, :] = jnp.zeros((ATTN_BLOCK, 2 * LANES), BF16)
        vt_s[0] = jnp.zeros((N_KV, LANES, ATTN_BLOCK), BF16)

    u = proj(OFF_CG, D_MODEL) * proj(OFF_XIN, D_MODEL)
    ubuf[8:tm + 8, :] = u
    cw = convw_ref[...]
    conv = ubuf[6:tm + 6, :] * cw[0:1, :] + ubuf[7:tm + 7, :] * cw[1:2, :] + u * cw[2:3, :]
    merged = jax.nn.sigmoid(proj(OFF_GC, D_MODEL)) * (proj(OFF_BG, D_MODEL) * conv)
    tail = ubuf[tm + 6:tm + 8, :]
    cout_ref[0] = tail
    ubuf[6:8, :] = tail

    q = proj(OFF_Q, D_MODEL)
    kv = proj(OFF_K, 2 * LANES)
    k = _pair_norm(kv[:, :LANES], knw_ref[...])
    v = kv[:, LANES:]
    qnw = qnw_ref[...]
    for p in range(N_HEADS // 2):
        sl = slice(p * LANES, (p + 1) * LANES)
        q_s[:, sl] = (_pair_norm(q[:, sl], qnw) * ATTN_SCALE).astype(BF16)
    k_sw = pltpu.roll(k, HEAD_DIM, axis=1)
    v_sw = pltpu.roll(v, HEAD_DIM, axis=1)
    kd_s[ATTN_BLOCK:tm + ATTN_BLOCK, 0:LANES] = jnp.where(lo, k, k_sw).astype(BF16)
    kd_s[ATTN_BLOCK:tm + ATTN_BLOCK, LANES:2 * LANES] = jnp.where(lo, k_sw, k).astype(BF16)
    v_dup = (jnp.where(lo, v, v_sw), jnp.where(lo, v_sw, v))
    for b in range(nblk):
        for g in range(N_KV):
            vt_s[b + 1, g] = v_dup[g][b * ATTN_BLOCK:(b + 1) * ATTN_BLOCK, :].T.astype(BF16)

    @pl.when(j == pl.num_programs(1) - 1)
    def _():
        kout_ref[0] = k[tm - ATTN_BLOCK:, :]
        vout_ref[0] = v[tm - ATTN_BLOCK:, :]

    prev_rows = lax.broadcasted_iota(I32, (2 * ATTN_BLOCK, 1), 0) < ATTN_BLOCK
    feat_lo = lax.broadcasted_iota(I32, (LANES, 1), 0) < HEAD_DIM

    def attn_block(blk):
        r0 = blk * ATTN_BLOCK
        qb = q_s[r0:r0 + ATTN_BLOCK, :]
        for g in range(N_KV):
            parts = []
            for t in range(GROUP):
                hd = g * GROUP + t
                slab = qb[:, (hd // 2) * LANES:(hd // 2 + 1) * LANES]
                keep = lo if hd % 2 == 0 else jnp.logical_not(lo)
                parts.append(jnp.where(keep, slab, jnp.zeros_like(slab)))
            lhs = jnp.concatenate(parts, axis=0)
            st = lax.dot_general(kd_s[r0:r0 + 2 * ATTN_BLOCK, g * LANES:(g + 1) * LANES], lhs,
                                 (((1,), (1,)), ((), ())), preferred_element_type=F32)
            st = st + bias_ref[g]
            if blk == 0:
                st = jnp.where(jnp.logical_and(prev_rows, first_tile), NEG_INF, st)
            st_s[...] = st
            sink = sink_ref[g:g + 1, :]
            rdens = []
            for t in range(GROUP):
                cols = slice(t * LANES, (t + 1) * LANES)
                s = st_s[:, cols]
                m = jnp.maximum(jnp.max(s, axis=0, keepdims=True), sink[:, cols])
                pr = jnp.exp(s - m)
                den = jnp.sum(pr, axis=0, keepdims=True) + jnp.exp(sink[:, cols] - m)
                pt_s[:, cols] = pr.astype(BF16)
                rdens.append(1.0 / den)
            vt = jnp.concatenate([vt_s[blk, g], vt_s[blk + 1, g]], axis=1)
            ot = jnp.dot(vt, pt_s[...], preferred_element_type=F32)
            for i in range(GROUP // 2):
                pair = g * (GROUP // 2) + i
                even = ot[:, (2 * i) * LANES:(2 * i + 1) * LANES] * rdens[2 * i]
                odd = ot[:, (2 * i + 1) * LANES:(2 * i + 2) * LANES] * rdens[2 * i + 1]
                ya_s[r0:r0 + ATTN_BLOCK, pair * LANES:(pair + 1) * LANES] = jnp.where(feat_lo, even, odd).T

    for blk in range(nblk):
        attn_block(blk)

    kd_s[0:ATTN_BLOCK, :] = kd_s[tm:tm + ATTN_BLOCK, :]
    vt_s[0] = vt_s[nblk]

    merged = merged + jax.nn.sigmoid(proj(OFF_GA, D_MODEL)) * ya_s[...]
    _epilogue(x, merged, wout_ref, fnw_ref, wr_ref, br_ref, x1_ref, xs_ref, lp_ref, stat_ref)


def _const_spec(shape):
    nd = len(shape)
    return pl.BlockSpec(shape, lambda *_: (0,) * nd, pipeline_mode=pl.Buffered(1))


def _mixer_prompt(x, xs_rows, sinks, anw, win_bf, convw, qnw, knw, bias_tab, wout_bf, fnw, wr, br):
    batch, seq, _ = x.shape
    tm = MIXER_TILE
    lrows = _local_rows(tm)
    nj = seq // tm
    tokens = batch * seq
    n_tiles = batch * nj
    x2 = x.reshape(tokens, D_MODEL)
    tok_spec = lambda width: pl.BlockSpec((tm, width), lambda b, j: (b * nj + j, 0))
    per_batch = lambda rows, width: pl.BlockSpec((1, rows, width), lambda b, j: (b, 0, 0))
    in_specs = [
        _const_spec((N_KV, GROUP * ATTN_BLOCK)),
        tok_spec(D_MODEL),
        _const_spec((1, D_MODEL)),
        _const_spec((D_MODEL, IN_DIM)),
        _const_spec((3, D_MODEL)),
        _const_spec((1, LANES)),
        _const_spec((1, LANES)),
        _const_spec((N_KV, 2 * ATTN_BLOCK, GROUP * ATTN_BLOCK)),
        _const_spec((D_MODEL, D_MODEL)),
        _const_spec((1, D_MODEL)),
        _const_spec((D_MODEL, N_EXPERTS)),
        _const_spec((1, N_EXPERTS)),
    ]
    out_shape = (
        jax.ShapeDtypeStruct((tokens, D_MODEL), F32),
        jax.ShapeDtypeStruct((xs_rows, XS_WIDTH), BF16),
        jax.ShapeDtypeStruct((tokens, TOP_K), I32),
        jax.ShapeDtypeStruct((n_tiles, 8, LANES), F32),
        jax.ShapeDtypeStruct((batch, ATTN_BLOCK, LANES), F32),
        jax.ShapeDtypeStruct((batch, ATTN_BLOCK, LANES), F32),
        jax.ShapeDtypeStruct((batch, 2, D_MODEL), F32),
    )
    out_specs = (
        tok_spec(D_MODEL),
        pl.BlockSpec((lrows, XS_WIDTH), lambda b, j: (b * nj + j, 0)),
        tok_spec(TOP_K),
        pl.BlockSpec((1, 8, LANES), lambda b, j: (b * nj + j, 0, 0)),
        per_batch(ATTN_BLOCK, LANES), per_batch(ATTN_BLOCK, LANES), per_batch(2, D_MODEL),
    )
    scratch = [
        pltpu.VMEM((tm + 8, D_MODEL), F32),
        pltpu.VMEM((tm, D_MODEL), BF16),
        pltpu.VMEM((tm + ATTN_BLOCK, 2 * LANES), BF16),
        pltpu.VMEM((tm // ATTN_BLOCK + 1, N_KV, LANES, ATTN_BLOCK), BF16),
        pltpu.VMEM((tm, D_MODEL), F32),
        pltpu.VMEM((2 * ATTN_BLOCK, GROUP * ATTN_BLOCK), F32),
        pltpu.VMEM((2 * ATTN_BLOCK, GROUP * ATTN_BLOCK), BF16),
    ]
    return pl.pallas_call(
        _mixer_prompt_kernel,
        grid=(batch, nj),
        in_specs=in_specs,
        out_specs=out_specs,
        out_shape=out_shape,
        scratch_shapes=scratch,
        compiler_params=pltpu.CompilerParams(
            dimension_semantics=("arbitrary", "arbitrary"), vmem_limit_bytes=VMEM_LIMIT),
        name="mixer_prompt",
    )(sinks, x2, anw, win_bf, convw, qnw, knw, bias_tab, wout_bf, fnw, wr, br)


def _mixer_sample_kernel(xs_in_ref, x_ref, p0_ref, p1_ref, kp_ref, vp_ref, sink_ref, anw_ref, win_ref, convw_ref,
                         qnw_ref, knw_ref, bias_ref, wout_ref, fnw_ref, wr_ref, br_ref,
                         x1_ref, xs_ref, lp_ref, stat_ref, knew_ref, vnew_ref, unew_ref,
                         qh_s, o_s, kn_s, vn_s):
    del xs_in_ref
    nb = x_ref.shape[0]
    lo = _lo_half()
    x = x_ref[...]
    h = _rms(x, anw_ref[...]).astype(BF16)

    def proj(off, n):
        return jnp.dot(h, win_ref[:, off:off + n], preferred_element_type=F32)

    u = proj(OFF_CG, D_MODEL) * proj(OFF_XIN, D_MODEL)
    unew_ref[...] = u
    cw = convw_ref[...]
    conv = p0_ref[...] * cw[0:1, :] + p1_ref[...] * cw[1:2, :] + u * cw[2:3, :]
    merged = jax.nn.sigmoid(proj(OFF_GC, D_MODEL)) * (proj(OFF_BG, D_MODEL) * conv)

    q = proj(OFF_Q, D_MODEL)
    kv = proj(OFF_K, 2 * LANES)
    k = _pair_norm(kv[:, :LANES], knw_ref[...])
    v = kv[:, LANES:]
    knew_ref[...] = k
    vnew_ref[...] = v
    kn_s[...] = k
    vn_s[...] = v

    qnw = qnw_ref[...]
    for hd in range(N_HEADS):
        pair, half, grp = hd // 2, hd % 2, hd // GROUP
        slab = _pair_norm(q[:, pair * LANES:(pair + 1) * LANES], qnw) * ATTN_SCALE
        slab = jnp.where(lo if half == 0 else jnp.logical_not(lo), slab, 0.0)
        if half != grp:
            slab = pltpu.roll(slab, HEAD_DIM, axis=1)
        qh_s[hd * nb:(hd + 1) * nb, :] = slab

    sink = sink_ref[...]
    bias = bias_ref[...]
    row0 = lax.broadcasted_iota(I32, (WINDOW, 1), 0) == 0

    def token(b, carry):
        qb = qh_s[pl.ds(b, N_HEADS, stride=nb), :]
        kb = jnp.where(row0, kn_s[pl.ds(b, 1), :], kp_ref[b])
        vb = jnp.where(row0, vn_s[pl.ds(b, 1), :], vp_ref[b])
        s = lax.dot_general(qb.astype(BF16), kb.astype(BF16), (((1,), (1,)), ((), ())),
                            preferred_element_type=F32) + bias
        m = jnp.maximum(jnp.max(s, axis=-1, keepdims=True), sink)
        pr = jnp.exp(s - m)
        den = jnp.sum(pr, axis=-1, keepdims=True) + jnp.exp(sink - m)
        o = jnp.dot(pr.astype(BF16), vb.astype(BF16), preferred_element_type=F32) / den
        o_s[pl.ds(b, N_HEADS, stride=nb), :] = o
        return carry

    lax.fori_loop(0, nb, token, 0)

    cols = []
    for pair in range(N_HEADS // 2):
        halves = []
        for half in range(2):
            hd = 2 * pair + half
            slab = o_s[hd * nb:(hd + 1) * nb, :]
            if half != hd // GROUP:
                slab = pltpu.roll(slab, HEAD_DIM, axis=1)
            halves.append(slab)
        cols.append(jnp.where(lo, halves[0], halves[1]))
    y_attn = jnp.concatenate(cols, axis=1)

    merged = merged + jax.nn.sigmoid(proj(OFF_GA, D_MODEL)) * y_attn
    _epilogue(x, merged, wout_ref, fnw_ref, wr_ref, br_ref, x1_ref, xs_ref, lp_ref, stat_ref)


def _mixer_sample(xs_big, xs_block, x, p0, p1, k_past, v_past, sink_col, anw, win_bf, convw, qnw, knw, bias_s,
                  wout_bf, fnw, wr, br):
    nb = x.shape[0]
    lrows = _local_rows(nb)
    args = (x, p0, p1, k_past, v_past, sink_col, anw, win_bf, convw, qnw, knw, bias_s, wout_bf, fnw, wr, br)
    out_shape = (
        jax.ShapeDtypeStruct((nb, D_MODEL), F32),
        jax.ShapeDtypeStruct(xs_big.shape, BF16),
        jax.ShapeDtypeStruct((nb, TOP_K), I32),
        jax.ShapeDtypeStruct((1, 8, LANES), F32),
        jax.ShapeDtypeStruct((nb, LANES), F32),
        jax.ShapeDtypeStruct((nb, LANES), F32),
        jax.ShapeDtypeStruct((nb, D_MODEL), F32),
    )
    full = lambda s: pl.BlockSpec(s.shape, lambda i: (0,) * len(s.shape))
    out_specs = (
        full(out_shape[0]),
        pl.BlockSpec((lrows, XS_WIDTH), lambda i: (xs_block, 0)),
        full(out_shape[2]), full(out_shape[3]), full(out_shape[4]), full(out_shape[5]), full(out_shape[6]),
    )
    scratch = [
        pltpu.VMEM((N_HEADS * nb, LANES), F32),
        pltpu.VMEM((N_HEADS * nb, LANES), F32),
        pltpu.VMEM((nb, LANES), F32),
        pltpu.VMEM((nb, LANES), F32),
    ]
    return pl.pallas_call(
        _mixer_sample_kernel,
        grid=(1,),
        in_specs=[pl.BlockSpec(memory_space=pl.ANY)] + [_const_spec(a.shape) for a in args],
        out_specs=out_specs,
        out_shape=out_shape,
        scratch_shapes=scratch,
        input_output_aliases={0: 1},
        compiler_params=pltpu.CompilerParams(dimension_semantics=("arbitrary",), vmem_limit_bytes=VMEM_LIMIT),
        name="mixer_sample",
    )(xs_big, *args)


def _expert_kernel(be_ref, nact_ref, src_ref, dst_ref, xs_hbm, wup_ref, bup_ref, wdn_ref, bdn_ref, ys_hbm,
                   wup_s, wdn_s, perm_s, xbuf, ybuf, in_sem, out_sem):
    i = pl.program_id(0)
    n_real = nact_ref[0]
    n_active = nact_ref[1]
    e = be_ref[i]
    e_prev = be_ref[jnp.maximum(i - 1, 0)]
    slot = lax.rem(i, 2)

    def gather_copy(blk, slt, c):
        row = pl.multiple_of(src_ref[blk * CHUNKS_PER_BLOCK + c] * CHUNK, CHUNK)
        return pltpu.make_async_copy(xs_hbm.at[pl.ds(row, CHUNK), :],
                                     xbuf.at[slt, pl.ds(c * CHUNK, CHUNK), :], in_sem.at[slt])

    def scatter_chunks(blk, slt, wait):
        for c in range(CHUNKS_PER_BLOCK):
            dst = dst_ref[blk * CHUNKS_PER_BLOCK + c]

            @pl.when(dst >= 0)
            def _():
                row = pl.multiple_of(dst * CHUNK, CHUNK)
                cp = pltpu.make_async_copy(ybuf.at[slt, pl.ds(c * CHUNK, CHUNK), :],
                                           ys_hbm.at[pl.ds(row, CHUNK), :], out_sem.at[slt])
                if wait:
                    cp.wait()
                else:
                    cp.start()

    @pl.when(jnp.logical_and(i == 0, n_real > 0))
    def _():
        for c in range(CHUNKS_PER_BLOCK):
            gather_copy(0, 0, c).start()

    @pl.when(jnp.logical_and(i < n_real, jnp.logical_or(i == 0, e != e_prev)))
    def _():
        wup_s[...] = wup_ref[0].astype(BF16)
        half = LANES // 2
        for cs in range(D_MODEL // LANES):
            cols = slice(cs * LANES, (cs + 1) * LANES)
            for c in range(D_MODEL // LANES):
                for par in range(2):
                    s0 = c * LANES + par * half
                    perm_s[cs, pl.ds(c * LANES + par, half, stride=2), :] = wdn_ref[0, s0:s0 + half, cols]
            wdn_s[:, cols] = perm_s[cs].astype(BF16)

    @pl.when(i + 1 < n_real)
    def _():
        for c in range(CHUNKS_PER_BLOCK):
            gather_copy(i + 1, 1 - slot, c).start()

    @pl.when(jnp.logical_and(i >= 2, i < n_active))
    def _():
        scatter_chunks(i - 2, slot, wait=True)

    @pl.when(jnp.logical_and(i >= n_real, i < n_active))
    def _():
        ybuf[slot] = jnp.zeros((EXPERT_BLOCK, D_MODEL), BF16)
        scatter_chunks(i, slot, wait=False)

    @pl.when(i < n_real)
    def _():
        for c in range(CHUNKS_PER_BLOCK):
            gather_copy(i, slot, c).wait()

        xb = xbuf[slot]
        meta = xb[:, D_MODEL:].astype(F32)
        gate = meta[:, 0:1] + meta[:, 1:2]
        u = jnp.dot(xb[:, :D_MODEL], wup_s[...], preferred_element_type=F32) + bup_ref[0]
        even = (_lane_iota() & 1) == 0
        cols = []
        for c in range(D_MODEL // LANES):
            c0 = u[:, (2 * c) * LANES:(2 * c + 1) * LANES]
            c1 = u[:, (2 * c + 1) * LANES:(2 * c + 2) * LANES]
            glu = jnp.where(even, c0, pltpu.roll(c1, 1, axis=1))
            lin = jnp.where(even, pltpu.roll(c0, LANES - 1, axis=1), c1)
            glu = jnp.minimum(glu, SWIGLU_LIMIT)
            lin = jnp.clip(lin, -SWIGLU_LIMIT, SWIGLU_LIMIT)
            cols.append(glu * jax.nn.sigmoid(SWIGLU_ALPHA * glu) * (lin + 1.0))
        a = jnp.concatenate(cols, axis=1).astype(BF16)
        y = jnp.dot(a, wdn_s[...], preferred_element_type=F32) + bdn_ref[0]
        ybuf[slot] = (y * gate).astype(BF16)
        scatter_chunks(i, slot, wait=False)

    @pl.when(i == n_active - 1)
    def _():
        @pl.when(i >= 1)
        def _():
            scatter_chunks(i - 1, 1 - slot, wait=True)

        scatter_chunks(i, slot, wait=True)


def _experts(block_expert, n_active, chunk_src, chunk_dst, xs_big, ys_rows, w_up, b_up, w_down, b_down):
    n_blocks = block_expert.shape[0]
    wspec = lambda shape: pl.BlockSpec(shape, lambda i, be, na, cs, cd: (jnp.minimum(be[i], N_EXPERTS - 1), 0, 0))
    grid_spec = pltpu.PrefetchScalarGridSpec(
        num_scalar_prefetch=4,
        grid=(n_blocks,),
        in_specs=[
            pl.BlockSpec(memory_space=pl.ANY),
            wspec((1, D_MODEL, 2 * D_MODEL)),
            wspec((1, 1, 2 * D_MODEL)),
            wspec((1, D_MODEL, D_MODEL)),
            wspec((1, 1, D_MODEL)),
        ],
        out_specs=pl.BlockSpec(memory_space=pl.ANY),
        scratch_shapes=[
            pltpu.VMEM((D_MODEL, 2 * D_MODEL), BF16),
            pltpu.VMEM((D_MODEL, D_MODEL), BF16),
            pltpu.VMEM((D_MODEL // LANES, D_MODEL, LANES), F32),
            pltpu.VMEM((2, EXPERT_BLOCK, XS_WIDTH), BF16),
            pltpu.VMEM((2, EXPERT_BLOCK, D_MODEL), BF16),
            pltpu.SemaphoreType.DMA((2,)),
            pltpu.SemaphoreType.DMA((2,)),
        ],
    )
    return pl.pallas_call(
        _expert_kernel,
        grid_spec=grid_spec,
        out_shape=jax.ShapeDtypeStruct((ys_rows, D_MODEL), BF16),
        compiler_params=pltpu.CompilerParams(dimension_semantics=("arbitrary",), vmem_limit_bytes=VMEM_LIMIT),
        name="experts",
    )(block_expert, n_active, chunk_src, chunk_dst, xs_big, w_up, b_up.reshape(N_EXPERTS, 1, 2 * D_MODEL),
      w_down, b_down.reshape(N_EXPERTS, 1, D_MODEL))


def _combine_kernel(lp_ref, ys_ref, x1_ref, out_ref):
    tm = x1_ref.shape[0]
    lrows = ys_ref.shape[0]
    lp = lp_ref[...].astype(F32)
    rid = lax.broadcasted_iota(I32, (tm, lrows), 1).astype(F32)
    sel = rid == lp[:, 0:1]
    for k in range(1, TOP_K):
        sel = jnp.logical_or(sel, rid == lp[:, k:k + 1])
    out_ref[...] = x1_ref[...] + jnp.dot(sel.astype(BF16), ys_ref[...], preferred_element_type=F32)


def _combine(lp, ys_big, x1, tm, first_block):
    tokens = x1.shape[0]
    lrows = _local_rows(tm)
    return pl.pallas_call(
        _combine_kernel,
        grid=(tokens // tm,),
        in_specs=[
            pl.BlockSpec((tm, TOP_K), lambda j: (j, 0)),
            pl.BlockSpec((lrows, D_MODEL), lambda j: (first_block + j, 0)),
            pl.BlockSpec((tm, D_MODEL), lambda j: (j, 0)),
        ],
        out_specs=pl.BlockSpec((tm, D_MODEL), lambda j: (j, 0)),
        out_shape=jax.ShapeDtypeStruct((tokens, D_MODEL), F32),
        compiler_params=pltpu.CompilerParams(dimension_semantics=("arbitrary",), vmem_limit_bytes=VMEM_LIMIT),
        name="combine",
    )(lp, ys_big, x1)


def _bias_tables(rel_bias):
    qi = np.arange(ATTN_BLOCK)[:, None]
    kj = np.arange(2 * ATTN_BLOCK)[None, :]
    rel = qi + ATTN_BLOCK - kj
    valid = (rel >= 0) & (rel < WINDOW)
    bucket = np.where(valid, _t5_bucket_np(rel), -1)
    slot_rel = np.where(np.arange(WINDOW) == 0, 0, WINDOW - np.arange(WINDOW))
    slot_bucket = _t5_bucket_np(slot_rel)
    tab = jnp.full((N_HEADS, ATTN_BLOCK, 2 * ATTN_BLOCK), NEG_INF, F32)
    sample_tab = jnp.zeros((N_HEADS, WINDOW), F32)
    for b in range(N_BUCKETS):
        tab = jnp.where((bucket == b)[None], rel_bias[b][:, None, None], tab)
        sample_tab = jnp.where((slot_bucket == b)[None], rel_bias[b][:, None], sample_tab)
    prompt_tab = tab.reshape(N_KV, GROUP, ATTN_BLOCK, 2 * ATTN_BLOCK).transpose(0, 3, 1, 2)
    return prompt_tab.reshape(N_KV, 2 * ATTN_BLOCK, GROUP * ATTN_BLOCK), sample_tab


def _chunk_tables(seg, starts, tile_base, tile_rows, n_blocks):
    n_tiles = seg.shape[0]
    n_seg = (N_EXPERTS + 1) * n_tiles
    used = jnp.sum(seg, axis=1)
    seg_e = jnp.concatenate([seg.T, (tile_rows - used)[None, :]], axis=0)
    src0 = jnp.concatenate([tile_base[None, :] + starts.T, (tile_base + used)[None, :]], axis=0).reshape(-1)
    total = jnp.sum(seg_e, axis=1)
    region = (total + EXPERT_BLOCK - 1) // EXPERT_BLOCK * EXPERT_BLOCK
    pad_end = jnp.cumsum(region)
    pad_start = pad_end - region
    g_start = (pad_start[:, None] + jnp.cumsum(seg_e, axis=1) - seg_e).reshape(-1)
    g_end = g_start + seg_e.reshape(-1)
    rows = jnp.arange(n_blocks * CHUNKS_PER_BLOCK, dtype=I32) * CHUNK
    passed = g_end[None, :] <= rows[:, None]

    def at_segment(table, sentinel):
        ext = jnp.concatenate([table, jnp.array([sentinel], I32)])
        return ext[0] + jnp.sum(jnp.where(passed, (ext[1:] - ext[:-1])[None, :], 0), axis=1)

    seg_start = at_segment(g_start, 1 << 30)
    valid = rows >= seg_start
    src_row = at_segment(src0, 0) + rows - seg_start
    chunk_src = jnp.where(valid, src_row // CHUNK, 0).astype(I32)
    chunk_dst = jnp.where(valid, src_row // CHUNK, -1).astype(I32)
    blk_rows = jnp.arange(n_blocks, dtype=I32) * EXPERT_BLOCK
    block_region = jnp.minimum(jnp.sum((pad_end[None, :] <= blk_rows[:, None]).astype(I32), axis=1),
                               N_EXPERTS).astype(I32)
    counts = jnp.stack([pad_end[N_EXPERTS - 1], pad_end[N_EXPERTS]]).astype(I32) // EXPERT_BLOCK
    return block_region, counts, chunk_src, chunk_dst


def kernel(x_prompt, x_sample, state_conv, cache_k_win, cache_v_win, rel_bias, attn_norm_w, w_in, conv_w,
           q_norm_w, k_norm_w, sinks, w_out, ffn_norm_w, w_router, b_router, w_up, b_up, w_down, b_down):
    batch, seq, _ = x_prompt.shape
    nb = x_sample.shape[0]
    anw = attn_norm_w[0].reshape(1, D_MODEL)
    fnw = ffn_norm_w[0].reshape(1, D_MODEL)
    win_bf = w_in[0].astype(BF16)
    wout_bf = w_out[0].astype(BF16)
    qnw = jnp.tile(q_norm_w[0], 2).reshape(1, LANES)
    knw = jnp.tile(k_norm_w[0], 2).reshape(1, LANES)
    br = b_router[0].reshape(1, N_EXPERTS)
    prompt_tab, sample_tab = _bias_tables(rel_bias)

    n_tiles_p = batch * seq // MIXER_TILE
    lrows_p, lrows_s = _local_rows(MIXER_TILE), _local_rows(nb)
    sample_base = n_tiles_p * lrows_p
    assert sample_base % lrows_s == 0
    xs_rows = sample_base + lrows_s
    ys_rows = xs_rows

    sink_rows = jnp.repeat(sinks[0].reshape(N_KV, GROUP), ATTN_BLOCK, axis=1)
    x1p, xs_big, lpp, statp, kp, vp, cp = _mixer_prompt(
        x_prompt, xs_rows, sink_rows, anw, win_bf, conv_w[0], qnw, knw, prompt_tab, wout_bf, fnw, w_router[0], br)

    k_past = cache_k_win[0].reshape(nb, WINDOW, LANES)
    v_past = cache_v_win[0].reshape(nb, WINDOW, LANES)
    x1s, xs_big, lps, stats, knew, vnew, unew = _mixer_sample(
        xs_big, sample_base // lrows_s, x_sample.reshape(nb, D_MODEL), state_conv[0, :, 0, :], state_conv[0, :, 1, :],
        k_past, v_past, sinks[0].reshape(N_HEADS, 1), anw, win_bf, conv_w[0], qnw, knw, sample_tab, wout_bf, fnw,
        w_router[0], br)

    stat = jnp.concatenate([statp, stats], axis=0)
    seg = stat[:, 0, :N_EXPERTS].astype(I32)
    starts = stat[:, 1, :N_EXPERTS].astype(I32)
    tile_base = jnp.concatenate([jnp.arange(n_tiles_p, dtype=I32) * lrows_p, jnp.array([sample_base], I32)])
    tile_rows = jnp.array([lrows_p] * n_tiles_p + [lrows_s], I32)
    n_blocks = -(-(xs_rows + (N_EXPERTS + 1) * (EXPERT_BLOCK - 1)) // EXPERT_BLOCK)
    block_expert, n_active, chunk_src, chunk_dst = _chunk_tables(seg, starts, tile_base, tile_rows, n_blocks)

    ys_big = _experts(block_expert, n_active, chunk_src, chunk_dst, xs_big, ys_rows, w_up[0], b_up[0], w_down[0],
                      b_down[0])

    y_prompt = _combine(lpp, ys_big, x1p, MIXER_TILE, 0).reshape(batch, seq, D_MODEL)
    y_sample = _combine(lps, ys_big, x1s, nb, sample_base // lrows_s).reshape(nb, 1, D_MODEL)

    new_k_sample = jnp.concatenate([k_past[:, 1:], knew[:, None, :]], axis=1)
    new_v_sample = jnp.concatenate([v_past[:, 1:], vnew[:, None, :]], axis=1)
    new_conv_sample = jnp.stack([state_conv[0, :, 1, :], unew], axis=1)
    return (
        y_prompt,
        y_sample,
        cp[None],
        kp.reshape(1, batch, ATTN_BLOCK, N_KV, HEAD_DIM),
        vp.reshape(1, batch, ATTN_BLOCK, N_KV, HEAD_DIM),
        new_conv_sample[None],
        new_k_sample.reshape(1, nb, WINDOW, N_KV, HEAD_DIM),
        new_v_sample.reshape(1, nb, WINDOW, N_KV, HEAD_DIM),
    )
```

```python
import math

import numpy as np
import jax
import jax.numpy as jnp
from jax import lax
from jax.experimental import pallas as pl
from jax.experimental.pallas import tpu as pltpu

F32 = jnp.float32
BF16 = jnp.bfloat16
I32 = jnp.int32

D_MODEL = 1024
HEAD_DIM = 64
N_HEADS = 16
N_KV = 2
GROUP = N_HEADS // N_KV
WINDOW = 128
ATTN_BLOCK = 128
N_BUCKETS = 32
MAX_DISTANCE = 128
NEG_INF = -1e30
N_EXPERTS = 32
TOP_K = 4
SWIGLU_ALPHA = 1.702
SWIGLU_LIMIT = 7.0
EPS = 1e-5
ATTN_SCALE = HEAD_DIM ** -0.5

OFF_XIN, OFF_BG, OFF_CG, OFF_Q = 0, 1024, 2048, 3072
OFF_K, OFF_V, OFF_GC, OFF_GA = 4096, 4224, 4352, 5376
IN_DIM = 6400

LANES = 128
MIXER_TILE = 512
EXPERT_BLOCK = 256
CHUNK = 16
CHUNKS_PER_BLOCK = EXPERT_BLOCK // CHUNK
XS_WIDTH = D_MODEL + LANES
VMEM_LIMIT = 60 * 1024 * 1024


def _local_rows(tm):
    need = tm * TOP_K + N_EXPERTS * (CHUNK - 1)
    return -(-need // 512) * 512


def _t5_bucket_np(rel):
    n = np.maximum(rel, 0)
    max_exact = N_BUCKETS // 2
    nf = np.maximum(n, 1).astype(np.float64)
    large = max_exact + (np.log(nf / max_exact) / math.log(MAX_DISTANCE / max_exact)
                         * (N_BUCKETS - max_exact)).astype(np.int32)
    large = np.minimum(large, N_BUCKETS - 1)
    return np.where(n < max_exact, n, large).astype(np.int32)


def _rms(x, w):
    return x * lax.rsqrt(jnp.mean(x * x, axis=-1, keepdims=True) + EPS) * w


def _lane_iota(rows=1):
    return lax.broadcasted_iota(I32, (rows, LANES), 1)


def _lo_half():
    return _lane_iota() < HEAD_DIM


def _pair_norm(t, w128):
    lo = _lo_half()
    sq = t * t
    s_lo = jnp.sum(jnp.where(lo, sq, 0.0), axis=-1, keepdims=True)
    s_hi = jnp.sum(jnp.where(lo, 0.0, sq), axis=-1, keepdims=True)
    r = jnp.where(lo, lax.rsqrt(s_lo * (1.0 / HEAD_DIM) + EPS), lax.rsqrt(s_hi * (1.0 / HEAD_DIM) + EPS))
    return t * r * w128


def _top4_gates(logits):
    rows = logits.shape[0]
    lane = lax.broadcasted_iota(I32, (rows, N_EXPERTS), 1).astype(F32)
    vals, idxs = [], []
    l = logits
    for _ in range(TOP_K):
        m = jnp.max(l, axis=-1, keepdims=True)
        idx = jnp.min(jnp.where(l == m, lane, float(N_EXPERTS)), axis=-1, keepdims=True)
        vals.append(m)
        idxs.append(idx)
        l = jnp.where(lane == idx, -jnp.inf, l)
    es = [jnp.exp(v - vals[0]) for v in vals]
    den = es[0] + es[1] + es[2] + es[3]
    return idxs, [e / den for e in es]


def _cols_to_lanes(cols, rows, fill=0.0):
    lane = _lane_iota(rows)
    out = jnp.full((rows, LANES), fill, F32)
    for k, c in enumerate(cols):
        out = jnp.where(lane == k, c, out)
    return out


def _dispatch(h2, idxs, gates, xs_ref, lp_ref, stat_ref):
    tm = h2.shape[0]
    lrows = xs_ref.shape[0]
    lane = _lane_iota(tm)
    lane_f = lane.astype(F32)
    member = jnp.zeros((tm, LANES), F32)
    for k in range(TOP_K):
        member = member + jnp.where(lane_f == idxs[k], 1.0, 0.0)
    cnt = jnp.sum(member, axis=0, keepdims=True)
    seg = jnp.floor((cnt + (CHUNK - 1)) * (1.0 / CHUNK)) * CHUNK
    lane1 = _lane_iota()
    incl = seg
    for s in (1, 2, 4, 8, 16):
        incl = incl + jnp.where(lane1 >= s, pltpu.roll(incl, s, axis=1), 0.0)
    starts = incl - seg
    row8 = lax.broadcasted_iota(I32, (8, LANES), 0)
    stat_ref[0] = jnp.where(row8 == 0, seg, jnp.where(row8 == 1, starts, 0.0))

    tri = (lax.broadcasted_iota(I32, (tm, tm), 1) < lax.broadcasted_iota(I32, (tm, tm), 0)).astype(BF16)
    rank = jnp.dot(tri, member.astype(BF16), preferred_element_type=F32)
    pos = starts + rank
    lps = [jnp.sum(jnp.where(lane_f == idxs[k], pos, 0.0), axis=-1, keepdims=True) for k in range(TOP_K)]
    lane4 = lax.broadcasted_iota(I32, (tm, TOP_K), 1)
    lp_ref[...] = jnp.where(lane4 == 0, lps[0], jnp.where(lane4 == 1, lps[1],
                                                          jnp.where(lane4 == 2, lps[2], lps[3]))).astype(I32)
    lp_t = _cols_to_lanes(lps, tm, fill=-1.0).T

    g_hi = [g.astype(BF16).astype(F32) for g in gates]
    g_lo = [g - h for g, h in zip(gates, g_hi)]
    meta_in = _cols_to_lanes(list(idxs) + g_hi + g_lo, tm)
    rhs = jnp.concatenate([h2.astype(BF16), meta_in.astype(BF16)], axis=1)

    sub = 512
    starts32 = jnp.where(lane1 < N_EXPERTS, starts, 1e9)

    for c in range(lrows // sub):
        r0 = c * sub
        rid = (lax.broadcasted_iota(I32, (sub, 1), 0) + r0).astype(F32)
        rid_t = (lax.broadcasted_iota(I32, (sub, tm), 0) + r0).astype(F32)
        perm = rid_t == lp_t[0:1, :]
        for k in range(1, TOP_K):
            perm = jnp.logical_or(perm, rid_t == lp_t[k:k + 1, :])
        full = jnp.dot(perm.astype(BF16), rhs, preferred_element_type=F32)
        got = full[:, D_MODEL:]
        e_row = jnp.sum(jnp.where(rid >= starts32, 1.0, 0.0), axis=-1, keepdims=True) - 1.0
        g_sum = pltpu.roll(got, LANES - TOP_K, axis=1) + pltpu.roll(got, LANES - 2 * TOP_K, axis=1)
        lane_s = _lane_iota(sub)
        pick = jnp.logical_and(lane_s < TOP_K, got == e_row)
        gate_row = jnp.sum(jnp.where(pick, g_sum, 0.0), axis=-1, keepdims=True)
        gr_hi = gate_row.astype(BF16).astype(F32)
        meta = jnp.where(lane_s == 0, gr_hi, jnp.where(lane_s == 1, gate_row - gr_hi, 0.0))
        xs_ref[r0:r0 + sub, 0:D_MODEL] = full[:, :D_MODEL].astype(BF16)
        xs_ref[r0:r0 + sub, D_MODEL:XS_WIDTH] = meta.astype(BF16)


def _epilogue(x, merged, wout_ref, fnw_ref, wr_ref, br_ref, x1_ref, xs_ref, lp_ref, stat_ref):
    x1 = x + jnp.dot(merged.astype(BF16), wout_ref[...], preferred_element_type=F32)
    x1_ref[...] = x1
    h2 = _rms(x1, fnw_ref[...])
    logits = jnp.dot(h2.astype(BF16), wr_ref[...].astype(BF16), preferred_element_type=F32) + br_ref[...]
    idxs, gates = _top4_gates(logits)
    _dispatch(h2, idxs, gates, xs_ref, lp_ref, stat_ref)


def _mixer_prompt_kernel(sink_ref, x_ref, anw_ref, win_ref, convw_ref, qnw_ref, knw_ref, bias_ref, wout_ref,
                         fnw_ref, wr_ref, br_ref,
                         x1_ref, xs_ref, lp_ref, stat_ref, kout_ref, vout_ref, cout_ref,
                         ubuf, q_s, kd_s, vt_s, ya_s, st_s, pt_s):
    tm = x_ref.shape[0]
    nblk = tm // ATTN_BLOCK
    j = pl.program_id(1)
    first_tile = j == 0
    lo = _lo_half()

    x = x_ref[...]
    h = _rms(x, anw_ref[...]).astype(BF16)

    def proj(off, n):
        return jnp.dot(h, win_ref[:, off:off + n], preferred_element_type=F32)

    @pl.when(first_tile)
    def _():
        ubuf[0:8, :] = jnp.zeros((8, D_MODEL), F32)
        kd_s[0:ATTN_BLOCK, :] = jnp.zeros((ATTN_BLOCK, 2 * LANES), BF16)
        vt_s[0] = jnp.zeros((N_KV, LANES, ATTN_BLOCK), BF16)

    u = proj(OFF_CG, D_MODEL) * proj(OFF_XIN, D_MODEL)
    ubuf[8:tm + 8, :] = u
    cw = convw_ref[...]
    conv = ubuf[6:tm + 6, :] * cw[0:1, :] + ubuf[7:tm + 7, :] * cw[1:2, :] + u * cw[2:3, :]
    merged = jax.nn.sigmoid(proj(OFF_GC, D_MODEL)) * (proj(OFF_BG, D_MODEL) * conv)
    tail = ubuf[tm + 6:tm + 8, :]
    cout_ref[0] = tail
    ubuf[6:8, :] = tail

    q = proj(OFF_Q, D_MODEL)
    kv = proj(OFF_K, 2 * LANES)
    k = _pair_norm(kv[:, :LANES], knw_ref[...])
    v = kv[:, LANES:]
    qnw = qnw_ref[...]
    for p in range(N_HEADS // 2):
        sl = slice(p * LANES, (p + 1) * LANES)
        q_s[:, sl] = (_pair_norm(q[:, sl], qnw) * ATTN_SCALE).astype(BF16)
    k_sw = pltpu.roll(k, HEAD_DIM, axis=1)
    v_sw = pltpu.roll(v, HEAD_DIM, axis=1)
    kd_s[ATTN_BLOCK:tm + ATTN_BLOCK, 0:LANES] = jnp.where(lo, k, k_sw).astype(BF16)
    kd_s[ATTN_BLOCK:tm + ATTN_BLOCK, LANES:2 * LANES] = jnp.where(lo, k_sw, k).astype(BF16)
    v_dup = (jnp.where(lo, v, v_sw), jnp.where(lo, v_sw, v))
    for b in range(nblk):
        for g in range(N_KV):
            vt_s[b + 1, g] = v_dup[g][b * ATTN_BLOCK:(b + 1) * ATTN_BLOCK, :].T.astype(BF16)

    @pl.when(j == pl.num_programs(1) - 1)
    def _():
        kout_ref[0] = k[tm - ATTN_BLOCK:, :]
        vout_ref[0] = v[tm - ATTN_BLOCK:, :]

    prev_rows = lax.broadcasted_iota(I32, (2 * ATTN_BLOCK, 1), 0) < ATTN_BLOCK
    feat_lo = lax.broadcasted_iota(I32, (LANES, 1), 0) < HEAD_DIM

    def attn_block(blk):
        r0 = blk * ATTN_BLOCK
        qb = q_s[r0:r0 + ATTN_BLOCK, :]
        for g in range(N_KV):
            parts = []
            for t in range(GROUP):
                hd = g * GROUP + t
                slab = qb[:, (hd // 2) * LANES:(hd // 2 + 1) * LANES]
                keep = lo if hd % 2 == 0 else jnp.logical_not(lo)
                parts.append(jnp.where(keep, slab, jnp.zeros_like(slab)))
            lhs = jnp.concatenate(parts, axis=0)
            st = lax.dot_general(kd_s[r0:r0 + 2 * ATTN_BLOCK, g * LANES:(g + 1) * LANES], lhs,
                                 (((1,), (1,)), ((), ())), preferred_element_type=F32)
            st = st + bias_ref[g]
            if blk == 0:
                st = jnp.where(jnp.logical_and(prev_rows, first_tile), NEG_INF, st)
            st_s[...] = st
            sink = sink_ref[g:g + 1, :]
            rdens = []
            for t in range(GROUP):
                cols = slice(t * LANES, (t + 1) * LANES)
                s = st_s[:, cols]
                m = jnp.maximum(jnp.max(s, axis=0, keepdims=True), sink[:, cols])
                pr = jnp.exp(s - m)
                den = jnp.sum(pr, axis=0, keepdims=True) + jnp.exp(sink[:, cols] - m)
                pt_s[:, cols] = pr.astype(BF16)
                rdens.append(1.0 / den)
            vt = jnp.concatenate([vt_s[blk, g], vt_s[blk + 1, g]], axis=1)
            ot = jnp.dot(vt, pt_s[...], preferred_element_type=F32)
            for i in range(GROUP // 2):
                pair = g * (GROUP // 2) + i
                even = ot[:, (2 * i) * LANES:(2 * i + 1) * LANES] * rdens[2 * i]
                odd = ot[:, (2 * i + 1) * LANES:(2 * i + 2) * LANES] * rdens[2 * i + 1]
                ya_s[r0:r0 + ATTN_BLOCK, pair * LANES:(pair + 1) * LANES] = jnp.where(feat_lo, even, odd).T

    for blk in range(nblk):
        attn_block(blk)

    kd_s[0:ATTN_BLOCK, :] = kd_s[tm:tm + ATTN_BLOCK, :]
    vt_s[0] = vt_s[nblk]

    merged = merged + jax.nn.sigmoid(proj(OFF_GA, D_MODEL)) * ya_s[...]
    _epilogue(x, merged, wout_ref, fnw_ref, wr_ref, br_ref, x1_ref, xs_ref, lp_ref, stat_ref)


def _const_spec(shape):
    nd = len(shape)
    return pl.BlockSpec(shape, lambda *_: (0,) * nd, pipeline_mode=pl.Buffered(1))


def _mixer_prompt(x, xs_rows, sinks, anw, win_bf, convw, qnw, knw, bias_tab, wout_bf, fnw, wr, br):
    batch, seq, _ = x.shape
    tm = MIXER_TILE
    lrows = _local_rows(tm)
    nj = seq // tm
    tokens = batch * seq
    n_tiles = batch * nj
    x2 = x.reshape(tokens, D_MODEL)
    tok_spec = lambda width: pl.BlockSpec((tm, width), lambda b, j: (b * nj + j, 0))
    per_batch = lambda rows, width: pl.BlockSpec((1, rows, width), lambda b, j: (b, 0, 0))
    in_specs = [
        _const_spec((N_KV, GROUP * ATTN_BLOCK)),
        tok_spec(D_MODEL),
        _const_spec((1, D_MODEL)),
        _const_spec((D_MODEL, IN_DIM)),
        _const_spec((3, D_MODEL)),
        _const_spec((1, LANES)),
        _const_spec((1, LANES)),
        _const_spec((N_KV, 2 * ATTN_BLOCK, GROUP * ATTN_BLOCK)),
        _const_spec((D_MODEL, D_MODEL)),
        _const_spec((1, D_MODEL)),
        _const_spec((D_MODEL, N_EXPERTS)),
        _const_spec((1, N_EXPERTS)),
    ]
    out_shape = (
        jax.ShapeDtypeStruct((tokens, D_MODEL), F32),
        jax.ShapeDtypeStruct((xs_rows, XS_WIDTH), BF16),
        jax.ShapeDtypeStruct((tokens, TOP_K), I32),
        jax.ShapeDtypeStruct((n_tiles, 8, LANES), F32),
        jax.ShapeDtypeStruct((batch, ATTN_BLOCK, LANES), F32),
        jax.ShapeDtypeStruct((batch, ATTN_BLOCK, LANES), F32),
        jax.ShapeDtypeStruct((batch, 2, D_MODEL), F32),
    )
    out_specs = (
        tok_spec(D_MODEL),
        pl.BlockSpec((lrows, XS_WIDTH), lambda b, j: (b * nj + j, 0)),
        tok_spec(TOP_K),
        pl.BlockSpec((1, 8, LANES), lambda b, j: (b * nj + j, 0, 0)),
        per_batch(ATTN_BLOCK, LANES), per_batch(ATTN_BLOCK, LANES), per_batch(2, D_MODEL),
    )
    scratch = [
        pltpu.VMEM((tm + 8, D_MODEL), F32),
        pltpu.VMEM((tm, D_MODEL), BF16),
        pltpu.VMEM((tm + ATTN_BLOCK, 2 * LANES), BF16),
        pltpu.VMEM((tm // ATTN_BLOCK + 1, N_KV, LANES, ATTN_BLOCK), BF16),
        pltpu.VMEM((tm, D_MODEL), F32),
        pltpu.VMEM((2 * ATTN_BLOCK, GROUP * ATTN_BLOCK), F32),
        pltpu.VMEM((2 * ATTN_BLOCK, GROUP * ATTN_BLOCK), BF16),
    ]
    return pl.pallas_call(
        _mixer_prompt_kernel,
        grid=(batch, nj),
        in_specs=in_specs,
        out_specs=out_specs,
        out_shape=out_shape,
        scratch_shapes=scratch,
        compiler_params=pltpu.CompilerParams(
            dimension_semantics=("arbitrary", "arbitrary"), vmem_limit_bytes=VMEM_LIMIT),
        name="mixer_prompt",
    )(sinks, x2, anw, win_bf, convw, qnw, knw, bias_tab, wout_bf, fnw, wr, br)


def _mixer_sample_kernel(xs_in_ref, x_ref, p0_ref, p1_ref, kp_ref, vp_ref, sink_ref, anw_ref, win_ref, convw_ref,
                         qnw_ref, knw_ref, bias_ref, wout_ref, fnw_ref, wr_ref, br_ref,
                         x1_ref, xs_ref, lp_ref, stat_ref, knew_ref, vnew_ref, unew_ref,
                         qh_s, o_s, kn_s, vn_s):
    del xs_in_ref
    nb = x_ref.shape[0]
    lo = _lo_half()
    x = x_ref[...]
    h = _rms(x, anw_ref[...]).astype(BF16)

    def proj(off, n):
        return jnp.dot(h, win_ref[:, off:off + n], preferred_element_type=F32)

    u = proj(OFF_CG, D_MODEL) * proj(OFF_XIN, D_MODEL)
    unew_ref[...] = u
    cw = convw_ref[...]
    conv = p0_ref[...] * cw[0:1, :] + p1_ref[...] * cw[1:2, :] + u * cw[2:3, :]
    merged = jax.nn.sigmoid(proj(OFF_GC, D_MODEL)) * (proj(OFF_BG, D_MODEL) * conv)

    q = proj(OFF_Q, D_MODEL)
    kv = proj(OFF_K, 2 * LANES)
    k = _pair_norm(kv[:, :LANES], knw_ref[...])
    v = kv[:, LANES:]
    knew_ref[...] = k
    vnew_ref[...] = v
    kn_s[...] = k
    vn_s[...] = v

    qnw = qnw_ref[...]
    for hd in range(N_HEADS):
        pair, half, grp = hd // 2, hd % 2, hd // GROUP
        slab = _pair_norm(q[:, pair * LANES:(pair + 1) * LANES], qnw) * ATTN_SCALE
        slab = jnp.where(lo if half == 0 else jnp.logical_not(lo), slab, 0.0)
        if half != grp:
            slab = pltpu.roll(slab, HEAD_DIM, axis=1)
        qh_s[hd * nb:(hd + 1) * nb, :] = slab

    sink = sink_ref[...]
    bias = bias_ref[...]
    row0 = lax.broadcasted_iota(I32, (WINDOW, 1), 0) == 0

    def token(b, carry):
        qb = qh_s[pl.ds(b, N_HEADS, stride=nb), :]
        kb = jnp.where(row0, kn_s[pl.ds(b, 1), :], kp_ref[b])
        vb = jnp.where(row0, vn_s[pl.ds(b, 1), :], vp_ref[b])
        s = lax.dot_general(qb.astype(BF16), kb.astype(BF16), (((1,), (1,)), ((), ())),
                            preferred_element_type=F32) + bias
        m = jnp.maximum(jnp.max(s, axis=-1, keepdims=True), sink)
        pr = jnp.exp(s - m)
        den = jnp.sum(pr, axis=-1, keepdims=True) + jnp.exp(sink - m)
        o = jnp.dot(pr.astype(BF16), vb.astype(BF16), preferred_element_type=F32) / den
        o_s[pl.ds(b, N_HEADS, stride=nb), :] = o
        return carry

    lax.fori_loop(0, nb, token, 0)

    cols = []
    for pair in range(N_HEADS // 2):
        halves = []
        for half in range(2):
            hd = 2 * pair + half
            slab = o_s[hd * nb:(hd + 1) * nb, :]
            if half != hd // GROUP:
                slab = pltpu.roll(slab, HEAD_DIM, axis=1)
            halves.append(slab)
        cols.append(jnp.where(lo, halves[0], halves[1]))
    y_attn = jnp.concatenate(cols, axis=1)

    merged = merged + jax.nn.sigmoid(proj(OFF_GA, D_MODEL)) * y_attn
    _epilogue(x, merged, wout_ref, fnw_ref, wr_ref, br_ref, x1_ref, xs_ref, lp_ref, stat_ref)


def _mixer_sample(xs_big, xs_block, x, p0, p1, k_past, v_past, sink_col, anw, win_bf, convw, qnw, knw, bias_s,
                  wout_bf, fnw, wr, br):
    nb = x.shape[0]
    lrows = _local_rows(nb)
    args = (x, p0, p1, k_past, v_past, sink_col, anw, win_bf, convw, qnw, knw, bias_s, wout_bf, fnw, wr, br)
    out_shape = (
        jax.ShapeDtypeStruct((nb, D_MODEL), F32),
        jax.ShapeDtypeStruct(xs_big.shape, BF16),
        jax.ShapeDtypeStruct((nb, TOP_K), I32),
        jax.ShapeDtypeStruct((1, 8, LANES), F32),
        jax.ShapeDtypeStruct((nb, LANES), F32),
        jax.ShapeDtypeStruct((nb, LANES), F32),
        jax.ShapeDtypeStruct((nb, D_MODEL), F32),
    )
    full = lambda s: pl.BlockSpec(s.shape, lambda i: (0,) * len(s.shape))
    out_specs = (
        full(out_shape[0]),
        pl.BlockSpec((lrows, XS_WIDTH), lambda i: (xs_block, 0)),
        full(out_shape[2]), full(out_shape[3]), full(out_shape[4]), full(out_shape[5]), full(out_shape[6]),
    )
    scratch = [
        pltpu.VMEM((N_HEADS * nb, LANES), F32),
        pltpu.VMEM((N_HEADS * nb, LANES), F32),
        pltpu.VMEM((nb, LANES), F32),
        pltpu.VMEM((nb, LANES), F32),
    ]
    return pl.pallas_call(
        _mixer_sample_kernel,
        grid=(1,),
        in_specs=[pl.BlockSpec(memory_space=pl.ANY)] + [_const_spec(a.shape) for a in args],
        out_specs=out_specs,
        out_shape=out_shape,
        scratch_shapes=scratch,
        input_output_aliases={0: 1},
        compiler_params=pltpu.CompilerParams(dimension_semantics=("arbitrary",), vmem_limit_bytes=VMEM_LIMIT),
        name="mixer_sample",
    )(xs_big, *args)


def _expert_kernel(be_ref, nact_ref, src_ref, dst_ref, xs_hbm, wup_ref, bup_ref, wdn_ref, bdn_ref, ys_hbm,
                   wup_s, wdn_s, perm_s, xbuf, ybuf, in_sem, out_sem):
    i = pl.program_id(0)
    n_real = nact_ref[0]
    n_active = nact_ref[1]
    e = be_ref[i]
    e_prev = be_ref[jnp.maximum(i - 1, 0)]
    slot = lax.rem(i, 2)

    def gather_copy(blk, slt, c):
        row = pl.multiple_of(src_ref[blk * CHUNKS_PER_BLOCK + c] * CHUNK, CHUNK)
        return pltpu.make_async_copy(xs_hbm.at[pl.ds(row, CHUNK), :],
                                     xbuf.at[slt, pl.ds(c * CHUNK, CHUNK), :], in_sem.at[slt])

    def scatter_chunks(blk, slt, wait):
        base = blk * CHUNKS_PER_BLOCK

        def one(c, dst):
            row = pl.multiple_of(dst * CHUNK, CHUNK)
            cp = pltpu.make_async_copy(ybuf.at[slt, pl.ds(c * CHUNK, CHUNK), :],
                                       ys_hbm.at[pl.ds(row, CHUNK), :], out_sem.at[slt])
            if wait:
                cp.wait()
            else:
                cp.start()

        full = dst_ref[base + CHUNKS_PER_BLOCK - 1] >= 0

        @pl.when(full)
        def _():
            for c in range(CHUNKS_PER_BLOCK):
                one(c, dst_ref[base + c])

        @pl.when(jnp.logical_not(full))
        def _():
            for c in range(CHUNKS_PER_BLOCK - 1):
                dst = dst_ref[base + c]

                @pl.when(dst >= 0)
                def _():
                    one(c, dst)

    @pl.when(jnp.logical_and(i == 0, n_real > 0))
    def _():
        for c in range(CHUNKS_PER_BLOCK):
            gather_copy(0, 0, c).start()

    @pl.when(jnp.logical_and(i < n_real, jnp.logical_or(i == 0, e != e_prev)))
    def _():
        wup_s[...] = wup_ref[0].astype(BF16)
        half = LANES // 2
        for cs in range(D_MODEL // LANES):
            cols = slice(cs * LANES, (cs + 1) * LANES)
            for c in range(D_MODEL // LANES):
                for par in range(2):
                    s0 = c * LANES + par * half
                    perm_s[cs, pl.ds(c * LANES + par, half, stride=2), :] = wdn_ref[0, s0:s0 + half, cols]
            wdn_s[:, cols] = perm_s[cs].astype(BF16)

    @pl.when(i + 1 < n_real)
    def _():
        for c in range(CHUNKS_PER_BLOCK):
            gather_copy(i + 1, 1 - slot, c).start()

    @pl.when(jnp.logical_and(i >= 2, i < n_active))
    def _():
        scatter_chunks(i - 2, slot, wait=True)

    @pl.when(jnp.logical_and(i >= n_real, i < n_active))
    def _():
        ybuf[slot] = jnp.zeros((EXPERT_BLOCK, D_MODEL), BF16)
        scatter_chunks(i, slot, wait=False)

    @pl.when(i < n_real)
    def _():
        for c in range(CHUNKS_PER_BLOCK):
            gather_copy(i, slot, c).wait()

        xb = xbuf[slot]
        meta = xb[:, D_MODEL:].astype(F32)
        gate = meta[:, 0:1] + meta[:, 1:2]
        u = jnp.dot(xb[:, :D_MODEL], wup_s[...], preferred_element_type=F32) + bup_ref[0]
        even = (_lane_iota() & 1) == 0
        cols = []
        for c in range(D_MODEL // LANES):
            c0 = u[:, (2 * c) * LANES:(2 * c + 1) * LANES]
            c1 = u[:, (2 * c + 1) * LANES:(2 * c + 2) * LANES]
            glu = jnp.where(even, c0, pltpu.roll(c1, 1, axis=1))
            lin = jnp.where(even, pltpu.roll(c0, LANES - 1, axis=1), c1)
            glu = jnp.minimum(glu, SWIGLU_LIMIT)
            lin = jnp.clip(lin, -SWIGLU_LIMIT, SWIGLU_LIMIT)
            cols.append(glu * jax.nn.sigmoid(SWIGLU_ALPHA * glu) * (lin + 1.0))
        a = jnp.concatenate(cols, axis=1).astype(BF16)
        y = jnp.dot(a, wdn_s[...], preferred_element_type=F32) + bdn_ref[0]
        ybuf[slot] = (y * gate).astype(BF16)
        scatter_chunks(i, slot, wait=False)

    @pl.when(i == n_active - 1)
    def _():
        @pl.when(i >= 1)
        def _():
            scatter_chunks(i - 1, 1 - slot, wait=True)

        scatter_chunks(i, slot, wait=True)


def _experts(block_expert, n_active, chunk_src, chunk_dst, xs_big, ys_rows, w_up, b_up, w_down, b_down):
    n_blocks = block_expert.shape[0]
    wspec = lambda shape: pl.BlockSpec(shape, lambda i, be, na, cs, cd: (jnp.minimum(be[i], N_EXPERTS - 1), 0, 0))
    grid_spec = pltpu.PrefetchScalarGridSpec(
        num_scalar_prefetch=4,
        grid=(n_blocks,),
        in_specs=[
            pl.BlockSpec(memory_space=pl.ANY),
            wspec((1, D_MODEL, 2 * D_MODEL)),
            wspec((1, 1, 2 * D_MODEL)),
            wspec((1, D_MODEL, D_MODEL)),
            wspec((1, 1, D_MODEL)),
        ],
        out_specs=pl.BlockSpec(memory_space=pl.ANY),
        scratch_shapes=[
            pltpu.VMEM((D_MODEL, 2 * D_MODEL), BF16),
            pltpu.VMEM((D_MODEL, D_MODEL), BF16),
            pltpu.VMEM((D_MODEL // LANES, D_MODEL, LANES), F32),
            pltpu.VMEM((2, EXPERT_BLOCK, XS_WIDTH), BF16),
            pltpu.VMEM((2, EXPERT_BLOCK, D_MODEL), BF16),
            pltpu.SemaphoreType.DMA((2,)),
            pltpu.SemaphoreType.DMA((2,)),
        ],
    )
    return pl.pallas_call(
        _expert_kernel,
        grid_spec=grid_spec,
        out_shape=jax.ShapeDtypeStruct((ys_rows, D_MODEL), BF16),
        compiler_params=pltpu.CompilerParams(dimension_semantics=("arbitrary",), vmem_limit_bytes=VMEM_LIMIT),
        name="experts",
    )(block_expert, n_active, chunk_src, chunk_dst, xs_big, w_up, b_up.reshape(N_EXPERTS, 1, 2 * D_MODEL),
      w_down, b_down.reshape(N_EXPERTS, 1, D_MODEL))


def _combine_kernel(lp_ref, ys_ref, x1_ref, out_ref):
    tm = x1_ref.shape[0]
    lrows = ys_ref.shape[0]
    lp = lp_ref[...].astype(F32)
    rid = lax.broadcasted_iota(I32, (tm, lrows), 1).astype(F32)
    sel = rid == lp[:, 0:1]
    for k in range(1, TOP_K):
        sel = jnp.logical_or(sel, rid == lp[:, k:k + 1])
    out_ref[...] = x1_ref[...] + jnp.dot(sel.astype(BF16), ys_ref[...], preferred_element_type=F32)


def _combine(lp, ys_big, x1, tm, first_block):
    tokens = x1.shape[0]
    lrows = _local_rows(tm)
    return pl.pallas_call(
        _combine_kernel,
        grid=(tokens // tm,),
        in_specs=[
            pl.BlockSpec((tm, TOP_K), lambda j: (j, 0)),
            pl.BlockSpec((lrows, D_MODEL), lambda j: (first_block + j, 0)),
            pl.BlockSpec((tm, D_MODEL), lambda j: (j, 0)),
        ],
        out_specs=pl.BlockSpec((tm, D_MODEL), lambda j: (j, 0)),
        out_shape=jax.ShapeDtypeStruct((tokens, D_MODEL), F32),
        compiler_params=pltpu.CompilerParams(dimension_semantics=("arbitrary",), vmem_limit_bytes=VMEM_LIMIT),
        name="combine",
    )(lp, ys_big, x1)


def _bias_tables(rel_bias):
    qi = np.arange(ATTN_BLOCK)[:, None]
    kj = np.arange(2 * ATTN_BLOCK)[None, :]
    rel = qi + ATTN_BLOCK - kj
    valid = (rel >= 0) & (rel < WINDOW)
    bucket = np.where(valid, _t5_bucket_np(rel), -1)
    slot_rel = np.where(np.arange(WINDOW) == 0, 0, WINDOW - np.arange(WINDOW))
    slot_bucket = _t5_bucket_np(slot_rel)
    tab = jnp.full((N_HEADS, ATTN_BLOCK, 2 * ATTN_BLOCK), NEG_INF, F32)
    sample_tab = jnp.zeros((N_HEADS, WINDOW), F32)
    for b in range(N_BUCKETS):
        tab = jnp.where((bucket == b)[None], rel_bias[b][:, None, None], tab)
        sample_tab = jnp.where((slot_bucket == b)[None], rel_bias[b][:, None], sample_tab)
    prompt_tab = tab.reshape(N_KV, GROUP, ATTN_BLOCK, 2 * ATTN_BLOCK).transpose(0, 3, 1, 2)
    return prompt_tab.reshape(N_KV, 2 * ATTN_BLOCK, GROUP * ATTN_BLOCK), sample_tab


def _chunk_tables(seg, starts, tile_base, tile_rows, n_blocks):
    n_tiles = seg.shape[0]
    n_seg = (N_EXPERTS + 1) * n_tiles
    used = jnp.sum(seg, axis=1)
    seg_e = jnp.concatenate([seg.T, (tile_rows - used)[None, :]], axis=0)
    src0 = jnp.concatenate([tile_base[None, :] + starts.T, (tile_base + used)[None, :]], axis=0).reshape(-1)
    total = jnp.sum(seg_e, axis=1)
    region = (total + EXPERT_BLOCK - 1) // EXPERT_BLOCK * EXPERT_BLOCK
    pad_end = jnp.cumsum(region)
    pad_start = pad_end - region
    g_start = (pad_start[:, None] + jnp.cumsum(seg_e, axis=1) - seg_e).reshape(-1)
    g_end = g_start + seg_e.reshape(-1)
    rows = jnp.arange(n_blocks * CHUNKS_PER_BLOCK, dtype=I32) * CHUNK
    passed = g_end[None, :] <= rows[:, None]

    def at_segment(table, sentinel):
        ext = jnp.concatenate([table, jnp.array([sentinel], I32)])
        return ext[0] + jnp.sum(jnp.where(passed, (ext[1:] - ext[:-1])[None, :], 0), axis=1)

    seg_start = at_segment(g_start, 1 << 30)
    valid = rows >= seg_start
    src_row = at_segment(src0, 0) + rows - seg_start
    chunk_src = jnp.where(valid, src_row // CHUNK, 0).astype(I32)
    chunk_dst = jnp.where(valid, src_row // CHUNK, -1).astype(I32)
    blk_rows = jnp.arange(n_blocks, dtype=I32) * EXPERT_BLOCK
    block_region = jnp.minimum(jnp.sum((pad_end[None, :] <= blk_rows[:, None]).astype(I32), axis=1),
                               N_EXPERTS).astype(I32)
    counts = jnp.stack([pad_end[N_EXPERTS - 1], pad_end[N_EXPERTS]]).astype(I32) // EXPERT_BLOCK
    return block_region, counts, chunk_src, chunk_dst


def kernel(x_prompt, x_sample, state_conv, cache_k_win, cache_v_win, rel_bias, attn_norm_w, w_in, conv_w,
           q_norm_w, k_norm_w, sinks, w_out, ffn_norm_w, w_router, b_router, w_up, b_up, w_down, b_down):
    batch, seq, _ = x_prompt.shape
    nb = x_sample.shape[0]
    anw = attn_norm_w[0].reshape(1, D_MODEL)
    fnw = ffn_norm_w[0].reshape(1, D_MODEL)
    win_bf = w_in[0].astype(BF16)
    wout_bf = w_out[0].astype(BF16)
    qnw = jnp.tile(q_norm_w[0], 2).reshape(1, LANES)
    knw = jnp.tile(k_norm_w[0], 2).reshape(1, LANES)
    br = b_router[0].reshape(1, N_EXPERTS)
    prompt_tab, sample_tab = _bias_tables(rel_bias)

    n_tiles_p = batch * seq // MIXER_TILE
    lrows_p, lrows_s = _local_rows(MIXER_TILE), _local_rows(nb)
    sample_base = n_tiles_p * lrows_p
    assert sample_base % lrows_s == 0
    xs_rows = sample_base + lrows_s
    ys_rows = xs_rows

    sink_rows = jnp.repeat(sinks[0].reshape(N_KV, GROUP), ATTN_BLOCK, axis=1)
    x1p, xs_big, lpp, statp, kp, vp, cp = _mixer_prompt(
        x_prompt, xs_rows, sink_rows, anw, win_bf, conv_w[0], qnw, knw, prompt_tab, wout_bf, fnw, w_router[0], br)

    k_past = cache_k_win[0].reshape(nb, WINDOW, LANES)
    v_past = cache_v_win[0].reshape(nb, WINDOW, LANES)
    x1s, xs_big, lps, stats, knew, vnew, unew = _mixer_sample(
        xs_big, sample_base // lrows_s, x_sample.reshape(nb, D_MODEL), state_conv[0, :, 0, :], state_conv[0, :, 1, :],
        k_past, v_past, sinks[0].reshape(N_HEADS, 1), anw, win_bf, conv_w[0], qnw, knw, sample_tab, wout_bf, fnw,
        w_router[0], br)

    stat = jnp.concatenate([statp, stats], axis=0)
    seg = stat[:, 0, :N_EXPERTS].astype(I32)
    starts = stat[:, 1, :N_EXPERTS].astype(I32)
    tile_base = jnp.concatenate([jnp.arange(n_tiles_p, dtype=I32) * lrows_p, jnp.array([sample_base], I32)])
    tile_rows = jnp.array([lrows_p] * n_tiles_p + [lrows_s], I32)
    n_blocks = -(-(xs_rows + (N_EXPERTS + 1) * (EXPERT_BLOCK - 1)) // EXPERT_BLOCK)
    block_expert, n_active, chunk_src, chunk_dst = _chunk_tables(seg, starts, tile_base, tile_rows, n_blocks)

    ys_big = _experts(block_expert, n_active, chunk_src, chunk_dst, xs_big, ys_rows, w_up[0], b_up[0], w_down[0],
                      b_down[0])

    y_prompt = _combine(lpp, ys_big, x1p, MIXER_TILE, 0).reshape(batch, seq, D_MODEL)
    y_sample = _combine(lps, ys_big, x1s, nb, sample_base // lrows_s).reshape(nb, 1, D_MODEL)

    new_k_sample = jnp.concatenate([k_past[:, 1:], knew[:, None, :]], axis=1)
    new_v_sample = jnp.concatenate([v_past[:, 1:], vnew[:, None, :]], axis=1)
    new_conv_sample = jnp.stack([state_conv[0, :, 1, :], unew], axis=1)
    return (
        y_prompt,
        y_sample,
        cp[None],
        kp.reshape(1, batch, ATTN_BLOCK, N_KV, HEAD_DIM),
        vp.reshape(1, batch, ATTN_BLOCK, N_KV, HEAD_DIM),
        new_conv_sample[None],
        new_k_sample.reshape(1, nb, WINDOW, N_KV, HEAD_DIM),
        new_v_sample.reshape(1, nb, WINDOW, N_KV, HEAD_DIM),
    )
```

```python
import math

import numpy as np
import jax
import jax.numpy as jnp
from jax import lax
from jax.experimental import pallas as pl
from jax.experimental.pallas import tpu as pltpu

F32 = jnp.float32
BF16 = jnp.bfloat16
I32 = jnp.int32

D_MODEL = 1024
HEAD_DIM = 64
N_HEADS = 16
N_KV = 2
GROUP = N_HEADS // N_KV
WINDOW = 128
ATTN_BLOCK = 128
N_BUCKETS = 32
MAX_DISTANCE = 128
NEG_INF = -1e30
N_EXPERTS = 32
TOP_K = 4
SWIGLU_ALPHA = 1.702
SWIGLU_LIMIT = 7.0
EPS = 1e-5
ATTN_SCALE = HEAD_DIM ** -0.5

OFF_XIN, OFF_BG, OFF_CG, OFF_Q = 0, 1024, 2048, 3072
OFF_K, OFF_V, OFF_GC, OFF_GA = 4096, 4224, 4352, 5376
IN_DIM = 6400

LANES = 128
MIXER_TILE = 512
EXPERT_BLOCK = 512
CHUNK = 16
CHUNKS_PER_BLOCK = EXPERT_BLOCK // CHUNK
XS_WIDTH = D_MODEL + LANES
VMEM_LIMIT = 60 * 1024 * 1024


def _local_rows(tm):
    need = tm * TOP_K + N_EXPERTS * (CHUNK - 1)
    return -(-need // 512) * 512


def _t5_bucket_np(rel):
    n = np.maximum(rel, 0)
    max_exact = N_BUCKETS // 2
    nf = np.maximum(n, 1).astype(np.float64)
    large = max_exact + (np.log(nf / max_exact) / math.log(MAX_DISTANCE / max_exact)
                         * (N_BUCKETS - max_exact)).astype(np.int32)
    large = np.minimum(large, N_BUCKETS - 1)
    return np.where(n < max_exact, n, large).astype(np.int32)


def _rms(x, w):
    return x * lax.rsqrt(jnp.mean(x * x, axis=-1, keepdims=True) + EPS) * w


def _lane_iota(rows=1):
    return lax.broadcasted_iota(I32, (rows, LANES), 1)


def _lo_half():
    return _lane_iota() < HEAD_DIM


def _pair_norm(t, w128):
    lo = _lo_half()
    sq = t * t
    s_lo = jnp.sum(jnp.where(lo, sq, 0.0), axis=-1, keepdims=True)
    s_hi = jnp.sum(jnp.where(lo, 0.0, sq), axis=-1, keepdims=True)
    r = jnp.where(lo, lax.rsqrt(s_lo * (1.0 / HEAD_DIM) + EPS), lax.rsqrt(s_hi * (1.0 / HEAD_DIM) + EPS))
    return t * r * w128


def _top4_gates(logits):
    rows = logits.shape[0]
    lane = lax.broadcasted_iota(I32, (rows, N_EXPERTS), 1).astype(F32)
    vals, idxs = [], []
    l = logits
    for _ in range(TOP_K):
        m = jnp.max(l, axis=-1, keepdims=True)
        idx = jnp.min(jnp.where(l == m, lane, float(N_EXPERTS)), axis=-1, keepdims=True)
        vals.append(m)
        idxs.append(idx)
        l = jnp.where(lane == idx, -jnp.inf, l)
    es = [jnp.exp(v - vals[0]) for v in vals]
    den = es[0] + es[1] + es[2] + es[3]
    return idxs, [e / den for e in es]


def _cols_to_lanes(cols, rows, fill=0.0):
    lane = _lane_iota(rows)
    out = jnp.full((rows, LANES), fill, F32)
    for k, c in enumerate(cols):
        out = jnp.where(lane == k, c, out)
    return out


def _dispatch(h2, idxs, gates, xs_ref, lp_ref, stat_ref):
    tm = h2.shape[0]
    lrows = xs_ref.shape[0]
    lane = _lane_iota(tm)
    lane_f = lane.astype(F32)
    member = jnp.zeros((tm, LANES), F32)
    for k in range(TOP_K):
        member = member + jnp.where(lane_f == idxs[k], 1.0, 0.0)
    cnt = jnp.sum(member, axis=0, keepdims=True)
    seg = jnp.floor((cnt + (CHUNK - 1)) * (1.0 / CHUNK)) * CHUNK
    lane1 = _lane_iota()
    incl = seg
    for s in (1, 2, 4, 8, 16):
        incl = incl + jnp.where(lane1 >= s, pltpu.roll(incl, s, axis=1), 0.0)
    starts = incl - seg
    row8 = lax.broadcasted_iota(I32, (8, LANES), 0)
    stat_ref[0] = jnp.where(row8 == 0, seg, jnp.where(row8 == 1, starts, 0.0))

    tri = (lax.broadcasted_iota(I32, (tm, tm), 1) < lax.broadcasted_iota(I32, (tm, tm), 0)).astype(BF16)
    rank = jnp.dot(tri, member.astype(BF16), preferred_element_type=F32)
    pos = starts + rank
    lps = [jnp.sum(jnp.where(lane_f == idxs[k], pos, 0.0), axis=-1, keepdims=True) for k in range(TOP_K)]
    lane4 = lax.broadcasted_iota(I32, (tm, TOP_K), 1)
    lp_ref[...] = jnp.where(lane4 == 0, lps[0], jnp.where(lane4 == 1, lps[1],
                                                          jnp.where(lane4 == 2, lps[2], lps[3]))).astype(I32)
    lp_t = _cols_to_lanes(lps, tm, fill=-1.0).T

    g_hi = [g.astype(BF16).astype(F32) for g in gates]
    g_lo = [g - h for g, h in zip(gates, g_hi)]
    meta_in = _cols_to_lanes(list(idxs) + g_hi + g_lo, tm)
    rhs = jnp.concatenate([h2.astype(BF16), meta_in.astype(BF16)], axis=1)

    sub = 512
    starts32 = jnp.where(lane1 < N_EXPERTS, starts, 1e9)

    for c in range(lrows // sub):
        r0 = c * sub
        rid = (lax.broadcasted_iota(I32, (sub, 1), 0) + r0).astype(F32)
        rid_t = (lax.broadcasted_iota(I32, (sub, tm), 0) + r0).astype(F32)
        perm = rid_t == lp_t[0:1, :]
        for k in range(1, TOP_K):
            perm = jnp.logical_or(perm, rid_t == lp_t[k:k + 1, :])
        full = jnp.dot(perm.astype(BF16), rhs, preferred_element_type=F32)
        got = full[:, D_MODEL:]
        e_row = jnp.sum(jnp.where(rid >= starts32, 1.0, 0.0), axis=-1, keepdims=True) - 1.0
        g_sum = pltpu.roll(got, LANES - TOP_K, axis=1) + pltpu.roll(got, LANES - 2 * TOP_K, axis=1)
        lane_s = _lane_iota(sub)
        pick = jnp.logical_and(lane_s < TOP_K, got == e_row)
        gate_row = jnp.sum(jnp.where(pick, g_sum, 0.0), axis=-1, keepdims=True)
        gr_hi = gate_row.astype(BF16).astype(F32)
        meta = jnp.where(lane_s == 0, gr_hi, jnp.where(lane_s == 1, gate_row - gr_hi, 0.0))
        xs_ref[r0:r0 + sub, 0:D_MODEL] = full[:, :D_MODEL].astype(BF16)
        xs_ref[r0:r0 + sub, D_MODEL:XS_WIDTH] = meta.astype(BF16)


def _epilogue(x, merged, wout_ref, fnw_ref, wr_ref, br_ref, x1_ref, xs_ref, lp_ref, stat_ref):
    x1 = x + jnp.dot(merged.astype(BF16), wout_ref[...], preferred_element_type=F32)
    x1_ref[...] = x1
    h2 = _rms(x1, fnw_ref[...])
    logits = jnp.dot(h2.astype(BF16), wr_ref[...].astype(BF16), preferred_element_type=F32) + br_ref[...]
    idxs, gates = _top4_gates(logits)
    _dispatch(h2, idxs, gates, xs_ref, lp_ref, stat_ref)


def _mixer_prompt_kernel(sink_ref, x_ref, anw_ref, win_ref, convw_ref, qnw_ref, knw_ref, bias_ref, wout_ref,
                         fnw_ref, wr_ref, br_ref,
                         x1_ref, xs_ref, lp_ref, stat_ref, kout_ref, vout_ref, cout_ref,
                         ubuf, q_s, kd_s, vt_s, ya_s, st_s, pt_s):
    tm = x_ref.shape[0]
    nblk = tm // ATTN_BLOCK
    j = pl.program_id(1)
    first_tile = j == 0
    lo = _lo_half()

    x = x_ref[...]
    h = _rms(x, anw_ref[...]).astype(BF16)

    def proj(off, n):
        return jnp.dot(h, win_ref[:, off:off + n], preferred_element_type=F32)

    @pl.when(first_tile)
    def _():
        ubuf[0:8, :] = jnp.zeros((8, D_MODEL), F32)
        kd_s[0:ATTN_BLOCK, :] = jnp.zeros((ATTN_BLOCK, 2 * LANES), BF16)
        vt_s[0] = jnp.zeros((N_KV, LANES, ATTN_BLOCK), BF16)

    u = proj(OFF_CG, D_MODEL) * proj(OFF_XIN, D_MODEL)
    ubuf[8:tm + 8, :] = u
    cw = convw_ref[...]
    conv = ubuf[6:tm + 6, :] * cw[0:1, :] + ubuf[7:tm + 7, :] * cw[1:2, :] + u * cw[2:3, :]
    merged = jax.nn.sigmoid(proj(OFF_GC, D_MODEL)) * (proj(OFF_BG, D_MODEL) * conv)
    tail = ubuf[tm + 6:tm + 8, :]
    cout_ref[0] = tail
    ubuf[6:8, :] = tail

    q = proj(OFF_Q, D_MODEL)
    kv = proj(OFF_K, 2 * LANES)
    k = _pair_norm(kv[:, :LANES], knw_ref[...])
    v = kv[:, LANES:]
    qnw = qnw_ref[...]
    for p in range(N_HEADS // 2):
        sl = slice(p * LANES, (p + 1) * LANES)
        q_s[:, sl] = (_pair_norm(q[:, sl], qnw) * ATTN_SCALE).astype(BF16)
    k_sw = pltpu.roll(k, HEAD_DIM, axis=1)
    v_sw = pltpu.roll(v, HEAD_DIM, axis=1)
    kd_s[ATTN_BLOCK:tm + ATTN_BLOCK, 0:LANES] = jnp.where(lo, k, k_sw).astype(BF16)
    kd_s[ATTN_BLOCK:tm + ATTN_BLOCK, LANES:2 * LANES] = jnp.where(lo, k_sw, k).astype(BF16)
    v_dup = (jnp.where(lo, v, v_sw), jnp.where(lo, v_sw, v))
    for b in range(nblk):
        for g in range(N_KV):
            vt_s[b + 1, g] = v_dup[g][b * ATTN_BLOCK:(b + 1) * ATTN_BLOCK, :].T.astype(BF16)

    @pl.when(j == pl.num_programs(1) - 1)
    def _():
        kout_ref[0] = k[tm - ATTN_BLOCK:, :]
        vout_ref[0] = v[tm - ATTN_BLOCK:, :]

    prev_rows = lax.broadcasted_iota(I32, (2 * ATTN_BLOCK, 1), 0) < ATTN_BLOCK
    feat_lo = lax.broadcasted_iota(I32, (LANES, 1), 0) < HEAD_DIM

    def attn_block(blk):
        r0 = blk * ATTN_BLOCK
        qb = q_s[r0:r0 + ATTN_BLOCK, :]
        for g in range(N_KV):
            parts = []
            for t in range(GROUP):
                hd = g * GROUP + t
                slab = qb[:, (hd // 2) * LANES:(hd // 2 + 1) * LANES]
                keep = lo if hd % 2 == 0 else jnp.logical_not(lo)
                parts.append(jnp.where(keep, slab, jnp.zeros_like(slab)))
            lhs = jnp.concatenate(parts, axis=0)
            st = lax.dot_general(kd_s[r0:r0 + 2 * ATTN_BLOCK, g * LANES:(g + 1) * LANES], lhs,
                                 (((1,), (1,)), ((), ())), preferred_element_type=F32)
            st = st + bias_ref[g]
            if blk == 0:
                st = jnp.where(jnp.logical_and(prev_rows, first_tile), NEG_INF, st)
            st_s[...] = st
            sink = sink_ref[g:g + 1, :]
            rdens = []
            for t in range(GROUP):
                cols = slice(t * LANES, (t + 1) * LANES)
                s = st_s[:, cols]
                m = jnp.maximum(jnp.max(s, axis=0, keepdims=True), sink[:, cols])
                pr = jnp.exp(s - m)
                den = jnp.sum(pr, axis=0, keepdims=True) + jnp.exp(sink[:, cols] - m)
                pt_s[:, cols] = pr.astype(BF16)
                rdens.append(1.0 / den)
            vt = jnp.concatenate([vt_s[blk, g], vt_s[blk + 1, g]], axis=1)
            ot = jnp.dot(vt, pt_s[...], preferred_element_type=F32)
            for i in range(GROUP // 2):
                pair = g * (GROUP // 2) + i
                even = ot[:, (2 * i) * LANES:(2 * i + 1) * LANES] * rdens[2 * i]
                odd = ot[:, (2 * i + 1) * LANES:(2 * i + 2) * LANES] * rdens[2 * i + 1]
                ya_s[r0:r0 + ATTN_BLOCK, pair * LANES:(pair + 1) * LANES] = jnp.where(feat_lo, even, odd).T

    for blk in range(nblk):
        attn_block(blk)

    kd_s[0:ATTN_BLOCK, :] = kd_s[tm:tm + ATTN_BLOCK, :]
    vt_s[0] = vt_s[nblk]

    merged = merged + jax.nn.sigmoid(proj(OFF_GA, D_MODEL)) * ya_s[...]
    _epilogue(x, merged, wout_ref, fnw_ref, wr_ref, br_ref, x1_ref, xs_ref, lp_ref, stat_ref)


def _const_spec(shape):
    nd = len(shape)
    return pl.BlockSpec(shape, lambda *_: (0,) * nd, pipeline_mode=pl.Buffered(1))


def _mixer_prompt(x, xs_rows, sinks, anw, win_bf, convw, qnw, knw, bias_tab, wout_bf, fnw, wr, br):
    batch, seq, _ = x.shape
    tm = MIXER_TILE
    lrows = _local_rows(tm)
    nj = seq // tm
    tokens = batch * seq
    n_tiles = batch * nj
    x2 = x.reshape(tokens, D_MODEL)
    tok_spec = lambda width: pl.BlockSpec((tm, width), lambda b, j: (b * nj + j, 0))
    per_batch = lambda rows, width: pl.BlockSpec((1, rows, width), lambda b, j: (b, 0, 0))
    in_specs = [
        _const_spec((N_KV, GROUP * ATTN_BLOCK)),
        tok_spec(D_MODEL),
        _const_spec((1, D_MODEL)),
        _const_spec((D_MODEL, IN_DIM)),
        _const_spec((3, D_MODEL)),
        _const_spec((1, LANES)),
        _const_spec((1, LANES)),
        _const_spec((N_KV, 2 * ATTN_BLOCK, GROUP * ATTN_BLOCK)),
        _const_spec((D_MODEL, D_MODEL)),
        _const_spec((1, D_MODEL)),
        _const_spec((D_MODEL, N_EXPERTS)),
        _const_spec((1, N_EXPERTS)),
    ]
    out_shape = (
        jax.ShapeDtypeStruct((tokens, D_MODEL), F32),
        jax.ShapeDtypeStruct((xs_rows, XS_WIDTH), BF16),
        jax.ShapeDtypeStruct((tokens, TOP_K), I32),
        jax.ShapeDtypeStruct((n_tiles, 8, LANES), F32),
        jax.ShapeDtypeStruct((batch, ATTN_BLOCK, LANES), F32),
        jax.ShapeDtypeStruct((batch, ATTN_BLOCK, LANES), F32),
        jax.ShapeDtypeStruct((batch, 2, D_MODEL), F32),
    )
    out_specs = (
        tok_spec(D_MODEL),
        pl.BlockSpec((lrows, XS_WIDTH), lambda b, j: (b * nj + j, 0)),
        tok_spec(TOP_K),
        pl.BlockSpec((1, 8, LANES), lambda b, j: (b * nj + j, 0, 0)),
        per_batch(ATTN_BLOCK, LANES), per_batch(ATTN_BLOCK, LANES), per_batch(2, D_MODEL),
    )
    scratch = [
        pltpu.VMEM((tm + 8, D_MODEL), F32),
        pltpu.VMEM((tm, D_MODEL), BF16),
        pltpu.VMEM((tm + ATTN_BLOCK, 2 * LANES), BF16),
        pltpu.VMEM((tm // ATTN_BLOCK + 1, N_KV, LANES, ATTN_BLOCK), BF16),
        pltpu.VMEM((tm, D_MODEL), F32),
        pltpu.VMEM((2 * ATTN_BLOCK, GROUP * ATTN_BLOCK), F32),
        pltpu.VMEM((2 * ATTN_BLOCK, GROUP * ATTN_BLOCK), BF16),
    ]
    return pl.pallas_call(
        _mixer_prompt_kernel,
        grid=(batch, nj),
        in_specs=in_specs,
        out_specs=out_specs,
        out_shape=out_shape,
        scratch_shapes=scratch,
        compiler_params=pltpu.CompilerParams(
            dimension_semantics=("arbitrary", "arbitrary"), vmem_limit_bytes=VMEM_LIMIT),
        name="mixer_prompt",
    )(sinks, x2, anw, win_bf, convw, qnw, knw, bias_tab, wout_bf, fnw, wr, br)


def _mixer_sample_kernel(xs_in_ref, x_ref, p0_ref, p1_ref, kp_ref, vp_ref, sink_ref, anw_ref, win_ref, convw_ref,
                         qnw_ref, knw_ref, bias_ref, wout_ref, fnw_ref, wr_ref, br_ref,
                         x1_ref, xs_ref, lp_ref, stat_ref, knew_ref, vnew_ref, unew_ref,
                         qh_s, o_s, kn_s, vn_s):
    del xs_in_ref
    nb = x_ref.shape[0]
    lo = _lo_half()
    x = x_ref[...]
    h = _rms(x, anw_ref[...]).astype(BF16)

    def proj(off, n):
        return jnp.dot(h, win_ref[:, off:off + n], preferred_element_type=F32)

    u = proj(OFF_CG, D_MODEL) * proj(OFF_XIN, D_MODEL)
    unew_ref[...] = u
    cw = convw_ref[...]
    conv = p0_ref[...] * cw[0:1, :] + p1_ref[...] * cw[1:2, :] + u * cw[2:3, :]
    merged = jax.nn.sigmoid(proj(OFF_GC, D_MODEL)) * (proj(OFF_BG, D_MODEL) * conv)

    q = proj(OFF_Q, D_MODEL)
    kv = proj(OFF_K, 2 * LANES)
    k = _pair_norm(kv[:, :LANES], knw_ref[...])
    v = kv[:, LANES:]
    knew_ref[...] = k
    vnew_ref[...] = v
    kn_s[...] = k
    vn_s[...] = v

    qnw = qnw_ref[...]
    for hd in range(N_HEADS):
        pair, half, grp = hd // 2, hd % 2, hd // GROUP
        slab = _pair_norm(q[:, pair * LANES:(pair + 1) * LANES], qnw) * ATTN_SCALE
        slab = jnp.where(lo if half == 0 else jnp.logical_not(lo), slab, 0.0)
        if half != grp:
            slab = pltpu.roll(slab, HEAD_DIM, axis=1)
        qh_s[hd * nb:(hd + 1) * nb, :] = slab

    sink = sink_ref[...]
    bias = bias_ref[...]
    row0 = lax.broadcasted_iota(I32, (WINDOW, 1), 0) == 0

    def token(b, carry):
        qb = qh_s[pl.ds(b, N_HEADS, stride=nb), :]
        kb = jnp.where(row0, kn_s[pl.ds(b, 1), :], kp_ref[b])
        vb = jnp.where(row0, vn_s[pl.ds(b, 1), :], vp_ref[b])
        s = lax.dot_general(qb.astype(BF16), kb.astype(BF16), (((1,), (1,)), ((), ())),
                            preferred_element_type=F32) + bias
        m = jnp.maximum(jnp.max(s, axis=-1, keepdims=True), sink)
        pr = jnp.exp(s - m)
        den = jnp.sum(pr, axis=-1, keepdims=True) + jnp.exp(sink - m)
        o = jnp.dot(pr.astype(BF16), vb.astype(BF16), preferred_element_type=F32) / den
        o_s[pl.ds(b, N_HEADS, stride=nb), :] = o
        return carry

    lax.fori_loop(0, nb, token, 0)

    cols = []
    for pair in range(N_HEADS // 2):
        halves = []
        for half in range(2):
            hd = 2 * pair + half
            slab = o_s[hd * nb:(hd + 1) * nb, :]
            if half != hd // GROUP:
                slab = pltpu.roll(slab, HEAD_DIM, axis=1)
            halves.append(slab)
        cols.append(jnp.where(lo, halves[0], halves[1]))
    y_attn = jnp.concatenate(cols, axis=1)

    merged = merged + jax.nn.sigmoid(proj(OFF_GA, D_MODEL)) * y_attn
    _epilogue(x, merged, wout_ref, fnw_ref, wr_ref, br_ref, x1_ref, xs_ref, lp_ref, stat_ref)


def _mixer_sample(xs_big, xs_block, x, p0, p1, k_past, v_past, sink_col, anw, win_bf, convw, qnw, knw, bias_s,
                  wout_bf, fnw, wr, br):
    nb = x.shape[0]
    lrows = _local_rows(nb)
    args = (x, p0, p1, k_past, v_past, sink_col, anw, win_bf, convw, qnw, knw, bias_s, wout_bf, fnw, wr, br)
    out_shape = (
        jax.ShapeDtypeStruct((nb, D_MODEL), F32),
        jax.ShapeDtypeStruct(xs_big.shape, BF16),
        jax.ShapeDtypeStruct((nb, TOP_K), I32),
        jax.ShapeDtypeStruct((1, 8, LANES), F32),
        jax.ShapeDtypeStruct((nb, LANES), F32),
        jax.ShapeDtypeStruct((nb, LANES), F32),
        jax.ShapeDtypeStruct((nb, D_MODEL), F32),
    )
    full = lambda s: pl.BlockSpec(s.shape, lambda i: (0,) * len(s.shape))
    out_specs = (
        full(out_shape[0]),
        pl.BlockSpec((lrows, XS_WIDTH), lambda i: (xs_block, 0)),
        full(out_shape[2]), full(out_shape[3]), full(out_shape[4]), full(out_shape[5]), full(out_shape[6]),
    )
    scratch = [
        pltpu.VMEM((N_HEADS * nb, LANES), F32),
        pltpu.VMEM((N_HEADS * nb, LANES), F32),
        pltpu.VMEM((nb, LANES), F32),
        pltpu.VMEM((nb, LANES), F32),
    ]
    return pl.pallas_call(
        _mixer_sample_kernel,
        grid=(1,),
        in_specs=[pl.BlockSpec(memory_space=pl.ANY)] + [_const_spec(a.shape) for a in args],
        out_specs=out_specs,
        out_shape=out_shape,
        scratch_shapes=scratch,
        input_output_aliases={0: 1},
        compiler_params=pltpu.CompilerParams(dimension_semantics=("arbitrary",), vmem_limit_bytes=VMEM_LIMIT),
        name="mixer_sample",
    )(xs_big, *args)


def _expert_kernel(be_ref, nact_ref, src_ref, dst_ref, xs_hbm, wup_ref, bup_ref, wdn_ref, bdn_ref, ys_hbm,
                   wup_s, wdn_s, perm_s, xbuf, ybuf, in_sem, out_sem):
    i = pl.program_id(0)
    n_real = nact_ref[0]
    n_active = nact_ref[1]
    e = be_ref[i]
    e_prev = be_ref[jnp.maximum(i - 1, 0)]
    slot = lax.rem(i, 2)

    def gather_copy(blk, slt, c):
        row = pl.multiple_of(src_ref[blk * CHUNKS_PER_BLOCK + c] * CHUNK, CHUNK)
        return pltpu.make_async_copy(xs_hbm.at[pl.ds(row, CHUNK), :],
                                     xbuf.at[slt, pl.ds(c * CHUNK, CHUNK), :], in_sem.at[slt])

    def scatter_chunks(blk, slt, wait):
        base = blk * CHUNKS_PER_BLOCK

        def one(c, dst):
            row = pl.multiple_of(dst * CHUNK, CHUNK)
            cp = pltpu.make_async_copy(ybuf.at[slt, pl.ds(c * CHUNK, CHUNK), :],
                                       ys_hbm.at[pl.ds(row, CHUNK), :], out_sem.at[slt])
            if wait:
                cp.wait()
            else:
                cp.start()

        full = dst_ref[base + CHUNKS_PER_BLOCK - 1] >= 0

        @pl.when(full)
        def _():
            for c in range(CHUNKS_PER_BLOCK):
                one(c, dst_ref[base + c])

        @pl.when(jnp.logical_not(full))
        def _():
            for c in range(CHUNKS_PER_BLOCK - 1):
                dst = dst_ref[base + c]

                @pl.when(dst >= 0)
                def _():
                    one(c, dst)

    @pl.when(jnp.logical_and(i == 0, n_real > 0))
    def _():
        for c in range(CHUNKS_PER_BLOCK):
            gather_copy(0, 0, c).start()

    @pl.when(jnp.logical_and(i < n_real, jnp.logical_or(i == 0, e != e_prev)))
    def _():
        wup_s[...] = wup_ref[0].astype(BF16)
        half = LANES // 2
        for cs in range(D_MODEL // LANES):
            cols = slice(cs * LANES, (cs + 1) * LANES)
            for c in range(D_MODEL // LANES):
                for par in range(2):
                    s0 = c * LANES + par * half
                    perm_s[cs, pl.ds(c * LANES + par, half, stride=2), :] = wdn_ref[0, s0:s0 + half, cols]
            wdn_s[:, cols] = perm_s[cs].astype(BF16)

    @pl.when(i + 1 < n_real)
    def _():
        for c in range(CHUNKS_PER_BLOCK):
            gather_copy(i + 1, 1 - slot, c).start()

    @pl.when(jnp.logical_and(i >= 2, i < n_active))
    def _():
        scatter_chunks(i - 2, slot, wait=True)

    @pl.when(jnp.logical_and(i >= n_real, i < n_active))
    def _():
        ybuf[slot] = jnp.zeros((EXPERT_BLOCK, D_MODEL), BF16)
        scatter_chunks(i, slot, wait=False)

    @pl.when(i < n_real)
    def _():
        for c in range(CHUNKS_PER_BLOCK):
            gather_copy(i, slot, c).wait()

        xb = xbuf[slot]
        meta = xb[:, D_MODEL:].astype(F32)
        gate = meta[:, 0:1] + meta[:, 1:2]
        u = jnp.dot(xb[:, :D_MODEL], wup_s[...], preferred_element_type=F32) + bup_ref[0]
        even = (_lane_iota() & 1) == 0
        cols = []
        for c in range(D_MODEL // LANES):
            c0 = u[:, (2 * c) * LANES:(2 * c + 1) * LANES]
            c1 = u[:, (2 * c + 1) * LANES:(2 * c + 2) * LANES]
            glu = jnp.where(even, c0, pltpu.roll(c1, 1, axis=1))
            lin = jnp.where(even, pltpu.roll(c0, LANES - 1, axis=1), c1)
            glu = jnp.minimum(glu, SWIGLU_LIMIT)
            lin = jnp.clip(lin, -SWIGLU_LIMIT, SWIGLU_LIMIT)
            cols.append(glu * jax.nn.sigmoid(SWIGLU_ALPHA * glu) * (lin + 1.0))
        a = jnp.concatenate(cols, axis=1).astype(BF16)
        y = jnp.dot(a, wdn_s[...], preferred_element_type=F32) + bdn_ref[0]
        ybuf[slot] = (y * gate).astype(BF16)
        scatter_chunks(i, slot, wait=False)

    @pl.when(i == n_active - 1)
    def _():
        @pl.when(i >= 1)
        def _():
            scatter_chunks(i - 1, 1 - slot, wait=True)

        scatter_chunks(i, slot, wait=True)


def _experts(block_expert, n_active, chunk_src, chunk_dst, xs_big, ys_rows, w_up, b_up, w_down, b_down):
    n_blocks = block_expert.shape[0]
    wspec = lambda shape: pl.BlockSpec(shape, lambda i, be, na, cs, cd: (jnp.minimum(be[i], N_EXPERTS - 1), 0, 0))
    grid_spec = pltpu.PrefetchScalarGridSpec(
        num_scalar_prefetch=4,
        grid=(n_blocks,),
        in_specs=[
            pl.BlockSpec(memory_space=pl.ANY),
            wspec((1, D_MODEL, 2 * D_MODEL)),
            wspec((1, 1, 2 * D_MODEL)),
            wspec((1, D_MODEL, D_MODEL)),
            wspec((1, 1, D_MODEL)),
        ],
        out_specs=pl.BlockSpec(memory_space=pl.ANY),
        scratch_shapes=[
            pltpu.VMEM((D_MODEL, 2 * D_MODEL), BF16),
            pltpu.VMEM((D_MODEL, D_MODEL), BF16),
            pltpu.VMEM((D_MODEL // LANES, D_MODEL, LANES), F32),
            pltpu.VMEM((2, EXPERT_BLOCK, XS_WIDTH), BF16),
            pltpu.VMEM((2, EXPERT_BLOCK, D_MODEL), BF16),
            pltpu.SemaphoreType.DMA((2,)),
            pltpu.SemaphoreType.DMA((2,)),
        ],
    )
    return pl.pallas_call(
        _expert_kernel,
        grid_spec=grid_spec,
        out_shape=jax.ShapeDtypeStruct((ys_rows, D_MODEL), BF16),
        compiler_params=pltpu.CompilerParams(dimension_semantics=("arbitrary",), vmem_limit_bytes=VMEM_LIMIT),
        name="experts",
    )(block_expert, n_active, chunk_src, chunk_dst, xs_big, w_up, b_up.reshape(N_EXPERTS, 1, 2 * D_MODEL),
      w_down, b_down.reshape(N_EXPERTS, 1, D_MODEL))


def _combine_kernel(lp_ref, ys_ref, x1_ref, out_ref):
    tm = x1_ref.shape[0]
    lrows = ys_ref.shape[0]
    lp = lp_ref[...].astype(F32)
    rid = lax.broadcasted_iota(I32, (tm, lrows), 1).astype(F32)
    sel = rid == lp[:, 0:1]
    for k in range(1, TOP_K):
        sel = jnp.logical_or(sel, rid == lp[:, k:k + 1])
    out_ref[...] = x1_ref[...] + jnp.dot(sel.astype(BF16), ys_ref[...], preferred_element_type=F32)


def _combine(lp, ys_big, x1, tm, first_block):
    tokens = x1.shape[0]
    lrows = _local_rows(tm)
    return pl.pallas_call(
        _combine_kernel,
        grid=(tokens // tm,),
        in_specs=[
            pl.BlockSpec((tm, TOP_K), lambda j: (j, 0)),
            pl.BlockSpec((lrows, D_MODEL), lambda j: (first_block + j, 0)),
            pl.BlockSpec((tm, D_MODEL), lambda j: (j, 0)),
        ],
        out_specs=pl.BlockSpec((tm, D_MODEL), lambda j: (j, 0)),
        out_shape=jax.ShapeDtypeStruct((tokens, D_MODEL), F32),
        compiler_params=pltpu.CompilerParams(dimension_semantics=("arbitrary",), vmem_limit_bytes=VMEM_LIMIT),
        name="combine",
    )(lp, ys_big, x1)


def _bias_tables(rel_bias):
    qi = np.arange(ATTN_BLOCK)[:, None]
    kj = np.arange(2 * ATTN_BLOCK)[None, :]
    rel = qi + ATTN_BLOCK - kj
    valid = (rel >= 0) & (rel < WINDOW)
    bucket = np.where(valid, _t5_bucket_np(rel), -1)
    slot_rel = np.where(np.arange(WINDOW) == 0, 0, WINDOW - np.arange(WINDOW))
    slot_bucket = _t5_bucket_np(slot_rel)
    tab = jnp.full((N_HEADS, ATTN_BLOCK, 2 * ATTN_BLOCK), NEG_INF, F32)
    sample_tab = jnp.zeros((N_HEADS, WINDOW), F32)
    for b in range(N_BUCKETS):
        tab = jnp.where((bucket == b)[None], rel_bias[b][:, None, None], tab)
        sample_tab = jnp.where((slot_bucket == b)[None], rel_bias[b][:, None], sample_tab)
    prompt_tab = tab.reshape(N_KV, GROUP, ATTN_BLOCK, 2 * ATTN_BLOCK).transpose(0, 3, 1, 2)
    return prompt_tab.reshape(N_KV, 2 * ATTN_BLOCK, GROUP * ATTN_BLOCK), sample_tab


def _chunk_tables(seg, starts, tile_base, tile_rows, n_blocks):
    n_tiles = seg.shape[0]
    n_seg = (N_EXPERTS + 1) * n_tiles
    used = jnp.sum(seg, axis=1)
    seg_e = jnp.concatenate([seg.T, (tile_rows - used)[None, :]], axis=0)
    src0 = jnp.concatenate([tile_base[None, :] + starts.T, (tile_base + used)[None, :]], axis=0).reshape(-1)
    total = jnp.sum(seg_e, axis=1)
    region = (total + EXPERT_BLOCK - 1) // EXPERT_BLOCK * EXPERT_BLOCK
    pad_end = jnp.cumsum(region)
    pad_start = pad_end - region
    g_start = (pad_start[:, None] + jnp.cumsum(seg_e, axis=1) - seg_e).reshape(-1)
    g_end = g_start + seg_e.reshape(-1)
    rows = jnp.arange(n_blocks * CHUNKS_PER_BLOCK, dtype=I32) * CHUNK
    passed = g_end[None, :] <= rows[:, None]

    def at_segment(table, sentinel):
        ext = jnp.concatenate([table, jnp.array([sentinel], I32)])
        return ext[0] + jnp.sum(jnp.where(passed, (ext[1:] - ext[:-1])[None, :], 0), axis=1)

    seg_start = at_segment(g_start, 1 << 30)
    valid = rows >= seg_start
    src_row = at_segment(src0, 0) + rows - seg_start
    chunk_src = jnp.where(valid, src_row // CHUNK, 0).astype(I32)
    chunk_dst = jnp.where(valid, src_row // CHUNK, -1).astype(I32)
    blk_rows = jnp.arange(n_blocks, dtype=I32) * EXPERT_BLOCK
    block_region = jnp.minimum(jnp.sum((pad_end[None, :] <= blk_rows[:, None]).astype(I32), axis=1),
                               N_EXPERTS).astype(I32)
    counts = jnp.stack([pad_end[N_EXPERTS - 1], pad_end[N_EXPERTS]]).astype(I32) // EXPERT_BLOCK
    return block_region, counts, chunk_src, chunk_dst


def kernel(x_prompt, x_sample, state_conv, cache_k_win, cache_v_win, rel_bias, attn_norm_w, w_in, conv_w,
           q_norm_w, k_norm_w, sinks, w_out, ffn_norm_w, w_router, b_router, w_up, b_up, w_down, b_down):
    batch, seq, _ = x_prompt.shape
    nb = x_sample.shape[0]
    anw = attn_norm_w[0].reshape(1, D_MODEL)
    fnw = ffn_norm_w[0].reshape(1, D_MODEL)
    win_bf = w_in[0].astype(BF16)
    wout_bf = w_out[0].astype(BF16)
    qnw = jnp.tile(q_norm_w[0], 2).reshape(1, LANES)
    knw = jnp.tile(k_norm_w[0], 2).reshape(1, LANES)
    br = b_router[0].reshape(1, N_EXPERTS)
    prompt_tab, sample_tab = _bias_tables(rel_bias)

    n_tiles_p = batch * seq // MIXER_TILE
    lrows_p, lrows_s = _local_rows(MIXER_TILE), _local_rows(nb)
    sample_base = n_tiles_p * lrows_p
    assert sample_base % lrows_s == 0
    xs_rows = sample_base + lrows_s
    ys_rows = xs_rows

    sink_rows = jnp.repeat(sinks[0].reshape(N_KV, GROUP), ATTN_BLOCK, axis=1)
    x1p, xs_big, lpp, statp, kp, vp, cp = _mixer_prompt(
        x_prompt, xs_rows, sink_rows, anw, win_bf, conv_w[0], qnw, knw, prompt_tab, wout_bf, fnw, w_router[0], br)

    k_past = cache_k_win[0].reshape(nb, WINDOW, LANES)
    v_past = cache_v_win[0].reshape(nb, WINDOW, LANES)
    x1s, xs_big, lps, stats, knew, vnew, unew = _mixer_sample(
        xs_big, sample_base // lrows_s, x_sample.reshape(nb, D_MODEL), state_conv[0, :, 0, :], state_conv[0, :, 1, :],
        k_past, v_past, sinks[0].reshape(N_HEADS, 1), anw, win_bf, conv_w[0], qnw, knw, sample_tab, wout_bf, fnw,
        w_router[0], br)

    stat = jnp.concatenate([statp, stats], axis=0)
    seg = stat[:, 0, :N_EXPERTS].astype(I32)
    starts = stat[:, 1, :N_EXPERTS].astype(I32)
    tile_base = jnp.concatenate([jnp.arange(n_tiles_p, dtype=I32) * lrows_p, jnp.array([sample_base], I32)])
    tile_rows = jnp.array([lrows_p] * n_tiles_p + [lrows_s], I32)
    n_blocks = -(-(xs_rows + (N_EXPERTS + 1) * (EXPERT_BLOCK - 1)) // EXPERT_BLOCK)
    block_expert, n_active, chunk_src, chunk_dst = _chunk_tables(seg, starts, tile_base, tile_rows, n_blocks)

    ys_big = _experts(block_expert, n_active, chunk_src, chunk_dst, xs_big, ys_rows, w_up[0], b_up[0], w_down[0],
                      b_down[0])

    y_prompt = _combine(lpp, ys_big, x1p, MIXER_TILE, 0).reshape(batch, seq, D_MODEL)
    y_sample = _combine(lps, ys_big, x1s, nb, sample_base // lrows_s).reshape(nb, 1, D_MODEL)

    new_k_sample = jnp.concatenate([k_past[:, 1:], knew[:, None, :]], axis=1)
    new_v_sample = jnp.concatenate([v_past[:, 1:], vnew[:, None, :]], axis=1)
    new_conv_sample = jnp.stack([state_conv[0, :, 1, :], unew], axis=1)
    return (
        y_prompt,
        y_sample,
        cp[None],
        kp.reshape(1, batch, ATTN_BLOCK, N_KV, HEAD_DIM),
        vp.reshape(1, batch, ATTN_BLOCK, N_KV, HEAD_DIM),
        new_conv_sample[None],
        new_k_sample.reshape(1, nb, WINDOW, N_KV, HEAD_DIM),
        new_v_sample.reshape(1, nb, WINDOW, N_KV, HEAD_DIM),
    )
```

```python
import functools
import math

import numpy as np
import jax
import jax.numpy as jnp
from jax import lax
from jax.experimental import pallas as pl
from jax.experimental.pallas import tpu as pltpu

F32 = jnp.float32
BF16 = jnp.bfloat16
I32 = jnp.int32

D_MODEL = 1024
HEAD_DIM = 64
N_HEADS = 16
N_KV = 2
GROUP = N_HEADS // N_KV
WINDOW = 128
ATTN_BLOCK = 128
N_BUCKETS = 32
MAX_DISTANCE = 128
NEG_INF = -1e30
N_EXPERTS = 32
TOP_K = 4
SWIGLU_ALPHA = 1.702
SWIGLU_LIMIT = 7.0
EPS = 1e-5
ATTN_SCALE = HEAD_DIM ** -0.5

OFF_XIN, OFF_BG, OFF_CG, OFF_Q = 0, 1024, 2048, 3072
OFF_K, OFF_V, OFF_GC, OFF_GA = 4096, 4224, 4352, 5376
IN_DIM = 6400

LANES = 128
MIXER_TILE = 512
EXPERT_BLOCK = 512
CHUNK = 16
CHUNKS_PER_BLOCK = EXPERT_BLOCK // CHUNK
XS_WIDTH = D_MODEL + LANES
VMEM_LIMIT = 60 * 1024 * 1024


def _local_rows(tm):
    need = tm * TOP_K + N_EXPERTS * (CHUNK - 1)
    return -(-need // 512) * 512


def _t5_bucket_np(rel):
    n = np.maximum(rel, 0)
    max_exact = N_BUCKETS // 2
    nf = np.maximum(n, 1).astype(np.float64)
    large = max_exact + (np.log(nf / max_exact) / math.log(MAX_DISTANCE / max_exact)
                         * (N_BUCKETS - max_exact)).astype(np.int32)
    large = np.minimum(large, N_BUCKETS - 1)
    return np.where(n < max_exact, n, large).astype(np.int32)


def _rms(x, w):
    return x * lax.rsqrt(jnp.mean(x * x, axis=-1, keepdims=True) + EPS) * w


def _lane_iota(rows=1):
    return lax.broadcasted_iota(I32, (rows, LANES), 1)


def _lo_half():
    return _lane_iota() < HEAD_DIM


def _pair_norm(t, w128):
    lo = _lo_half()
    sq = t * t
    s_lo = jnp.sum(jnp.where(lo, sq, 0.0), axis=-1, keepdims=True)
    s_hi = jnp.sum(jnp.where(lo, 0.0, sq), axis=-1, keepdims=True)
    r = jnp.where(lo, lax.rsqrt(s_lo * (1.0 / HEAD_DIM) + EPS), lax.rsqrt(s_hi * (1.0 / HEAD_DIM) + EPS))
    return t * r * w128


def _top4_gates(logits):
    rows = logits.shape[0]
    lane = lax.broadcasted_iota(I32, (rows, N_EXPERTS), 1).astype(F32)
    vals, idxs = [], []
    l = logits
    for _ in range(TOP_K):
        m = jnp.max(l, axis=-1, keepdims=True)
        idx = jnp.min(jnp.where(l == m, lane, float(N_EXPERTS)), axis=-1, keepdims=True)
        vals.append(m)
        idxs.append(idx)
        l = jnp.where(lane == idx, -jnp.inf, l)
    es = [jnp.exp(v - vals[0]) for v in vals]
    den = es[0] + es[1] + es[2] + es[3]
    return idxs, [e / den for e in es]


def _cols_to_lanes(cols, rows, fill=0.0):
    lane = _lane_iota(rows)
    out = jnp.full((rows, LANES), fill, F32)
    for k, c in enumerate(cols):
        out = jnp.where(lane == k, c, out)
    return out


def _dispatch(h2, idxs, gates, xs_ref, lp_ref, stat_ref):
    tm = h2.shape[0]
    lrows = xs_ref.shape[0]
    lane = _lane_iota(tm)
    lane_f = lane.astype(F32)
    member = jnp.zeros((tm, LANES), F32)
    for k in range(TOP_K):
        member = member + jnp.where(lane_f == idxs[k], 1.0, 0.0)
    cnt = jnp.sum(member, axis=0, keepdims=True)
    seg = jnp.floor((cnt + (CHUNK - 1)) * (1.0 / CHUNK)) * CHUNK
    lane1 = _lane_iota()
    incl = seg
    for s in (1, 2, 4, 8, 16):
        incl = incl + jnp.where(lane1 >= s, pltpu.roll(incl, s, axis=1), 0.0)
    starts = incl - seg
    row8 = lax.broadcasted_iota(I32, (8, LANES), 0)
    stat_ref[0] = jnp.where(row8 == 0, seg, jnp.where(row8 == 1, starts, 0.0))

    tri = (lax.broadcasted_iota(I32, (tm, tm), 1) < lax.broadcasted_iota(I32, (tm, tm), 0)).astype(BF16)
    rank = jnp.dot(tri, member.astype(BF16), preferred_element_type=F32)
    pos = starts + rank
    lps = [jnp.sum(jnp.where(lane_f == idxs[k], pos, 0.0), axis=-1, keepdims=True) for k in range(TOP_K)]
    lane4 = lax.broadcasted_iota(I32, (tm, TOP_K), 1)
    lp_ref[...] = jnp.where(lane4 == 0, lps[0], jnp.where(lane4 == 1, lps[1],
                                                          jnp.where(lane4 == 2, lps[2], lps[3]))).astype(I32)
    lp_t = _cols_to_lanes(lps, tm, fill=-1.0).T

    g_hi = [g.astype(BF16).astype(F32) for g in gates]
    g_lo = [g - h for g, h in zip(gates, g_hi)]
    meta_in = _cols_to_lanes(list(idxs) + g_hi + g_lo, tm)
    rhs = jnp.concatenate([h2.astype(BF16), meta_in.astype(BF16)], axis=1)

    sub = 512
    starts32 = jnp.where(lane1 < N_EXPERTS, starts, 1e9)

    for c in range(lrows // sub):
        r0 = c * sub
        rid = (lax.broadcasted_iota(I32, (sub, 1), 0) + r0).astype(F32)
        rid_t = (lax.broadcasted_iota(I32, (sub, tm), 0) + r0).astype(F32)
        perm = rid_t == lp_t[0:1, :]
        for k in range(1, TOP_K):
            perm = jnp.logical_or(perm, rid_t == lp_t[k:k + 1, :])
        full = jnp.dot(perm.astype(BF16), rhs, preferred_element_type=F32)
        got = full[:, D_MODEL:]
        e_row = jnp.sum(jnp.where(rid >= starts32, 1.0, 0.0), axis=-1, keepdims=True) - 1.0
        g_sum = pltpu.roll(got, LANES - TOP_K, axis=1) + pltpu.roll(got, LANES - 2 * TOP_K, axis=1)
        lane_s = _lane_iota(sub)
        pick = jnp.logical_and(lane_s < TOP_K, got == e_row)
        gate_row = jnp.sum(jnp.where(pick, g_sum, 0.0), axis=-1, keepdims=True)
        gr_hi = gate_row.astype(BF16).astype(F32)
        meta = jnp.where(lane_s == 0, gr_hi, jnp.where(lane_s == 1, gate_row - gr_hi, 0.0))
        xs_ref[r0:r0 + sub, 0:D_MODEL] = full[:, :D_MODEL].astype(BF16)
        xs_ref[r0:r0 + sub, D_MODEL:XS_WIDTH] = meta.astype(BF16)


def _epilogue(x, merged, wout_ref, fnw_ref, wr_ref, br_ref, x1_ref):
    x1 = x + jnp.dot(merged.astype(BF16), wout_ref[...], preferred_element_type=F32)
    x1_ref[...] = x1
    h2 = _rms(x1, fnw_ref[...]).astype(BF16)
    logits = jnp.dot(h2, wr_ref[...].astype(BF16), preferred_element_type=F32) + br_ref[...]
    idxs, gates = _top4_gates(logits)
    return h2, idxs, gates


def _mixer_prompt_kernel(nj, sink_ref, x_ref, anw_ref, win_ref, convw_ref, qnw_ref, knw_ref, bias_ref, wout_ref,
                         fnw_ref, wr_ref, br_ref, h2s_ref, idxs_ref, gates_ref,
                         x1_ref, xs_ref, lp_ref, lps_ref, stat_ref, kout_ref, vout_ref, cout_ref,
                         ubuf, q_s, kd_s, vt_s, ya_s, st_s, pt_s):
    i = pl.program_id(0)
    n_tiles = pl.num_programs(0) - 1

    @pl.when(i < n_tiles)
    def _():
        _prompt_tile(lax.rem(i, nj), nj, sink_ref, x_ref, anw_ref, win_ref, convw_ref, qnw_ref, knw_ref, bias_ref,
                     wout_ref, fnw_ref, wr_ref, br_ref, x1_ref, xs_ref, lp_ref, stat_ref, kout_ref, vout_ref,
                     cout_ref, ubuf, q_s, kd_s, vt_s, ya_s, st_s, pt_s)

    @pl.when(i == n_tiles)
    def _():
        idx4 = idxs_ref[...]
        gate4 = gates_ref[...]
        _dispatch(h2s_ref[...], [idx4[:, k:k + 1] for k in range(TOP_K)], [gate4[:, k:k + 1] for k in range(TOP_K)],
                  xs_ref, lps_ref, stat_ref)


def _prompt_tile(j, nj, sink_ref, x_ref, anw_ref, win_ref, convw_ref, qnw_ref, knw_ref, bias_ref, wout_ref,
                 fnw_ref, wr_ref, br_ref, x1_ref, xs_ref, lp_ref, stat_ref, kout_ref, vout_ref, cout_ref,
                 ubuf, q_s, kd_s, vt_s, ya_s, st_s, pt_s):
    tm = x_ref.shape[0]
    nblk = tm // ATTN_BLOCK
    first_tile = j == 0
    lo = _lo_half()

    x = x_ref[...]
    h = _rms(x, anw_ref[...]).astype(BF16)

    def proj(off, n):
        return jnp.dot(h, win_ref[:, off:off + n], preferred_element_type=F32)

    @pl.when(first_tile)
    def _():
        ubuf[0:8, :] = jnp.zeros((8, D_MODEL), F32)
        kd_s[0:ATTN_BLOCK, :] = jnp.zeros((ATTN_BLOCK, 2 * LANES), BF16)
        vt_s[0] = jnp.zeros((N_KV, LANES, ATTN_BLOCK), BF16)

    u = proj(OFF_CG, D_MODEL) * proj(OFF_XIN, D_MODEL)
    ubuf[8:tm + 8, :] = u
    cw = convw_ref[...]
    conv = ubuf[6:tm + 6, :] * cw[0:1, :] + ubuf[7:tm + 7, :] * cw[1:2, :] + u * cw[2:3, :]
    merged = jax.nn.sigmoid(proj(OFF_GC, D_MODEL)) * (proj(OFF_BG, D_MODEL) * conv)
    tail = ubuf[tm + 6:tm + 8, :]
    cout_ref[0] = tail
    ubuf[6:8, :] = tail

    q = proj(OFF_Q, D_MODEL)
    kv = proj(OFF_K, 2 * LANES)
    k = _pair_norm(kv[:, :LANES], knw_ref[...])
    v = kv[:, LANES:]
    qnw = qnw_ref[...]
    for p in range(N_HEADS // 2):
        sl = slice(p * LANES, (p + 1) * LANES)
        q_s[:, sl] = (_pair_norm(q[:, sl], qnw) * ATTN_SCALE).astype(BF16)
    k_sw = pltpu.roll(k, HEAD_DIM, axis=1)
    v_sw = pltpu.roll(v, HEAD_DIM, axis=1)
    kd_s[ATTN_BLOCK:tm + ATTN_BLOCK, 0:LANES] = jnp.where(lo, k, k_sw).astype(BF16)
    kd_s[ATTN_BLOCK:tm + ATTN_BLOCK, LANES:2 * LANES] = jnp.where(lo, k_sw, k).astype(BF16)
    v_dup = (jnp.where(lo, v, v_sw), jnp.where(lo, v_sw, v))
    for b in range(nblk):
        for g in range(N_KV):
            vt_s[b + 1, g] = v_dup[g][b * ATTN_BLOCK:(b + 1) * ATTN_BLOCK, :].T.astype(BF16)

    @pl.when(j == nj - 1)
    def _():
        kout_ref[0] = k[tm - ATTN_BLOCK:, :]
        vout_ref[0] = v[tm - ATTN_BLOCK:, :]

    prev_rows = lax.broadcasted_iota(I32, (2 * ATTN_BLOCK, 1), 0) < ATTN_BLOCK
    feat_lo = lax.broadcasted_iota(I32, (LANES, 1), 0) < HEAD_DIM

    def attn_block(blk):
        r0 = blk * ATTN_BLOCK
        qb = q_s[r0:r0 + ATTN_BLOCK, :]
        for g in range(N_KV):
            parts = []
            for t in range(GROUP):
                hd = g * GROUP + t
                slab = qb[:, (hd // 2) * LANES:(hd // 2 + 1) * LANES]
                keep = lo if hd % 2 == 0 else jnp.logical_not(lo)
                parts.append(jnp.where(keep, slab, jnp.zeros_like(slab)))
            lhs = jnp.concatenate(parts, axis=0)
            st = lax.dot_general(kd_s[r0:r0 + 2 * ATTN_BLOCK, g * LANES:(g + 1) * LANES], lhs,
                                 (((1,), (1,)), ((), ())), preferred_element_type=F32)
            st = st + bias_ref[g]
            if blk == 0:
                st = jnp.where(jnp.logical_and(prev_rows, first_tile), NEG_INF, st)
            st_s[...] = st
            sink = sink_ref[g:g + 1, :]
            rdens = []
            for t in range(GROUP):
                cols = slice(t * LANES, (t + 1) * LANES)
                s = st_s[:, cols]
                m = jnp.maximum(jnp.max(s, axis=0, keepdims=True), sink[:, cols])
                pr = jnp.exp(s - m)
                den = jnp.sum(pr, axis=0, keepdims=True) + jnp.exp(sink[:, cols] - m)
                pt_s[:, cols] = pr.astype(BF16)
                rdens.append(1.0 / den)
            vt = jnp.concatenate([vt_s[blk, g], vt_s[blk + 1, g]], axis=1)
            ot = jnp.dot(vt, pt_s[...], preferred_element_type=F32)
            for i in range(GROUP // 2):
                pair = g * (GROUP // 2) + i
                even = ot[:, (2 * i) * LANES:(2 * i + 1) * LANES] * rdens[2 * i]
                odd = ot[:, (2 * i + 1) * LANES:(2 * i + 2) * LANES] * rdens[2 * i + 1]
                ya_s[r0:r0 + ATTN_BLOCK, pair * LANES:(pair + 1) * LANES] = jnp.where(feat_lo, even, odd).T

    for blk in range(nblk):
        attn_block(blk)

    kd_s[0:ATTN_BLOCK, :] = kd_s[tm:tm + ATTN_BLOCK, :]
    vt_s[0] = vt_s[nblk]

    merged = merged + jax.nn.sigmoid(proj(OFF_GA, D_MODEL)) * ya_s[...]
    h2, idxs, gates = _epilogue(x, merged, wout_ref, fnw_ref, wr_ref, br_ref, x1_ref)
    _dispatch(h2, idxs, gates, xs_ref, lp_ref, stat_ref)


def _const_spec(shape):
    nd = len(shape)
    return pl.BlockSpec(shape, lambda *_: (0,) * nd, pipeline_mode=pl.Buffered(1))


def _mixer_prompt(x, sink_rows, anw, win_bf, convw, qnw, knw, bias_tab, wout_bf, fnw, wr, br, h2s, idxs, gates):
    batch, seq, _ = x.shape
    tm = MIXER_TILE
    lrows = _local_rows(tm)
    nj = seq // tm
    tokens = batch * seq
    n_tiles = batch * nj
    nb = h2s.shape[0]
    x2 = x.reshape(tokens, D_MODEL)
    tile = lambda i: jnp.minimum(i, n_tiles - 1)
    tok_spec = lambda width: pl.BlockSpec((tm, width), lambda i: (tile(i), 0))
    per_batch = lambda rows, width: pl.BlockSpec((1, rows, width), lambda i: (tile(i) // nj, 0, 0))
    in_specs = [
        _const_spec((N_KV, GROUP * ATTN_BLOCK)),
        tok_spec(D_MODEL),
        _const_spec((1, D_MODEL)),
        _const_spec((D_MODEL, IN_DIM)),
        _const_spec((3, D_MODEL)),
        _const_spec((1, LANES)),
        _const_spec((1, LANES)),
        _const_spec((N_KV, 2 * ATTN_BLOCK, GROUP * ATTN_BLOCK)),
        _const_spec((D_MODEL, D_MODEL)),
        _const_spec((1, D_MODEL)),
        _const_spec((D_MODEL, N_EXPERTS)),
        _const_spec((1, N_EXPERTS)),
        _const_spec((nb, D_MODEL)),
        _const_spec((nb, TOP_K)),
        _const_spec((nb, TOP_K)),
    ]
    out_shape = (
        jax.ShapeDtypeStruct((tokens, D_MODEL), F32),
        jax.ShapeDtypeStruct(((n_tiles + 1) * lrows, XS_WIDTH), BF16),
        jax.ShapeDtypeStruct((tokens, TOP_K), I32),
        jax.ShapeDtypeStruct((nb, TOP_K), I32),
        jax.ShapeDtypeStruct((n_tiles + 1, 8, LANES), F32),
        jax.ShapeDtypeStruct((batch, ATTN_BLOCK, LANES), F32),
        jax.ShapeDtypeStruct((batch, ATTN_BLOCK, LANES), F32),
        jax.ShapeDtypeStruct((batch, 2, D_MODEL), F32),
    )
    out_specs = (
        tok_spec(D_MODEL),
        pl.BlockSpec((lrows, XS_WIDTH), lambda i: (i, 0)),
        tok_spec(TOP_K),
        pl.BlockSpec((nb, TOP_K), lambda i: (0, 0)),
        pl.BlockSpec((1, 8, LANES), lambda i: (i, 0, 0)),
        per_batch(ATTN_BLOCK, LANES), per_batch(ATTN_BLOCK, LANES), per_batch(2, D_MODEL),
    )
    scratch = [
        pltpu.VMEM((tm + 8, D_MODEL), F32),
        pltpu.VMEM((tm, D_MODEL), BF16),
        pltpu.VMEM((tm + ATTN_BLOCK, 2 * LANES), BF16),
        pltpu.VMEM((tm // ATTN_BLOCK + 1, N_KV, LANES, ATTN_BLOCK), BF16),
        pltpu.VMEM((tm, D_MODEL), F32),
        pltpu.VMEM((2 * ATTN_BLOCK, GROUP * ATTN_BLOCK), F32),
        pltpu.VMEM((2 * ATTN_BLOCK, GROUP * ATTN_BLOCK), BF16),
    ]
    return pl.pallas_call(
        functools.partial(_mixer_prompt_kernel, nj),
        grid=(n_tiles + 1,),
        in_specs=in_specs,
        out_specs=out_specs,
        out_shape=out_shape,
        scratch_shapes=scratch,
        compiler_params=pltpu.CompilerParams(dimension_semantics=("arbitrary",), vmem_limit_bytes=VMEM_LIMIT),
        name="mixer_prompt",
    )(sink_rows, x2, anw, win_bf, convw, qnw, knw, bias_tab, wout_bf, fnw, wr, br, h2s, idxs, gates)


SAMPLE_CHUNK = 32


def _mixer_sample_kernel(x_ref, p0_ref, p1_ref, kp_ref, vp_ref, sink_ref, anw_ref, win_ref, convw_ref,
                         qnw_ref, knw_ref, bias_ref, wout_ref, fnw_ref, wr_ref, br_ref,
                         x1_ref, h2_ref, idx_ref, gate_ref, unew_ref, kc_ref, vc_ref,
                         qh_s, o_s, kn_s, vn_s, conv_s, ga_s):
    c = pl.program_id(0)
    nb = x_ref.shape[0]
    tc = kp_ref.shape[0]
    lo = _lo_half()

    @pl.when(c == 0)
    def _():
        h = _rms(x_ref[...], anw_ref[...]).astype(BF16)

        def proj(off, n):
            return jnp.dot(h, win_ref[:, off:off + n], preferred_element_type=F32)

        u = proj(OFF_CG, D_MODEL) * proj(OFF_XIN, D_MODEL)
        unew_ref[...] = u
        cw = convw_ref[...]
        conv = p0_ref[...] * cw[0:1, :] + p1_ref[...] * cw[1:2, :] + u * cw[2:3, :]
        conv_s[...] = jax.nn.sigmoid(proj(OFF_GC, D_MODEL)) * (proj(OFF_BG, D_MODEL) * conv)
        ga_s[...] = jax.nn.sigmoid(proj(OFF_GA, D_MODEL))

        q = proj(OFF_Q, D_MODEL)
        kv = proj(OFF_K, 2 * LANES)
        kn_s[...] = _pair_norm(kv[:, :LANES], knw_ref[...])
        vn_s[...] = kv[:, LANES:]

        qnw = qnw_ref[...]
        for hd in range(N_HEADS):
            pair, half, grp = hd // 2, hd % 2, hd // GROUP
            slab = _pair_norm(q[:, pair * LANES:(pair + 1) * LANES], qnw) * ATTN_SCALE
            slab = jnp.where(lo if half == 0 else jnp.logical_not(lo), slab, 0.0)
            if half != grp:
                slab = pltpu.roll(slab, HEAD_DIM, axis=1)
            qh_s[hd * nb:(hd + 1) * nb, :] = slab

    sink = sink_ref[...]
    bias = bias_ref[...]
    rows = lax.broadcasted_iota(I32, (WINDOW, 1), 0)
    row0 = rows == 0
    row_last = rows == WINDOW - 1

    def token(t, carry):
        b = c * tc + t
        qb = qh_s[pl.ds(b, N_HEADS, stride=nb), :]
        k_new = kn_s[pl.ds(b, 1), :]
        v_new = vn_s[pl.ds(b, 1), :]
        k_old = kp_ref[t]
        v_old = vp_ref[t]
        kc_ref[t] = jnp.where(row_last, k_new, pltpu.roll(k_old, WINDOW - 1, axis=0))
        vc_ref[t] = jnp.where(row_last, v_new, pltpu.roll(v_old, WINDOW - 1, axis=0))
        kb = jnp.where(row0, k_new, k_old)
        vb = jnp.where(row0, v_new, v_old)
        s = lax.dot_general(qb.astype(BF16), kb.astype(BF16), (((1,), (1,)), ((), ())),
                            preferred_element_type=F32) + bias
        m = jnp.maximum(jnp.max(s, axis=-1, keepdims=True), sink)
        pr = jnp.exp(s - m)
        den = jnp.sum(pr, axis=-1, keepdims=True) + jnp.exp(sink - m)
        o = jnp.dot(pr.astype(BF16), vb.astype(BF16), preferred_element_type=F32) / den
        o_s[pl.ds(b, N_HEADS, stride=nb), :] = o
        return carry

    lax.fori_loop(0, tc, token, 0, unroll=4)

    @pl.when(c == pl.num_programs(0) - 1)
    def _():
        cols = []
        for pair in range(N_HEADS // 2):
            halves = []
            for half in range(2):
                hd = 2 * pair + half
                slab = o_s[hd * nb:(hd + 1) * nb, :]
                if half != hd // GROUP:
                    slab = pltpu.roll(slab, HEAD_DIM, axis=1)
                halves.append(slab)
            cols.append(jnp.where(lo, halves[0], halves[1]))
        merged = conv_s[...] + ga_s[...] * jnp.concatenate(cols, axis=1)
        h2, idxs, gates = _epilogue(x_ref[...], merged, wout_ref, fnw_ref, wr_ref, br_ref, x1_ref)
        h2_ref[...] = h2
        lane4 = lax.broadcasted_iota(I32, (nb, TOP_K), 1)

        def pack(cs):
            return jnp.where(lane4 == 0, cs[0], jnp.where(lane4 == 1, cs[1], jnp.where(lane4 == 2, cs[2], cs[3])))

        idx_ref[...] = pack(idxs)
        gate_ref[...] = pack(gates)


def _mixer_sample(x, p0, p1, k_past, v_past, sink_col, anw, win_bf, convw, qnw, knw, bias_s, wout_bf, fnw, wr, br):
    nb = x.shape[0]
    tc = SAMPLE_CHUNK
    consts = (x, p0, p1)
    params = (sink_col, anw, win_bf, convw, qnw, knw, bias_s, wout_bf, fnw, wr, br)
    cache_spec = pl.BlockSpec((tc, WINDOW, LANES), lambda c: (c, 0, 0))
    full = lambda shape: pl.BlockSpec(shape, lambda c: (0,) * len(shape))
    out_shape = (
        jax.ShapeDtypeStruct((nb, D_MODEL), F32),
        jax.ShapeDtypeStruct((nb, D_MODEL), BF16),
        jax.ShapeDtypeStruct((nb, TOP_K), F32),
        jax.ShapeDtypeStruct((nb, TOP_K), F32),
        jax.ShapeDtypeStruct((nb, D_MODEL), F32),
        jax.ShapeDtypeStruct((nb, WINDOW, LANES), F32),
        jax.ShapeDtypeStruct((nb, WINDOW, LANES), F32),
    )
    out_specs = tuple(full(s.shape) for s in out_shape[:5]) + (cache_spec, cache_spec)
    scratch = [
        pltpu.VMEM((N_HEADS * nb, LANES), F32),
        pltpu.VMEM((N_HEADS * nb, LANES), F32),
        pltpu.VMEM((nb, LANES), F32),
        pltpu.VMEM((nb, LANES), F32),
        pltpu.VMEM((nb, D_MODEL), F32),
        pltpu.VMEM((nb, D_MODEL), F32),
    ]
    return pl.pallas_call(
        _mixer_sample_kernel,
        grid=(nb // tc,),
        in_specs=[_const_spec(a.shape) for a in consts] + [cache_spec, cache_spec]
        + [_const_spec(a.shape) for a in params],
        out_specs=out_specs,
        out_shape=out_shape,
        scratch_shapes=scratch,
        compiler_params=pltpu.CompilerParams(dimension_semantics=("arbitrary",), vmem_limit_bytes=VMEM_LIMIT),
        name="mixer_sample",
    )(*consts, k_past, v_past, *params)


def _expert_kernel(be_ref, nact_ref, src_ref, dst_ref, xs_hbm, wup_ref, bup_ref, wdn_ref, bdn_ref, ys_hbm,
                   wup_s, wdn_s, perm_s, xbuf, ybuf, in_sem, out_sem):
    i = pl.program_id(0)
    n_real = nact_ref[0]
    n_active = nact_ref[1]
    e = be_ref[i]
    e_prev = be_ref[jnp.maximum(i - 1, 0)]
    slot = lax.rem(i, 2)

    def gather_copy(blk, slt, c):
        row = pl.multiple_of(src_ref[blk * CHUNKS_PER_BLOCK + c] * CHUNK, CHUNK)
        return pltpu.make_async_copy(xs_hbm.at[pl.ds(row, CHUNK), :],
                                     xbuf.at[slt, pl.ds(c * CHUNK, CHUNK), :], in_sem.at[slt])

    def scatter_chunks(blk, slt, wait):
        base = blk * CHUNKS_PER_BLOCK

        def one(c, dst):
            row = pl.multiple_of(dst * CHUNK, CHUNK)
            cp = pltpu.make_async_copy(ybuf.at[slt, pl.ds(c * CHUNK, CHUNK), :],
                                       ys_hbm.at[pl.ds(row, CHUNK), :], out_sem.at[slt])
            if wait:
                cp.wait()
            else:
                cp.start()

        full = dst_ref[base + CHUNKS_PER_BLOCK - 1] >= 0

        @pl.when(full)
        def _():
            for c in range(CHUNKS_PER_BLOCK):
                one(c, dst_ref[base + c])

        @pl.when(jnp.logical_not(full))
        def _():
            for c in range(CHUNKS_PER_BLOCK - 1):
                dst = dst_ref[base + c]

                @pl.when(dst >= 0)
                def _():
                    one(c, dst)

    @pl.when(jnp.logical_and(i == 0, n_real > 0))
    def _():
        for c in range(CHUNKS_PER_BLOCK):
            gather_copy(0, 0, c).start()

    @pl.when(jnp.logical_and(i < n_real, jnp.logical_or(i == 0, e != e_prev)))
    def _():
        wup_s[...] = wup_ref[0].astype(BF16)
        half = LANES // 2
        for cs in range(D_MODEL // LANES):
            cols = slice(cs * LANES, (cs + 1) * LANES)
            for c in range(D_MODEL // LANES):
                for par in range(2):
                    s0 = c * LANES + par * half
                    perm_s[cs, pl.ds(c * LANES + par, half, stride=2), :] = wdn_ref[0, s0:s0 + half, cols]
            wdn_s[:, cols] = perm_s[cs].astype(BF16)

    @pl.when(i + 1 < n_real)
    def _():
        for c in range(CHUNKS_PER_BLOCK):
            gather_copy(i + 1, 1 - slot, c).start()

    @pl.when(jnp.logical_and(i >= 2, i < n_active))
    def _():
        scatter_chunks(i - 2, slot, wait=True)

    @pl.when(jnp.logical_and(i >= n_real, i < n_active))
    def _():
        ybuf[slot] = jnp.zeros((EXPERT_BLOCK, D_MODEL), BF16)
        scatter_chunks(i, slot, wait=False)

    @pl.when(i < n_real)
    def _():
        for c in range(CHUNKS_PER_BLOCK):
            gather_copy(i, slot, c).wait()

        xb = xbuf[slot]
        meta = xb[:, D_MODEL:].astype(F32)
        gate = meta[:, 0:1] + meta[:, 1:2]
        u = jnp.dot(xb[:, :D_MODEL], wup_s[...], preferred_element_type=F32) + bup_ref[0]
        even = (_lane_iota() & 1) == 0
        cols = []
        for c in range(D_MODEL // LANES):
            c0 = u[:, (2 * c) * LANES:(2 * c + 1) * LANES]
            c1 = u[:, (2 * c + 1) * LANES:(2 * c + 2) * LANES]
            glu = jnp.where(even, c0, pltpu.roll(c1, 1, axis=1))
            lin = jnp.where(even, pltpu.roll(c0, LANES - 1, axis=1), c1)
            glu = jnp.minimum(glu, SWIGLU_LIMIT)
            lin = jnp.clip(lin, -SWIGLU_LIMIT, SWIGLU_LIMIT)
            cols.append(glu * jax.nn.sigmoid(SWIGLU_ALPHA * glu) * (lin + 1.0))
        a = jnp.concatenate(cols, axis=1).astype(BF16)
        y = jnp.dot(a, wdn_s[...], preferred_element_type=F32) + bdn_ref[0]
        ybuf[slot] = (y * gate).astype(BF16)
        scatter_chunks(i, slot, wait=False)

    @pl.when(i == n_active - 1)
    def _():
        @pl.when(i >= 1)
        def _():
            scatter_chunks(i - 1, 1 - slot, wait=True)

        scatter_chunks(i, slot, wait=True)


def _experts(block_expert, n_active, chunk_src, chunk_dst, xs_big, ys_rows, w_up, b_up, w_down, b_down):
    n_blocks = block_expert.shape[0]
    wspec = lambda shape: pl.BlockSpec(shape, lambda i, be, na, cs, cd: (jnp.minimum(be[i], N_EXPERTS - 1), 0, 0))
    grid_spec = pltpu.PrefetchScalarGridSpec(
        num_scalar_prefetch=4,
        grid=(n_blocks,),
        in_specs=[
            pl.BlockSpec(memory_space=pl.ANY),
            wspec((1, D_MODEL, 2 * D_MODEL)),
            wspec((1, 1, 2 * D_MODEL)),
            wspec((1, D_MODEL, D_MODEL)),
            wspec((1, 1, D_MODEL)),
        ],
        out_specs=pl.BlockSpec(memory_space=pl.ANY),
        scratch_shapes=[
            pltpu.VMEM((D_MODEL, 2 * D_MODEL), BF16),
            pltpu.VMEM((D_MODEL, D_MODEL), BF16),
            pltpu.VMEM((D_MODEL // LANES, D_MODEL, LANES), F32),
            pltpu.VMEM((2, EXPERT_BLOCK, XS_WIDTH), BF16),
            pltpu.VMEM((2, EXPERT_BLOCK, D_MODEL), BF16),
            pltpu.SemaphoreType.DMA((2,)),
            pltpu.SemaphoreType.DMA((2,)),
        ],
    )
    return pl.pallas_call(
        _expert_kernel,
        grid_spec=grid_spec,
        out_shape=jax.ShapeDtypeStruct((ys_rows, D_MODEL), BF16),
        compiler_params=pltpu.CompilerParams(dimension_semantics=("arbitrary",), vmem_limit_bytes=VMEM_LIMIT),
        name="experts",
    )(block_expert, n_active, chunk_src, chunk_dst, xs_big, w_up, b_up.reshape(N_EXPERTS, 1, 2 * D_MODEL),
      w_down, b_down.reshape(N_EXPERTS, 1, D_MODEL))


def _combine_kernel(lp_ref, ys_ref, x1_ref, out_ref):
    tm = x1_ref.shape[0]
    lrows = ys_ref.shape[0]
    lp = lp_ref[...].astype(F32)
    rid = lax.broadcasted_iota(I32, (tm, lrows), 1).astype(F32)
    sel = rid == lp[:, 0:1]
    for k in range(1, TOP_K):
        sel = jnp.logical_or(sel, rid == lp[:, k:k + 1])
    out_ref[...] = x1_ref[...] + jnp.dot(sel.astype(BF16), ys_ref[...], preferred_element_type=F32)


def _combine(lp, ys_big, x1, tm, lrows, first_block):
    tokens = x1.shape[0]
    return pl.pallas_call(
        _combine_kernel,
        grid=(tokens // tm,),
        in_specs=[
            pl.BlockSpec((tm, TOP_K), lambda j: (j, 0)),
            pl.BlockSpec((lrows, D_MODEL), lambda j: (first_block + j, 0)),
            pl.BlockSpec((tm, D_MODEL), lambda j: (j, 0)),
        ],
        out_specs=pl.BlockSpec((tm, D_MODEL), lambda j: (j, 0)),
        out_shape=jax.ShapeDtypeStruct((tokens, D_MODEL), F32),
        compiler_params=pltpu.CompilerParams(dimension_semantics=("arbitrary",), vmem_limit_bytes=VMEM_LIMIT),
        name="combine",
    )(lp, ys_big, x1)


def _bias_tables(rel_bias):
    qi = np.arange(ATTN_BLOCK)[:, None]
    kj = np.arange(2 * ATTN_BLOCK)[None, :]
    rel = qi + ATTN_BLOCK - kj
    valid = (rel >= 0) & (rel < WINDOW)
    bucket = np.where(valid, _t5_bucket_np(rel), -1)
    slot_rel = np.where(np.arange(WINDOW) == 0, 0, WINDOW - np.arange(WINDOW))
    slot_bucket = _t5_bucket_np(slot_rel)
    tab = jnp.full((N_HEADS, ATTN_BLOCK, 2 * ATTN_BLOCK), NEG_INF, F32)
    sample_tab = jnp.zeros((N_HEADS, WINDOW), F32)
    for b in range(N_BUCKETS):
        tab = jnp.where((bucket == b)[None], rel_bias[b][:, None, None], tab)
        sample_tab = jnp.where((slot_bucket == b)[None], rel_bias[b][:, None], sample_tab)
    prompt_tab = tab.reshape(N_KV, GROUP, ATTN_BLOCK, 2 * ATTN_BLOCK).transpose(0, 3, 1, 2)
    return prompt_tab.reshape(N_KV, 2 * ATTN_BLOCK, GROUP * ATTN_BLOCK), sample_tab


def _chunk_tables(seg, starts, tile_base, tile_rows, n_blocks):
    n_tiles = seg.shape[0]
    n_seg = (N_EXPERTS + 1) * n_tiles
    used = jnp.sum(seg, axis=1)
    seg_e = jnp.concatenate([seg.T, (tile_rows - used)[None, :]], axis=0)
    src0 = jnp.concatenate([tile_base[None, :] + starts.T, (tile_base + used)[None, :]], axis=0).reshape(-1)
    total = jnp.sum(seg_e, axis=1)
    region = (total + EXPERT_BLOCK - 1) // EXPERT_BLOCK * EXPERT_BLOCK
    pad_end = jnp.cumsum(region)
    pad_start = pad_end - region
    g_start = (pad_start[:, None] + jnp.cumsum(seg_e, axis=1) - seg_e).reshape(-1)
    g_end = g_start + seg_e.reshape(-1)
    rows = jnp.arange(n_blocks * CHUNKS_PER_BLOCK, dtype=I32) * CHUNK
    passed = g_end[None, :] <= rows[:, None]

    def at_segment(table, sentinel):
        ext = jnp.concatenate([table, jnp.array([sentinel], I32)])
        return ext[0] + jnp.sum(jnp.where(passed, (ext[1:] - ext[:-1])[None, :], 0), axis=1)

    seg_start = at_segment(g_start, 1 << 30)
    valid = rows >= seg_start
    src_row = at_segment(src0, 0) + rows - seg_start
    chunk_src = jnp.where(valid, src_row // CHUNK, 0).astype(I32)
    chunk_dst = jnp.where(valid, src_row // CHUNK, -1).astype(I32)
    blk_rows = jnp.arange(n_blocks, dtype=I32) * EXPERT_BLOCK
    block_region = jnp.minimum(jnp.sum((pad_end[None, :] <= blk_rows[:, None]).astype(I32), axis=1),
                               N_EXPERTS).astype(I32)
    counts = jnp.stack([pad_end[N_EXPERTS - 1], pad_end[N_EXPERTS]]).astype(I32) // EXPERT_BLOCK
    return block_region, counts, chunk_src, chunk_dst


def kernel(x_prompt, x_sample, state_conv, cache_k_win, cache_v_win, rel_bias, attn_norm_w, w_in, conv_w,
           q_norm_w, k_norm_w, sinks, w_out, ffn_norm_w, w_router, b_router, w_up, b_up, w_down, b_down):
    batch, seq, _ = x_prompt.shape
    nb = x_sample.shape[0]
    anw = attn_norm_w[0].reshape(1, D_MODEL)
    fnw = ffn_norm_w[0].reshape(1, D_MODEL)
    win_bf = w_in[0].astype(BF16)
    wout_bf = w_out[0].astype(BF16)
    qnw = jnp.tile(q_norm_w[0], 2).reshape(1, LANES)
    knw = jnp.tile(k_norm_w[0], 2).reshape(1, LANES)
    br = b_router[0].reshape(1, N_EXPERTS)
    prompt_tab, sample_tab = _bias_tables(rel_bias)

    k_past = cache_k_win[0].reshape(nb, WINDOW, LANES)
    v_past = cache_v_win[0].reshape(nb, WINDOW, LANES)
    x1s, h2s, idxs, gates, unew, new_k_sample, new_v_sample = _mixer_sample(
        x_sample.reshape(nb, D_MODEL), state_conv[0, :, 0, :], state_conv[0, :, 1, :], k_past, v_past,
        sinks[0].reshape(N_HEADS, 1), anw, win_bf, conv_w[0], qnw, knw, sample_tab, wout_bf, fnw, w_router[0], br)

    sink_rows = jnp.repeat(sinks[0].reshape(N_KV, GROUP), ATTN_BLOCK, axis=1)
    x1p, xs_big, lpp, lps, stat, kp, vp, cp = _mixer_prompt(
        x_prompt, sink_rows, anw, win_bf, conv_w[0], qnw, knw, prompt_tab, wout_bf, fnw, w_router[0], br,
        h2s, idxs, gates)

    n_tiles = stat.shape[0]
    lrows = _local_rows(MIXER_TILE)
    xs_rows = n_tiles * lrows
    seg = stat[:, 0, :N_EXPERTS].astype(I32)
    starts = stat[:, 1, :N_EXPERTS].astype(I32)
    tile_base = jnp.arange(n_tiles, dtype=I32) * lrows
    tile_rows = jnp.full((n_tiles,), lrows, I32)
    n_blocks = -(-(xs_rows + (N_EXPERTS + 1) * (EXPERT_BLOCK - 1)) // EXPERT_BLOCK)
    block_expert, n_active, chunk_src, chunk_dst = _chunk_tables(seg, starts, tile_base, tile_rows, n_blocks)

    ys_big = _experts(block_expert, n_active, chunk_src, chunk_dst, xs_big, xs_rows, w_up[0], b_up[0], w_down[0],
                      b_down[0])

    y_prompt = _combine(lpp, ys_big, x1p, MIXER_TILE, lrows, 0).reshape(batch, seq, D_MODEL)
    y_sample = _combine(lps, ys_big, x1s, nb, lrows, n_tiles - 1).reshape(nb, 1, D_MODEL)

    new_conv_sample = jnp.stack([state_conv[0, :, 1, :], unew], axis=1)
    return (
        y_prompt,
        y_sample,
        cp[None],
        kp.reshape(1, batch, ATTN_BLOCK, N_KV, HEAD_DIM),
        vp.reshape(1, batch, ATTN_BLOCK, N_KV, HEAD_DIM),
        new_conv_sample[None],
        new_k_sample.reshape(1, nb, WINDOW, N_KV, HEAD_DIM),
        new_v_sample.reshape(1, nb, WINDOW, N_KV, HEAD_DIM),
    )
```

```python
import functools
import math

import numpy as np
import jax
import jax.numpy as jnp
from jax import lax
from jax.experimental import pallas as pl
from jax.experimental.pallas import tpu as pltpu

F32 = jnp.float32
BF16 = jnp.bfloat16
I32 = jnp.int32

D_MODEL = 1024
HEAD_DIM = 64
N_HEADS = 16
N_KV = 2
GROUP = N_HEADS // N_KV
WINDOW = 128
ATTN_BLOCK = 128
N_BUCKETS = 32
MAX_DISTANCE = 128
NEG_INF = -1e30
N_EXPERTS = 32
TOP_K = 4
SWIGLU_ALPHA = 1.702
SWIGLU_LIMIT = 7.0
EPS = 1e-5
ATTN_SCALE = HEAD_DIM ** -0.5

OFF_XIN, OFF_BG, OFF_CG, OFF_Q = 0, 1024, 2048, 3072
OFF_K, OFF_V, OFF_GC, OFF_GA = 4096, 4224, 4352, 5376
IN_DIM = 6400

LANES = 128
MIXER_TILE = 512
EXPERT_BLOCK = 512
CHUNK = 16
CHUNKS_PER_BLOCK = EXPERT_BLOCK // CHUNK
XS_WIDTH = D_MODEL + LANES
VMEM_LIMIT = 60 * 1024 * 1024


def _local_rows(tm):
    need = tm * TOP_K + N_EXPERTS * (CHUNK - 1)
    return -(-need // 512) * 512


def _t5_bucket_np(rel):
    n = np.maximum(rel, 0)
    max_exact = N_BUCKETS // 2
    nf = np.maximum(n, 1).astype(np.float64)
    large = max_exact + (np.log(nf / max_exact) / math.log(MAX_DISTANCE / max_exact)
                         * (N_BUCKETS - max_exact)).astype(np.int32)
    large = np.minimum(large, N_BUCKETS - 1)
    return np.where(n < max_exact, n, large).astype(np.int32)


def _rms(x, w):
    return x * lax.rsqrt(jnp.mean(x * x, axis=-1, keepdims=True) + EPS) * w


def _lane_iota(rows=1):
    return lax.broadcasted_iota(I32, (rows, LANES), 1)


def _lo_half():
    return _lane_iota() < HEAD_DIM


def _pair_norm(t, w128):
    lo = _lo_half()
    sq = t * t
    s_lo = jnp.sum(jnp.where(lo, sq, 0.0), axis=-1, keepdims=True)
    s_hi = jnp.sum(jnp.where(lo, 0.0, sq), axis=-1, keepdims=True)
    r = jnp.where(lo, lax.rsqrt(s_lo * (1.0 / HEAD_DIM) + EPS), lax.rsqrt(s_hi * (1.0 / HEAD_DIM) + EPS))
    return t * r * w128


def _top4_gates(logits):
    rows = logits.shape[0]
    lane = lax.broadcasted_iota(I32, (rows, N_EXPERTS), 1).astype(F32)
    vals, idxs = [], []
    l = logits
    for _ in range(TOP_K):
        m = jnp.max(l, axis=-1, keepdims=True)
        idx = jnp.min(jnp.where(l == m, lane, float(N_EXPERTS)), axis=-1, keepdims=True)
        vals.append(m)
        idxs.append(idx)
        l = jnp.where(lane == idx, -jnp.inf, l)
    es = [jnp.exp(v - vals[0]) for v in vals]
    den = es[0] + es[1] + es[2] + es[3]
    return idxs, [e / den for e in es]


def _cols_to_lanes(cols, rows, fill=0.0):
    lane = _lane_iota(rows)
    out = jnp.full((rows, LANES), fill, F32)
    for k, c in enumerate(cols):
        out = jnp.where(lane == k, c, out)
    return out


def _dispatch(h2, idxs, gates, xs_ref, lp_ref, stat_ref):
    tm = h2.shape[0]
    lrows = xs_ref.shape[0]
    lane = _lane_iota(tm)
    lane_f = lane.astype(F32)
    member = jnp.zeros((tm, LANES), F32)
    for k in range(TOP_K):
        member = member + jnp.where(lane_f == idxs[k], 1.0, 0.0)
    cnt = jnp.sum(member, axis=0, keepdims=True)
    seg = jnp.floor((cnt + (CHUNK - 1)) * (1.0 / CHUNK)) * CHUNK
    lane1 = _lane_iota()
    incl = seg
    for s in (1, 2, 4, 8, 16):
        incl = incl + jnp.where(lane1 >= s, pltpu.roll(incl, s, axis=1), 0.0)
    starts = incl - seg
    row8 = lax.broadcasted_iota(I32, (8, LANES), 0)
    stat_ref[0] = jnp.where(row8 == 0, seg, jnp.where(row8 == 1, starts, 0.0))

    tri = (lax.broadcasted_iota(I32, (tm, tm), 1) < lax.broadcasted_iota(I32, (tm, tm), 0)).astype(BF16)
    rank = jnp.dot(tri, member.astype(BF16), preferred_element_type=F32)
    pos = starts + rank
    lps = [jnp.sum(jnp.where(lane_f == idxs[k], pos, 0.0), axis=-1, keepdims=True) for k in range(TOP_K)]
    lane4 = lax.broadcasted_iota(I32, (tm, TOP_K), 1)
    lp_ref[...] = jnp.where(lane4 == 0, lps[0], jnp.where(lane4 == 1, lps[1],
                                                          jnp.where(lane4 == 2, lps[2], lps[3]))).astype(I32)
    lp_t = _cols_to_lanes(lps, tm, fill=-1.0).T

    g_hi = [g.astype(BF16).astype(F32) for g in gates]
    g_lo = [g - h for g, h in zip(gates, g_hi)]
    meta_in = _cols_to_lanes(list(idxs) + g_hi + g_lo, tm)
    rhs = jnp.concatenate([h2.astype(BF16), meta_in.astype(BF16)], axis=1)

    sub = 512
    starts32 = jnp.where(lane1 < N_EXPERTS, starts, 1e9)
    lp_group = jnp.floor(lp_t * (1.0 / LANES))
    lp_off = lp_t - lp_group * LANES
    row_off = lax.broadcasted_iota(I32, (LANES, tm), 0).astype(F32).astype(BF16)
    one_bf = jnp.ones((LANES, tm), BF16)
    zero_bf = jnp.zeros((LANES, tm), BF16)

    for c in range(lrows // sub):
        r0 = c * sub
        rid = (lax.broadcasted_iota(I32, (sub, 1), 0) + r0).astype(F32)
        parts = []
        for s in range(sub // LANES):
            group = float(r0 // LANES + s)
            hit = None
            for k in range(TOP_K):
                off_k = jnp.where(lp_group[k:k + 1, :] == group, lp_off[k:k + 1, :], -1.0).astype(BF16)
                hit_k = row_off == off_k
                hit = hit_k if hit is None else jnp.logical_or(hit, hit_k)
            parts.append(jnp.where(hit, one_bf, zero_bf))
        full = jnp.dot(jnp.concatenate(parts, axis=0), rhs, preferred_element_type=F32)
        got = full[:, D_MODEL:]
        e_row = jnp.sum(jnp.where(rid >= starts32, 1.0, 0.0), axis=-1, keepdims=True) - 1.0
        g_sum = pltpu.roll(got, LANES - TOP_K, axis=1) + pltpu.roll(got, LANES - 2 * TOP_K, axis=1)
        lane_s = _lane_iota(sub)
        pick = jnp.logical_and(lane_s < TOP_K, got == e_row)
        gate_row = jnp.sum(jnp.where(pick, g_sum, 0.0), axis=-1, keepdims=True)
        gr_hi = gate_row.astype(BF16).astype(F32)
        meta = jnp.where(lane_s == 0, gr_hi, jnp.where(lane_s == 1, gate_row - gr_hi, 0.0))
        xs_ref[r0:r0 + sub, 0:D_MODEL] = full[:, :D_MODEL].astype(BF16)
        xs_ref[r0:r0 + sub, D_MODEL:XS_WIDTH] = meta.astype(BF16)


def _epilogue(x, merged, wout_ref, fnw_ref, wr_ref, br_ref, x1_ref):
    x1 = x + jnp.dot(merged.astype(BF16), wout_ref[...], preferred_element_type=F32)
    x1_ref[...] = x1
    h2 = _rms(x1, fnw_ref[...]).astype(BF16)
    logits = jnp.dot(h2, wr_ref[...].astype(BF16), preferred_element_type=F32) + br_ref[...]
    idxs, gates = _top4_gates(logits)
    return h2, idxs, gates


def _mixer_prompt_kernel(nj, sink_ref, x_ref, anw_ref, win_ref, convw_ref, qnw_ref, knw_ref, bias_ref, wout_ref,
                         fnw_ref, wr_ref, br_ref, h2s_ref, idxs_ref, gates_ref,
                         x1_ref, xs_ref, lp_ref, lps_ref, stat_ref, kout_ref, vout_ref, cout_ref,
                         ubuf, q_s, kd_s, vt_s, ya_s, st_s, pt_s):
    i = pl.program_id(0)
    n_tiles = pl.num_programs(0) - 1

    @pl.when(i < n_tiles)
    def _():
        _prompt_tile(lax.rem(i, nj), nj, sink_ref, x_ref, anw_ref, win_ref, convw_ref, qnw_ref, knw_ref, bias_ref,
                     wout_ref, fnw_ref, wr_ref, br_ref, x1_ref, xs_ref, lp_ref, stat_ref, kout_ref, vout_ref,
                     cout_ref, ubuf, q_s, kd_s, vt_s, ya_s, st_s, pt_s)

    @pl.when(i == n_tiles)
    def _():
        idx4 = idxs_ref[...]
        gate4 = gates_ref[...]
        _dispatch(h2s_ref[...], [idx4[:, k:k + 1] for k in range(TOP_K)], [gate4[:, k:k + 1] for k in range(TOP_K)],
                  xs_ref, lps_ref, stat_ref)


def _prompt_tile(j, nj, sink_ref, x_ref, anw_ref, win_ref, convw_ref, qnw_ref, knw_ref, bias_ref, wout_ref,
                 fnw_ref, wr_ref, br_ref, x1_ref, xs_ref, lp_ref, stat_ref, kout_ref, vout_ref, cout_ref,
                 ubuf, q_s, kd_s, vt_s, ya_s, st_s, pt_s):
    tm = x_ref.shape[0]
    nblk = tm // ATTN_BLOCK
    first_tile = j == 0
    lo = _lo_half()

    x = x_ref[...]
    h = _rms(x, anw_ref[...]).astype(BF16)

    def proj(off, n):
        return jnp.dot(h, win_ref[:, off:off + n], preferred_element_type=F32)

    @pl.when(first_tile)
    def _():
        ubuf[0:8, :] = jnp.zeros((8, D_MODEL), F32)
        kd_s[0:ATTN_BLOCK, :] = jnp.zeros((ATTN_BLOCK, 2 * LANES), BF16)
        vt_s[0] = jnp.zeros((N_KV, LANES, ATTN_BLOCK), BF16)

    u = proj(OFF_CG, D_MODEL) * proj(OFF_XIN, D_MODEL)
    ubuf[8:tm + 8, :] = u
    cw = convw_ref[...]
    conv = ubuf[6:tm + 6, :] * cw[0:1, :] + ubuf[7:tm + 7, :] * cw[1:2, :] + u * cw[2:3, :]
    merged = jax.nn.sigmoid(proj(OFF_GC, D_MODEL)) * (proj(OFF_BG, D_MODEL) * conv)
    tail = ubuf[tm + 6:tm + 8, :]
    cout_ref[0] = tail
    ubuf[6:8, :] = tail

    q = proj(OFF_Q, D_MODEL)
    kv = proj(OFF_K, 2 * LANES)
    k = _pair_norm(kv[:, :LANES], knw_ref[...])
    v = kv[:, LANES:]
    qnw = qnw_ref[...]
    for p in range(N_HEADS // 2):
        sl = slice(p * LANES, (p + 1) * LANES)
        q_s[:, sl] = (_pair_norm(q[:, sl], qnw) * ATTN_SCALE).astype(BF16)
    k_sw = pltpu.roll(k, HEAD_DIM, axis=1)
    v_sw = pltpu.roll(v, HEAD_DIM, axis=1)
    kd_s[ATTN_BLOCK:tm + ATTN_BLOCK, 0:LANES] = jnp.where(lo, k, k_sw).astype(BF16)
    kd_s[ATTN_BLOCK:tm + ATTN_BLOCK, LANES:2 * LANES] = jnp.where(lo, k_sw, k).astype(BF16)
    v_dup = (jnp.where(lo, v, v_sw), jnp.where(lo, v_sw, v))
    for b in range(nblk):
        for g in range(N_KV):
            vt_s[b + 1, g] = v_dup[g][b * ATTN_BLOCK:(b + 1) * ATTN_BLOCK, :].T.astype(BF16)

    @pl.when(j == nj - 1)
    def _():
        kout_ref[0] = k[tm - ATTN_BLOCK:, :]
        vout_ref[0] = v[tm - ATTN_BLOCK:, :]

    prev_rows = lax.broadcasted_iota(I32, (2 * ATTN_BLOCK, 1), 0) < ATTN_BLOCK
    feat_lo = lax.broadcasted_iota(I32, (LANES, 1), 0) < HEAD_DIM

    def attn_block(blk):
        r0 = blk * ATTN_BLOCK
        qb = q_s[r0:r0 + ATTN_BLOCK, :]
        for g in range(N_KV):
            parts = []
            for t in range(GROUP):
                hd = g * GROUP + t
                slab = qb[:, (hd // 2) * LANES:(hd // 2 + 1) * LANES]
                keep = lo if hd % 2 == 0 else jnp.logical_not(lo)
                parts.append(jnp.where(keep, slab, jnp.zeros_like(slab)))
            lhs = jnp.concatenate(parts, axis=0)
            st = lax.dot_general(kd_s[r0:r0 + 2 * ATTN_BLOCK, g * LANES:(g + 1) * LANES], lhs,
                                 (((1,), (1,)), ((), ())), preferred_element_type=F32)
            st = st + bias_ref[g]
            if blk == 0:
                st = jnp.where(jnp.logical_and(prev_rows, first_tile), NEG_INF, st)
            st_s[...] = st
            sink = sink_ref[g:g + 1, :]
            rdens = []
            for t in range(GROUP):
                cols = slice(t * LANES, (t + 1) * LANES)
                s = st_s[:, cols]
                m = jnp.maximum(jnp.max(s, axis=0, keepdims=True), sink[:, cols])
                pr = jnp.exp(s - m)
                den = jnp.sum(pr, axis=0, keepdims=True) + jnp.exp(sink[:, cols] - m)
                pt_s[:, cols] = pr.astype(BF16)
                rdens.append(1.0 / den)
            vt = jnp.concatenate([vt_s[blk, g], vt_s[blk + 1, g]], axis=1)
            ot = jnp.dot(vt, pt_s[...], preferred_element_type=F32)
            for i in range(GROUP // 2):
                pair = g * (GROUP // 2) + i
                even = ot[:, (2 * i) * LANES:(2 * i + 1) * LANES] * rdens[2 * i]
                odd = ot[:, (2 * i + 1) * LANES:(2 * i + 2) * LANES] * rdens[2 * i + 1]
                ya_s[r0:r0 + ATTN_BLOCK, pair * LANES:(pair + 1) * LANES] = jnp.where(feat_lo, even, odd).T

    for blk in range(nblk):
        attn_block(blk)

    kd_s[0:ATTN_BLOCK, :] = kd_s[tm:tm + ATTN_BLOCK, :]
    vt_s[0] = vt_s[nblk]

    merged = merged + jax.nn.sigmoid(proj(OFF_GA, D_MODEL)) * ya_s[...]
    h2, idxs, gates = _epilogue(x, merged, wout_ref, fnw_ref, wr_ref, br_ref, x1_ref)
    _dispatch(h2, idxs, gates, xs_ref, lp_ref, stat_ref)


def _const_spec(shape):
    nd = len(shape)
    return pl.BlockSpec(shape, lambda *_: (0,) * nd, pipeline_mode=pl.Buffered(1))


def _mixer_prompt(x, sink_rows, anw, win_bf, convw, qnw, knw, bias_tab, wout_bf, fnw, wr, br, h2s, idxs, gates):
    batch, seq, _ = x.shape
    tm = MIXER_TILE
    lrows = _local_rows(tm)
    nj = seq // tm
    tokens = batch * seq
    n_tiles = batch * nj
    nb = h2s.shape[0]
    x2 = x.reshape(tokens, D_MODEL)
    tile = lambda i: jnp.minimum(i, n_tiles - 1)
    tok_spec = lambda width: pl.BlockSpec((tm, width), lambda i: (tile(i), 0))
    per_batch = lambda rows, width: pl.BlockSpec((1, rows, width), lambda i: (tile(i) // nj, 0, 0))
    in_specs = [
        _const_spec((N_KV, GROUP * ATTN_BLOCK)),
        tok_spec(D_MODEL),
        _const_spec((1, D_MODEL)),
        _const_spec((D_MODEL, IN_DIM)),
        _const_spec((3, D_MODEL)),
        _const_spec((1, LANES)),
        _const_spec((1, LANES)),
        _const_spec((N_KV, 2 * ATTN_BLOCK, GROUP * ATTN_BLOCK)),
        _const_spec((D_MODEL, D_MODEL)),
        _const_spec((1, D_MODEL)),
        _const_spec((D_MODEL, N_EXPERTS)),
        _const_spec((1, N_EXPERTS)),
        _const_spec((nb, D_MODEL)),
        _const_spec((nb, TOP_K)),
        _const_spec((nb, TOP_K)),
    ]
    out_shape = (
        jax.ShapeDtypeStruct((tokens, D_MODEL), F32),
        jax.ShapeDtypeStruct(((n_tiles + 1) * lrows, XS_WIDTH), BF16),
        jax.ShapeDtypeStruct((tokens, TOP_K), I32),
        jax.ShapeDtypeStruct((nb, TOP_K), I32),
        jax.ShapeDtypeStruct((n_tiles + 1, 8, LANES), F32),
        jax.ShapeDtypeStruct((batch, ATTN_BLOCK, LANES), F32),
        jax.ShapeDtypeStruct((batch, ATTN_BLOCK, LANES), F32),
        jax.ShapeDtypeStruct((batch, 2, D_MODEL), F32),
    )
    out_specs = (
        tok_spec(D_MODEL),
        pl.BlockSpec((lrows, XS_WIDTH), lambda i: (i, 0)),
        tok_spec(TOP_K),
        pl.BlockSpec((nb, TOP_K), lambda i: (0, 0)),
        pl.BlockSpec((1, 8, LANES), lambda i: (i, 0, 0)),
        per_batch(ATTN_BLOCK, LANES), per_batch(ATTN_BLOCK, LANES), per_batch(2, D_MODEL),
    )
    scratch = [
        pltpu.VMEM((tm + 8, D_MODEL), F32),
        pltpu.VMEM((tm, D_MODEL), BF16),
        pltpu.VMEM((tm + ATTN_BLOCK, 2 * LANES), BF16),
        pltpu.VMEM((tm // ATTN_BLOCK + 1, N_KV, LANES, ATTN_BLOCK), BF16),
        pltpu.VMEM((tm, D_MODEL), F32),
        pltpu.VMEM((2 * ATTN_BLOCK, GROUP * ATTN_BLOCK), F32),
        pltpu.VMEM((2 * ATTN_BLOCK, GROUP * ATTN_BLOCK), BF16),
    ]
    return pl.pallas_call(
        functools.partial(_mixer_prompt_kernel, nj),
        grid=(n_tiles + 1,),
        in_specs=in_specs,
        out_specs=out_specs,
        out_shape=out_shape,
        scratch_shapes=scratch,
        compiler_params=pltpu.CompilerParams(dimension_semantics=("arbitrary",), vmem_limit_bytes=VMEM_LIMIT),
        name="mixer_prompt",
    )(sink_rows, x2, anw, win_bf, convw, qnw, knw, bias_tab, wout_bf, fnw, wr, br, h2s, idxs, gates)


SAMPLE_CHUNK = 32


def _mixer_sample_kernel(x_ref, p0_ref, p1_ref, kp_ref, vp_ref, sink_ref, anw_ref, win_ref, convw_ref,
                         qnw_ref, knw_ref, bias_ref, wout_ref, fnw_ref, wr_ref, br_ref,
                         x1_ref, h2_ref, idx_ref, gate_ref, unew_ref, kc_ref, vc_ref,
                         qh_s, o_s, kn_s, vn_s, conv_s, ga_s):
    c = pl.program_id(0)
    nb = x_ref.shape[0]
    tc = kp_ref.shape[0]
    lo = _lo_half()

    @pl.when(c == 0)
    def _():
        h = _rms(x_ref[...], anw_ref[...]).astype(BF16)

        def proj(off, n):
            return jnp.dot(h, win_ref[:, off:off + n], preferred_element_type=F32)

        u = proj(OFF_CG, D_MODEL) * proj(OFF_XIN, D_MODEL)
        unew_ref[...] = u
        cw = convw_ref[...]
        conv = p0_ref[...] * cw[0:1, :] + p1_ref[...] * cw[1:2, :] + u * cw[2:3, :]
        conv_s[...] = jax.nn.sigmoid(proj(OFF_GC, D_MODEL)) * (proj(OFF_BG, D_MODEL) * conv)
        ga_s[...] = jax.nn.sigmoid(proj(OFF_GA, D_MODEL))

        q = proj(OFF_Q, D_MODEL)
        kv = proj(OFF_K, 2 * LANES)
        kn_s[...] = _pair_norm(kv[:, :LANES], knw_ref[...])
        vn_s[...] = kv[:, LANES:]

        qnw = qnw_ref[...]
        for hd in range(N_HEADS):
            pair, half, grp = hd // 2, hd % 2, hd // GROUP
            slab = _pair_norm(q[:, pair * LANES:(pair + 1) * LANES], qnw) * ATTN_SCALE
            slab = jnp.where(lo if half == 0 else jnp.logical_not(lo), slab, 0.0)
            if half != grp:
                slab = pltpu.roll(slab, HEAD_DIM, axis=1)
            qh_s[hd * nb:(hd + 1) * nb, :] = slab

    sink = sink_ref[...]
    bias = bias_ref[...]
    rows = lax.broadcasted_iota(I32, (WINDOW, 1), 0)
    row0 = rows == 0
    row_last = rows == WINDOW - 1

    def token(t, carry):
        b = c * tc + t
        qb = qh_s[pl.ds(b, N_HEADS, stride=nb), :]
        k_new = kn_s[pl.ds(b, 1), :]
        v_new = vn_s[pl.ds(b, 1), :]
        k_old = kp_ref[t]
        v_old = vp_ref[t]
        kc_ref[t] = jnp.where(row_last, k_new, pltpu.roll(k_old, WINDOW - 1, axis=0))
        vc_ref[t] = jnp.where(row_last, v_new, pltpu.roll(v_old, WINDOW - 1, axis=0))
        kb = jnp.where(row0, k_new, k_old)
        vb = jnp.where(row0, v_new, v_old)
        s = lax.dot_general(qb.astype(BF16), kb.astype(BF16), (((1,), (1,)), ((), ())),
                            preferred_element_type=F32) + bias
        m = jnp.maximum(jnp.max(s, axis=-1, keepdims=True), sink)
        pr = jnp.exp(s - m)
        den = jnp.sum(pr, axis=-1, keepdims=True) + jnp.exp(sink - m)
        o = jnp.dot(pr.astype(BF16), vb.astype(BF16), preferred_element_type=F32) / den
        o_s[pl.ds(b, N_HEADS, stride=nb), :] = o
        return carry

    lax.fori_loop(0, tc, token, 0, unroll=4)

    @pl.when(c == pl.num_programs(0) - 1)
    def _():
        cols = []
        for pair in range(N_HEADS // 2):
            halves = []
            for half in range(2):
                hd = 2 * pair + half
                slab = o_s[hd * nb:(hd + 1) * nb, :]
                if half != hd // GROUP:
                    slab = pltpu.roll(slab, HEAD_DIM, axis=1)
                halves.append(slab)
            cols.append(jnp.where(lo, halves[0], halves[1]))
        merged = conv_s[...] + ga_s[...] * jnp.concatenate(cols, axis=1)
        h2, idxs, gates = _epilogue(x_ref[...], merged, wout_ref, fnw_ref, wr_ref, br_ref, x1_ref)
        h2_ref[...] = h2
        lane4 = lax.broadcasted_iota(I32, (nb, TOP_K), 1)

        def pack(cs):
            return jnp.where(lane4 == 0, cs[0], jnp.where(lane4 == 1, cs[1], jnp.where(lane4 == 2, cs[2], cs[3])))

        idx_ref[...] = pack(idxs)
        gate_ref[...] = pack(gates)


def _mixer_sample(x, p0, p1, k_past, v_past, sink_col, anw, win_bf, convw, qnw, knw, bias_s, wout_bf, fnw, wr, br):
    nb = x.shape[0]
    tc = SAMPLE_CHUNK
    consts = (x, p0, p1)
    params = (sink_col, anw, win_bf, convw, qnw, knw, bias_s, wout_bf, fnw, wr, br)
    cache_spec = pl.BlockSpec((tc, WINDOW, LANES), lambda c: (c, 0, 0))
    full = lambda shape: pl.BlockSpec(shape, lambda c: (0,) * len(shape))
    out_shape = (
        jax.ShapeDtypeStruct((nb, D_MODEL), F32),
        jax.ShapeDtypeStruct((nb, D_MODEL), BF16),
        jax.ShapeDtypeStruct((nb, TOP_K), F32),
        jax.ShapeDtypeStruct((nb, TOP_K), F32),
        jax.ShapeDtypeStruct((nb, D_MODEL), F32),
        jax.ShapeDtypeStruct((nb, WINDOW, LANES), F32),
        jax.ShapeDtypeStruct((nb, WINDOW, LANES), F32),
    )
    out_specs = tuple(full(s.shape) for s in out_shape[:5]) + (cache_spec, cache_spec)
    scratch = [
        pltpu.VMEM((N_HEADS * nb, LANES), F32),
        pltpu.VMEM((N_HEADS * nb, LANES), F32),
        pltpu.VMEM((nb, LANES), F32),
        pltpu.VMEM((nb, LANES), F32),
        pltpu.VMEM((nb, D_MODEL), F32),
        pltpu.VMEM((nb, D_MODEL), F32),
    ]
    return pl.pallas_call(
        _mixer_sample_kernel,
        grid=(nb // tc,),
        in_specs=[_const_spec(a.shape) for a in consts] + [cache_spec, cache_spec]
        + [_const_spec(a.shape) for a in params],
        out_specs=out_specs,
        out_shape=out_shape,
        scratch_shapes=scratch,
        compiler_params=pltpu.CompilerParams(dimension_semantics=("arbitrary",), vmem_limit_bytes=VMEM_LIMIT),
        name="mixer_sample",
    )(*consts, k_past, v_past, *params)


def _expert_kernel(be_ref, nact_ref, src_ref, dst_ref, xs_hbm, wup_ref, bup_ref, wdn_ref, bdn_ref, ys_hbm,
                   wup_s, wdn_s, perm_s, xbuf, ybuf, in_sem, out_sem):
    i = pl.program_id(0)
    n_real = nact_ref[0]
    n_active = nact_ref[1]
    e = be_ref[i]
    e_prev = be_ref[jnp.maximum(i - 1, 0)]
    slot = lax.rem(i, 2)

    def gather_copy(blk, slt, c):
        row = pl.multiple_of(src_ref[blk * CHUNKS_PER_BLOCK + c] * CHUNK, CHUNK)
        return pltpu.make_async_copy(xs_hbm.at[pl.ds(row, CHUNK), :],
                                     xbuf.at[slt, pl.ds(c * CHUNK, CHUNK), :], in_sem.at[slt])

    def scatter_chunks(blk, slt, wait):
        base = blk * CHUNKS_PER_BLOCK

        def one(c, dst):
            row = pl.multiple_of(dst * CHUNK, CHUNK)
            cp = pltpu.make_async_copy(ybuf.at[slt, pl.ds(c * CHUNK, CHUNK), :],
                                       ys_hbm.at[pl.ds(row, CHUNK), :], out_sem.at[slt])
            if wait:
                cp.wait()
            else:
                cp.start()

        full = dst_ref[base + CHUNKS_PER_BLOCK - 1] >= 0

        @pl.when(full)
        def _():
            for c in range(CHUNKS_PER_BLOCK):
                one(c, dst_ref[base + c])

        @pl.when(jnp.logical_not(full))
        def _():
            for c in range(CHUNKS_PER_BLOCK - 1):
                dst = dst_ref[base + c]

                @pl.when(dst >= 0)
                def _():
                    one(c, dst)

    @pl.when(jnp.logical_and(i == 0, n_real > 0))
    def _():
        for c in range(CHUNKS_PER_BLOCK):
            gather_copy(0, 0, c).start()

    @pl.when(jnp.logical_and(i < n_real, jnp.logical_or(i == 0, e != e_prev)))
    def _():
        wup_s[...] = wup_ref[0].astype(BF16)
        half = LANES // 2
        for cs in range(D_MODEL // LANES):
            cols = slice(cs * LANES, (cs + 1) * LANES)
            for c in range(D_MODEL // LANES):
                for par in range(2):
                    s0 = c * LANES + par * half
                    perm_s[cs, pl.ds(c * LANES + par, half, stride=2), :] = wdn_ref[0, s0:s0 + half, cols]
            wdn_s[:, cols] = perm_s[cs].astype(BF16)

    @pl.when(i + 1 < n_real)
    def _():
        for c in range(CHUNKS_PER_BLOCK):
            gather_copy(i + 1, 1 - slot, c).start()

    @pl.when(jnp.logical_and(i >= 2, i < n_active))
    def _():
        scatter_chunks(i - 2, slot, wait=True)

    @pl.when(jnp.logical_and(i >= n_real, i < n_active))
    def _():
        ybuf[slot] = jnp.zeros((EXPERT_BLOCK, D_MODEL), BF16)
        scatter_chunks(i, slot, wait=False)

    def ffn(rows):
        for c in range(CHUNKS_PER_BLOCK):
            gather_copy(i, slot, c).wait()

        xb = xbuf[slot, 0:rows, :]
        meta = xb[:, D_MODEL:].astype(F32)
        gate = meta[:, 0:1] + meta[:, 1:2]
        u = jnp.dot(xb[:, :D_MODEL], wup_s[...], preferred_element_type=F32) + bup_ref[0]
        even = (_lane_iota() & 1) == 0
        cols = []
        for c in range(D_MODEL // LANES):
            c0 = u[:, (2 * c) * LANES:(2 * c + 1) * LANES]
            c1 = u[:, (2 * c + 1) * LANES:(2 * c + 2) * LANES]
            glu = jnp.where(even, c0, pltpu.roll(c1, 1, axis=1))
            lin = jnp.where(even, pltpu.roll(c0, LANES - 1, axis=1), c1)
            glu = jnp.minimum(glu, SWIGLU_LIMIT)
            lin = jnp.clip(lin, -SWIGLU_LIMIT, SWIGLU_LIMIT)
            cols.append(glu * jax.nn.sigmoid(SWIGLU_ALPHA * glu) * (lin + 1.0))
        a = jnp.concatenate(cols, axis=1).astype(BF16)
        y = jnp.dot(a, wdn_s[...], preferred_element_type=F32) + bdn_ref[0]
        ybuf[slot, 0:rows, :] = (y * gate).astype(BF16)
        scatter_chunks(i, slot, wait=False)

    half_chunks = CHUNKS_PER_BLOCK // 2
    half_empty = dst_ref[i * CHUNKS_PER_BLOCK + half_chunks] < 0

    @pl.when(jnp.logical_and(i < n_real, jnp.logical_not(half_empty)))
    def _():
        ffn(EXPERT_BLOCK)

    @pl.when(jnp.logical_and(i < n_real, half_empty))
    def _():
        ffn(EXPERT_BLOCK // 2)

    @pl.when(i == n_active - 1)
    def _():
        @pl.when(i >= 1)
        def _():
            scatter_chunks(i - 1, 1 - slot, wait=True)

        scatter_chunks(i, slot, wait=True)


def _experts(block_expert, n_active, chunk_src, chunk_dst, xs_big, ys_rows, w_up, b_up, w_down, b_down):
    n_blocks = block_expert.shape[0]
    wspec = lambda shape: pl.BlockSpec(shape, lambda i, be, na, cs, cd: (jnp.minimum(be[i], N_EXPERTS - 1), 0, 0))
    grid_spec = pltpu.PrefetchScalarGridSpec(
        num_scalar_prefetch=4,
        grid=(n_blocks,),
        in_specs=[
            pl.BlockSpec(memory_space=pl.ANY),
            wspec((1, D_MODEL, 2 * D_MODEL)),
            wspec((1, 1, 2 * D_MODEL)),
            wspec((1, D_MODEL, D_MODEL)),
            wspec((1, 1, D_MODEL)),
        ],
        out_specs=pl.BlockSpec(memory_space=pl.ANY),
        scratch_shapes=[
            pltpu.VMEM((D_MODEL, 2 * D_MODEL), BF16),
            pltpu.VMEM((D_MODEL, D_MODEL), BF16),
            pltpu.VMEM((D_MODEL // LANES, D_MODEL, LANES), F32),
            pltpu.VMEM((2, EXPERT_BLOCK, XS_WIDTH), BF16),
            pltpu.VMEM((2, EXPERT_BLOCK, D_MODEL), BF16),
            pltpu.SemaphoreType.DMA((2,)),
            pltpu.SemaphoreType.DMA((2,)),
        ],
    )
    return pl.pallas_call(
        _expert_kernel,
        grid_spec=grid_spec,
        out_shape=jax.ShapeDtypeStruct((ys_rows, D_MODEL), BF16),
        compiler_params=pltpu.CompilerParams(dimension_semantics=("arbitrary",), vmem_limit_bytes=VMEM_LIMIT),
        name="experts",
    )(block_expert, n_active, chunk_src, chunk_dst, xs_big, w_up, b_up.reshape(N_EXPERTS, 1, 2 * D_MODEL),
      w_down, b_down.reshape(N_EXPERTS, 1, D_MODEL))


def _combine_kernel(lp_ref, ys_ref, x1_ref, out_ref):
    tm = x1_ref.shape[0]
    lrows = ys_ref.shape[0]
    lp = lp_ref[...].astype(F32)
    rid = lax.broadcasted_iota(I32, (tm, lrows), 1).astype(F32)
    sel = rid == lp[:, 0:1]
    for k in range(1, TOP_K):
        sel = jnp.logical_or(sel, rid == lp[:, k:k + 1])
    out_ref[...] = x1_ref[...] + jnp.dot(sel.astype(BF16), ys_ref[...], preferred_element_type=F32)


def _combine(lp, ys_big, x1, tm, lrows, first_block):
    tokens = x1.shape[0]
    return pl.pallas_call(
        _combine_kernel,
        grid=(tokens // tm,),
        in_specs=[
            pl.BlockSpec((tm, TOP_K), lambda j: (j, 0)),
            pl.BlockSpec((lrows, D_MODEL), lambda j: (first_block + j, 0)),
            pl.BlockSpec((tm, D_MODEL), lambda j: (j, 0)),
        ],
        out_specs=pl.BlockSpec((tm, D_MODEL), lambda j: (j, 0)),
        out_shape=jax.ShapeDtypeStruct((tokens, D_MODEL), F32),
        compiler_params=pltpu.CompilerParams(dimension_semantics=("arbitrary",), vmem_limit_bytes=VMEM_LIMIT),
        name="combine",
    )(lp, ys_big, x1)


def _bias_tables(rel_bias):
    qi = np.arange(ATTN_BLOCK)[:, None]
    kj = np.arange(2 * ATTN_BLOCK)[None, :]
    rel = qi + ATTN_BLOCK - kj
    valid = (rel >= 0) & (rel < WINDOW)
    bucket = np.where(valid, _t5_bucket_np(rel), -1)
    slot_rel = np.where(np.arange(WINDOW) == 0, 0, WINDOW - np.arange(WINDOW))
    slot_bucket = _t5_bucket_np(slot_rel)
    buckets = np.arange(N_BUCKETS)
    onehot = bucket[None, :, :] == buckets[:, None, None]
    tab = jnp.sum(jnp.where(onehot[None], rel_bias.T[:, :, None, None], 0.0), axis=1)
    tab = jnp.where(valid[None], tab, NEG_INF)
    slot_onehot = slot_bucket[None, :] == buckets[:, None]
    sample_tab = jnp.sum(jnp.where(slot_onehot[None], rel_bias.T[:, :, None], 0.0), axis=1)
    prompt_tab = tab.reshape(N_KV, GROUP, ATTN_BLOCK, 2 * ATTN_BLOCK).transpose(0, 3, 1, 2)
    return prompt_tab.reshape(N_KV, 2 * ATTN_BLOCK, GROUP * ATTN_BLOCK), sample_tab


def _chunk_tables(seg, starts, tile_base, tile_rows, n_blocks):
    n_tiles = seg.shape[0]
    n_seg = (N_EXPERTS + 1) * n_tiles
    used = jnp.sum(seg, axis=1)
    seg_e = jnp.concatenate([seg.T, (tile_rows - used)[None, :]], axis=0)
    src0 = jnp.concatenate([tile_base[None, :] + starts.T, (tile_base + used)[None, :]], axis=0).reshape(-1)
    total = jnp.sum(seg_e, axis=1)
    region = (total + EXPERT_BLOCK - 1) // EXPERT_BLOCK * EXPERT_BLOCK
    pad_end = jnp.cumsum(region)
    pad_start = pad_end - region
    g_start = (pad_start[:, None] + jnp.cumsum(seg_e, axis=1) - seg_e).reshape(-1)
    g_end = g_start + seg_e.reshape(-1)
    rows = jnp.arange(n_blocks * CHUNKS_PER_BLOCK, dtype=I32) * CHUNK
    passed = g_end[None, :] <= rows[:, None]

    def at_segment(table, sentinel):
        ext = jnp.concatenate([table, jnp.array([sentinel], I32)])
        return ext[0] + jnp.sum(jnp.where(passed, (ext[1:] - ext[:-1])[None, :], 0), axis=1)

    seg_start = at_segment(g_start, 1 << 30)
    valid = rows >= seg_start
    src_row = at_segment(src0, 0) + rows - seg_start
    chunk_src = jnp.where(valid, src_row // CHUNK, 0).astype(I32)
    chunk_dst = jnp.where(valid, src_row // CHUNK, -1).astype(I32)
    blk_rows = jnp.arange(n_blocks, dtype=I32) * EXPERT_BLOCK
    block_region = jnp.minimum(jnp.sum((pad_end[None, :] <= blk_rows[:, None]).astype(I32), axis=1),
                               N_EXPERTS).astype(I32)
    counts = jnp.stack([pad_end[N_EXPERTS - 1], pad_end[N_EXPERTS]]).astype(I32) // EXPERT_BLOCK
    return block_region, counts, chunk_src, chunk_dst


def kernel(x_prompt, x_sample, state_conv, cache_k_win, cache_v_win, rel_bias, attn_norm_w, w_in, conv_w,
           q_norm_w, k_norm_w, sinks, w_out, ffn_norm_w, w_router, b_router, w_up, b_up, w_down, b_down):
    batch, seq, _ = x_prompt.shape
    nb = x_sample.shape[0]
    anw = attn_norm_w[0].reshape(1, D_MODEL)
    fnw = ffn_norm_w[0].reshape(1, D_MODEL)
    win_bf = w_in[0].astype(BF16)
    wout_bf = w_out[0].astype(BF16)
    qnw = jnp.tile(q_norm_w[0], 2).reshape(1, LANES)
    knw = jnp.tile(k_norm_w[0], 2).reshape(1, LANES)
    br = b_router[0].reshape(1, N_EXPERTS)
    prompt_tab, sample_tab = _bias_tables(rel_bias)

    k_past = cache_k_win[0].reshape(nb, WINDOW, LANES)
    v_past = cache_v_win[0].reshape(nb, WINDOW, LANES)
    x1s, h2s, idxs, gates, unew, new_k_sample, new_v_sample = _mixer_sample(
        x_sample.reshape(nb, D_MODEL), state_conv[0, :, 0, :], state_conv[0, :, 1, :], k_past, v_past,
        sinks[0].reshape(N_HEADS, 1), anw, win_bf, conv_w[0], qnw, knw, sample_tab, wout_bf, fnw, w_router[0], br)

    sink_rows = jnp.repeat(sinks[0].reshape(N_KV, GROUP), ATTN_BLOCK, axis=1)
    x1p, xs_big, lpp, lps, stat, kp, vp, cp = _mixer_prompt(
        x_prompt, sink_rows, anw, win_bf, conv_w[0], qnw, knw, prompt_tab, wout_bf, fnw, w_router[0], br,
        h2s, idxs, gates)

    n_tiles = stat.shape[0]
    lrows = _local_rows(MIXER_TILE)
    xs_rows = n_tiles * lrows
    seg = stat[:, 0, :N_EXPERTS].astype(I32)
    starts = stat[:, 1, :N_EXPERTS].astype(I32)
    tile_base = jnp.arange(n_tiles, dtype=I32) * lrows
    tile_rows = jnp.full((n_tiles,), lrows, I32)
    n_blocks = -(-(xs_rows + (N_EXPERTS + 1) * (EXPERT_BLOCK - 1)) // EXPERT_BLOCK)
    block_expert, n_active, chunk_src, chunk_dst = _chunk_tables(seg, starts, tile_base, tile_rows, n_blocks)

    ys_big = _experts(block_expert, n_active, chunk_src, chunk_dst, xs_big, xs_rows, w_up[0], b_up[0], w_down[0],
                      b_down[0])

    y_prompt = _combine(lpp, ys_big, x1p, MIXER_TILE, lrows, 0).reshape(batch, seq, D_MODEL)
    y_sample = _combine(lps, ys_big, x1s, nb, lrows, n_tiles - 1).reshape(nb, 1, D_MODEL)

    new_conv_sample = jnp.stack([state_conv[0, :, 1, :], unew], axis=1)
    return (
        y_prompt,
        y_sample,
        cp[None],
        kp.reshape(1, batch, ATTN_BLOCK, N_KV, HEAD_DIM),
        vp.reshape(1, batch, ATTN_BLOCK, N_KV, HEAD_DIM),
        new_conv_sample[None],
        new_k_sample.reshape(1, nb, WINDOW, N_KV, HEAD_DIM),
        new_v_sample.reshape(1, nb, WINDOW, N_KV, HEAD_DIM),
    )
```

```python
import functools
import math

import numpy as np
import jax
import jax.numpy as jnp
from jax import lax
from jax.experimental import pallas as pl
from jax.experimental.pallas import tpu as pltpu

F32 = jnp.float32
BF16 = jnp.bfloat16
I32 = jnp.int32

D_MODEL = 1024
HEAD_DIM = 64
N_HEADS = 16
N_KV = 2
GROUP = N_HEADS // N_KV
WINDOW = 128
ATTN_BLOCK = 128
N_BUCKETS = 32
MAX_DISTANCE = 128
NEG_INF = -1e30
N_EXPERTS = 32
TOP_K = 4
SWIGLU_ALPHA = 1.702
SWIGLU_LIMIT = 7.0
EPS = 1e-5
ATTN_SCALE = HEAD_DIM ** -0.5

OFF_XIN, OFF_BG, OFF_CG, OFF_Q = 0, 1024, 2048, 3072
OFF_K, OFF_V, OFF_GC, OFF_GA = 4096, 4224, 4352, 5376
IN_DIM = 6400

LANES = 128
MIXER_TILE = 512
EXPERT_BLOCK = 512
CHUNK = 16
CHUNKS_PER_BLOCK = EXPERT_BLOCK // CHUNK
XS_WIDTH = D_MODEL + LANES
VMEM_LIMIT = 60 * 1024 * 1024


def _local_rows(tm):
    need = tm * TOP_K + N_EXPERTS * (CHUNK - 1)
    return -(-need // 512) * 512


def _t5_bucket_np(rel):
    n = np.maximum(rel, 0)
    max_exact = N_BUCKETS // 2
    nf = np.maximum(n, 1).astype(np.float64)
    large = max_exact + (np.log(nf / max_exact) / math.log(MAX_DISTANCE / max_exact)
                         * (N_BUCKETS - max_exact)).astype(np.int32)
    large = np.minimum(large, N_BUCKETS - 1)
    return np.where(n < max_exact, n, large).astype(np.int32)


def _rms(x, w):
    return x * lax.rsqrt(jnp.mean(x * x, axis=-1, keepdims=True) + EPS) * w


def _lane_iota(rows=1):
    return lax.broadcasted_iota(I32, (rows, LANES), 1)


def _lo_half():
    return _lane_iota() < HEAD_DIM


def _pair_norm(t, w128):
    lo = _lo_half()
    sq = t * t
    s_lo = jnp.sum(jnp.where(lo, sq, 0.0), axis=-1, keepdims=True)
    s_hi = jnp.sum(jnp.where(lo, 0.0, sq), axis=-1, keepdims=True)
    r = jnp.where(lo, lax.rsqrt(s_lo * (1.0 / HEAD_DIM) + EPS), lax.rsqrt(s_hi * (1.0 / HEAD_DIM) + EPS))
    return t * r * w128


def _top4_gates(logits):
    rows = logits.shape[0]
    lane = lax.broadcasted_iota(I32, (rows, N_EXPERTS), 1).astype(F32)
    vals, idxs = [], []
    l = logits
    for _ in range(TOP_K):
        m = jnp.max(l, axis=-1, keepdims=True)
        idx = jnp.min(jnp.where(l == m, lane, float(N_EXPERTS)), axis=-1, keepdims=True)
        vals.append(m)
        idxs.append(idx)
        l = jnp.where(lane == idx, -jnp.inf, l)
    es = [jnp.exp(v - vals[0]) for v in vals]
    den = es[0] + es[1] + es[2] + es[3]
    return idxs, [e / den for e in es]


def _cols_to_lanes(cols, rows, fill=0.0):
    lane = _lane_iota(rows)
    out = jnp.full((rows, LANES), fill, F32)
    for k, c in enumerate(cols):
        out = jnp.where(lane == k, c, out)
    return out


def _dispatch(h2, idxs, gates, xs_ref, lp_ref, stat_ref):
    tm = h2.shape[0]
    lrows = xs_ref.shape[0]
    lane = _lane_iota(tm)
    lane_f = lane.astype(F32)
    member = jnp.zeros((tm, LANES), F32)
    for k in range(TOP_K):
        member = member + jnp.where(lane_f == idxs[k], 1.0, 0.0)
    cnt = jnp.sum(member, axis=0, keepdims=True)
    seg = jnp.floor((cnt + (CHUNK - 1)) * (1.0 / CHUNK)) * CHUNK
    lane1 = _lane_iota()
    incl = seg
    for s in (1, 2, 4, 8, 16):
        incl = incl + jnp.where(lane1 >= s, pltpu.roll(incl, s, axis=1), 0.0)
    starts = incl - seg
    row8 = lax.broadcasted_iota(I32, (8, LANES), 0)
    stat_ref[0] = jnp.where(row8 == 0, seg, jnp.where(row8 == 1, starts, 0.0))

    tri = (lax.broadcasted_iota(I32, (tm, tm), 1) < lax.broadcasted_iota(I32, (tm, tm), 0)).astype(BF16)
    rank = jnp.dot(tri, member.astype(BF16), preferred_element_type=F32)
    pos = starts + rank
    lps = [jnp.sum(jnp.where(lane_f == idxs[k], pos, 0.0), axis=-1, keepdims=True) for k in range(TOP_K)]
    lane4 = lax.broadcasted_iota(I32, (tm, TOP_K), 1)
    lp_ref[...] = jnp.where(lane4 == 0, lps[0], jnp.where(lane4 == 1, lps[1],
                                                          jnp.where(lane4 == 2, lps[2], lps[3]))).astype(I32)
    lp_t = _cols_to_lanes(lps, tm, fill=-1.0).T

    g_hi = [g.astype(BF16).astype(F32) for g in gates]
    g_lo = [g - h for g, h in zip(gates, g_hi)]
    meta_in = _cols_to_lanes(list(idxs) + g_hi + g_lo, tm)
    rhs = jnp.concatenate([h2.astype(BF16), meta_in.astype(BF16)], axis=1)

    sub = 512
    starts32 = jnp.where(lane1 < N_EXPERTS, starts, 1e9)
    lp_group = jnp.floor(lp_t * (1.0 / LANES))
    lp_off = lp_t - lp_group * LANES
    row_off = lax.broadcasted_iota(I32, (LANES, tm), 0).astype(F32).astype(BF16)
    one_bf = jnp.ones((LANES, tm), BF16)
    zero_bf = jnp.zeros((LANES, tm), BF16)

    for c in range(lrows // sub):
        r0 = c * sub
        rid = (lax.broadcasted_iota(I32, (sub, 1), 0) + r0).astype(F32)
        parts = []
        for s in range(sub // LANES):
            group = float(r0 // LANES + s)
            hit = None
            for k in range(TOP_K):
                off_k = jnp.where(lp_group[k:k + 1, :] == group, lp_off[k:k + 1, :], -1.0).astype(BF16)
                hit_k = row_off == off_k
                hit = hit_k if hit is None else jnp.logical_or(hit, hit_k)
            parts.append(jnp.where(hit, one_bf, zero_bf))
        full = jnp.dot(jnp.concatenate(parts, axis=0), rhs, preferred_element_type=F32)
        got = full[:, D_MODEL:]
        e_row = jnp.sum(jnp.where(rid >= starts32, 1.0, 0.0), axis=-1, keepdims=True) - 1.0
        g_sum = pltpu.roll(got, LANES - TOP_K, axis=1) + pltpu.roll(got, LANES - 2 * TOP_K, axis=1)
        lane_s = _lane_iota(sub)
        pick = jnp.logical_and(lane_s < TOP_K, got == e_row)
        gate_row = jnp.sum(jnp.where(pick, g_sum, 0.0), axis=-1, keepdims=True)
        gr_hi = gate_row.astype(BF16).astype(F32)
        meta = jnp.where(lane_s == 0, gr_hi, jnp.where(lane_s == 1, gate_row - gr_hi, 0.0))
        xs_ref[r0:r0 + sub, 0:D_MODEL] = full[:, :D_MODEL].astype(BF16)
        xs_ref[r0:r0 + sub, D_MODEL:XS_WIDTH] = meta.astype(BF16)


def _epilogue(x, merged, wout_ref, fnw_ref, wr_ref, br_ref, x1_ref):
    x1 = x + jnp.dot(merged.astype(BF16), wout_ref[...], preferred_element_type=F32)
    x1_ref[...] = x1
    h2 = _rms(x1, fnw_ref[...]).astype(BF16)
    logits = jnp.dot(h2, wr_ref[...].astype(BF16), preferred_element_type=F32) + br_ref[...]
    idxs, gates = _top4_gates(logits)
    return h2, idxs, gates


def _mixer_prompt_kernel(nj, sink_ref, x_ref, anw_ref, win_ref, convw_ref, qnw_ref, knw_ref, bias_ref, wout_ref,
                         fnw_ref, wr_ref, br_ref, h2s_ref, idxs_ref, gates_ref,
                         x1_ref, xs_ref, lp_ref, lps_ref, stat_ref, kout_ref, vout_ref, cout_ref,
                         ubuf, q_s, kd_s, vt_s, ya_s, st_s, pt_s):
    i = pl.program_id(0)
    n_tiles = pl.num_programs(0) - 1

    @pl.when(i < n_tiles)
    def _():
        _prompt_tile(lax.rem(i, nj), nj, sink_ref, x_ref, anw_ref, win_ref, convw_ref, qnw_ref, knw_ref, bias_ref,
                     wout_ref, fnw_ref, wr_ref, br_ref, x1_ref, xs_ref, lp_ref, stat_ref, kout_ref, vout_ref,
                     cout_ref, ubuf, q_s, kd_s, vt_s, ya_s, st_s, pt_s)

    @pl.when(i == n_tiles)
    def _():
        idx4 = idxs_ref[...]
        gate4 = gates_ref[...]
        _dispatch(h2s_ref[...], [idx4[:, k:k + 1] for k in range(TOP_K)], [gate4[:, k:k + 1] for k in range(TOP_K)],
                  xs_ref, lps_ref, stat_ref)


def _prompt_tile(j, nj, sink_ref, x_ref, anw_ref, win_ref, convw_ref, qnw_ref, knw_ref, bias_ref, wout_ref,
                 fnw_ref, wr_ref, br_ref, x1_ref, xs_ref, lp_ref, stat_ref, kout_ref, vout_ref, cout_ref,
                 ubuf, q_s, kd_s, vt_s, ya_s, st_s, pt_s):
    tm = x_ref.shape[0]
    nblk = tm // ATTN_BLOCK
    first_tile = j == 0
    lo = _lo_half()

    x = x_ref[...]
    h = _rms(x, anw_ref[...]).astype(BF16)

    def proj(off, n):
        return jnp.dot(h, win_ref[:, off:off + n], preferred_element_type=F32)

    @pl.when(first_tile)
    def _():
        ubuf[0:8, :] = jnp.zeros((8, D_MODEL), F32)
        kd_s[0:ATTN_BLOCK, :] = jnp.zeros((ATTN_BLOCK, 2 * LANES), BF16)
        vt_s[0] = jnp.zeros((N_KV, LANES, ATTN_BLOCK), BF16)

    u = proj(OFF_CG, D_MODEL) * proj(OFF_XIN, D_MODEL)
    ubuf[8:tm + 8, :] = u
    cw = convw_ref[...]
    conv = ubuf[6:tm + 6, :] * cw[0:1, :] + ubuf[7:tm + 7, :] * cw[1:2, :] + u * cw[2:3, :]
    merged = jax.nn.sigmoid(proj(OFF_GC, D_MODEL)) * (proj(OFF_BG, D_MODEL) * conv)
    tail = ubuf[tm + 6:tm + 8, :]
    cout_ref[0] = tail
    ubuf[6:8, :] = tail

    q = proj(OFF_Q, D_MODEL)
    kv = proj(OFF_K, 2 * LANES)
    k = _pair_norm(kv[:, :LANES], knw_ref[...])
    v = kv[:, LANES:]
    qnw = qnw_ref[...]
    for p in range(N_HEADS // 2):
        sl = slice(p * LANES, (p + 1) * LANES)
        q_s[:, sl] = (_pair_norm(q[:, sl], qnw) * ATTN_SCALE).astype(BF16)
    k_sw = pltpu.roll(k, HEAD_DIM, axis=1)
    v_sw = pltpu.roll(v, HEAD_DIM, axis=1)
    kd_s[ATTN_BLOCK:tm + ATTN_BLOCK, 0:LANES] = jnp.where(lo, k, k_sw).astype(BF16)
    kd_s[ATTN_BLOCK:tm + ATTN_BLOCK, LANES:2 * LANES] = jnp.where(lo, k_sw, k).astype(BF16)
    v_dup = (jnp.where(lo, v, v_sw), jnp.where(lo, v_sw, v))
    for b in range(nblk):
        for g in range(N_KV):
            vt_s[b + 1, g] = v_dup[g][b * ATTN_BLOCK:(b + 1) * ATTN_BLOCK, :].T.astype(BF16)

    @pl.when(j == nj - 1)
    def _():
        kout_ref[0] = k[tm - ATTN_BLOCK:, :]
        vout_ref[0] = v[tm - ATTN_BLOCK:, :]

    prev_rows = lax.broadcasted_iota(I32, (2 * ATTN_BLOCK, 1), 0) < ATTN_BLOCK
    feat_lo = lax.broadcasted_iota(I32, (LANES, 1), 0) < HEAD_DIM

    def attn_block(blk):
        r0 = blk * ATTN_BLOCK
        qb = q_s[r0:r0 + ATTN_BLOCK, :]
        for g in range(N_KV):
            parts = []
            for t in range(GROUP):
                hd = g * GROUP + t
                slab = qb[:, (hd // 2) * LANES:(hd // 2 + 1) * LANES]
                keep = lo if hd % 2 == 0 else jnp.logical_not(lo)
                parts.append(jnp.where(keep, slab, jnp.zeros_like(slab)))
            lhs = jnp.concatenate(parts, axis=0)
            st = lax.dot_general(kd_s[r0:r0 + 2 * ATTN_BLOCK, g * LANES:(g + 1) * LANES], lhs,
                                 (((1,), (1,)), ((), ())), preferred_element_type=F32)
            st = st + bias_ref[g]
            if blk == 0:
                st = jnp.where(jnp.logical_and(prev_rows, first_tile), NEG_INF, st)
            st_s[...] = st
            sink = sink_ref[g:g + 1, :]
            rdens = []
            for t in range(GROUP):
                cols = slice(t * LANES, (t + 1) * LANES)
                s = st_s[:, cols]
                m = jnp.maximum(jnp.max(s, axis=0, keepdims=True), sink[:, cols])
                pr = jnp.exp(s - m)
                den = jnp.sum(pr, axis=0, keepdims=True) + jnp.exp(sink[:, cols] - m)
                pt_s[:, cols] = pr.astype(BF16)
                rdens.append(1.0 / den)
            vt = jnp.concatenate([vt_s[blk, g], vt_s[blk + 1, g]], axis=1)
            ot = jnp.dot(vt, pt_s[...], preferred_element_type=F32)
            for i in range(GROUP // 2):
                pair = g * (GROUP // 2) + i
                even = ot[:, (2 * i) * LANES:(2 * i + 1) * LANES] * rdens[2 * i]
                odd = ot[:, (2 * i + 1) * LANES:(2 * i + 2) * LANES] * rdens[2 * i + 1]
                ya_s[r0:r0 + ATTN_BLOCK, pair * LANES:(pair + 1) * LANES] = jnp.where(feat_lo, even, odd).T

    for blk in range(nblk):
        attn_block(blk)

    kd_s[0:ATTN_BLOCK, :] = kd_s[tm:tm + ATTN_BLOCK, :]
    vt_s[0] = vt_s[nblk]

    merged = merged + jax.nn.sigmoid(proj(OFF_GA, D_MODEL)) * ya_s[...]
    h2, idxs, gates = _epilogue(x, merged, wout_ref, fnw_ref, wr_ref, br_ref, x1_ref)
    _dispatch(h2, idxs, gates, xs_ref, lp_ref, stat_ref)


def _const_spec(shape):
    nd = len(shape)
    return pl.BlockSpec(shape, lambda *_: (0,) * nd, pipeline_mode=pl.Buffered(1))


def _mixer_prompt(x, sink_rows, anw, win_bf, convw, qnw, knw, bias_tab, wout_bf, fnw, wr, br, h2s, idxs, gates):
    batch, seq, _ = x.shape
    tm = MIXER_TILE
    lrows = _local_rows(tm)
    nj = seq // tm
    tokens = batch * seq
    n_tiles = batch * nj
    nb = h2s.shape[0]
    x2 = x.reshape(tokens, D_MODEL)
    tile = lambda i: jnp.minimum(i, n_tiles - 1)
    tok_spec = lambda width: pl.BlockSpec((tm, width), lambda i: (tile(i), 0))
    per_batch = lambda rows, width: pl.BlockSpec((1, rows, width), lambda i: (tile(i) // nj, 0, 0))
    in_specs = [
        _const_spec((N_KV, GROUP * ATTN_BLOCK)),
        tok_spec(D_MODEL),
        _const_spec((1, D_MODEL)),
        _const_spec((D_MODEL, IN_DIM)),
        _const_spec((3, D_MODEL)),
        _const_spec((1, LANES)),
        _const_spec((1, LANES)),
        _const_spec((N_KV, 2 * ATTN_BLOCK, GROUP * ATTN_BLOCK)),
        _const_spec((D_MODEL, D_MODEL)),
        _const_spec((1, D_MODEL)),
        _const_spec((D_MODEL, N_EXPERTS)),
        _const_spec((1, N_EXPERTS)),
        _const_spec((nb, D_MODEL)),
        _const_spec((nb, TOP_K)),
        _const_spec((nb, TOP_K)),
    ]
    out_shape = (
        jax.ShapeDtypeStruct((tokens, D_MODEL), F32),
        jax.ShapeDtypeStruct(((n_tiles + 1) * lrows, XS_WIDTH), BF16),
        jax.ShapeDtypeStruct((tokens, TOP_K), I32),
        jax.ShapeDtypeStruct((nb, TOP_K), I32),
        jax.ShapeDtypeStruct((n_tiles + 1, 8, LANES), F32),
        jax.ShapeDtypeStruct((batch, ATTN_BLOCK, LANES), F32),
        jax.ShapeDtypeStruct((batch, ATTN_BLOCK, LANES), F32),
        jax.ShapeDtypeStruct((batch, 2, D_MODEL), F32),
    )
    out_specs = (
        tok_spec(D_MODEL),
        pl.BlockSpec((lrows, XS_WIDTH), lambda i: (i, 0)),
        tok_spec(TOP_K),
        pl.BlockSpec((nb, TOP_K), lambda i: (0, 0)),
        pl.BlockSpec((1, 8, LANES), lambda i: (i, 0, 0)),
        per_batch(ATTN_BLOCK, LANES), per_batch(ATTN_BLOCK, LANES), per_batch(2, D_MODEL),
    )
    scratch = [
        pltpu.VMEM((tm + 8, D_MODEL), F32),
        pltpu.VMEM((tm, D_MODEL), BF16),
        pltpu.VMEM((tm + ATTN_BLOCK, 2 * LANES), BF16),
        pltpu.VMEM((tm // ATTN_BLOCK + 1, N_KV, LANES, ATTN_BLOCK), BF16),
        pltpu.VMEM((tm, D_MODEL), F32),
        pltpu.VMEM((2 * ATTN_BLOCK, GROUP * ATTN_BLOCK), F32),
        pltpu.VMEM((2 * ATTN_BLOCK, GROUP * ATTN_BLOCK), BF16),
    ]
    return pl.pallas_call(
        functools.partial(_mixer_prompt_kernel, nj),
        grid=(n_tiles + 1,),
        in_specs=in_specs,
        out_specs=out_specs,
        out_shape=out_shape,
        scratch_shapes=scratch,
        compiler_params=pltpu.CompilerParams(dimension_semantics=("arbitrary",), vmem_limit_bytes=VMEM_LIMIT),
        name="mixer_prompt",
    )(sink_rows, x2, anw, win_bf, convw, qnw, knw, bias_tab, wout_bf, fnw, wr, br, h2s, idxs, gates)


SAMPLE_CHUNK = 32


def _mixer_sample_kernel(x_ref, p0_ref, p1_ref, kp_ref, vp_ref, sink_ref, anw_ref, win_ref, convw_ref,
                         qnw_ref, knw_ref, bias_ref, wout_ref, fnw_ref, wr_ref, br_ref,
                         x1_ref, h2_ref, idx_ref, gate_ref, unew_ref, kc_ref, vc_ref,
                         qh_s, o_s, kn_s, vn_s, conv_s, ga_s):
    c = pl.program_id(0)
    nb = x_ref.shape[0]
    tc = kp_ref.shape[0]
    lo = _lo_half()

    @pl.when(c == 0)
    def _():
        h = _rms(x_ref[...], anw_ref[...]).astype(BF16)

        def proj(off, n):
            return jnp.dot(h, win_ref[:, off:off + n], preferred_element_type=F32)

        u = proj(OFF_CG, D_MODEL) * proj(OFF_XIN, D_MODEL)
        unew_ref[...] = u
        cw = convw_ref[...]
        conv = p0_ref[...] * cw[0:1, :] + p1_ref[...] * cw[1:2, :] + u * cw[2:3, :]
        conv_s[...] = jax.nn.sigmoid(proj(OFF_GC, D_MODEL)) * (proj(OFF_BG, D_MODEL) * conv)
        ga_s[...] = jax.nn.sigmoid(proj(OFF_GA, D_MODEL))

        q = proj(OFF_Q, D_MODEL)
        kv = proj(OFF_K, 2 * LANES)
        kn_s[...] = _pair_norm(kv[:, :LANES], knw_ref[...])
        vn_s[...] = kv[:, LANES:]

        qnw = qnw_ref[...]
        for hd in range(N_HEADS):
            pair, half, grp = hd // 2, hd % 2, hd // GROUP
            slab = _pair_norm(q[:, pair * LANES:(pair + 1) * LANES], qnw) * ATTN_SCALE
            slab = jnp.where(lo if half == 0 else jnp.logical_not(lo), slab, 0.0)
            if half != grp:
                slab = pltpu.roll(slab, HEAD_DIM, axis=1)
            qh_s[hd * nb:(hd + 1) * nb, :] = slab

    sink = sink_ref[...]
    bias = bias_ref[...]
    rows = lax.broadcasted_iota(I32, (WINDOW, 1), 0)
    row0 = rows == 0
    row_last = rows == WINDOW - 1

    def token(t, carry):
        b = c * tc + t
        qb = qh_s[pl.ds(b, N_HEADS, stride=nb), :]
        k_new = kn_s[pl.ds(b, 1), :]
        v_new = vn_s[pl.ds(b, 1), :]
        k_old = kp_ref[t]
        v_old = vp_ref[t]
        kc_ref[t] = jnp.where(row_last, k_new, pltpu.roll(k_old, WINDOW - 1, axis=0))
        vc_ref[t] = jnp.where(row_last, v_new, pltpu.roll(v_old, WINDOW - 1, axis=0))
        kb = jnp.where(row0, k_new, k_old)
        vb = jnp.where(row0, v_new, v_old)
        s = lax.dot_general(qb.astype(BF16), kb.astype(BF16), (((1,), (1,)), ((), ())),
                            preferred_element_type=F32) + bias
        m = jnp.maximum(jnp.max(s, axis=-1, keepdims=True), sink)
        pr = jnp.exp(s - m)
        den = jnp.sum(pr, axis=-1, keepdims=True) + jnp.exp(sink - m)
        o = jnp.dot(pr.astype(BF16), vb.astype(BF16), preferred_element_type=F32) / den
        o_s[pl.ds(b, N_HEADS, stride=nb), :] = o
        return carry

    lax.fori_loop(0, tc, token, 0, unroll=4)

    @pl.when(c == pl.num_programs(0) - 1)
    def _():
        cols = []
        for pair in range(N_HEADS // 2):
            halves = []
            for half in range(2):
                hd = 2 * pair + half
                slab = o_s[hd * nb:(hd + 1) * nb, :]
                if half != hd // GROUP:
                    slab = pltpu.roll(slab, HEAD_DIM, axis=1)
                halves.append(slab)
            cols.append(jnp.where(lo, halves[0], halves[1]))
        merged = conv_s[...] + ga_s[...] * jnp.concatenate(cols, axis=1)
        h2, idxs, gates = _epilogue(x_ref[...], merged, wout_ref, fnw_ref, wr_ref, br_ref, x1_ref)
        h2_ref[...] = h2
        lane4 = lax.broadcasted_iota(I32, (nb, TOP_K), 1)

        def pack(cs):
            return jnp.where(lane4 == 0, cs[0], jnp.where(lane4 == 1, cs[1], jnp.where(lane4 == 2, cs[2], cs[3])))

        idx_ref[...] = pack(idxs)
        gate_ref[...] = pack(gates)


def _mixer_sample(x, p0, p1, k_past, v_past, sink_col, anw, win_bf, convw, qnw, knw, bias_s, wout_bf, fnw, wr, br):
    nb = x.shape[0]
    tc = SAMPLE_CHUNK
    consts = (x, p0, p1)
    params = (sink_col, anw, win_bf, convw, qnw, knw, bias_s, wout_bf, fnw, wr, br)
    cache_spec = pl.BlockSpec((tc, WINDOW, LANES), lambda c: (c, 0, 0))
    full = lambda shape: pl.BlockSpec(shape, lambda c: (0,) * len(shape))
    out_shape = (
        jax.ShapeDtypeStruct((nb, D_MODEL), F32),
        jax.ShapeDtypeStruct((nb, D_MODEL), BF16),
        jax.ShapeDtypeStruct((nb, TOP_K), F32),
        jax.ShapeDtypeStruct((nb, TOP_K), F32),
        jax.ShapeDtypeStruct((nb, D_MODEL), F32),
        jax.ShapeDtypeStruct((nb, WINDOW, LANES), F32),
        jax.ShapeDtypeStruct((nb, WINDOW, LANES), F32),
    )
    out_specs = tuple(full(s.shape) for s in out_shape[:5]) + (cache_spec, cache_spec)
    scratch = [
        pltpu.VMEM((N_HEADS * nb, LANES), F32),
        pltpu.VMEM((N_HEADS * nb, LANES), F32),
        pltpu.VMEM((nb, LANES), F32),
        pltpu.VMEM((nb, LANES), F32),
        pltpu.VMEM((nb, D_MODEL), F32),
        pltpu.VMEM((nb, D_MODEL), F32),
    ]
    return pl.pallas_call(
        _mixer_sample_kernel,
        grid=(nb // tc,),
        in_specs=[_const_spec(a.shape) for a in consts] + [cache_spec, cache_spec]
        + [_const_spec(a.shape) for a in params],
        out_specs=out_specs,
        out_shape=out_shape,
        scratch_shapes=scratch,
        compiler_params=pltpu.CompilerParams(dimension_semantics=("arbitrary",), vmem_limit_bytes=VMEM_LIMIT),
        name="mixer_sample",
    )(*consts, k_past, v_past, *params)


def _expert_kernel(be_ref, nxt_ref, nact_ref, src_ref, dst_ref, xs_hbm, wup_hbm, bup_ref, wdn_hbm, bdn_ref, ys_hbm,
                   wup_s, wdn_s, perm_s, wup_stage, wdn_stage, xbuf, ybuf, in_sem, out_sem, w_sem):
    i = pl.program_id(0)
    n_real = nact_ref[0]
    n_active = nact_ref[1]
    e = be_ref[i]
    e_prev = be_ref[jnp.maximum(i - 1, 0)]
    slot = lax.rem(i, 2)

    def weight_copies(expert):
        return (pltpu.make_async_copy(wup_hbm.at[expert], wup_stage, w_sem.at[0]),
                pltpu.make_async_copy(wdn_hbm.at[expert], wdn_stage, w_sem.at[1]))

    def gather_copy(blk, slt, c):
        row = pl.multiple_of(src_ref[blk * CHUNKS_PER_BLOCK + c] * CHUNK, CHUNK)
        return pltpu.make_async_copy(xs_hbm.at[pl.ds(row, CHUNK), :],
                                     xbuf.at[slt, pl.ds(c * CHUNK, CHUNK), :], in_sem.at[slt])

    def scatter_chunks(blk, slt, wait):
        base = blk * CHUNKS_PER_BLOCK

        def one(c, dst):
            row = pl.multiple_of(dst * CHUNK, CHUNK)
            cp = pltpu.make_async_copy(ybuf.at[slt, pl.ds(c * CHUNK, CHUNK), :],
                                       ys_hbm.at[pl.ds(row, CHUNK), :], out_sem.at[slt])
            if wait:
                cp.wait()
            else:
                cp.start()

        full = dst_ref[base + CHUNKS_PER_BLOCK - 1] >= 0

        @pl.when(full)
        def _():
            for c in range(CHUNKS_PER_BLOCK):
                one(c, dst_ref[base + c])

        @pl.when(jnp.logical_not(full))
        def _():
            for c in range(CHUNKS_PER_BLOCK - 1):
                dst = dst_ref[base + c]

                @pl.when(dst >= 0)
                def _():
                    one(c, dst)

    @pl.when(jnp.logical_and(i == 0, n_real > 0))
    def _():
        for c in range(CHUNKS_PER_BLOCK):
            gather_copy(0, 0, c).start()

    @pl.when(jnp.logical_and(i == 0, n_real > 0))
    def _():
        for cp in weight_copies(e):
            cp.start()

    @pl.when(jnp.logical_and(i < n_real, jnp.logical_or(i == 0, e != e_prev)))
    def _():
        for cp in weight_copies(e):
            cp.wait()
        wup_s[...] = wup_stage[...].astype(BF16)
        half = LANES // 2
        for cs in range(D_MODEL // LANES):
            cols = slice(cs * LANES, (cs + 1) * LANES)
            for c in range(D_MODEL // LANES):
                for par in range(2):
                    s0 = c * LANES + par * half
                    perm_s[cs, pl.ds(c * LANES + par, half, stride=2), :] = wdn_stage[s0:s0 + half, cols]
            wdn_s[:, cols] = perm_s[cs].astype(BF16)
        nxt = nxt_ref[i]

        @pl.when(nxt < N_EXPERTS)
        def _():
            for cp in weight_copies(nxt):
                cp.start()

    @pl.when(i + 1 < n_real)
    def _():
        for c in range(CHUNKS_PER_BLOCK):
            gather_copy(i + 1, 1 - slot, c).start()

    @pl.when(jnp.logical_and(i >= 2, i < n_active))
    def _():
        scatter_chunks(i - 2, slot, wait=True)

    @pl.when(jnp.logical_and(i >= n_real, i < n_active))
    def _():
        ybuf[slot] = jnp.zeros((EXPERT_BLOCK, D_MODEL), BF16)
        scatter_chunks(i, slot, wait=False)

    def ffn(rows):
        for c in range(CHUNKS_PER_BLOCK):
            gather_copy(i, slot, c).wait()

        xb = xbuf[slot, 0:rows, :]
        meta = xb[:, D_MODEL:].astype(F32)
        gate = meta[:, 0:1] + meta[:, 1:2]
        u = jnp.dot(xb[:, :D_MODEL], wup_s[...], preferred_element_type=F32) + bup_ref[0]
        even = (_lane_iota() & 1) == 0
        cols = []
        for c in range(D_MODEL // LANES):
            c0 = u[:, (2 * c) * LANES:(2 * c + 1) * LANES]
            c1 = u[:, (2 * c + 1) * LANES:(2 * c + 2) * LANES]
            glu = jnp.where(even, c0, pltpu.roll(c1, 1, axis=1))
            lin = jnp.where(even, pltpu.roll(c0, LANES - 1, axis=1), c1)
            glu = jnp.minimum(glu, SWIGLU_LIMIT)
            lin = jnp.clip(lin, -SWIGLU_LIMIT, SWIGLU_LIMIT)
            cols.append(glu * jax.nn.sigmoid(SWIGLU_ALPHA * glu) * (lin + 1.0))
        a = jnp.concatenate(cols, axis=1).astype(BF16)
        y = jnp.dot(a, wdn_s[...], preferred_element_type=F32) + bdn_ref[0]
        ybuf[slot, 0:rows, :] = (y * gate).astype(BF16)
        scatter_chunks(i, slot, wait=False)

    quarter = CHUNKS_PER_BLOCK // 4
    base = i * CHUNKS_PER_BLOCK
    quarters = 1 + sum((dst_ref[base + q * quarter] >= 0).astype(I32) for q in range(1, 4))
    for q in range(1, 5):
        @pl.when(jnp.logical_and(i < n_real, quarters == q))
        def _(q=q):
            ffn(q * quarter * CHUNK)

    @pl.when(i == n_active - 1)
    def _():
        @pl.when(i >= 1)
        def _():
            scatter_chunks(i - 1, 1 - slot, wait=True)

        scatter_chunks(i, slot, wait=True)


def _experts(block_expert, next_expert, n_active, chunk_src, chunk_dst, xs_big, ys_rows, w_up, b_up, w_down, b_down):
    n_blocks = block_expert.shape[0]
    bspec = lambda shape: pl.BlockSpec(
        shape, lambda i, be, nx, na, cs, cd: (jnp.minimum(be[i], N_EXPERTS - 1), 0, 0))
    grid_spec = pltpu.PrefetchScalarGridSpec(
        num_scalar_prefetch=5,
        grid=(n_blocks,),
        in_specs=[
            pl.BlockSpec(memory_space=pl.ANY),
            pl.BlockSpec(memory_space=pl.ANY),
            bspec((1, 1, 2 * D_MODEL)),
            pl.BlockSpec(memory_space=pl.ANY),
            bspec((1, 1, D_MODEL)),
        ],
        out_specs=pl.BlockSpec(memory_space=pl.ANY),
        scratch_shapes=[
            pltpu.VMEM((D_MODEL, 2 * D_MODEL), BF16),
            pltpu.VMEM((D_MODEL, D_MODEL), BF16),
            pltpu.VMEM((D_MODEL // LANES, D_MODEL, LANES), F32),
            pltpu.VMEM((D_MODEL, 2 * D_MODEL), F32),
            pltpu.VMEM((D_MODEL, D_MODEL), F32),
            pltpu.VMEM((2, EXPERT_BLOCK, XS_WIDTH), BF16),
            pltpu.VMEM((2, EXPERT_BLOCK, D_MODEL), BF16),
            pltpu.SemaphoreType.DMA((2,)),
            pltpu.SemaphoreType.DMA((2,)),
            pltpu.SemaphoreType.DMA((2,)),
        ],
    )
    return pl.pallas_call(
        _expert_kernel,
        grid_spec=grid_spec,
        out_shape=jax.ShapeDtypeStruct((ys_rows, D_MODEL), BF16),
        compiler_params=pltpu.CompilerParams(dimension_semantics=("arbitrary",), vmem_limit_bytes=VMEM_LIMIT),
        name="experts",
    )(block_expert, next_expert, n_active, chunk_src, chunk_dst, xs_big, w_up,
      b_up.reshape(N_EXPERTS, 1, 2 * D_MODEL), w_down, b_down.reshape(N_EXPERTS, 1, D_MODEL))


def _combine_kernel(lp_ref, ys_ref, x1_ref, out_ref):
    tm = x1_ref.shape[0]
    lrows = ys_ref.shape[0]
    lp = lp_ref[...].astype(F32)
    rid = lax.broadcasted_iota(I32, (tm, lrows), 1).astype(F32)
    sel = rid == lp[:, 0:1]
    for k in range(1, TOP_K):
        sel = jnp.logical_or(sel, rid == lp[:, k:k + 1])
    out_ref[...] = x1_ref[...] + jnp.dot(sel.astype(BF16), ys_ref[...], preferred_element_type=F32)


def _combine(lp, ys_big, x1, tm, lrows, first_block):
    tokens = x1.shape[0]
    return pl.pallas_call(
        _combine_kernel,
        grid=(tokens // tm,),
        in_specs=[
            pl.BlockSpec((tm, TOP_K), lambda j: (j, 0)),
            pl.BlockSpec((lrows, D_MODEL), lambda j: (first_block + j, 0)),
            pl.BlockSpec((tm, D_MODEL), lambda j: (j, 0)),
        ],
        out_specs=pl.BlockSpec((tm, D_MODEL), lambda j: (j, 0)),
        out_shape=jax.ShapeDtypeStruct((tokens, D_MODEL), F32),
        compiler_params=pltpu.CompilerParams(dimension_semantics=("arbitrary",), vmem_limit_bytes=VMEM_LIMIT),
        name="combine",
    )(lp, ys_big, x1)


def _bias_tables(rel_bias):
    qi = np.arange(ATTN_BLOCK)[:, None]
    kj = np.arange(2 * ATTN_BLOCK)[None, :]
    rel = qi + ATTN_BLOCK - kj
    valid = (rel >= 0) & (rel < WINDOW)
    bucket = np.where(valid, _t5_bucket_np(rel), -1)
    slot_rel = np.where(np.arange(WINDOW) == 0, 0, WINDOW - np.arange(WINDOW))
    slot_bucket = _t5_bucket_np(slot_rel)
    buckets = np.arange(N_BUCKETS)
    onehot = bucket[None, :, :] == buckets[:, None, None]
    tab = jnp.sum(jnp.where(onehot[None], rel_bias.T[:, :, None, None], 0.0), axis=1)
    tab = jnp.where(valid[None], tab, NEG_INF)
    slot_onehot = slot_bucket[None, :] == buckets[:, None]
    sample_tab = jnp.sum(jnp.where(slot_onehot[None], rel_bias.T[:, :, None], 0.0), axis=1)
    prompt_tab = tab.reshape(N_KV, GROUP, ATTN_BLOCK, 2 * ATTN_BLOCK).transpose(0, 3, 1, 2)
    return prompt_tab.reshape(N_KV, 2 * ATTN_BLOCK, GROUP * ATTN_BLOCK), sample_tab


def _chunk_tables(seg, starts, tile_base, tile_rows, n_blocks):
    n_tiles = seg.shape[0]
    n_seg = (N_EXPERTS + 1) * n_tiles
    used = jnp.sum(seg, axis=1)
    seg_e = jnp.concatenate([seg.T, (tile_rows - used)[None, :]], axis=0)
    src0 = jnp.concatenate([tile_base[None, :] + starts.T, (tile_base + used)[None, :]], axis=0).reshape(-1)
    total = jnp.sum(seg_e, axis=1)
    region = (total + EXPERT_BLOCK - 1) // EXPERT_BLOCK * EXPERT_BLOCK
    pad_end = jnp.cumsum(region)
    pad_start = pad_end - region
    g_start = (pad_start[:, None] + jnp.cumsum(seg_e, axis=1) - seg_e).reshape(-1)
    g_end = g_start + seg_e.reshape(-1)
    rows = jnp.arange(n_blocks * CHUNKS_PER_BLOCK, dtype=I32) * CHUNK
    passed = g_end[None, :] <= rows[:, None]

    def at_segment(table, sentinel):
        ext = jnp.concatenate([table, jnp.array([sentinel], I32)])
        return ext[0] + jnp.sum(jnp.where(passed, (ext[1:] - ext[:-1])[None, :], 0), axis=1)

    seg_start = at_segment(g_start, 1 << 30)
    valid = rows >= seg_start
    src_row = at_segment(src0, 0) + rows - seg_start
    chunk_src = jnp.where(valid, src_row // CHUNK, 0).astype(I32)
    chunk_dst = jnp.where(valid, src_row // CHUNK, -1).astype(I32)
    blk_rows = jnp.arange(n_blocks, dtype=I32) * EXPERT_BLOCK
    block_region = jnp.minimum(jnp.sum((pad_end[None, :] <= blk_rows[:, None]).astype(I32), axis=1),
                               N_EXPERTS).astype(I32)
    counts = jnp.stack([pad_end[N_EXPERTS - 1], pad_end[N_EXPERTS]]).astype(I32) // EXPERT_BLOCK
    ridx = jnp.arange(N_EXPERTS + 1, dtype=I32)
    later = jnp.logical_and(ridx[None, :] > ridx[:, None], (region > 0)[None, :])
    next_region = jnp.min(jnp.where(later, ridx[None, :], N_EXPERTS + 1), axis=1)
    next_of_block = jnp.sum(jnp.where(block_region[:, None] == ridx[None, :], next_region[None, :], 0), axis=1)
    return block_region, next_of_block.astype(I32), counts, chunk_src, chunk_dst


def kernel(x_prompt, x_sample, state_conv, cache_k_win, cache_v_win, rel_bias, attn_norm_w, w_in, conv_w,
           q_norm_w, k_norm_w, sinks, w_out, ffn_norm_w, w_router, b_router, w_up, b_up, w_down, b_down):
    batch, seq, _ = x_prompt.shape
    nb = x_sample.shape[0]
    anw = attn_norm_w[0].reshape(1, D_MODEL)
    fnw = ffn_norm_w[0].reshape(1, D_MODEL)
    win_bf = w_in[0].astype(BF16)
    wout_bf = w_out[0].astype(BF16)
    qnw = jnp.tile(q_norm_w[0], 2).reshape(1, LANES)
    knw = jnp.tile(k_norm_w[0], 2).reshape(1, LANES)
    br = b_router[0].reshape(1, N_EXPERTS)
    prompt_tab, sample_tab = _bias_tables(rel_bias)

    k_past = cache_k_win[0].reshape(nb, WINDOW, LANES)
    v_past = cache_v_win[0].reshape(nb, WINDOW, LANES)
    x1s, h2s, idxs, gates, unew, new_k_sample, new_v_sample = _mixer_sample(
        x_sample.reshape(nb, D_MODEL), state_conv[0, :, 0, :], state_conv[0, :, 1, :], k_past, v_past,
        sinks[0].reshape(N_HEADS, 1), anw, win_bf, conv_w[0], qnw, knw, sample_tab, wout_bf, fnw, w_router[0], br)

    sink_rows = jnp.repeat(sinks[0].reshape(N_KV, GROUP), ATTN_BLOCK, axis=1)
    x1p, xs_big, lpp, lps, stat, kp, vp, cp = _mixer_prompt(
        x_prompt, sink_rows, anw, win_bf, conv_w[0], qnw, knw, prompt_tab, wout_bf, fnw, w_router[0], br,
        h2s, idxs, gates)

    n_tiles = stat.shape[0]
    lrows = _local_rows(MIXER_TILE)
    xs_rows = n_tiles * lrows
    seg = stat[:, 0, :N_EXPERTS].astype(I32)
    starts = stat[:, 1, :N_EXPERTS].astype(I32)
    tile_base = jnp.arange(n_tiles, dtype=I32) * lrows
    tile_rows = jnp.full((n_tiles,), lrows, I32)
    n_blocks = -(-(xs_rows + (N_EXPERTS + 1) * (EXPERT_BLOCK - 1)) // EXPERT_BLOCK)
    block_expert, next_expert, n_active, chunk_src, chunk_dst = _chunk_tables(seg, starts, tile_base, tile_rows,
                                                                              n_blocks)

    ys_big = _experts(block_expert, next_expert, n_active, chunk_src, chunk_dst, xs_big, xs_rows, w_up[0], b_up[0],
                      w_down[0], b_down[0])

    y_prompt = _combine(lpp, ys_big, x1p, MIXER_TILE, lrows, 0).reshape(batch, seq, D_MODEL)
    y_sample = _combine(lps, ys_big, x1s, nb, lrows, n_tiles - 1).reshape(nb, 1, D_MODEL)

    new_conv_sample = jnp.stack([state_conv[0, :, 1, :], unew], axis=1)
    return (
        y_prompt,
        y_sample,
        cp[None],
        kp.reshape(1, batch, ATTN_BLOCK, N_KV, HEAD_DIM),
        vp.reshape(1, batch, ATTN_BLOCK, N_KV, HEAD_DIM),
        new_conv_sample[None],
        new_k_sample.reshape(1, nb, WINDOW, N_KV, HEAD_DIM),
        new_v_sample.reshape(1, nb, WINDOW, N_KV, HEAD_DIM),
    )
```

```python
import functools
import math

import numpy as np
import jax
import jax.numpy as jnp
from jax import lax
from jax.experimental import pallas as pl
from jax.experimental.pallas import tpu as pltpu

F32 = jnp.float32
BF16 = jnp.bfloat16
I32 = jnp.int32

D_MODEL = 1024
HEAD_DIM = 64
N_HEADS = 16
N_KV = 2
GROUP = N_HEADS // N_KV
WINDOW = 128
ATTN_BLOCK = 128
N_BUCKETS = 32
MAX_DISTANCE = 128
NEG_INF = -1e30
N_EXPERTS = 32
TOP_K = 4
SWIGLU_ALPHA = 1.702
SWIGLU_LIMIT = 7.0
EPS = 1e-5
ATTN_SCALE = HEAD_DIM ** -0.5

OFF_XIN, OFF_BG, OFF_CG, OFF_Q = 0, 1024, 2048, 3072
OFF_K, OFF_V, OFF_GC, OFF_GA = 4096, 4224, 4352, 5376
IN_DIM = 6400

LANES = 128
MIXER_TILE = 512
EXPERT_BLOCK = 512
FFN_ROWS = 128
CHUNK = 16
CHUNKS_PER_BLOCK = EXPERT_BLOCK // CHUNK
XS_WIDTH = D_MODEL + LANES
VMEM_LIMIT = 60 * 1024 * 1024


def _local_rows(tm):
    need = tm * TOP_K + N_EXPERTS * (CHUNK - 1)
    return -(-need // 512) * 512


def _t5_bucket_np(rel):
    n = np.maximum(rel, 0)
    max_exact = N_BUCKETS // 2
    nf = np.maximum(n, 1).astype(np.float64)
    large = max_exact + (np.log(nf / max_exact) / math.log(MAX_DISTANCE / max_exact)
                         * (N_BUCKETS - max_exact)).astype(np.int32)
    large = np.minimum(large, N_BUCKETS - 1)
    return np.where(n < max_exact, n, large).astype(np.int32)


def _rms(x, w):
    return x * lax.rsqrt(jnp.mean(x * x, axis=-1, keepdims=True) + EPS) * w


def _lane_iota(rows=1):
    return lax.broadcasted_iota(I32, (rows, LANES), 1)


def _lo_half():
    return _lane_iota() < HEAD_DIM


def _pair_norm(t, w128):
    lo = _lo_half()
    sq = t * t
    s_lo = jnp.sum(jnp.where(lo, sq, 0.0), axis=-1, keepdims=True)
    s_hi = jnp.sum(jnp.where(lo, 0.0, sq), axis=-1, keepdims=True)
    r = jnp.where(lo, lax.rsqrt(s_lo * (1.0 / HEAD_DIM) + EPS), lax.rsqrt(s_hi * (1.0 / HEAD_DIM) + EPS))
    return t * r * w128


def _top4_gates(logits):
    rows = logits.shape[0]
    lane = lax.broadcasted_iota(I32, (rows, N_EXPERTS), 1).astype(F32)
    vals, idxs = [], []
    l = logits
    for _ in range(TOP_K):
        m = jnp.max(l, axis=-1, keepdims=True)
        idx = jnp.min(jnp.where(l == m, lane, float(N_EXPERTS)), axis=-1, keepdims=True)
        vals.append(m)
        idxs.append(idx)
        l = jnp.where(lane == idx, -jnp.inf, l)
    es = [jnp.exp(v - vals[0]) for v in vals]
    den = es[0] + es[1] + es[2] + es[3]
    return idxs, [e / den for e in es]


def _cols_to_lanes(cols, rows, fill=0.0):
    lane = _lane_iota(rows)
    out = jnp.full((rows, LANES), fill, F32)
    for k, c in enumerate(cols):
        out = jnp.where(lane == k, c, out)
    return out


def _onehot_row_builder(lp_t, tm):
    lp_group = jnp.floor(lp_t * (1.0 / LANES))
    lp_off = lp_t - lp_group * LANES
    row_off = lax.broadcasted_iota(I32, (LANES, tm), 0).astype(F32).astype(BF16)
    one_bf = jnp.ones((LANES, tm), BF16)
    zero_bf = jnp.zeros((LANES, tm), BF16)

    def build(r0, rows):
        parts = []
        for s in range(rows // LANES):
            group = float(r0 // LANES + s)
            hit = None
            for k in range(TOP_K):
                off_k = jnp.where(lp_group[k:k + 1, :] == group, lp_off[k:k + 1, :], -1.0).astype(BF16)
                hit_k = row_off == off_k
                hit = hit_k if hit is None else jnp.logical_or(hit, hit_k)
            parts.append(jnp.where(hit, one_bf, zero_bf))
        return jnp.concatenate(parts, axis=0)

    return build


def _dispatch(h2, idxs, gates, xs_ref, lp_ref, stat_ref):
    tm = h2.shape[0]
    lrows = xs_ref.shape[0]
    lane = _lane_iota(tm)
    lane_f = lane.astype(F32)
    member = jnp.zeros((tm, LANES), F32)
    for k in range(TOP_K):
        member = member + jnp.where(lane_f == idxs[k], 1.0, 0.0)
    cnt = jnp.sum(member, axis=0, keepdims=True)
    seg = jnp.floor((cnt + (CHUNK - 1)) * (1.0 / CHUNK)) * CHUNK
    lane1 = _lane_iota()
    incl = seg
    for s in (1, 2, 4, 8, 16):
        incl = incl + jnp.where(lane1 >= s, pltpu.roll(incl, s, axis=1), 0.0)
    starts = incl - seg
    row8 = lax.broadcasted_iota(I32, (8, LANES), 0)
    stat_ref[0] = jnp.where(row8 == 0, seg, jnp.where(row8 == 1, starts, 0.0))

    tri = (lax.broadcasted_iota(I32, (tm, tm), 1) < lax.broadcasted_iota(I32, (tm, tm), 0)).astype(BF16)
    rank = jnp.dot(tri, member.astype(BF16), preferred_element_type=F32)
    pos = starts + rank
    lps = [jnp.sum(jnp.where(lane_f == idxs[k], pos, 0.0), axis=-1, keepdims=True) for k in range(TOP_K)]
    lane4 = lax.broadcasted_iota(I32, (tm, TOP_K), 1)
    lp_ref[...] = jnp.where(lane4 == 0, lps[0], jnp.where(lane4 == 1, lps[1],
                                                          jnp.where(lane4 == 2, lps[2], lps[3]))).astype(I32)
    lp_t = _cols_to_lanes(lps, tm, fill=-1.0).T

    g_hi = [g.astype(BF16).astype(F32) for g in gates]
    g_lo = [g - h for g, h in zip(gates, g_hi)]
    meta_in = _cols_to_lanes(list(idxs) + g_hi + g_lo, tm)
    rhs = jnp.concatenate([h2.astype(BF16), meta_in.astype(BF16)], axis=1)

    sub = 512
    starts32 = jnp.where(lane1 < N_EXPERTS, starts, 1e9)
    onehot_rows = _onehot_row_builder(lp_t, tm)

    for c in range(lrows // sub):
        r0 = c * sub
        rid = (lax.broadcasted_iota(I32, (sub, 1), 0) + r0).astype(F32)
        full = jnp.dot(onehot_rows(r0, sub), rhs, preferred_element_type=F32)
        got = full[:, D_MODEL:]
        e_row = jnp.sum(jnp.where(rid >= starts32, 1.0, 0.0), axis=-1, keepdims=True) - 1.0
        g_sum = pltpu.roll(got, LANES - TOP_K, axis=1) + pltpu.roll(got, LANES - 2 * TOP_K, axis=1)
        lane_s = _lane_iota(sub)
        pick = jnp.logical_and(lane_s < TOP_K, got == e_row)
        gate_row = jnp.sum(jnp.where(pick, g_sum, 0.0), axis=-1, keepdims=True)
        gr_hi = gate_row.astype(BF16).astype(F32)
        meta = jnp.where(lane_s == 0, gr_hi, jnp.where(lane_s == 1, gate_row - gr_hi, 0.0))
        xs_ref[r0:r0 + sub, 0:D_MODEL] = full[:, :D_MODEL].astype(BF16)
        xs_ref[r0:r0 + sub, D_MODEL:XS_WIDTH] = meta.astype(BF16)


def _epilogue(x, merged, wout_ref, fnw_ref, wr_ref, br_ref, x1_ref):
    x1 = x + jnp.dot(merged.astype(BF16), wout_ref[...], preferred_element_type=F32)
    x1_ref[...] = x1
    h2 = _rms(x1, fnw_ref[...]).astype(BF16)
    logits = jnp.dot(h2, wr_ref[...].astype(BF16), preferred_element_type=F32) + br_ref[...]
    idxs, gates = _top4_gates(logits)
    return h2, idxs, gates


def _mixer_prompt_kernel(nj, sink_ref, x_ref, anw_ref, win_ref, convw_ref, qnw_ref, knw_ref, bias_ref, wout_ref,
                         fnw_ref, wr_ref, br_ref, h2s_ref, idxs_ref, gates_ref,
                         x1_ref, xs_ref, lp_ref, lps_ref, stat_ref, kout_ref, vout_ref, cout_ref,
                         ubuf, q_s, kd_s, vt_s, ya_s, st_s, pt_s):
    i = pl.program_id(0)
    n_tiles = pl.num_programs(0) - 1

    @pl.when(i < n_tiles)
    def _():
        _prompt_tile(lax.rem(i, nj), nj, sink_ref, x_ref, anw_ref, win_ref, convw_ref, qnw_ref, knw_ref, bias_ref,
                     wout_ref, fnw_ref, wr_ref, br_ref, x1_ref, xs_ref, lp_ref, stat_ref, kout_ref, vout_ref,
                     cout_ref, ubuf, q_s, kd_s, vt_s, ya_s, st_s, pt_s)

    @pl.when(i == n_tiles)
    def _():
        idx4 = idxs_ref[...]
        gate4 = gates_ref[...]
        _dispatch(h2s_ref[...], [idx4[:, k:k + 1] for k in range(TOP_K)], [gate4[:, k:k + 1] for k in range(TOP_K)],
                  xs_ref, lps_ref, stat_ref)


def _prompt_tile(j, nj, sink_ref, x_ref, anw_ref, win_ref, convw_ref, qnw_ref, knw_ref, bias_ref, wout_ref,
                 fnw_ref, wr_ref, br_ref, x1_ref, xs_ref, lp_ref, stat_ref, kout_ref, vout_ref, cout_ref,
                 ubuf, q_s, kd_s, vt_s, ya_s, st_s, pt_s):
    tm = x_ref.shape[0]
    nblk = tm // ATTN_BLOCK
    first_tile = j == 0
    lo = _lo_half()

    x = x_ref[...]
    h = _rms(x, anw_ref[...]).astype(BF16)

    def proj(off, n):
        return jnp.dot(h, win_ref[:, off:off + n], preferred_element_type=F32)

    @pl.when(first_tile)
    def _():
        ubuf[0:8, :] = jnp.zeros((8, D_MODEL), F32)
        kd_s[0:ATTN_BLOCK, :] = jnp.zeros((ATTN_BLOCK, 2 * LANES), BF16)
        vt_s[0] = jnp.zeros((N_KV, LANES, ATTN_BLOCK), BF16)

    u = proj(OFF_CG, D_MODEL) * proj(OFF_XIN, D_MODEL)
    ubuf[8:tm + 8, :] = u
    cw = convw_ref[...]
    conv = ubuf[6:tm + 6, :] * cw[0:1, :] + ubuf[7:tm + 7, :] * cw[1:2, :] + u * cw[2:3, :]
    merged = jax.nn.sigmoid(proj(OFF_GC, D_MODEL)) * (proj(OFF_BG, D_MODEL) * conv)
    tail = ubuf[tm + 6:tm + 8, :]
    cout_ref[0] = tail
    ubuf[6:8, :] = tail

    q = proj(OFF_Q, D_MODEL)
    kv = proj(OFF_K, 2 * LANES)
    k = _pair_norm(kv[:, :LANES], knw_ref[...])
    v = kv[:, LANES:]
    qnw = qnw_ref[...]
    for p in range(N_HEADS // 2):
        sl = slice(p * LANES, (p + 1) * LANES)
        q_s[:, sl] = (_pair_norm(q[:, sl], qnw) * ATTN_SCALE).astype(BF16)
    k_sw = pltpu.roll(k, HEAD_DIM, axis=1)
    v_sw = pltpu.roll(v, HEAD_DIM, axis=1)
    kd_s[ATTN_BLOCK:tm + ATTN_BLOCK, 0:LANES] = jnp.where(lo, k, k_sw).astype(BF16)
    kd_s[ATTN_BLOCK:tm + ATTN_BLOCK, LANES:2 * LANES] = jnp.where(lo, k_sw, k).astype(BF16)
    v_dup = (jnp.where(lo, v, v_sw), jnp.where(lo, v_sw, v))
    for b in range(nblk):
        for g in range(N_KV):
            vt_s[b + 1, g] = v_dup[g][b * ATTN_BLOCK:(b + 1) * ATTN_BLOCK, :].T.astype(BF16)

    @pl.when(j == nj - 1)
    def _():
        kout_ref[0] = k[tm - ATTN_BLOCK:, :]
        vout_ref[0] = v[tm - ATTN_BLOCK:, :]

    prev_rows = lax.broadcasted_iota(I32, (2 * ATTN_BLOCK, 1), 0) < ATTN_BLOCK
    feat_lo = lax.broadcasted_iota(I32, (LANES, 1), 0) < HEAD_DIM

    def attn_block(blk):
        r0 = blk * ATTN_BLOCK
        qb = q_s[r0:r0 + ATTN_BLOCK, :]
        for g in range(N_KV):
            parts = []
            for t in range(GROUP):
                hd = g * GROUP + t
                slab = qb[:, (hd // 2) * LANES:(hd // 2 + 1) * LANES]
                keep = lo if hd % 2 == 0 else jnp.logical_not(lo)
                parts.append(jnp.where(keep, slab, jnp.zeros_like(slab)))
            lhs = jnp.concatenate(parts, axis=0)
            st = lax.dot_general(kd_s[r0:r0 + 2 * ATTN_BLOCK, g * LANES:(g + 1) * LANES], lhs,
                                 (((1,), (1,)), ((), ())), preferred_element_type=F32)
            st = st + bias_ref[g]
            if blk == 0:
                st = jnp.where(jnp.logical_and(prev_rows, first_tile), NEG_INF, st)
            st_s[...] = st
            sink = sink_ref[g:g + 1, :]
            rdens = []
            for t in range(GROUP):
                cols = slice(t * LANES, (t + 1) * LANES)
                s = st_s[:, cols]
                m = jnp.maximum(jnp.max(s, axis=0, keepdims=True), sink[:, cols])
                pr = jnp.exp(s - m)
                den = jnp.sum(pr, axis=0, keepdims=True) + jnp.exp(sink[:, cols] - m)
                pt_s[:, cols] = pr.astype(BF16)
                rdens.append(1.0 / den)
            vt = jnp.concatenate([vt_s[blk, g], vt_s[blk + 1, g]], axis=1)
            ot = jnp.dot(vt, pt_s[...], preferred_element_type=F32)
            for i in range(GROUP // 2):
                pair = g * (GROUP // 2) + i
                even = ot[:, (2 * i) * LANES:(2 * i + 1) * LANES] * rdens[2 * i]
                odd = ot[:, (2 * i + 1) * LANES:(2 * i + 2) * LANES] * rdens[2 * i + 1]
                ya_s[r0:r0 + ATTN_BLOCK, pair * LANES:(pair + 1) * LANES] = jnp.where(feat_lo, even, odd).T

    for blk in range(nblk):
        attn_block(blk)

    kd_s[0:ATTN_BLOCK, :] = kd_s[tm:tm + ATTN_BLOCK, :]
    vt_s[0] = vt_s[nblk]

    merged = merged + jax.nn.sigmoid(proj(OFF_GA, D_MODEL)) * ya_s[...]
    h2, idxs, gates = _epilogue(x_ref[...], merged, wout_ref, fnw_ref, wr_ref, br_ref, x1_ref)
    _dispatch(h2, idxs, gates, xs_ref, lp_ref, stat_ref)


def _const_spec(shape):
    nd = len(shape)
    return pl.BlockSpec(shape, lambda *_: (0,) * nd, pipeline_mode=pl.Buffered(1))


def _mixer_prompt(x, sink_rows, anw, win_bf, convw, qnw, knw, bias_tab, wout_bf, fnw, wr, br, h2s, idxs, gates):
    batch, seq, _ = x.shape
    tm = MIXER_TILE
    lrows = _local_rows(tm)
    nj = seq // tm
    tokens = batch * seq
    n_tiles = batch * nj
    nb = h2s.shape[0]
    x2 = x.reshape(tokens, D_MODEL)
    tile = lambda i: jnp.minimum(i, n_tiles - 1)
    tok_spec = lambda width: pl.BlockSpec((tm, width), lambda i: (tile(i), 0))
    per_batch = lambda rows, width: pl.BlockSpec((1, rows, width), lambda i: (tile(i) // nj, 0, 0))
    in_specs = [
        _const_spec((N_KV, GROUP * ATTN_BLOCK)),
        tok_spec(D_MODEL),
        _const_spec((1, D_MODEL)),
        _const_spec((D_MODEL, IN_DIM)),
        _const_spec((3, D_MODEL)),
        _const_spec((1, LANES)),
        _const_spec((1, LANES)),
        _const_spec((N_KV, 2 * ATTN_BLOCK, GROUP * ATTN_BLOCK)),
        _const_spec((D_MODEL, D_MODEL)),
        _const_spec((1, D_MODEL)),
        _const_spec((D_MODEL, N_EXPERTS)),
        _const_spec((1, N_EXPERTS)),
        _const_spec((nb, D_MODEL)),
        _const_spec((nb, TOP_K)),
        _const_spec((nb, TOP_K)),
    ]
    out_shape = (
        jax.ShapeDtypeStruct((tokens, D_MODEL), F32),
        jax.ShapeDtypeStruct(((n_tiles + 1) * lrows, XS_WIDTH), BF16),
        jax.ShapeDtypeStruct((tokens, TOP_K), I32),
        jax.ShapeDtypeStruct((nb, TOP_K), I32),
        jax.ShapeDtypeStruct((n_tiles + 1, 8, LANES), F32),
        jax.ShapeDtypeStruct((batch, ATTN_BLOCK, LANES), F32),
        jax.ShapeDtypeStruct((batch, ATTN_BLOCK, LANES), F32),
        jax.ShapeDtypeStruct((batch, 2, D_MODEL), F32),
    )
    out_specs = (
        tok_spec(D_MODEL),
        pl.BlockSpec((lrows, XS_WIDTH), lambda i: (i, 0)),
        tok_spec(TOP_K),
        pl.BlockSpec((nb, TOP_K), lambda i: (0, 0)),
        pl.BlockSpec((1, 8, LANES), lambda i: (i, 0, 0)),
        per_batch(ATTN_BLOCK, LANES), per_batch(ATTN_BLOCK, LANES), per_batch(2, D_MODEL),
    )
    scratch = [
        pltpu.VMEM((tm + 8, D_MODEL), F32),
        pltpu.VMEM((tm, D_MODEL), BF16),
        pltpu.VMEM((tm + ATTN_BLOCK, 2 * LANES), BF16),
        pltpu.VMEM((tm // ATTN_BLOCK + 1, N_KV, LANES, ATTN_BLOCK), BF16),
        pltpu.VMEM((tm, D_MODEL), F32),
        pltpu.VMEM((2 * ATTN_BLOCK, GROUP * ATTN_BLOCK), F32),
        pltpu.VMEM((2 * ATTN_BLOCK, GROUP * ATTN_BLOCK), BF16),
    ]
    return pl.pallas_call(
        functools.partial(_mixer_prompt_kernel, nj),
        grid=(n_tiles + 1,),
        in_specs=in_specs,
        out_specs=out_specs,
        out_shape=out_shape,
        scratch_shapes=scratch,
        compiler_params=pltpu.CompilerParams(dimension_semantics=("arbitrary",), vmem_limit_bytes=VMEM_LIMIT),
        name="mixer_prompt",
    )(sink_rows, x2, anw, win_bf, convw, qnw, knw, bias_tab, wout_bf, fnw, wr, br, h2s, idxs, gates)


SAMPLE_CHUNK = 32
SAMPLE_GROUP = 8


def _mixer_sample_kernel(x_ref, p0_ref, p1_ref, kp_ref, vp_ref, sink_ref, anw_ref, win_ref, convw_ref,
                         qnw_ref, knw_ref, bias_ref, wout_ref, fnw_ref, wr_ref, br_ref,
                         x1_ref, h2_ref, idx_ref, gate_ref, unew_ref, kc_ref, vc_ref,
                         qh_s, o_s, kn_s, vn_s, conv_s, ga_s):
    c = pl.program_id(0)
    nb = x_ref.shape[0]
    tc = kp_ref.shape[0]
    lo = _lo_half()

    @pl.when(c == 0)
    def _():
        h = _rms(x_ref[...], anw_ref[...]).astype(BF16)

        def proj(off, n):
            return jnp.dot(h, win_ref[:, off:off + n], preferred_element_type=F32)

        u = proj(OFF_CG, D_MODEL) * proj(OFF_XIN, D_MODEL)
        unew_ref[...] = u
        cw = convw_ref[...]
        conv = p0_ref[...] * cw[0:1, :] + p1_ref[...] * cw[1:2, :] + u * cw[2:3, :]
        conv_s[...] = jax.nn.sigmoid(proj(OFF_GC, D_MODEL)) * (proj(OFF_BG, D_MODEL) * conv)
        ga_s[...] = jax.nn.sigmoid(proj(OFF_GA, D_MODEL))

        q = proj(OFF_Q, D_MODEL)
        kv = proj(OFF_K, 2 * LANES)
        kn_s[...] = _pair_norm(kv[:, :LANES], knw_ref[...])
        vn_s[...] = kv[:, LANES:]

        qnw = qnw_ref[...]
        for hd in range(N_HEADS):
            pair, half, grp = hd // 2, hd % 2, hd // GROUP
            slab = _pair_norm(q[:, pair * LANES:(pair + 1) * LANES], qnw) * ATTN_SCALE
            slab = jnp.where(lo if half == 0 else jnp.logical_not(lo), slab, 0.0)
            if half != grp:
                slab = pltpu.roll(slab, HEAD_DIM, axis=1)
            qh_s[hd * nb:(hd + 1) * nb, :] = slab

    sink = sink_ref[...]
    bias = bias_ref[...]
    rows = lax.broadcasted_iota(I32, (WINDOW, 1), 0)
    row0 = rows == 0
    row_last = rows == WINDOW - 1

    grp = SAMPLE_GROUP
    for ci in range(tc // grp):
        t0 = ci * grp
        qs, ks, vs = [], [], []
        for t in range(t0, t0 + grp):
            b = c * tc + t
            qs.append(qh_s[pl.ds(b, N_HEADS, stride=nb), :])
            k_new = kn_s[pl.ds(b, 1), :]
            v_new = vn_s[pl.ds(b, 1), :]
            k_old = kp_ref[t]
            v_old = vp_ref[t]
            kc_ref[t] = jnp.where(row_last, k_new, pltpu.roll(k_old, WINDOW - 1, axis=0))
            vc_ref[t] = jnp.where(row_last, v_new, pltpu.roll(v_old, WINDOW - 1, axis=0))
            ks.append(jnp.where(row0, k_new, k_old).astype(BF16))
            vs.append(jnp.where(row0, v_new, v_old).astype(BF16))
        q_all = jnp.concatenate(qs, axis=0).astype(BF16)
        s_all = lax.dot_general(q_all, jnp.concatenate(ks, axis=0), (((1,), (1,)), ((), ())),
                                preferred_element_type=F32)
        p_rows, rdens = [], []
        for g in range(grp):
            s = s_all[g * N_HEADS:(g + 1) * N_HEADS, g * WINDOW:(g + 1) * WINDOW] + bias
            m = jnp.maximum(jnp.max(s, axis=-1, keepdims=True), sink)
            pr = jnp.exp(s - m)
            rdens.append(1.0 / (jnp.sum(pr, axis=-1, keepdims=True) + jnp.exp(sink - m)))
            zero = jnp.zeros((N_HEADS, WINDOW), BF16)
            p_rows.append(jnp.concatenate([pr.astype(BF16) if j == g else zero for j in range(grp)], axis=1))
        o_all = jnp.dot(jnp.concatenate(p_rows, axis=0), jnp.concatenate(vs, axis=0),
                        preferred_element_type=F32)
        for g in range(grp):
            b = c * tc + t0 + g
            o_s[pl.ds(b, N_HEADS, stride=nb), :] = o_all[g * N_HEADS:(g + 1) * N_HEADS, :] * rdens[g]

    @pl.when(c == pl.num_programs(0) - 1)
    def _():
        cols = []
        for pair in range(N_HEADS // 2):
            halves = []
            for half in range(2):
                hd = 2 * pair + half
                slab = o_s[hd * nb:(hd + 1) * nb, :]
                if half != hd // GROUP:
                    slab = pltpu.roll(slab, HEAD_DIM, axis=1)
                halves.append(slab)
            cols.append(jnp.where(lo, halves[0], halves[1]))
        merged = conv_s[...] + ga_s[...] * jnp.concatenate(cols, axis=1)
        h2, idxs, gates = _epilogue(x_ref[...], merged, wout_ref, fnw_ref, wr_ref, br_ref, x1_ref)
        h2_ref[...] = h2
        lane4 = lax.broadcasted_iota(I32, (nb, TOP_K), 1)

        def pack(cs):
            return jnp.where(lane4 == 0, cs[0], jnp.where(lane4 == 1, cs[1], jnp.where(lane4 == 2, cs[2], cs[3])))

        idx_ref[...] = pack(idxs)
        gate_ref[...] = pack(gates)


def _mixer_sample(x, p0, p1, k_past, v_past, sink_col, anw, win_bf, convw, qnw, knw, bias_s, wout_bf, fnw, wr, br):
    nb = x.shape[0]
    tc = SAMPLE_CHUNK
    consts = (x, p0, p1)
    params = (sink_col, anw, win_bf, convw, qnw, knw, bias_s, wout_bf, fnw, wr, br)
    cache_spec = pl.BlockSpec((tc, WINDOW, LANES), lambda c: (c, 0, 0))
    full = lambda shape: pl.BlockSpec(shape, lambda c: (0,) * len(shape))
    out_shape = (
        jax.ShapeDtypeStruct((nb, D_MODEL), F32),
        jax.ShapeDtypeStruct((nb, D_MODEL), BF16),
        jax.ShapeDtypeStruct((nb, TOP_K), F32),
        jax.ShapeDtypeStruct((nb, TOP_K), F32),
        jax.ShapeDtypeStruct((nb, D_MODEL), F32),
        jax.ShapeDtypeStruct((nb, WINDOW, LANES), F32),
        jax.ShapeDtypeStruct((nb, WINDOW, LANES), F32),
    )
    out_specs = tuple(full(s.shape) for s in out_shape[:5]) + (cache_spec, cache_spec)
    scratch = [
        pltpu.VMEM((N_HEADS * nb, LANES), F32),
        pltpu.VMEM((N_HEADS * nb, LANES), F32),
        pltpu.VMEM((nb, LANES), F32),
        pltpu.VMEM((nb, LANES), F32),
        pltpu.VMEM((nb, D_MODEL), F32),
        pltpu.VMEM((nb, D_MODEL), F32),
    ]
    return pl.pallas_call(
        _mixer_sample_kernel,
        grid=(nb // tc,),
        in_specs=[_const_spec(a.shape) for a in consts] + [cache_spec, cache_spec]
        + [_const_spec(a.shape) for a in params],
        out_specs=out_specs,
        out_shape=out_shape,
        scratch_shapes=scratch,
        compiler_params=pltpu.CompilerParams(dimension_semantics=("arbitrary",), vmem_limit_bytes=VMEM_LIMIT),
        name="mixer_sample",
    )(*consts, k_past, v_past, *params)


def _expert_kernel(be_ref, nxt_ref, nact_ref, src_ref, dst_ref, xs_hbm, wup_hbm, bup_ref, wdn_hbm, bdn_ref, ys_hbm,
                   wup_s, wdn_s, perm_s, wup_stage, wdn_stage, xbuf, ybuf, in_sem, out_sem, w_sem):
    i = pl.program_id(0)
    n_real = nact_ref[0]
    n_active = nact_ref[1]
    e = be_ref[i]
    e_prev = be_ref[jnp.maximum(i - 1, 0)]
    slot = lax.rem(i, 2)

    def weight_copies(expert):
        return (pltpu.make_async_copy(wup_hbm.at[expert], wup_stage, w_sem.at[0]),
                pltpu.make_async_copy(wdn_hbm.at[expert], wdn_stage, w_sem.at[1]))

    def gather_copy(blk, slt, c):
        row = pl.multiple_of(src_ref[blk * CHUNKS_PER_BLOCK + c] * CHUNK, CHUNK)
        return pltpu.make_async_copy(xs_hbm.at[pl.ds(row, CHUNK), :],
                                     xbuf.at[slt, pl.ds(c * CHUNK, CHUNK), :], in_sem.at[slt])

    def scatter_chunks(blk, slt, wait):
        base = blk * CHUNKS_PER_BLOCK

        def one(c, dst):
            row = pl.multiple_of(dst * CHUNK, CHUNK)
            cp = pltpu.make_async_copy(ybuf.at[slt, pl.ds(c * CHUNK, CHUNK), :],
                                       ys_hbm.at[pl.ds(row, CHUNK), :], out_sem.at[slt])
            if wait:
                cp.wait()
            else:
                cp.start()

        full = dst_ref[base + CHUNKS_PER_BLOCK - 1] >= 0

        @pl.when(full)
        def _():
            for c in range(CHUNKS_PER_BLOCK):
                one(c, dst_ref[base + c])

        @pl.when(jnp.logical_not(full))
        def _():
            for c in range(CHUNKS_PER_BLOCK - 1):
                dst = dst_ref[base + c]

                @pl.when(dst >= 0)
                def _():
                    one(c, dst)

    @pl.when(jnp.logical_and(i == 0, n_real > 0))
    def _():
        for c in range(CHUNKS_PER_BLOCK):
            gather_copy(0, 0, c).start()

    @pl.when(jnp.logical_and(i == 0, n_real > 0))
    def _():
        for cp in weight_copies(e):
            cp.start()

    @pl.when(jnp.logical_and(i < n_real, jnp.logical_or(i == 0, e != e_prev)))
    def _():
        for cp in weight_copies(e):
            cp.wait()
        wup_s[...] = wup_stage[...].astype(BF16)
        half = LANES // 2
        for cs in range(D_MODEL // LANES):
            cols = slice(cs * LANES, (cs + 1) * LANES)
            for c in range(D_MODEL // LANES):
                for par in range(2):
                    s0 = c * LANES + par * half
                    perm_s[cs, pl.ds(c * LANES + par, half, stride=2), :] = wdn_stage[s0:s0 + half, cols]
            wdn_s[:, cols] = perm_s[cs].astype(BF16)
        nxt = nxt_ref[i]

        @pl.when(nxt < N_EXPERTS)
        def _():
            for cp in weight_copies(nxt):
                cp.start()

    @pl.when(i + 1 < n_real)
    def _():
        for c in range(CHUNKS_PER_BLOCK):
            gather_copy(i + 1, 1 - slot, c).start()

    @pl.when(jnp.logical_and(i >= 2, i < n_active))
    def _():
        scatter_chunks(i - 2, slot, wait=True)

    @pl.when(jnp.logical_and(i >= n_real, i < n_active))
    def _():
        ybuf[slot] = jnp.zeros((EXPERT_BLOCK, D_MODEL), BF16)
        scatter_chunks(i, slot, wait=False)

    def ffn(rows):
        for c in range(CHUNKS_PER_BLOCK):
            gather_copy(i, slot, c).wait()

        xb = xbuf[slot, 0:rows, :]
        meta = xb[:, D_MODEL:].astype(F32)
        gate = meta[:, 0:1] + meta[:, 1:2]
        u = jnp.dot(xb[:, :D_MODEL], wup_s[...], preferred_element_type=F32) + bup_ref[0]
        even = (_lane_iota() & 1) == 0
        cols = []
        for c in range(D_MODEL // LANES):
            c0 = u[:, (2 * c) * LANES:(2 * c + 1) * LANES]
            c1 = u[:, (2 * c + 1) * LANES:(2 * c + 2) * LANES]
            glu = jnp.where(even, c0, pltpu.roll(c1, 1, axis=1))
            lin = jnp.where(even, pltpu.roll(c0, LANES - 1, axis=1), c1)
            glu = jnp.minimum(glu, SWIGLU_LIMIT)
            lin = jnp.clip(lin, -SWIGLU_LIMIT, SWIGLU_LIMIT)
            cols.append(glu * jax.nn.sigmoid(SWIGLU_ALPHA * glu) * (lin + 1.0))
        a = jnp.concatenate(cols, axis=1).astype(BF16)
        y = jnp.dot(a, wdn_s[...], preferred_element_type=F32) + bdn_ref[0]
        ybuf[slot, 0:rows, :] = (y * gate).astype(BF16)
        scatter_chunks(i, slot, wait=False)

    part = FFN_ROWS // CHUNK
    n_parts = EXPERT_BLOCK // FFN_ROWS
    base = i * CHUNKS_PER_BLOCK
    parts_used = 1 + sum((dst_ref[base + q * part] >= 0).astype(I32) for q in range(1, n_parts))
    for q in range(1, n_parts + 1):
        @pl.when(jnp.logical_and(i < n_real, parts_used == q))
        def _(q=q):
            ffn(q * FFN_ROWS)

    @pl.when(i == n_active - 1)
    def _():
        @pl.when(i >= 1)
        def _():
            scatter_chunks(i - 1, 1 - slot, wait=True)

        scatter_chunks(i, slot, wait=True)


def _experts(block_expert, next_expert, n_active, chunk_src, chunk_dst, xs_big, ys_rows, w_up, b_up, w_down, b_down):
    n_blocks = block_expert.shape[0]
    bspec = lambda shape: pl.BlockSpec(
        shape, lambda i, be, nx, na, cs, cd: (jnp.minimum(be[i], N_EXPERTS - 1), 0, 0))
    grid_spec = pltpu.PrefetchScalarGridSpec(
        num_scalar_prefetch=5,
        grid=(n_blocks,),
        in_specs=[
            pl.BlockSpec(memory_space=pl.ANY),
            pl.BlockSpec(memory_space=pl.ANY),
            bspec((1, 1, 2 * D_MODEL)),
            pl.BlockSpec(memory_space=pl.ANY),
            bspec((1, 1, D_MODEL)),
        ],
        out_specs=pl.BlockSpec(memory_space=pl.ANY),
        scratch_shapes=[
            pltpu.VMEM((D_MODEL, 2 * D_MODEL), BF16),
            pltpu.VMEM((D_MODEL, D_MODEL), BF16),
            pltpu.VMEM((D_MODEL // LANES, D_MODEL, LANES), F32),
            pltpu.VMEM((D_MODEL, 2 * D_MODEL), F32),
            pltpu.VMEM((D_MODEL, D_MODEL), F32),
            pltpu.VMEM((2, EXPERT_BLOCK, XS_WIDTH), BF16),
            pltpu.VMEM((2, EXPERT_BLOCK, D_MODEL), BF16),
            pltpu.SemaphoreType.DMA((2,)),
            pltpu.SemaphoreType.DMA((2,)),
            pltpu.SemaphoreType.DMA((2,)),
        ],
    )
    return pl.pallas_call(
        _expert_kernel,
        grid_spec=grid_spec,
        out_shape=jax.ShapeDtypeStruct((ys_rows, D_MODEL), BF16),
        compiler_params=pltpu.CompilerParams(dimension_semantics=("arbitrary",), vmem_limit_bytes=VMEM_LIMIT),
        name="experts",
    )(block_expert, next_expert, n_active, chunk_src, chunk_dst, xs_big, w_up,
      b_up.reshape(N_EXPERTS, 1, 2 * D_MODEL), w_down, b_down.reshape(N_EXPERTS, 1, D_MODEL))


def _combine_kernel(lp_ref, ys_ref, x1_ref, out_ref):
    tm = x1_ref.shape[0]
    lrows = ys_ref.shape[0]
    lp = lp_ref[...].astype(F32)
    lp_t = _cols_to_lanes([lp[:, k:k + 1] for k in range(TOP_K)], tm, fill=-1.0).T
    onehot = _onehot_row_builder(lp_t, tm)(0, lrows)
    y = lax.dot_general(onehot, ys_ref[...], (((0,), (0,)), ((), ())), preferred_element_type=F32)
    out_ref[...] = x1_ref[...] + y


def _combine(lp, ys_big, x1, tm, lrows, first_block):
    tokens = x1.shape[0]
    return pl.pallas_call(
        _combine_kernel,
        grid=(tokens // tm,),
        in_specs=[
            pl.BlockSpec((tm, TOP_K), lambda j: (j, 0)),
            pl.BlockSpec((lrows, D_MODEL), lambda j: (first_block + j, 0)),
            pl.BlockSpec((tm, D_MODEL), lambda j: (j, 0)),
        ],
        out_specs=pl.BlockSpec((tm, D_MODEL), lambda j: (j, 0)),
        out_shape=jax.ShapeDtypeStruct((tokens, D_MODEL), F32),
        compiler_params=pltpu.CompilerParams(dimension_semantics=("arbitrary",), vmem_limit_bytes=VMEM_LIMIT),
        name="combine",
    )(lp, ys_big, x1)


def _bias_tables(rel_bias):
    qi = np.arange(ATTN_BLOCK)[:, None]
    kj = np.arange(2 * ATTN_BLOCK)[None, :]
    rel = qi + ATTN_BLOCK - kj
    valid = (rel >= 0) & (rel < WINDOW)
    bucket = np.where(valid, _t5_bucket_np(rel), -1)
    slot_rel = np.where(np.arange(WINDOW) == 0, 0, WINDOW - np.arange(WINDOW))
    slot_bucket = _t5_bucket_np(slot_rel)
    buckets = np.arange(N_BUCKETS)
    onehot = bucket[None, :, :] == buckets[:, None, None]
    tab = jnp.sum(jnp.where(onehot[None], rel_bias.T[:, :, None, None], 0.0), axis=1)
    tab = jnp.where(valid[None], tab, NEG_INF)
    slot_onehot = slot_bucket[None, :] == buckets[:, None]
    sample_tab = jnp.sum(jnp.where(slot_onehot[None], rel_bias.T[:, :, None], 0.0), axis=1)
    prompt_tab = tab.reshape(N_KV, GROUP, ATTN_BLOCK, 2 * ATTN_BLOCK).transpose(0, 3, 1, 2)
    return prompt_tab.reshape(N_KV, 2 * ATTN_BLOCK, GROUP * ATTN_BLOCK), sample_tab


def _chunk_tables(seg, starts, tile_base, tile_rows, n_blocks):
    n_tiles = seg.shape[0]
    n_seg = (N_EXPERTS + 1) * n_tiles
    used = jnp.sum(seg, axis=1)
    seg_e = jnp.concatenate([seg.T, (tile_rows - used)[None, :]], axis=0)
    src0 = jnp.concatenate([tile_base[None, :] + starts.T, (tile_base + used)[None, :]], axis=0).reshape(-1)
    total = jnp.sum(seg_e, axis=1)
    region = (total + EXPERT_BLOCK - 1) // EXPERT_BLOCK * EXPERT_BLOCK
    pad_end = jnp.cumsum(region)
    pad_start = pad_end - region
    g_start = (pad_start[:, None] + jnp.cumsum(seg_e, axis=1) - seg_e).reshape(-1)
    g_end = g_start + seg_e.reshape(-1)
    rows = jnp.arange(n_blocks * CHUNKS_PER_BLOCK, dtype=I32) * CHUNK
    passed = g_end[None, :] <= rows[:, None]

    def at_segment(table, sentinel):
        ext = jnp.concatenate([table, jnp.array([sentinel], I32)])
        return ext[0] + jnp.sum(jnp.where(passed, (ext[1:] - ext[:-1])[None, :], 0), axis=1)

    seg_start = at_segment(g_start, 1 << 30)
    valid = rows >= seg_start
    src_row = at_segment(src0, 0) + rows - seg_start
    chunk_src = jnp.where(valid, src_row // CHUNK, 0).astype(I32)
    chunk_dst = jnp.where(valid, src_row // CHUNK, -1).astype(I32)
    blk_rows = jnp.arange(n_blocks, dtype=I32) * EXPERT_BLOCK
    block_region = jnp.minimum(jnp.sum((pad_end[None, :] <= blk_rows[:, None]).astype(I32), axis=1),
                               N_EXPERTS).astype(I32)
    counts = jnp.stack([pad_end[N_EXPERTS - 1], pad_end[N_EXPERTS]]).astype(I32) // EXPERT_BLOCK
    ridx = jnp.arange(N_EXPERTS + 1, dtype=I32)
    later = jnp.logical_and(ridx[None, :] > ridx[:, None], (region > 0)[None, :])
    next_region = jnp.min(jnp.where(later, ridx[None, :], N_EXPERTS + 1), axis=1)
    next_of_block = jnp.sum(jnp.where(block_region[:, None] == ridx[None, :], next_region[None, :], 0), axis=1)
    return block_region, next_of_block.astype(I32), counts, chunk_src, chunk_dst


def kernel(x_prompt, x_sample, state_conv, cache_k_win, cache_v_win, rel_bias, attn_norm_w, w_in, conv_w,
           q_norm_w, k_norm_w, sinks, w_out, ffn_norm_w, w_router, b_router, w_up, b_up, w_down, b_down):
    batch, seq, _ = x_prompt.shape
    nb = x_sample.shape[0]
    anw = attn_norm_w[0].reshape(1, D_MODEL)
    fnw = ffn_norm_w[0].reshape(1, D_MODEL)
    win_bf = w_in[0].astype(BF16)
    wout_bf = w_out[0].astype(BF16)
    qnw = jnp.tile(q_norm_w[0], 2).reshape(1, LANES)
    knw = jnp.tile(k_norm_w[0], 2).reshape(1, LANES)
    br = b_router[0].reshape(1, N_EXPERTS)
    prompt_tab, sample_tab = _bias_tables(rel_bias)

    k_past = cache_k_win[0].reshape(nb, WINDOW, LANES)
    v_past = cache_v_win[0].reshape(nb, WINDOW, LANES)
    x1s, h2s, idxs, gates, unew, new_k_sample, new_v_sample = _mixer_sample(
        x_sample.reshape(nb, D_MODEL), state_conv[0, :, 0, :], state_conv[0, :, 1, :], k_past, v_past,
        sinks[0].reshape(N_HEADS, 1), anw, win_bf, conv_w[0], qnw, knw, sample_tab, wout_bf, fnw, w_router[0], br)

    sink_rows = jnp.repeat(sinks[0].reshape(N_KV, GROUP), ATTN_BLOCK, axis=1)
    x1p, xs_big, lpp, lps, stat, kp, vp, cp = _mixer_prompt(
        x_prompt, sink_rows, anw, win_bf, conv_w[0], qnw, knw, prompt_tab, wout_bf, fnw, w_router[0], br,
        h2s, idxs, gates)

    n_tiles = stat.shape[0]
    lrows = _local_rows(MIXER_TILE)
    xs_rows = n_tiles * lrows
    seg = stat[:, 0, :N_EXPERTS].astype(I32)
    starts = stat[:, 1, :N_EXPERTS].astype(I32)
    tile_base = jnp.arange(n_tiles, dtype=I32) * lrows
    tile_rows = jnp.full((n_tiles,), lrows, I32)
    n_blocks = -(-(xs_rows + (N_EXPERTS + 1) * (EXPERT_BLOCK - 1)) // EXPERT_BLOCK)
    block_expert, next_expert, n_active, chunk_src, chunk_dst = _chunk_tables(seg, starts, tile_base, tile_rows,
                                                                              n_blocks)

    ys_big = _experts(block_expert, next_expert, n_active, chunk_src, chunk_dst, xs_big, xs_rows, w_up[0], b_up[0],
                      w_down[0], b_down[0])

    y_prompt = _combine(lpp, ys_big, x1p, MIXER_TILE, lrows, 0).reshape(batch, seq, D_MODEL)
    y_sample = _combine(lps, ys_big, x1s, nb, lrows, n_tiles - 1).reshape(nb, 1, D_MODEL)

    new_conv_sample = jnp.stack([state_conv[0, :, 1, :], unew], axis=1)
    return (
        y_prompt,
        y_sample,
        cp[None],
        kp.reshape(1, batch, ATTN_BLOCK, N_KV, HEAD_DIM),
        vp.reshape(1, batch, ATTN_BLOCK, N_KV, HEAD_DIM),
        new_conv_sample[None],
        new_k_sample.reshape(1, nb, WINDOW, N_KV, HEAD_DIM),
        new_v_sample.reshape(1, nb, WINDOW, N_KV, HEAD_DIM),
    )
```

```python
import functools
import math

import numpy as np
import jax
import jax.numpy as jnp
from jax import lax
from jax.experimental import pallas as pl
from jax.experimental.pallas import tpu as pltpu

F32 = jnp.float32
BF16 = jnp.bfloat16
I32 = jnp.int32

D_MODEL = 1024
HEAD_DIM = 64
N_HEADS = 16
N_KV = 2
GROUP = N_HEADS // N_KV
WINDOW = 128
ATTN_BLOCK = 128
N_BUCKETS = 32
MAX_DISTANCE = 128
NEG_INF = -1e30
N_EXPERTS = 32
TOP_K = 4
SWIGLU_ALPHA = 1.702
SWIGLU_LIMIT = 7.0
EPS = 1e-5
ATTN_SCALE = HEAD_DIM ** -0.5

OFF_XIN, OFF_BG, OFF_CG, OFF_Q = 0, 1024, 2048, 3072
OFF_K, OFF_V, OFF_GC, OFF_GA = 4096, 4224, 4352, 5376
IN_DIM = 6400

LANES = 128
MIXER_TILE = 512
EXPERT_BLOCK = 512
FFN_ROWS = 128
CHUNK = 16
CHUNKS_PER_BLOCK = EXPERT_BLOCK // CHUNK
XS_WIDTH = D_MODEL + LANES
VMEM_LIMIT = 60 * 1024 * 1024


def _local_rows(tm):
    need = tm * TOP_K + N_EXPERTS * (CHUNK - 1)
    return -(-need // 512) * 512


def _t5_bucket_np(rel):
    n = np.maximum(rel, 0)
    max_exact = N_BUCKETS // 2
    nf = np.maximum(n, 1).astype(np.float64)
    large = max_exact + (np.log(nf / max_exact) / math.log(MAX_DISTANCE / max_exact)
                         * (N_BUCKETS - max_exact)).astype(np.int32)
    large = np.minimum(large, N_BUCKETS - 1)
    return np.where(n < max_exact, n, large).astype(np.int32)


def _rms(x, w):
    return x * lax.rsqrt(jnp.mean(x * x, axis=-1, keepdims=True) + EPS) * w


def _lane_iota(rows=1):
    return lax.broadcasted_iota(I32, (rows, LANES), 1)


def _lo_half():
    return _lane_iota() < HEAD_DIM


def _pair_norm(t, w128):
    lo = _lo_half()
    sq = t * t
    s_lo = jnp.sum(jnp.where(lo, sq, 0.0), axis=-1, keepdims=True)
    s_hi = jnp.sum(jnp.where(lo, 0.0, sq), axis=-1, keepdims=True)
    r = jnp.where(lo, lax.rsqrt(s_lo * (1.0 / HEAD_DIM) + EPS), lax.rsqrt(s_hi * (1.0 / HEAD_DIM) + EPS))
    return t * r * w128


def _top4_gates(logits_t):
    tokens = logits_t.shape[1]
    expert = lax.broadcasted_iota(I32, (N_EXPERTS, tokens), 0).astype(F32)
    vals, idxs = [], []
    l = logits_t
    for _ in range(TOP_K):
        m = jnp.max(l, axis=0, keepdims=True)
        idx = jnp.min(jnp.where(l == m, expert, float(N_EXPERTS)), axis=0, keepdims=True)
        vals.append(m)
        idxs.append(idx)
        l = jnp.where(expert == idx, -jnp.inf, l)
    es = [jnp.exp(v - vals[0]) for v in vals]
    den = es[0] + es[1] + es[2] + es[3]
    return idxs, [e / den for e in es]


def _rows_to_sublanes(rows, n_rows, tokens, fill=0.0):
    sub = lax.broadcasted_iota(I32, (n_rows, tokens), 0)
    out = jnp.full((n_rows, tokens), fill, F32)
    for k, r in enumerate(rows):
        out = jnp.where(sub == k, r, out)
    return out


def _onehot_row_builder(lp_t, tm):
    lp_group = jnp.floor(lp_t * (1.0 / LANES))
    lp_off = lp_t - lp_group * LANES
    row_off = lax.broadcasted_iota(I32, (LANES, tm), 0).astype(F32).astype(BF16)
    one_bf = jnp.ones((LANES, tm), BF16)
    zero_bf = jnp.zeros((LANES, tm), BF16)

    def build(r0, rows):
        parts = []
        for s in range(rows // LANES):
            group = float(r0 // LANES + s)
            hit = None
            for k in range(TOP_K):
                off_k = jnp.where(lp_group[k:k + 1, :] == group, lp_off[k:k + 1, :], -1.0).astype(BF16)
                hit_k = row_off == off_k
                hit = hit_k if hit is None else jnp.logical_or(hit, hit_k)
            parts.append(jnp.where(hit, one_bf, zero_bf))
        return jnp.concatenate(parts, axis=0)

    return build


def _dispatch(h2, idxs, gates, xs_ref, lp_ref, stat_ref):
    tm = h2.shape[0]
    lrows = xs_ref.shape[0]
    expert = lax.broadcasted_iota(I32, (N_EXPERTS, tm), 0).astype(F32)
    member = jnp.zeros((N_EXPERTS, tm), F32)
    for k in range(TOP_K):
        member = member + jnp.where(expert == idxs[k], 1.0, 0.0)
    cnt = jnp.broadcast_to(jnp.sum(member, axis=1, keepdims=True), (N_EXPERTS, LANES))
    seg = jnp.floor((cnt + (CHUNK - 1)) * (1.0 / CHUNK)) * CHUNK
    sub32 = lax.broadcasted_iota(I32, (N_EXPERTS, LANES), 0)
    incl = seg
    for s in (1, 2, 4, 8, 16):
        incl = incl + jnp.where(sub32 >= s, pltpu.roll(incl, s, axis=0), 0.0)
    starts = incl - seg
    lane32 = _lane_iota(N_EXPERTS)
    stat_ref[0] = jnp.where(lane32 == 0, seg, jnp.where(lane32 == 1, starts, 0.0))

    earlier = (lax.broadcasted_iota(I32, (tm, tm), 0) < lax.broadcasted_iota(I32, (tm, tm), 1)).astype(BF16)
    rank = jnp.dot(member.astype(BF16), earlier, preferred_element_type=F32)
    pos = starts[:, 0:1] + rank
    lps = [jnp.sum(jnp.where(expert == idxs[k], pos, 0.0), axis=0, keepdims=True) for k in range(TOP_K)]
    lp_t = _rows_to_sublanes(lps, 8, tm, fill=-1.0)
    lp_ref[0] = lp_t

    g_hi = [g.astype(BF16).astype(F32) for g in gates]
    g_lo = [g - h for g, h in zip(gates, g_hi)]
    meta_in = _rows_to_sublanes(list(idxs) + g_hi + g_lo, LANES, tm).T
    rhs = jnp.concatenate([h2.astype(BF16), meta_in.astype(BF16)], axis=1)

    sub = 512
    starts32 = jnp.concatenate([starts, jnp.full((LANES - N_EXPERTS, LANES), 1e9, F32)], axis=0).T[0:1, :]
    onehot_rows = _onehot_row_builder(lp_t, tm)

    for c in range(lrows // sub):
        r0 = c * sub
        rid = (lax.broadcasted_iota(I32, (sub, 1), 0) + r0).astype(F32)
        full = jnp.dot(onehot_rows(r0, sub), rhs, preferred_element_type=F32)
        got = full[:, D_MODEL:]
        e_row = jnp.sum(jnp.where(rid >= starts32, 1.0, 0.0), axis=-1, keepdims=True) - 1.0
        g_sum = pltpu.roll(got, LANES - TOP_K, axis=1) + pltpu.roll(got, LANES - 2 * TOP_K, axis=1)
        lane_s = _lane_iota(sub)
        pick = jnp.logical_and(lane_s < TOP_K, got == e_row)
        gate_row = jnp.sum(jnp.where(pick, g_sum, 0.0), axis=-1, keepdims=True)
        gr_hi = gate_row.astype(BF16).astype(F32)
        meta = jnp.where(lane_s == 0, gr_hi, jnp.where(lane_s == 1, gate_row - gr_hi, 0.0))
        xs_ref[r0:r0 + sub, 0:D_MODEL] = full[:, :D_MODEL].astype(BF16)
        xs_ref[r0:r0 + sub, D_MODEL:XS_WIDTH] = meta.astype(BF16)


def _epilogue(x, merged, wout_ref, fnw_ref, wr_ref, br_ref, x1_ref):
    x1 = x + jnp.dot(merged.astype(BF16), wout_ref[...], preferred_element_type=F32)
    x1_ref[...] = x1
    h2 = _rms(x1, fnw_ref[...]).astype(BF16)
    logits_t = lax.dot_general(wr_ref[...].astype(BF16), h2, (((1,), (1,)), ((), ())),
                               preferred_element_type=F32) + br_ref[...]
    idxs, gates = _top4_gates(logits_t)
    return h2, idxs, gates


def _mixer_prompt_kernel(nj, sink_ref, x_ref, anw_ref, win_ref, convw_ref, qnw_ref, knw_ref, bias_ref, wout_ref,
                         fnw_ref, wr_ref, br_ref, h2s_ref, routes_ref,
                         x1_ref, xs_ref, lp_ref, lps_ref, stat_ref, kout_ref, vout_ref, cout_ref,
                         ubuf, q_s, kd_s, vt_s, ya_s, st_s, pt_s):
    i = pl.program_id(0)
    n_tiles = pl.num_programs(0) - 1

    @pl.when(i < n_tiles)
    def _():
        _prompt_tile(lax.rem(i, nj), nj, sink_ref, x_ref, anw_ref, win_ref, convw_ref, qnw_ref, knw_ref, bias_ref,
                     wout_ref, fnw_ref, wr_ref, br_ref, x1_ref, xs_ref, lp_ref, stat_ref, kout_ref, vout_ref,
                     cout_ref, ubuf, q_s, kd_s, vt_s, ya_s, st_s, pt_s)

    @pl.when(i == n_tiles)
    def _():
        route = routes_ref[...]
        _dispatch(h2s_ref[...], [route[k:k + 1, :] for k in range(TOP_K)],
                  [route[TOP_K + k:TOP_K + k + 1, :] for k in range(TOP_K)], xs_ref, lps_ref, stat_ref)


def _prompt_tile(j, nj, sink_ref, x_ref, anw_ref, win_ref, convw_ref, qnw_ref, knw_ref, bias_ref, wout_ref,
                 fnw_ref, wr_ref, br_ref, x1_ref, xs_ref, lp_ref, stat_ref, kout_ref, vout_ref, cout_ref,
                 ubuf, q_s, kd_s, vt_s, ya_s, st_s, pt_s):
    tm = x_ref.shape[0]
    nblk = tm // ATTN_BLOCK
    first_tile = j == 0
    lo = _lo_half()

    x = x_ref[...]
    h = _rms(x, anw_ref[...]).astype(BF16)

    def proj(off, n):
        return jnp.dot(h, win_ref[:, off:off + n], preferred_element_type=F32)

    @pl.when(first_tile)
    def _():
        ubuf[0:8, :] = jnp.zeros((8, D_MODEL), F32)
        kd_s[0:ATTN_BLOCK, :] = jnp.zeros((ATTN_BLOCK, 2 * LANES), BF16)
        vt_s[0] = jnp.zeros((N_KV, LANES, ATTN_BLOCK), BF16)

    u = proj(OFF_CG, D_MODEL) * proj(OFF_XIN, D_MODEL)
    ubuf[8:tm + 8, :] = u
    cw = convw_ref[...]
    conv = ubuf[6:tm + 6, :] * cw[0:1, :] + ubuf[7:tm + 7, :] * cw[1:2, :] + u * cw[2:3, :]
    merged = jax.nn.sigmoid(proj(OFF_GC, D_MODEL)) * (proj(OFF_BG, D_MODEL) * conv)
    tail = ubuf[tm + 6:tm + 8, :]
    cout_ref[0] = tail
    ubuf[6:8, :] = tail

    q = proj(OFF_Q, D_MODEL)
    kv = proj(OFF_K, 2 * LANES)
    k = _pair_norm(kv[:, :LANES], knw_ref[...])
    v = kv[:, LANES:]
    qnw = qnw_ref[...]
    for p in range(N_HEADS // 2):
        sl = slice(p * LANES, (p + 1) * LANES)
        q_s[:, sl] = (_pair_norm(q[:, sl], qnw) * ATTN_SCALE).astype(BF16)
    k_sw = pltpu.roll(k, HEAD_DIM, axis=1)
    v_sw = pltpu.roll(v, HEAD_DIM, axis=1)
    kd_s[ATTN_BLOCK:tm + ATTN_BLOCK, 0:LANES] = jnp.where(lo, k, k_sw).astype(BF16)
    kd_s[ATTN_BLOCK:tm + ATTN_BLOCK, LANES:2 * LANES] = jnp.where(lo, k_sw, k).astype(BF16)
    v_dup = (jnp.where(lo, v, v_sw), jnp.where(lo, v_sw, v))
    for b in range(nblk):
        for g in range(N_KV):
            vt_s[b + 1, g] = v_dup[g][b * ATTN_BLOCK:(b + 1) * ATTN_BLOCK, :].T.astype(BF16)

    @pl.when(j == nj - 1)
    def _():
        kout_ref[0] = k[tm - ATTN_BLOCK:, :]
        vout_ref[0] = v[tm - ATTN_BLOCK:, :]

    prev_rows = lax.broadcasted_iota(I32, (2 * ATTN_BLOCK, 1), 0) < ATTN_BLOCK
    feat_lo = lax.broadcasted_iota(I32, (LANES, 1), 0) < HEAD_DIM

    def attn_block(blk):
        r0 = blk * ATTN_BLOCK
        qb = q_s[r0:r0 + ATTN_BLOCK, :]
        for g in range(N_KV):
            parts = []
            for t in range(GROUP):
                hd = g * GROUP + t
                slab = qb[:, (hd // 2) * LANES:(hd // 2 + 1) * LANES]
                keep = lo if hd % 2 == 0 else jnp.logical_not(lo)
                parts.append(jnp.where(keep, slab, jnp.zeros_like(slab)))
            lhs = jnp.concatenate(parts, axis=0)
            st = lax.dot_general(kd_s[r0:r0 + 2 * ATTN_BLOCK, g * LANES:(g + 1) * LANES], lhs,
                                 (((1,), (1,)), ((), ())), preferred_element_type=F32)
            st = st + bias_ref[g]
            if blk == 0:
                st = jnp.where(jnp.logical_and(prev_rows, first_tile), NEG_INF, st)
            st_s[...] = st
            sink = sink_ref[g:g + 1, :]
            rdens = []
            for t in range(GROUP):
                cols = slice(t * LANES, (t + 1) * LANES)
                s = st_s[:, cols]
                m = jnp.maximum(jnp.max(s, axis=0, keepdims=True), sink[:, cols])
                pr = jnp.exp(s - m)
                den = jnp.sum(pr, axis=0, keepdims=True) + jnp.exp(sink[:, cols] - m)
                pt_s[:, cols] = pr.astype(BF16)
                rdens.append(1.0 / den)
            vt = jnp.concatenate([vt_s[blk, g], vt_s[blk + 1, g]], axis=1)
            ot = jnp.dot(vt, pt_s[...], preferred_element_type=F32)
            for i in range(GROUP // 2):
                pair = g * (GROUP // 2) + i
                even = ot[:, (2 * i) * LANES:(2 * i + 1) * LANES] * rdens[2 * i]
                odd = ot[:, (2 * i + 1) * LANES:(2 * i + 2) * LANES] * rdens[2 * i + 1]
                ya_s[r0:r0 + ATTN_BLOCK, pair * LANES:(pair + 1) * LANES] = jnp.where(feat_lo, even, odd).T

    for blk in range(nblk):
        attn_block(blk)

    kd_s[0:ATTN_BLOCK, :] = kd_s[tm:tm + ATTN_BLOCK, :]
    vt_s[0] = vt_s[nblk]

    merged = merged + jax.nn.sigmoid(proj(OFF_GA, D_MODEL)) * ya_s[...]
    h2, idxs, gates = _epilogue(x_ref[...], merged, wout_ref, fnw_ref, wr_ref, br_ref, x1_ref)
    _dispatch(h2, idxs, gates, xs_ref, lp_ref, stat_ref)


def _const_spec(shape):
    nd = len(shape)
    return pl.BlockSpec(shape, lambda *_: (0,) * nd, pipeline_mode=pl.Buffered(1))


def _mixer_prompt(x, sink_rows, anw, win_bf, convw, qnw, knw, bias_tab, wout_bf, fnw, wr_t, br_col, h2s, routes):
    batch, seq, _ = x.shape
    tm = MIXER_TILE
    lrows = _local_rows(tm)
    nj = seq // tm
    tokens = batch * seq
    n_tiles = batch * nj
    nb = h2s.shape[0]
    x2 = x.reshape(tokens, D_MODEL)
    tile = lambda i: jnp.minimum(i, n_tiles - 1)
    tok_spec = lambda width: pl.BlockSpec((tm, width), lambda i: (tile(i), 0))
    per_batch = lambda rows, width: pl.BlockSpec((1, rows, width), lambda i: (tile(i) // nj, 0, 0))
    in_specs = [
        _const_spec((N_KV, GROUP * ATTN_BLOCK)),
        tok_spec(D_MODEL),
        _const_spec((1, D_MODEL)),
        _const_spec((D_MODEL, IN_DIM)),
        _const_spec((3, D_MODEL)),
        _const_spec((1, LANES)),
        _const_spec((1, LANES)),
        _const_spec((N_KV, 2 * ATTN_BLOCK, GROUP * ATTN_BLOCK)),
        _const_spec((D_MODEL, D_MODEL)),
        _const_spec((1, D_MODEL)),
        _const_spec((N_EXPERTS, D_MODEL)),
        _const_spec((N_EXPERTS, 1)),
        _const_spec((nb, D_MODEL)),
        _const_spec((2 * TOP_K, nb)),
    ]
    out_shape = (
        jax.ShapeDtypeStruct((tokens, D_MODEL), F32),
        jax.ShapeDtypeStruct(((n_tiles + 1) * lrows, XS_WIDTH), BF16),
        jax.ShapeDtypeStruct((n_tiles, 8, tm), F32),
        jax.ShapeDtypeStruct((1, 8, nb), F32),
        jax.ShapeDtypeStruct((n_tiles + 1, N_EXPERTS, LANES), F32),
        jax.ShapeDtypeStruct((batch, ATTN_BLOCK, LANES), F32),
        jax.ShapeDtypeStruct((batch, ATTN_BLOCK, LANES), F32),
        jax.ShapeDtypeStruct((batch, 2, D_MODEL), F32),
    )
    out_specs = (
        tok_spec(D_MODEL),
        pl.BlockSpec((lrows, XS_WIDTH), lambda i: (i, 0)),
        pl.BlockSpec((1, 8, tm), lambda i: (tile(i), 0, 0)),
        pl.BlockSpec((1, 8, nb), lambda i: (0, 0, 0)),
        pl.BlockSpec((1, N_EXPERTS, LANES), lambda i: (i, 0, 0)),
        per_batch(ATTN_BLOCK, LANES), per_batch(ATTN_BLOCK, LANES), per_batch(2, D_MODEL),
    )
    scratch = [
        pltpu.VMEM((tm + 8, D_MODEL), F32),
        pltpu.VMEM((tm, D_MODEL), BF16),
        pltpu.VMEM((tm + ATTN_BLOCK, 2 * LANES), BF16),
        pltpu.VMEM((tm // ATTN_BLOCK + 1, N_KV, LANES, ATTN_BLOCK), BF16),
        pltpu.VMEM((tm, D_MODEL), F32),
        pltpu.VMEM((2 * ATTN_BLOCK, GROUP * ATTN_BLOCK), F32),
        pltpu.VMEM((2 * ATTN_BLOCK, GROUP * ATTN_BLOCK), BF16),
    ]
    return pl.pallas_call(
        functools.partial(_mixer_prompt_kernel, nj),
        grid=(n_tiles + 1,),
        in_specs=in_specs,
        out_specs=out_specs,
        out_shape=out_shape,
        scratch_shapes=scratch,
        compiler_params=pltpu.CompilerParams(dimension_semantics=("arbitrary",), vmem_limit_bytes=VMEM_LIMIT),
        name="mixer_prompt",
    )(sink_rows, x2, anw, win_bf, convw, qnw, knw, bias_tab, wout_bf, fnw, wr_t, br_col, h2s, routes)


SAMPLE_CHUNK = 32
SAMPLE_GROUP = 8


def _mixer_sample_kernel(x_ref, p0_ref, p1_ref, kp_ref, vp_ref, sink_ref, anw_ref, win_ref, convw_ref,
                         qnw_ref, knw_ref, bias_ref, wout_ref, fnw_ref, wr_ref, br_ref,
                         x1_ref, h2_ref, route_ref, unew_ref, kc_ref, vc_ref,
                         qh_s, o_s, kn_s, vn_s, conv_s, ga_s):
    c = pl.program_id(0)
    nb = x_ref.shape[0]
    tc = kp_ref.shape[0]
    lo = _lo_half()

    @pl.when(c == 0)
    def _():
        h = _rms(x_ref[...], anw_ref[...]).astype(BF16)

        def proj(off, n):
            return jnp.dot(h, win_ref[:, off:off + n], preferred_element_type=F32)

        u = proj(OFF_CG, D_MODEL) * proj(OFF_XIN, D_MODEL)
        unew_ref[...] = u
        cw = convw_ref[...]
        conv = p0_ref[...] * cw[0:1, :] + p1_ref[...] * cw[1:2, :] + u * cw[2:3, :]
        conv_s[...] = jax.nn.sigmoid(proj(OFF_GC, D_MODEL)) * (proj(OFF_BG, D_MODEL) * conv)
        ga_s[...] = jax.nn.sigmoid(proj(OFF_GA, D_MODEL))

        q = proj(OFF_Q, D_MODEL)
        kv = proj(OFF_K, 2 * LANES)
        kn_s[...] = _pair_norm(kv[:, :LANES], knw_ref[...])
        vn_s[...] = kv[:, LANES:]

        qnw = qnw_ref[...]
        for hd in range(N_HEADS):
            pair, half, grp = hd // 2, hd % 2, hd // GROUP
            slab = _pair_norm(q[:, pair * LANES:(pair + 1) * LANES], qnw) * ATTN_SCALE
            slab = jnp.where(lo if half == 0 else jnp.logical_not(lo), slab, 0.0)
            if half != grp:
                slab = pltpu.roll(slab, HEAD_DIM, axis=1)
            qh_s[hd * nb:(hd + 1) * nb, :] = slab

    sink = sink_ref[...]
    bias = bias_ref[...]
    rows = lax.broadcasted_iota(I32, (WINDOW, 1), 0)
    row0 = rows == 0
    row_last = rows == WINDOW - 1

    grp = SAMPLE_GROUP
    for ci in range(tc // grp):
        t0 = ci * grp
        qs, ks, vs = [], [], []
        for t in range(t0, t0 + grp):
            b = c * tc + t
            qs.append(qh_s[pl.ds(b, N_HEADS, stride=nb), :])
            k_new = kn_s[pl.ds(b, 1), :]
            v_new = vn_s[pl.ds(b, 1), :]
            k_old = kp_ref[t]
            v_old = vp_ref[t]
            kc_ref[t] = jnp.where(row_last, k_new, pltpu.roll(k_old, WINDOW - 1, axis=0))
            vc_ref[t] = jnp.where(row_last, v_new, pltpu.roll(v_old, WINDOW - 1, axis=0))
            ks.append(jnp.where(row0, k_new, k_old).astype(BF16))
            vs.append(jnp.where(row0, v_new, v_old).astype(BF16))
        q_all = jnp.concatenate(qs, axis=0).astype(BF16)
        s_all = lax.dot_general(q_all, jnp.concatenate(ks, axis=0), (((1,), (1,)), ((), ())),
                                preferred_element_type=F32)
        p_rows, rdens = [], []
        for g in range(grp):
            s = s_all[g * N_HEADS:(g + 1) * N_HEADS, g * WINDOW:(g + 1) * WINDOW] + bias
            m = jnp.maximum(jnp.max(s, axis=-1, keepdims=True), sink)
            pr = jnp.exp(s - m)
            rdens.append(1.0 / (jnp.sum(pr, axis=-1, keepdims=True) + jnp.exp(sink - m)))
            zero = jnp.zeros((N_HEADS, WINDOW), BF16)
            p_rows.append(jnp.concatenate([pr.astype(BF16) if j == g else zero for j in range(grp)], axis=1))
        o_all = jnp.dot(jnp.concatenate(p_rows, axis=0), jnp.concatenate(vs, axis=0),
                        preferred_element_type=F32)
        for g in range(grp):
            b = c * tc + t0 + g
            o_s[pl.ds(b, N_HEADS, stride=nb), :] = o_all[g * N_HEADS:(g + 1) * N_HEADS, :] * rdens[g]

    @pl.when(c == pl.num_programs(0) - 1)
    def _():
        cols = []
        for pair in range(N_HEADS // 2):
            halves = []
            for half in range(2):
                hd = 2 * pair + half
                slab = o_s[hd * nb:(hd + 1) * nb, :]
                if half != hd // GROUP:
                    slab = pltpu.roll(slab, HEAD_DIM, axis=1)
                halves.append(slab)
            cols.append(jnp.where(lo, halves[0], halves[1]))
        merged = conv_s[...] + ga_s[...] * jnp.concatenate(cols, axis=1)
        h2, idxs, gates = _epilogue(x_ref[...], merged, wout_ref, fnw_ref, wr_ref, br_ref, x1_ref)
        h2_ref[...] = h2
        route_ref[...] = _rows_to_sublanes(list(idxs) + list(gates), 2 * TOP_K, nb)


def _mixer_sample(x, p0, p1, k_past, v_past, sink_col, anw, win_bf, convw, qnw, knw, bias_s, wout_bf, fnw, wr, br):
    nb = x.shape[0]
    tc = SAMPLE_CHUNK
    consts = (x, p0, p1)
    params = (sink_col, anw, win_bf, convw, qnw, knw, bias_s, wout_bf, fnw, wr, br)
    cache_spec = pl.BlockSpec((tc, WINDOW, LANES), lambda c: (c, 0, 0))
    full = lambda shape: pl.BlockSpec(shape, lambda c: (0,) * len(shape))
    out_shape = (
        jax.ShapeDtypeStruct((nb, D_MODEL), F32),
        jax.ShapeDtypeStruct((nb, D_MODEL), BF16),
        jax.ShapeDtypeStruct((2 * TOP_K, nb), F32),
        jax.ShapeDtypeStruct((nb, D_MODEL), F32),
        jax.ShapeDtypeStruct((nb, WINDOW, LANES), F32),
        jax.ShapeDtypeStruct((nb, WINDOW, LANES), F32),
    )
    out_specs = tuple(full(s.shape) for s in out_shape[:4]) + (cache_spec, cache_spec)
    scratch = [
        pltpu.VMEM((N_HEADS * nb, LANES), F32),
        pltpu.VMEM((N_HEADS * nb, LANES), F32),
        pltpu.VMEM((nb, LANES), F32),
        pltpu.VMEM((nb, LANES), F32),
        pltpu.VMEM((nb, D_MODEL), F32),
        pltpu.VMEM((nb, D_MODEL), F32),
    ]
    return pl.pallas_call(
        _mixer_sample_kernel,
        grid=(nb // tc,),
        in_specs=[_const_spec(a.shape) for a in consts] + [cache_spec, cache_spec]
        + [_const_spec(a.shape) for a in params],
        out_specs=out_specs,
        out_shape=out_shape,
        scratch_shapes=scratch,
        compiler_params=pltpu.CompilerParams(dimension_semantics=("arbitrary",), vmem_limit_bytes=VMEM_LIMIT),
        name="mixer_sample",
    )(*consts, k_past, v_past, *params)


def _expert_kernel(be_ref, nxt_ref, nact_ref, src_ref, dst_ref, xs_hbm, wup_hbm, bup_ref, wdn_hbm, bdn_ref, ys_hbm,
                   wup_s, wdn_s, perm_s, wup_stage, wdn_stage, xbuf, ybuf, in_sem, out_sem, w_sem):
    i = pl.program_id(0)
    n_real = nact_ref[0]
    n_active = nact_ref[1]
    e = be_ref[i]
    e_prev = be_ref[jnp.maximum(i - 1, 0)]
    slot = lax.rem(i, 2)

    def weight_copies(expert):
        return (pltpu.make_async_copy(wup_hbm.at[expert], wup_stage, w_sem.at[0]),
                pltpu.make_async_copy(wdn_hbm.at[expert], wdn_stage, w_sem.at[1]))

    def gather_copy(blk, slt, c):
        row = pl.multiple_of(src_ref[blk * CHUNKS_PER_BLOCK + c] * CHUNK, CHUNK)
        return pltpu.make_async_copy(xs_hbm.at[pl.ds(row, CHUNK), :],
                                     xbuf.at[slt, pl.ds(c * CHUNK, CHUNK), :], in_sem.at[slt])

    def scatter_chunks(blk, slt, wait):
        base = blk * CHUNKS_PER_BLOCK

        def one(c, dst):
            row = pl.multiple_of(dst * CHUNK, CHUNK)
            cp = pltpu.make_async_copy(ybuf.at[slt, pl.ds(c * CHUNK, CHUNK), :],
                                       ys_hbm.at[pl.ds(row, CHUNK), :], out_sem.at[slt])
            if wait:
                cp.wait()
            else:
                cp.start()

        full = dst_ref[base + CHUNKS_PER_BLOCK - 1] >= 0

        @pl.when(full)
        def _():
            for c in range(CHUNKS_PER_BLOCK):
                one(c, dst_ref[base + c])

        @pl.when(jnp.logical_not(full))
        def _():
            for c in range(CHUNKS_PER_BLOCK - 1):
                dst = dst_ref[base + c]

                @pl.when(dst >= 0)
                def _():
                    one(c, dst)

    @pl.when(jnp.logical_and(i == 0, n_real > 0))
    def _():
        for c in range(CHUNKS_PER_BLOCK):
            gather_copy(0, 0, c).start()

    @pl.when(jnp.logical_and(i == 0, n_real > 0))
    def _():
        for cp in weight_copies(e):
            cp.start()

    @pl.when(jnp.logical_and(i < n_real, jnp.logical_or(i == 0, e != e_prev)))
    def _():
        for cp in weight_copies(e):
            cp.wait()
        wup_s[...] = wup_stage[...].astype(BF16)
        half = LANES // 2
        for cs in range(D_MODEL // LANES):
            cols = slice(cs * LANES, (cs + 1) * LANES)
            for c in range(D_MODEL // LANES):
                for par in range(2):
                    s0 = c * LANES + par * half
                    perm_s[cs, pl.ds(c * LANES + par, half, stride=2), :] = wdn_stage[s0:s0 + half, cols]
            wdn_s[:, cols] = perm_s[cs].astype(BF16)
        nxt = nxt_ref[i]

        @pl.when(nxt < N_EXPERTS)
        def _():
            for cp in weight_copies(nxt):
                cp.start()

    @pl.when(i + 1 < n_real)
    def _():
        for c in range(CHUNKS_PER_BLOCK):
            gather_copy(i + 1, 1 - slot, c).start()

    @pl.when(jnp.logical_and(i >= 2, i < n_active))
    def _():
        scatter_chunks(i - 2, slot, wait=True)

    @pl.when(jnp.logical_and(i >= n_real, i < n_active))
    def _():
        ybuf[slot] = jnp.zeros((EXPERT_BLOCK, D_MODEL), BF16)
        scatter_chunks(i, slot, wait=False)

    def ffn(rows):
        for c in range(CHUNKS_PER_BLOCK):
            gather_copy(i, slot, c).wait()

        xb = xbuf[slot, 0:rows, :]
        meta = xb[:, D_MODEL:].astype(F32)
        gate = meta[:, 0:1] + meta[:, 1:2]
        u = jnp.dot(xb[:, :D_MODEL], wup_s[...], preferred_element_type=F32) + bup_ref[0]
        even = (_lane_iota() & 1) == 0
        cols = []
        for c in range(D_MODEL // LANES):
            c0 = u[:, (2 * c) * LANES:(2 * c + 1) * LANES]
            c1 = u[:, (2 * c + 1) * LANES:(2 * c + 2) * LANES]
            glu = jnp.where(even, c0, pltpu.roll(c1, 1, axis=1))
            lin = jnp.where(even, pltpu.roll(c0, LANES - 1, axis=1), c1)
            glu = jnp.minimum(glu, SWIGLU_LIMIT)
            lin = jnp.clip(lin, -SWIGLU_LIMIT, SWIGLU_LIMIT)
            cols.append(glu * jax.nn.sigmoid(SWIGLU_ALPHA * glu) * (lin + 1.0))
        a = jnp.concatenate(cols, axis=1).astype(BF16)
        y = jnp.dot(a, wdn_s[...], preferred_element_type=F32) + bdn_ref[0]
        ybuf[slot, 0:rows, :] = (y * gate).astype(BF16)
        scatter_chunks(i, slot, wait=False)

    part = FFN_ROWS // CHUNK
    n_parts = EXPERT_BLOCK // FFN_ROWS
    base = i * CHUNKS_PER_BLOCK
    parts_used = 1 + sum((dst_ref[base + q * part] >= 0).astype(I32) for q in range(1, n_parts))
    for q in range(1, n_parts + 1):
        @pl.when(jnp.logical_and(i < n_real, parts_used == q))
        def _(q=q):
            ffn(q * FFN_ROWS)

    @pl.when(i == n_active - 1)
    def _():
        @pl.when(i >= 1)
        def _():
            scatter_chunks(i - 1, 1 - slot, wait=True)

        scatter_chunks(i, slot, wait=True)


def _experts(block_expert, next_expert, n_active, chunk_src, chunk_dst, xs_big, ys_rows, w_up, b_up, w_down, b_down):
    n_blocks = block_expert.shape[0]
    bspec = lambda shape: pl.BlockSpec(
        shape, lambda i, be, nx, na, cs, cd: (jnp.minimum(be[i], N_EXPERTS - 1), 0, 0))
    grid_spec = pltpu.PrefetchScalarGridSpec(
        num_scalar_prefetch=5,
        grid=(n_blocks,),
        in_specs=[
            pl.BlockSpec(memory_space=pl.ANY),
            pl.BlockSpec(memory_space=pl.ANY),
            bspec((1, 1, 2 * D_MODEL)),
            pl.BlockSpec(memory_space=pl.ANY),
            bspec((1, 1, D_MODEL)),
        ],
        out_specs=pl.BlockSpec(memory_space=pl.ANY),
        scratch_shapes=[
            pltpu.VMEM((D_MODEL, 2 * D_MODEL), BF16),
            pltpu.VMEM((D_MODEL, D_MODEL), BF16),
            pltpu.VMEM((D_MODEL // LANES, D_MODEL, LANES), F32),
            pltpu.VMEM((D_MODEL, 2 * D_MODEL), F32),
            pltpu.VMEM((D_MODEL, D_MODEL), F32),
            pltpu.VMEM((2, EXPERT_BLOCK, XS_WIDTH), BF16),
            pltpu.VMEM((2, EXPERT_BLOCK, D_MODEL), BF16),
            pltpu.SemaphoreType.DMA((2,)),
            pltpu.SemaphoreType.DMA((2,)),
            pltpu.SemaphoreType.DMA((2,)),
        ],
    )
    return pl.pallas_call(
        _expert_kernel,
        grid_spec=grid_spec,
        out_shape=jax.ShapeDtypeStruct((ys_rows, D_MODEL), BF16),
        compiler_params=pltpu.CompilerParams(dimension_semantics=("arbitrary",), vmem_limit_bytes=VMEM_LIMIT),
        name="experts",
    )(block_expert, next_expert, n_active, chunk_src, chunk_dst, xs_big, w_up,
      b_up.reshape(N_EXPERTS, 1, 2 * D_MODEL), w_down, b_down.reshape(N_EXPERTS, 1, D_MODEL))


def _combine_kernel(lp_ref, ys_ref, x1_ref, out_ref):
    tm = x1_ref.shape[0]
    lrows = ys_ref.shape[0]
    onehot = _onehot_row_builder(lp_ref[0], tm)(0, lrows)
    y = lax.dot_general(onehot, ys_ref[...], (((0,), (0,)), ((), ())), preferred_element_type=F32)
    out_ref[...] = x1_ref[...] + y


def _combine(lp, ys_big, x1, tm, lrows, first_block):
    tokens = x1.shape[0]
    return pl.pallas_call(
        _combine_kernel,
        grid=(tokens // tm,),
        in_specs=[
            pl.BlockSpec((1, 8, tm), lambda j: (j, 0, 0)),
            pl.BlockSpec((lrows, D_MODEL), lambda j: (first_block + j, 0)),
            pl.BlockSpec((tm, D_MODEL), lambda j: (j, 0)),
        ],
        out_specs=pl.BlockSpec((tm, D_MODEL), lambda j: (j, 0)),
        out_shape=jax.ShapeDtypeStruct((tokens, D_MODEL), F32),
        compiler_params=pltpu.CompilerParams(dimension_semantics=("arbitrary",), vmem_limit_bytes=VMEM_LIMIT),
        name="combine",
    )(lp, ys_big, x1)


def _bias_tables(rel_bias):
    qi = np.arange(ATTN_BLOCK)[:, None]
    kj = np.arange(2 * ATTN_BLOCK)[None, :]
    rel = qi + ATTN_BLOCK - kj
    valid = (rel >= 0) & (rel < WINDOW)
    bucket = np.where(valid, _t5_bucket_np(rel), -1)
    slot_rel = np.where(np.arange(WINDOW) == 0, 0, WINDOW - np.arange(WINDOW))
    slot_bucket = _t5_bucket_np(slot_rel)
    buckets = np.arange(N_BUCKETS)
    onehot = bucket[None, :, :] == buckets[:, None, None]
    tab = jnp.sum(jnp.where(onehot[None], rel_bias.T[:, :, None, None], 0.0), axis=1)
    tab = jnp.where(valid[None], tab, NEG_INF)
    slot_onehot = slot_bucket[None, :] == buckets[:, None]
    sample_tab = jnp.sum(jnp.where(slot_onehot[None], rel_bias.T[:, :, None], 0.0), axis=1)
    prompt_tab = tab.reshape(N_KV, GROUP, ATTN_BLOCK, 2 * ATTN_BLOCK).transpose(0, 3, 1, 2)
    return prompt_tab.reshape(N_KV, 2 * ATTN_BLOCK, GROUP * ATTN_BLOCK), sample_tab


def _chunk_tables(seg, starts, tile_base, tile_rows, n_blocks):
    n_tiles = seg.shape[0]
    n_seg = (N_EXPERTS + 1) * n_tiles
    used = jnp.sum(seg, axis=1)
    seg_e = jnp.concatenate([seg.T, (tile_rows - used)[None, :]], axis=0)
    src0 = jnp.concatenate([tile_base[None, :] + starts.T, (tile_base + used)[None, :]], axis=0).reshape(-1)
    total = jnp.sum(seg_e, axis=1)
    region = (total + EXPERT_BLOCK - 1) // EXPERT_BLOCK * EXPERT_BLOCK
    pad_end = jnp.cumsum(region)
    pad_start = pad_end - region
    g_start = (pad_start[:, None] + jnp.cumsum(seg_e, axis=1) - seg_e).reshape(-1)
    g_end = g_start + seg_e.reshape(-1)
    rows = jnp.arange(n_blocks * CHUNKS_PER_BLOCK, dtype=I32) * CHUNK
    passed = g_end[None, :] <= rows[:, None]

    def at_segment(table, sentinel):
        ext = jnp.concatenate([table, jnp.array([sentinel], I32)])
        return ext[0] + jnp.sum(jnp.where(passed, (ext[1:] - ext[:-1])[None, :], 0), axis=1)

    seg_start = at_segment(g_start, 1 << 30)
    valid = rows >= seg_start
    src_row = at_segment(src0, 0) + rows - seg_start
    chunk_src = jnp.where(valid, src_row // CHUNK, 0).astype(I32)
    chunk_dst = jnp.where(valid, src_row // CHUNK, -1).astype(I32)
    blk_rows = jnp.arange(n_blocks, dtype=I32) * EXPERT_BLOCK
    block_region = jnp.minimum(jnp.sum((pad_end[None, :] <= blk_rows[:, None]).astype(I32), axis=1),
                               N_EXPERTS).astype(I32)
    counts = jnp.stack([pad_end[N_EXPERTS - 1], pad_end[N_EXPERTS]]).astype(I32) // EXPERT_BLOCK
    ridx = jnp.arange(N_EXPERTS + 1, dtype=I32)
    later = jnp.logical_and(ridx[None, :] > ridx[:, None], (region > 0)[None, :])
    next_region = jnp.min(jnp.where(later, ridx[None, :], N_EXPERTS + 1), axis=1)
    next_of_block = jnp.sum(jnp.where(block_region[:, None] == ridx[None, :], next_region[None, :], 0), axis=1)
    return block_region, next_of_block.astype(I32), counts, chunk_src, chunk_dst


def kernel(x_prompt, x_sample, state_conv, cache_k_win, cache_v_win, rel_bias, attn_norm_w, w_in, conv_w,
           q_norm_w, k_norm_w, sinks, w_out, ffn_norm_w, w_router, b_router, w_up, b_up, w_down, b_down):
    batch, seq, _ = x_prompt.shape
    nb = x_sample.shape[0]
    anw = attn_norm_w[0].reshape(1, D_MODEL)
    fnw = ffn_norm_w[0].reshape(1, D_MODEL)
    win_bf = w_in[0].astype(BF16)
    wout_bf = w_out[0].astype(BF16)
    qnw = jnp.tile(q_norm_w[0], 2).reshape(1, LANES)
    knw = jnp.tile(k_norm_w[0], 2).reshape(1, LANES)
    wr_t = w_router[0].T
    br = b_router[0].reshape(N_EXPERTS, 1)
    prompt_tab, sample_tab = _bias_tables(rel_bias)

    k_past = cache_k_win[0].reshape(nb, WINDOW, LANES)
    v_past = cache_v_win[0].reshape(nb, WINDOW, LANES)
    x1s, h2s, routes, unew, new_k_sample, new_v_sample = _mixer_sample(
        x_sample.reshape(nb, D_MODEL), state_conv[0, :, 0, :], state_conv[0, :, 1, :], k_past, v_past,
        sinks[0].reshape(N_HEADS, 1), anw, win_bf, conv_w[0], qnw, knw, sample_tab, wout_bf, fnw, wr_t, br)

    sink_rows = jnp.repeat(sinks[0].reshape(N_KV, GROUP), ATTN_BLOCK, axis=1)
    x1p, xs_big, lpp, lps, stat, kp, vp, cp = _mixer_prompt(
        x_prompt, sink_rows, anw, win_bf, conv_w[0], qnw, knw, prompt_tab, wout_bf, fnw, wr_t, br, h2s, routes)

    n_tiles = stat.shape[0]
    lrows = _local_rows(MIXER_TILE)
    xs_rows = n_tiles * lrows
    seg = stat[:, :, 0].astype(I32)
    starts = stat[:, :, 1].astype(I32)
    tile_base = jnp.arange(n_tiles, dtype=I32) * lrows
    tile_rows = jnp.full((n_tiles,), lrows, I32)
    n_blocks = -(-(xs_rows + (N_EXPERTS + 1) * (EXPERT_BLOCK - 1)) // EXPERT_BLOCK)
    block_expert, next_expert, n_active, chunk_src, chunk_dst = _chunk_tables(seg, starts, tile_base, tile_rows,
                                                                              n_blocks)

    ys_big = _experts(block_expert, next_expert, n_active, chunk_src, chunk_dst, xs_big, xs_rows, w_up[0], b_up[0],
                      w_down[0], b_down[0])

    y_prompt = _combine(lpp, ys_big, x1p, MIXER_TILE, lrows, 0).reshape(batch, seq, D_MODEL)
    y_sample = _combine(lps, ys_big, x1s, nb, lrows, n_tiles - 1).reshape(nb, 1, D_MODEL)

    new_conv_sample = jnp.stack([state_conv[0, :, 1, :], unew], axis=1)
    return (
        y_prompt,
        y_sample,
        cp[None],
        kp.reshape(1, batch, ATTN_BLOCK, N_KV, HEAD_DIM),
        vp.reshape(1, batch, ATTN_BLOCK, N_KV, HEAD_DIM),
        new_conv_sample[None],
        new_k_sample.reshape(1, nb, WINDOW, N_KV, HEAD_DIM),
        new_v_sample.reshape(1, nb, WINDOW, N_KV, HEAD_DIM),
    )
```

```python
import functools
import math

import numpy as np
import jax
import jax.numpy as jnp
from jax import lax
from jax.experimental import pallas as pl
from jax.experimental.pallas import tpu as pltpu

F32 = jnp.float32
BF16 = jnp.bfloat16
I32 = jnp.int32

D_MODEL = 1024
HEAD_DIM = 64
N_HEADS = 16
N_KV = 2
GROUP = N_HEADS // N_KV
WINDOW = 128
ATTN_BLOCK = 128
N_BUCKETS = 32
MAX_DISTANCE = 128
NEG_INF = -1e30
N_EXPERTS = 32
TOP_K = 4
SWIGLU_ALPHA = 1.702
SWIGLU_LIMIT = 7.0
EPS = 1e-5
ATTN_SCALE = HEAD_DIM ** -0.5

OFF_XIN, OFF_BG, OFF_CG, OFF_Q = 0, 1024, 2048, 3072
OFF_K, OFF_V, OFF_GC, OFF_GA = 4096, 4224, 4352, 5376
IN_DIM = 6400

LANES = 128
MIXER_TILE = 512
EXPERT_BLOCK = 512
FFN_ROWS = 128
CHUNK = 16
CHUNKS_PER_BLOCK = EXPERT_BLOCK // CHUNK
XS_WIDTH = D_MODEL + LANES
VMEM_LIMIT = 60 * 1024 * 1024


def _local_rows(tm):
    need = tm * TOP_K + N_EXPERTS * (CHUNK - 1)
    return -(-need // 512) * 512


def _t5_bucket_np(rel):
    n = np.maximum(rel, 0)
    max_exact = N_BUCKETS // 2
    nf = np.maximum(n, 1).astype(np.float64)
    large = max_exact + (np.log(nf / max_exact) / math.log(MAX_DISTANCE / max_exact)
                         * (N_BUCKETS - max_exact)).astype(np.int32)
    large = np.minimum(large, N_BUCKETS - 1)
    return np.where(n < max_exact, n, large).astype(np.int32)


def _rms(x, w):
    return x * lax.rsqrt(jnp.mean(x * x, axis=-1, keepdims=True) + EPS) * w


def _lane_iota(rows=1):
    return lax.broadcasted_iota(I32, (rows, LANES), 1)


def _lo_half():
    return _lane_iota() < HEAD_DIM


def _pair_norm(t, w128):
    lo = _lo_half()
    sq = t * t
    s_lo = jnp.sum(jnp.where(lo, sq, 0.0), axis=-1, keepdims=True)
    s_hi = jnp.sum(jnp.where(lo, 0.0, sq), axis=-1, keepdims=True)
    r = jnp.where(lo, lax.rsqrt(s_lo * (1.0 / HEAD_DIM) + EPS), lax.rsqrt(s_hi * (1.0 / HEAD_DIM) + EPS))
    return t * r * w128


def _top4_gates(logits_t):
    tokens = logits_t.shape[1]
    expert = lax.broadcasted_iota(I32, (N_EXPERTS, tokens), 0).astype(F32)
    vals, idxs = [], []
    l = logits_t
    for _ in range(TOP_K):
        m = jnp.max(l, axis=0, keepdims=True)
        idx = jnp.min(jnp.where(l == m, expert, float(N_EXPERTS)), axis=0, keepdims=True)
        vals.append(m)
        idxs.append(idx)
        l = jnp.where(expert == idx, -jnp.inf, l)
    es = [jnp.exp(v - vals[0]) for v in vals]
    den = es[0] + es[1] + es[2] + es[3]
    return idxs, [e / den for e in es]


def _rows_to_sublanes(rows, n_rows, tokens, fill=0.0):
    sub = lax.broadcasted_iota(I32, (n_rows, tokens), 0)
    out = jnp.full((n_rows, tokens), fill, F32)
    for k, r in enumerate(rows):
        out = jnp.where(sub == k, r, out)
    return out


def _onehot_row_builder(lp_t, tm):
    lp_group = jnp.floor(lp_t * (1.0 / LANES))
    lp_off = lp_t - lp_group * LANES
    row_off = lax.broadcasted_iota(I32, (LANES, tm), 0).astype(F32).astype(BF16)
    one_bf = jnp.ones((LANES, tm), BF16)
    zero_bf = jnp.zeros((LANES, tm), BF16)

    def build(r0, rows):
        parts = []
        for s in range(rows // LANES):
            group = float(r0 // LANES + s)
            hit = None
            for k in range(TOP_K):
                off_k = jnp.where(lp_group[k:k + 1, :] == group, lp_off[k:k + 1, :], -1.0).astype(BF16)
                hit_k = row_off == off_k
                hit = hit_k if hit is None else jnp.logical_or(hit, hit_k)
            parts.append(jnp.where(hit, one_bf, zero_bf))
        return jnp.concatenate(parts, axis=0)

    return build


def _dispatch(h2, idxs, gates, xs_ref, lp_ref, stat_ref):
    tm = h2.shape[0]
    lrows = xs_ref.shape[0]
    expert = lax.broadcasted_iota(I32, (N_EXPERTS, tm), 0).astype(F32)
    member = jnp.zeros((N_EXPERTS, tm), F32)
    for k in range(TOP_K):
        member = member + jnp.where(expert == idxs[k], 1.0, 0.0)
    cnt = jnp.broadcast_to(jnp.sum(member, axis=1, keepdims=True), (N_EXPERTS, LANES))
    seg = jnp.floor((cnt + (CHUNK - 1)) * (1.0 / CHUNK)) * CHUNK
    sub32 = lax.broadcasted_iota(I32, (N_EXPERTS, LANES), 0)
    incl = seg
    for s in (1, 2, 4, 8, 16):
        incl = incl + jnp.where(sub32 >= s, pltpu.roll(incl, s, axis=0), 0.0)
    starts = incl - seg
    lane32 = _lane_iota(N_EXPERTS)
    stat_ref[0] = jnp.where(lane32 == 0, seg, jnp.where(lane32 == 1, starts, 0.0))

    earlier = (lax.broadcasted_iota(I32, (tm, tm), 0) < lax.broadcasted_iota(I32, (tm, tm), 1)).astype(BF16)
    rank = jnp.dot(member.astype(BF16), earlier, preferred_element_type=F32)
    pos = starts[:, 0:1] + rank
    lps = [jnp.sum(jnp.where(expert == idxs[k], pos, 0.0), axis=0, keepdims=True) for k in range(TOP_K)]
    lp_t = _rows_to_sublanes(lps, 8, tm, fill=-1.0)
    lp_ref[0] = lp_t

    g_hi = [g.astype(BF16).astype(F32) for g in gates]
    g_lo = [g - h for g, h in zip(gates, g_hi)]
    meta_in = _rows_to_sublanes(list(idxs) + g_hi + g_lo, LANES, tm).T
    rhs = jnp.concatenate([h2.astype(BF16), meta_in.astype(BF16)], axis=1)

    sub = 512
    starts32 = jnp.concatenate([starts, jnp.full((LANES - N_EXPERTS, LANES), 1e9, F32)], axis=0).T[0:1, :]
    onehot_rows = _onehot_row_builder(lp_t, tm)

    for c in range(lrows // sub):
        r0 = c * sub
        rid = (lax.broadcasted_iota(I32, (sub, 1), 0) + r0).astype(F32)
        full = jnp.dot(onehot_rows(r0, sub), rhs, preferred_element_type=F32)
        got = full[:, D_MODEL:]
        e_row = jnp.sum(jnp.where(rid >= starts32, 1.0, 0.0), axis=-1, keepdims=True) - 1.0
        g_sum = pltpu.roll(got, LANES - TOP_K, axis=1) + pltpu.roll(got, LANES - 2 * TOP_K, axis=1)
        lane_s = _lane_iota(sub)
        pick = jnp.logical_and(lane_s < TOP_K, got == e_row)
        gate_row = jnp.sum(jnp.where(pick, g_sum, 0.0), axis=-1, keepdims=True)
        gr_hi = gate_row.astype(BF16).astype(F32)
        meta = jnp.where(lane_s == 0, gr_hi, jnp.where(lane_s == 1, gate_row - gr_hi, 0.0))
        xs_ref[r0:r0 + sub, 0:D_MODEL] = full[:, :D_MODEL].astype(BF16)
        xs_ref[r0:r0 + sub, D_MODEL:XS_WIDTH] = meta.astype(BF16)


def _epilogue(x, merged, wout_ref, fnw_ref, wr_ref, br_ref, x1_ref):
    x1 = x + jnp.dot(merged.astype(BF16), wout_ref[...], preferred_element_type=F32)
    x1_ref[...] = x1
    h2 = _rms(x1, fnw_ref[...]).astype(BF16)
    logits_t = lax.dot_general(wr_ref[...].astype(BF16), h2, (((1,), (1,)), ((), ())),
                               preferred_element_type=F32) + br_ref[...]
    idxs, gates = _top4_gates(logits_t)
    return h2, idxs, gates


def _mixer_prompt_kernel(nj, sink_ref, x_ref, anw_ref, win_ref, convw_ref, qnw_ref, knw_ref, bias_ref, wout_ref,
                         fnw_ref, wr_ref, br_ref, h2s_ref, routes_ref,
                         x1_ref, xs_ref, lp_ref, lps_ref, stat_ref, kout_ref, vout_ref, cout_ref,
                         ubuf, q_s, kd_s, vt_s, ya_s, st_s, pt_s):
    i = pl.program_id(0)
    n_tiles = pl.num_programs(0) - 1

    @pl.when(i < n_tiles)
    def _():
        _prompt_tile(lax.rem(i, nj), nj, sink_ref, x_ref, anw_ref, win_ref, convw_ref, qnw_ref, knw_ref, bias_ref,
                     wout_ref, fnw_ref, wr_ref, br_ref, x1_ref, xs_ref, lp_ref, stat_ref, kout_ref, vout_ref,
                     cout_ref, ubuf, q_s, kd_s, vt_s, ya_s, st_s, pt_s)

    @pl.when(i == n_tiles)
    def _():
        route = routes_ref[...]
        _dispatch(h2s_ref[...], [route[k:k + 1, :] for k in range(TOP_K)],
                  [route[TOP_K + k:TOP_K + k + 1, :] for k in range(TOP_K)], xs_ref, lps_ref, stat_ref)


def _prompt_tile(j, nj, sink_ref, x_ref, anw_ref, win_ref, convw_ref, qnw_ref, knw_ref, bias_ref, wout_ref,
                 fnw_ref, wr_ref, br_ref, x1_ref, xs_ref, lp_ref, stat_ref, kout_ref, vout_ref, cout_ref,
                 ubuf, q_s, kd_s, vt_s, ya_s, st_s, pt_s):
    tm = x_ref.shape[0]
    nblk = tm // ATTN_BLOCK
    first_tile = j == 0
    lo = _lo_half()

    x = x_ref[...]
    h = _rms(x, anw_ref[...]).astype(BF16)

    def proj(off, n):
        return jnp.dot(h, win_ref[:, off:off + n], preferred_element_type=F32)

    @pl.when(first_tile)
    def _():
        ubuf[0:8, :] = jnp.zeros((8, D_MODEL), F32)
        kd_s[0:ATTN_BLOCK, :] = jnp.zeros((ATTN_BLOCK, 2 * LANES), BF16)
        vt_s[0] = jnp.zeros((N_KV, LANES, ATTN_BLOCK), BF16)

    q = proj(OFF_Q, D_MODEL)
    kv = proj(OFF_K, 2 * LANES)
    k = _pair_norm(kv[:, :LANES], knw_ref[...])
    v = kv[:, LANES:]
    qnw = qnw_ref[...]
    for p in range(N_HEADS // 2):
        sl = slice(p * LANES, (p + 1) * LANES)
        q_s[:, sl] = (_pair_norm(q[:, sl], qnw) * ATTN_SCALE).astype(BF16)
    k_sw = pltpu.roll(k, HEAD_DIM, axis=1)
    v_sw = pltpu.roll(v, HEAD_DIM, axis=1)
    kd_s[ATTN_BLOCK:tm + ATTN_BLOCK, 0:LANES] = jnp.where(lo, k, k_sw).astype(BF16)
    kd_s[ATTN_BLOCK:tm + ATTN_BLOCK, LANES:2 * LANES] = jnp.where(lo, k_sw, k).astype(BF16)
    v_dup = (jnp.where(lo, v, v_sw), jnp.where(lo, v_sw, v))
    for b in range(nblk):
        for g in range(N_KV):
            vt_s[b + 1, g] = v_dup[g][b * ATTN_BLOCK:(b + 1) * ATTN_BLOCK, :].T.astype(BF16)

    u = proj(OFF_CG, D_MODEL) * proj(OFF_XIN, D_MODEL)
    ubuf[8:tm + 8, :] = u
    cw = convw_ref[...]
    conv = ubuf[6:tm + 6, :] * cw[0:1, :] + ubuf[7:tm + 7, :] * cw[1:2, :] + u * cw[2:3, :]
    merged = jax.nn.sigmoid(proj(OFF_GC, D_MODEL)) * (proj(OFF_BG, D_MODEL) * conv)
    tail = ubuf[tm + 6:tm + 8, :]
    cout_ref[0] = tail
    ubuf[6:8, :] = tail

    @pl.when(j == nj - 1)
    def _():
        kout_ref[0] = k[tm - ATTN_BLOCK:, :]
        vout_ref[0] = v[tm - ATTN_BLOCK:, :]

    prev_rows = lax.broadcasted_iota(I32, (2 * ATTN_BLOCK, 1), 0) < ATTN_BLOCK
    feat_lo = lax.broadcasted_iota(I32, (LANES, 1), 0) < HEAD_DIM

    units = [(blk, g) for blk in range(nblk) for g in range(N_KV)]

    def scores(n):
        blk, g = units[n]
        r0 = blk * ATTN_BLOCK
        qb = q_s[r0:r0 + ATTN_BLOCK, :]
        parts = []
        for t in range(GROUP):
            hd = g * GROUP + t
            slab = qb[:, (hd // 2) * LANES:(hd // 2 + 1) * LANES]
            keep = lo if hd % 2 == 0 else jnp.logical_not(lo)
            parts.append(jnp.where(keep, slab, jnp.zeros_like(slab)))
        lhs = jnp.concatenate(parts, axis=0)
        st = lax.dot_general(kd_s[r0:r0 + 2 * ATTN_BLOCK, g * LANES:(g + 1) * LANES], lhs,
                             (((1,), (1,)), ((), ())), preferred_element_type=F32)
        st = st + bias_ref[g]
        if blk == 0:
            st = jnp.where(jnp.logical_and(prev_rows, first_tile), NEG_INF, st)
        st_s[n % 2] = st

    def softmax_values(n):
        blk, g = units[n]
        r0 = blk * ATTN_BLOCK
        sink = sink_ref[g:g + 1, :]
        rdens = []
        for t in range(GROUP):
            cols = slice(t * LANES, (t + 1) * LANES)
            s = st_s[n % 2, :, cols]
            m = jnp.maximum(jnp.max(s, axis=0, keepdims=True), sink[:, cols])
            pr = jnp.exp(s - m)
            den = jnp.sum(pr, axis=0, keepdims=True) + jnp.exp(sink[:, cols] - m)
            pt_s[:, cols] = pr.astype(BF16)
            rdens.append(1.0 / den)
        vt = jnp.concatenate([vt_s[blk, g], vt_s[blk + 1, g]], axis=1)
        ot = jnp.dot(vt, pt_s[...], preferred_element_type=F32)
        for i in range(GROUP // 2):
            pair = g * (GROUP // 2) + i
            even = ot[:, (2 * i) * LANES:(2 * i + 1) * LANES] * rdens[2 * i]
            odd = ot[:, (2 * i + 1) * LANES:(2 * i + 2) * LANES] * rdens[2 * i + 1]
            ya_s[r0:r0 + ATTN_BLOCK, pair * LANES:(pair + 1) * LANES] = jnp.where(feat_lo, even, odd).T

    ga_parts = []
    ga_cols = D_MODEL // (len(units) // 2)
    scores(0)
    for n in range(len(units)):
        if n + 1 < len(units):
            scores(n + 1)
        if n % 2 == 0:
            ga_parts.append(jax.nn.sigmoid(proj(OFF_GA + (n // 2) * ga_cols, ga_cols)))
        softmax_values(n)

    kd_s[0:ATTN_BLOCK, :] = kd_s[tm:tm + ATTN_BLOCK, :]
    vt_s[0] = vt_s[nblk]

    merged = merged + jnp.concatenate(ga_parts, axis=1) * ya_s[...]
    h2, idxs, gates = _epilogue(x_ref[...], merged, wout_ref, fnw_ref, wr_ref, br_ref, x1_ref)
    _dispatch(h2, idxs, gates, xs_ref, lp_ref, stat_ref)


def _const_spec(shape):
    nd = len(shape)
    return pl.BlockSpec(shape, lambda *_: (0,) * nd, pipeline_mode=pl.Buffered(1))


def _mixer_prompt(x, sink_rows, anw, win_bf, convw, qnw, knw, bias_tab, wout_bf, fnw, wr_t, br_col, h2s, routes):
    batch, seq, _ = x.shape
    tm = MIXER_TILE
    lrows = _local_rows(tm)
    nj = seq // tm
    tokens = batch * seq
    n_tiles = batch * nj
    nb = h2s.shape[0]
    x2 = x.reshape(tokens, D_MODEL)
    tile = lambda i: jnp.minimum(i, n_tiles - 1)
    tok_spec = lambda width: pl.BlockSpec((tm, width), lambda i: (tile(i), 0))
    per_batch = lambda rows, width: pl.BlockSpec((1, rows, width), lambda i: (tile(i) // nj, 0, 0))
    in_specs = [
        _const_spec((N_KV, GROUP * ATTN_BLOCK)),
        tok_spec(D_MODEL),
        _const_spec((1, D_MODEL)),
        _const_spec((D_MODEL, IN_DIM)),
        _const_spec((3, D_MODEL)),
        _const_spec((1, LANES)),
        _const_spec((1, LANES)),
        _const_spec((N_KV, 2 * ATTN_BLOCK, GROUP * ATTN_BLOCK)),
        _const_spec((D_MODEL, D_MODEL)),
        _const_spec((1, D_MODEL)),
        _const_spec((N_EXPERTS, D_MODEL)),
        _const_spec((N_EXPERTS, 1)),
        _const_spec((nb, D_MODEL)),
        _const_spec((2 * TOP_K, nb)),
    ]
    out_shape = (
        jax.ShapeDtypeStruct((tokens, D_MODEL), F32),
        jax.ShapeDtypeStruct(((n_tiles + 1) * lrows, XS_WIDTH), BF16),
        jax.ShapeDtypeStruct((n_tiles, 8, tm), F32),
        jax.ShapeDtypeStruct((1, 8, nb), F32),
        jax.ShapeDtypeStruct((n_tiles + 1, N_EXPERTS, LANES), F32),
        jax.ShapeDtypeStruct((batch, ATTN_BLOCK, LANES), F32),
        jax.ShapeDtypeStruct((batch, ATTN_BLOCK, LANES), F32),
        jax.ShapeDtypeStruct((batch, 2, D_MODEL), F32),
    )
    out_specs = (
        tok_spec(D_MODEL),
        pl.BlockSpec((lrows, XS_WIDTH), lambda i: (i, 0)),
        pl.BlockSpec((1, 8, tm), lambda i: (tile(i), 0, 0)),
        pl.BlockSpec((1, 8, nb), lambda i: (0, 0, 0)),
        pl.BlockSpec((1, N_EXPERTS, LANES), lambda i: (i, 0, 0)),
        per_batch(ATTN_BLOCK, LANES), per_batch(ATTN_BLOCK, LANES), per_batch(2, D_MODEL),
    )
    scratch = [
        pltpu.VMEM((tm + 8, D_MODEL), F32),
        pltpu.VMEM((tm, D_MODEL), BF16),
        pltpu.VMEM((tm + ATTN_BLOCK, 2 * LANES), BF16),
        pltpu.VMEM((tm // ATTN_BLOCK + 1, N_KV, LANES, ATTN_BLOCK), BF16),
        pltpu.VMEM((tm, D_MODEL), F32),
        pltpu.VMEM((2, 2 * ATTN_BLOCK, GROUP * ATTN_BLOCK), F32),
        pltpu.VMEM((2 * ATTN_BLOCK, GROUP * ATTN_BLOCK), BF16),
    ]
    return pl.pallas_call(
        functools.partial(_mixer_prompt_kernel, nj),
        grid=(n_tiles + 1,),
        in_specs=in_specs,
        out_specs=out_specs,
        out_shape=out_shape,
        scratch_shapes=scratch,
        compiler_params=pltpu.CompilerParams(dimension_semantics=("arbitrary",), vmem_limit_bytes=VMEM_LIMIT),
        name="mixer_prompt",
    )(sink_rows, x2, anw, win_bf, convw, qnw, knw, bias_tab, wout_bf, fnw, wr_t, br_col, h2s, routes)


SAMPLE_CHUNK = 32
SAMPLE_GROUP = 8


def _mixer_sample_kernel(x_ref, p0_ref, p1_ref, kp_ref, vp_ref, sink_ref, anw_ref, win_ref, convw_ref,
                         qnw_ref, knw_ref, bias_ref, wout_ref, fnw_ref, wr_ref, br_ref,
                         x1_ref, h2_ref, route_ref, unew_ref, kc_ref, vc_ref,
                         qh_s, o_s, kn_s, vn_s, conv_s, ga_s):
    c = pl.program_id(0)
    nb = x_ref.shape[0]
    tc = kp_ref.shape[0]
    lo = _lo_half()

    @pl.when(c == 0)
    def _():
        h = _rms(x_ref[...], anw_ref[...]).astype(BF16)

        def proj(off, n):
            return jnp.dot(h, win_ref[:, off:off + n], preferred_element_type=F32)

        u = proj(OFF_CG, D_MODEL) * proj(OFF_XIN, D_MODEL)
        unew_ref[...] = u
        cw = convw_ref[...]
        conv = p0_ref[...] * cw[0:1, :] + p1_ref[...] * cw[1:2, :] + u * cw[2:3, :]
        conv_s[...] = jax.nn.sigmoid(proj(OFF_GC, D_MODEL)) * (proj(OFF_BG, D_MODEL) * conv)
        ga_s[...] = jax.nn.sigmoid(proj(OFF_GA, D_MODEL))

        q = proj(OFF_Q, D_MODEL)
        kv = proj(OFF_K, 2 * LANES)
        kn_s[...] = _pair_norm(kv[:, :LANES], knw_ref[...])
        vn_s[...] = kv[:, LANES:]

        qnw = qnw_ref[...]
        for hd in range(N_HEADS):
            pair, half, grp = hd // 2, hd % 2, hd // GROUP
            slab = _pair_norm(q[:, pair * LANES:(pair + 1) * LANES], qnw) * ATTN_SCALE
            slab = jnp.where(lo if half == 0 else jnp.logical_not(lo), slab, 0.0)
            if half != grp:
                slab = pltpu.roll(slab, HEAD_DIM, axis=1)
            qh_s[hd * nb:(hd + 1) * nb, :] = slab

    sink = sink_ref[...]
    bias = bias_ref[...]
    rows = lax.broadcasted_iota(I32, (WINDOW, 1), 0)
    row0 = rows == 0
    row_last = rows == WINDOW - 1

    grp = SAMPLE_GROUP
    for ci in range(tc // grp):
        t0 = ci * grp
        qs, ks, vs = [], [], []
        for t in range(t0, t0 + grp):
            b = c * tc + t
            qs.append(qh_s[pl.ds(b, N_HEADS, stride=nb), :])
            k_new = kn_s[pl.ds(b, 1), :]
            v_new = vn_s[pl.ds(b, 1), :]
            k_old = kp_ref[t]
            v_old = vp_ref[t]
            kc_ref[t] = jnp.where(row_last, k_new, pltpu.roll(k_old, WINDOW - 1, axis=0))
            vc_ref[t] = jnp.where(row_last, v_new, pltpu.roll(v_old, WINDOW - 1, axis=0))
            ks.append(jnp.where(row0, k_new, k_old).astype(BF16))
            vs.append(jnp.where(row0, v_new, v_old).astype(BF16))
        q_all = jnp.concatenate(qs, axis=0).astype(BF16)
        s_all = lax.dot_general(q_all, jnp.concatenate(ks, axis=0), (((1,), (1,)), ((), ())),
                                preferred_element_type=F32)
        p_rows, rdens = [], []
        for g in range(grp):
            s = s_all[g * N_HEADS:(g + 1) * N_HEADS, g * WINDOW:(g + 1) * WINDOW] + bias
            m = jnp.maximum(jnp.max(s, axis=-1, keepdims=True), sink)
            pr = jnp.exp(s - m)
            rdens.append(1.0 / (jnp.sum(pr, axis=-1, keepdims=True) + jnp.exp(sink - m)))
            zero = jnp.zeros((N_HEADS, WINDOW), BF16)
            p_rows.append(jnp.concatenate([pr.astype(BF16) if j == g else zero for j in range(grp)], axis=1))
        o_all = jnp.dot(jnp.concatenate(p_rows, axis=0), jnp.concatenate(vs, axis=0),
                        preferred_element_type=F32)
        for g in range(grp):
            b = c * tc + t0 + g
            o_s[pl.ds(b, N_HEADS, stride=nb), :] = o_all[g * N_HEADS:(g + 1) * N_HEADS, :] * rdens[g]

    @pl.when(c == pl.num_programs(0) - 1)
    def _():
        cols = []
        for pair in range(N_HEADS // 2):
            halves = []
            for half in range(2):
                hd = 2 * pair + half
                slab = o_s[hd * nb:(hd + 1) * nb, :]
                if half != hd // GROUP:
                    slab = pltpu.roll(slab, HEAD_DIM, axis=1)
                halves.append(slab)
            cols.append(jnp.where(lo, halves[0], halves[1]))
        merged = conv_s[...] + ga_s[...] * jnp.concatenate(cols, axis=1)
        h2, idxs, gates = _epilogue(x_ref[...], merged, wout_ref, fnw_ref, wr_ref, br_ref, x1_ref)
        h2_ref[...] = h2
        route_ref[...] = _rows_to_sublanes(list(idxs) + list(gates), 2 * TOP_K, nb)


def _mixer_sample(x, p0, p1, k_past, v_past, sink_col, anw, win_bf, convw, qnw, knw, bias_s, wout_bf, fnw, wr, br):
    nb = x.shape[0]
    tc = SAMPLE_CHUNK
    consts = (x, p0, p1)
    params = (sink_col, anw, win_bf, convw, qnw, knw, bias_s, wout_bf, fnw, wr, br)
    cache_spec = pl.BlockSpec((tc, WINDOW, LANES), lambda c: (c, 0, 0))
    full = lambda shape: pl.BlockSpec(shape, lambda c: (0,) * len(shape))
    out_shape = (
        jax.ShapeDtypeStruct((nb, D_MODEL), F32),
        jax.ShapeDtypeStruct((nb, D_MODEL), BF16),
        jax.ShapeDtypeStruct((2 * TOP_K, nb), F32),
        jax.ShapeDtypeStruct((nb, D_MODEL), F32),
        jax.ShapeDtypeStruct((nb, WINDOW, LANES), F32),
        jax.ShapeDtypeStruct((nb, WINDOW, LANES), F32),
    )
    out_specs = tuple(full(s.shape) for s in out_shape[:4]) + (cache_spec, cache_spec)
    scratch = [
        pltpu.VMEM((N_HEADS * nb, LANES), F32),
        pltpu.VMEM((N_HEADS * nb, LANES), F32),
        pltpu.VMEM((nb, LANES), F32),
        pltpu.VMEM((nb, LANES), F32),
        pltpu.VMEM((nb, D_MODEL), F32),
        pltpu.VMEM((nb, D_MODEL), F32),
    ]
    return pl.pallas_call(
        _mixer_sample_kernel,
        grid=(nb // tc,),
        in_specs=[_const_spec(a.shape) for a in consts] + [cache_spec, cache_spec]
        + [_const_spec(a.shape) for a in params],
        out_specs=out_specs,
        out_shape=out_shape,
        scratch_shapes=scratch,
        compiler_params=pltpu.CompilerParams(dimension_semantics=("arbitrary",), vmem_limit_bytes=VMEM_LIMIT),
        name="mixer_sample",
    )(*consts, k_past, v_past, *params)


def _expert_kernel(be_ref, nxt_ref, nact_ref, src_ref, dst_ref, xs_hbm, wup_hbm, bup_ref, wdn_hbm, bdn_ref, ys_hbm,
                   wup_s, wdn_s, perm_s, wup_stage, wdn_stage, xbuf, ybuf, in_sem, out_sem, w_sem):
    i = pl.program_id(0)
    n_real = nact_ref[0]
    n_active = nact_ref[1]
    e = be_ref[i]
    e_prev = be_ref[jnp.maximum(i - 1, 0)]
    slot = lax.rem(i, 2)

    def weight_copies(expert):
        return (pltpu.make_async_copy(wup_hbm.at[expert], wup_stage, w_sem.at[0]),
                pltpu.make_async_copy(wdn_hbm.at[expert], wdn_stage, w_sem.at[1]))

    def gather_copy(blk, slt, c):
        row = pl.multiple_of(src_ref[blk * CHUNKS_PER_BLOCK + c] * CHUNK, CHUNK)
        return pltpu.make_async_copy(xs_hbm.at[pl.ds(row, CHUNK), :],
                                     xbuf.at[slt, pl.ds(c * CHUNK, CHUNK), :], in_sem.at[slt])

    def scatter_chunks(blk, slt, wait):
        base = blk * CHUNKS_PER_BLOCK

        def one(c, dst):
            row = pl.multiple_of(dst * CHUNK, CHUNK)
            cp = pltpu.make_async_copy(ybuf.at[slt, pl.ds(c * CHUNK, CHUNK), :],
                                       ys_hbm.at[pl.ds(row, CHUNK), :], out_sem.at[slt])
            if wait:
                cp.wait()
            else:
                cp.start()

        full = dst_ref[base + CHUNKS_PER_BLOCK - 1] >= 0

        @pl.when(full)
        def _():
            for c in range(CHUNKS_PER_BLOCK):
                one(c, dst_ref[base + c])

        @pl.when(jnp.logical_not(full))
        def _():
            for c in range(CHUNKS_PER_BLOCK - 1):
                dst = dst_ref[base + c]

                @pl.when(dst >= 0)
                def _():
                    one(c, dst)

    @pl.when(jnp.logical_and(i == 0, n_real > 0))
    def _():
        for c in range(CHUNKS_PER_BLOCK):
            gather_copy(0, 0, c).start()

    @pl.when(jnp.logical_and(i == 0, n_real > 0))
    def _():
        for cp in weight_copies(e):
            cp.start()

    @pl.when(jnp.logical_and(i < n_real, jnp.logical_or(i == 0, e != e_prev)))
    def _():
        for cp in weight_copies(e):
            cp.wait()
        wup_s[...] = wup_stage[...].astype(BF16)
        half = LANES // 2
        for cs in range(D_MODEL // LANES):
            cols = slice(cs * LANES, (cs + 1) * LANES)
            for c in range(D_MODEL // LANES):
                for par in range(2):
                    s0 = c * LANES + par * half
                    perm_s[cs, pl.ds(c * LANES + par, half, stride=2), :] = wdn_stage[s0:s0 + half, cols]
            wdn_s[:, cols] = perm_s[cs].astype(BF16)
        nxt = nxt_ref[i]

        @pl.when(nxt < N_EXPERTS)
        def _():
            for cp in weight_copies(nxt):
                cp.start()

    @pl.when(i + 1 < n_real)
    def _():
        for c in range(CHUNKS_PER_BLOCK):
            gather_copy(i + 1, 1 - slot, c).start()

    @pl.when(jnp.logical_and(i >= 2, i < n_active))
    def _():
        scatter_chunks(i - 2, slot, wait=True)

    @pl.when(jnp.logical_and(i >= n_real, i < n_active))
    def _():
        ybuf[slot] = jnp.zeros((EXPERT_BLOCK, D_MODEL), BF16)
        scatter_chunks(i, slot, wait=False)

    def ffn(rows):
        for c in range(CHUNKS_PER_BLOCK):
            gather_copy(i, slot, c).wait()

        xb = xbuf[slot, 0:rows, :]
        meta = xb[:, D_MODEL:].astype(F32)
        gate = meta[:, 0:1] + meta[:, 1:2]
        u = jnp.dot(xb[:, :D_MODEL], wup_s[...], preferred_element_type=F32) + bup_ref[0]
        even = (_lane_iota() & 1) == 0
        cols = []
        for c in range(D_MODEL // LANES):
            c0 = u[:, (2 * c) * LANES:(2 * c + 1) * LANES]
            c1 = u[:, (2 * c + 1) * LANES:(2 * c + 2) * LANES]
            glu = jnp.where(even, c0, pltpu.roll(c1, 1, axis=1))
            lin = jnp.where(even, pltpu.roll(c0, LANES - 1, axis=1), c1)
            glu = jnp.minimum(glu, SWIGLU_LIMIT)
            lin = jnp.clip(lin, -SWIGLU_LIMIT, SWIGLU_LIMIT)
            cols.append(glu * jax.nn.sigmoid(SWIGLU_ALPHA * glu) * (lin + 1.0))
        a = jnp.concatenate(cols, axis=1).astype(BF16)
        y = jnp.dot(a, wdn_s[...], preferred_element_type=F32) + bdn_ref[0]
        ybuf[slot, 0:rows, :] = (y * gate).astype(BF16)
        scatter_chunks(i, slot, wait=False)

    part = FFN_ROWS // CHUNK
    n_parts = EXPERT_BLOCK // FFN_ROWS
    base = i * CHUNKS_PER_BLOCK
    parts_used = 1 + sum((dst_ref[base + q * part] >= 0).astype(I32) for q in range(1, n_parts))
    for q in range(1, n_parts + 1):
        @pl.when(jnp.logical_and(i < n_real, parts_used == q))
        def _(q=q):
            ffn(q * FFN_ROWS)

    @pl.when(i == n_active - 1)
    def _():
        @pl.when(i >= 1)
        def _():
            scatter_chunks(i - 1, 1 - slot, wait=True)

        scatter_chunks(i, slot, wait=True)


def _experts(block_expert, next_expert, n_active, chunk_src, chunk_dst, xs_big, ys_rows, w_up, b_up, w_down, b_down):
    n_blocks = block_expert.shape[0]
    bspec = lambda shape: pl.BlockSpec(
        shape, lambda i, be, nx, na, cs, cd: (jnp.minimum(be[i], N_EXPERTS - 1), 0, 0))
    grid_spec = pltpu.PrefetchScalarGridSpec(
        num_scalar_prefetch=5,
        grid=(n_blocks,),
        in_specs=[
            pl.BlockSpec(memory_space=pl.ANY),
            pl.BlockSpec(memory_space=pl.ANY),
            bspec((1, 1, 2 * D_MODEL)),
            pl.BlockSpec(memory_space=pl.ANY),
            bspec((1, 1, D_MODEL)),
        ],
        out_specs=pl.BlockSpec(memory_space=pl.ANY),
        scratch_shapes=[
            pltpu.VMEM((D_MODEL, 2 * D_MODEL), BF16),
            pltpu.VMEM((D_MODEL, D_MODEL), BF16),
            pltpu.VMEM((D_MODEL // LANES, D_MODEL, LANES), F32),
            pltpu.VMEM((D_MODEL, 2 * D_MODEL), F32),
            pltpu.VMEM((D_MODEL, D_MODEL), F32),
            pltpu.VMEM((2, EXPERT_BLOCK, XS_WIDTH), BF16),
            pltpu.VMEM((2, EXPERT_BLOCK, D_MODEL), BF16),
            pltpu.SemaphoreType.DMA((2,)),
            pltpu.SemaphoreType.DMA((2,)),
            pltpu.SemaphoreType.DMA((2,)),
        ],
    )
    return pl.pallas_call(
        _expert_kernel,
        grid_spec=grid_spec,
        out_shape=jax.ShapeDtypeStruct((ys_rows, D_MODEL), BF16),
        compiler_params=pltpu.CompilerParams(dimension_semantics=("arbitrary",), vmem_limit_bytes=VMEM_LIMIT),
        name="experts",
    )(block_expert, next_expert, n_active, chunk_src, chunk_dst, xs_big, w_up,
      b_up.reshape(N_EXPERTS, 1, 2 * D_MODEL), w_down, b_down.reshape(N_EXPERTS, 1, D_MODEL))


def _combine_kernel(lp_ref, ys_ref, x1_ref, out_ref):
    tm = x1_ref.shape[0]
    lrows = ys_ref.shape[0]
    onehot = _onehot_row_builder(lp_ref[0], tm)(0, lrows)
    y = lax.dot_general(onehot, ys_ref[...], (((0,), (0,)), ((), ())), preferred_element_type=F32)
    out_ref[...] = x1_ref[...] + y


def _combine(lp, ys_big, x1, tm, lrows, first_block):
    tokens = x1.shape[0]
    return pl.pallas_call(
        _combine_kernel,
        grid=(tokens // tm,),
        in_specs=[
            pl.BlockSpec((1, 8, tm), lambda j: (j, 0, 0)),
            pl.BlockSpec((lrows, D_MODEL), lambda j: (first_block + j, 0)),
            pl.BlockSpec((tm, D_MODEL), lambda j: (j, 0)),
        ],
        out_specs=pl.BlockSpec((tm, D_MODEL), lambda j: (j, 0)),
        out_shape=jax.ShapeDtypeStruct((tokens, D_MODEL), F32),
        compiler_params=pltpu.CompilerParams(dimension_semantics=("arbitrary",), vmem_limit_bytes=VMEM_LIMIT),
        name="combine",
    )(lp, ys_big, x1)


def _bias_tables(rel_bias):
    qi = np.arange(ATTN_BLOCK)[:, None]
    kj = np.arange(2 * ATTN_BLOCK)[None, :]
    rel = qi + ATTN_BLOCK - kj
    valid = (rel >= 0) & (rel < WINDOW)
    bucket = np.where(valid, _t5_bucket_np(rel), -1)
    slot_rel = np.where(np.arange(WINDOW) == 0, 0, WINDOW - np.arange(WINDOW))
    slot_bucket = _t5_bucket_np(slot_rel)
    buckets = np.arange(N_BUCKETS)
    onehot = bucket[None, :, :] == buckets[:, None, None]
    tab = jnp.sum(jnp.where(onehot[None], rel_bias.T[:, :, None, None], 0.0), axis=1)
    tab = jnp.where(valid[None], tab, NEG_INF)
    slot_onehot = slot_bucket[None, :] == buckets[:, None]
    sample_tab = jnp.sum(jnp.where(slot_onehot[None], rel_bias.T[:, :, None], 0.0), axis=1)
    prompt_tab = tab.reshape(N_KV, GROUP, ATTN_BLOCK, 2 * ATTN_BLOCK).transpose(0, 3, 1, 2)
    return prompt_tab.reshape(N_KV, 2 * ATTN_BLOCK, GROUP * ATTN_BLOCK), sample_tab


def _chunk_tables(seg, starts, tile_base, tile_rows, n_blocks):
    n_tiles = seg.shape[0]
    n_seg = (N_EXPERTS + 1) * n_tiles
    used = jnp.sum(seg, axis=1)
    seg_e = jnp.concatenate([seg.T, (tile_rows - used)[None, :]], axis=0)
    src0 = jnp.concatenate([tile_base[None, :] + starts.T, (tile_base + used)[None, :]], axis=0).reshape(-1)
    total = jnp.sum(seg_e, axis=1)
    region = (total + EXPERT_BLOCK - 1) // EXPERT_BLOCK * EXPERT_BLOCK
    pad_end = jnp.cumsum(region)
    pad_start = pad_end - region
    g_start = (pad_start[:, None] + jnp.cumsum(seg_e, axis=1) - seg_e).reshape(-1)
    g_end = g_start + seg_e.reshape(-1)
    rows = jnp.arange(n_blocks * CHUNKS_PER_BLOCK, dtype=I32) * CHUNK
    passed = g_end[None, :] <= rows[:, None]

    def at_segment(table, sentinel):
        ext = jnp.concatenate([table, jnp.array([sentinel], I32)])
        return ext[0] + jnp.sum(jnp.where(passed, (ext[1:] - ext[:-1])[None, :], 0), axis=1)

    seg_start = at_segment(g_start, 1 << 30)
    valid = rows >= seg_start
    src_row = at_segment(src0, 0) + rows - seg_start
    chunk_src = jnp.where(valid, src_row // CHUNK, 0).astype(I32)
    chunk_dst = jnp.where(valid, src_row // CHUNK, -1).astype(I32)
    blk_rows = jnp.arange(n_blocks, dtype=I32) * EXPERT_BLOCK
    block_region = jnp.minimum(jnp.sum((pad_end[None, :] <= blk_rows[:, None]).astype(I32), axis=1),
                               N_EXPERTS).astype(I32)
    counts = jnp.stack([pad_end[N_EXPERTS - 1], pad_end[N_EXPERTS]]).astype(I32) // EXPERT_BLOCK
    ridx = jnp.arange(N_EXPERTS + 1, dtype=I32)
    later = jnp.logical_and(ridx[None, :] > ridx[:, None], (region > 0)[None, :])
    next_region = jnp.min(jnp.where(later, ridx[None, :], N_EXPERTS + 1), axis=1)
    next_of_block = jnp.sum(jnp.where(block_region[:, None] == ridx[None, :], next_region[None, :], 0), axis=1)
    return block_region, next_of_block.astype(I32), counts, chunk_src, chunk_dst


def kernel(x_prompt, x_sample, state_conv, cache_k_win, cache_v_win, rel_bias, attn_norm_w, w_in, conv_w,
           q_norm_w, k_norm_w, sinks, w_out, ffn_norm_w, w_router, b_router, w_up, b_up, w_down, b_down):
    batch, seq, _ = x_prompt.shape
    nb = x_sample.shape[0]
    anw = attn_norm_w[0].reshape(1, D_MODEL)
    fnw = ffn_norm_w[0].reshape(1, D_MODEL)
    win_bf = w_in[0].astype(BF16)
    wout_bf = w_out[0].astype(BF16)
    qnw = jnp.tile(q_norm_w[0], 2).reshape(1, LANES)
    knw = jnp.tile(k_norm_w[0], 2).reshape(1, LANES)
    wr_t = w_router[0].T
    br = b_router[0].reshape(N_EXPERTS, 1)
    prompt_tab, sample_tab = _bias_tables(rel_bias)

    k_past = cache_k_win[0].reshape(nb, WINDOW, LANES)
    v_past = cache_v_win[0].reshape(nb, WINDOW, LANES)
    x1s, h2s, routes, unew, new_k_sample, new_v_sample = _mixer_sample(
        x_sample.reshape(nb, D_MODEL), state_conv[0, :, 0, :], state_conv[0, :, 1, :], k_past, v_past,
        sinks[0].reshape(N_HEADS, 1), anw, win_bf, conv_w[0], qnw, knw, sample_tab, wout_bf, fnw, wr_t, br)

    sink_rows = jnp.repeat(sinks[0].reshape(N_KV, GROUP), ATTN_BLOCK, axis=1)
    x1p, xs_big, lpp, lps, stat, kp, vp, cp = _mixer_prompt(
        x_prompt, sink_rows, anw, win_bf, conv_w[0], qnw, knw, prompt_tab, wout_bf, fnw, wr_t, br, h2s, routes)

    n_tiles = stat.shape[0]
    lrows = _local_rows(MIXER_TILE)
    xs_rows = n_tiles * lrows
    seg = stat[:, :, 0].astype(I32)
    starts = stat[:, :, 1].astype(I32)
    tile_base = jnp.arange(n_tiles, dtype=I32) * lrows
    tile_rows = jnp.full((n_tiles,), lrows, I32)
    n_blocks = -(-(xs_rows + (N_EXPERTS + 1) * (EXPERT_BLOCK - 1)) // EXPERT_BLOCK)
    block_expert, next_expert, n_active, chunk_src, chunk_dst = _chunk_tables(seg, starts, tile_base, tile_rows,
                                                                              n_blocks)

    ys_big = _experts(block_expert, next_expert, n_active, chunk_src, chunk_dst, xs_big, xs_rows, w_up[0], b_up[0],
                      w_down[0], b_down[0])

    y_prompt = _combine(lpp, ys_big, x1p, MIXER_TILE, lrows, 0).reshape(batch, seq, D_MODEL)
    y_sample = _combine(lps, ys_big, x1s, nb, lrows, n_tiles - 1).reshape(nb, 1, D_MODEL)

    new_conv_sample = jnp.stack([state_conv[0, :, 1, :], unew], axis=1)
    return (
        y_prompt,
        y_sample,
        cp[None],
        kp.reshape(1, batch, ATTN_BLOCK, N_KV, HEAD_DIM),
        vp.reshape(1, batch, ATTN_BLOCK, N_KV, HEAD_DIM),
        new_conv_sample[None],
        new_k_sample.reshape(1, nb, WINDOW, N_KV, HEAD_DIM),
        new_v_sample.reshape(1, nb, WINDOW, N_KV, HEAD_DIM),
    )
```

```python
import functools
import math

import numpy as np
import jax
import jax.numpy as jnp
from jax import lax
from jax.experimental import pallas as pl
from jax.experimental.pallas import tpu as pltpu

F32 = jnp.float32
BF16 = jnp.bfloat16
I32 = jnp.int32

D_MODEL = 1024
HEAD_DIM = 64
N_HEADS = 16
N_KV = 2
GROUP = N_HEADS // N_KV
WINDOW = 128
ATTN_BLOCK = 128
N_BUCKETS = 32
MAX_DISTANCE = 128
NEG_INF = -1e30
N_EXPERTS = 32
TOP_K = 4
SWIGLU_ALPHA = 1.702
SWIGLU_LIMIT = 7.0
EPS = 1e-5
ATTN_SCALE = HEAD_DIM ** -0.5

OFF_XIN, OFF_BG, OFF_CG, OFF_Q = 0, 1024, 2048, 3072
OFF_K, OFF_V, OFF_GC, OFF_GA = 4096, 4224, 4352, 5376
IN_DIM = 6400

LANES = 128
MIXER_TILE = 512
EXPERT_BLOCK = 512
FFN_ROWS = 128
CHUNK = 16
CHUNKS_PER_BLOCK = EXPERT_BLOCK // CHUNK
XS_WIDTH = D_MODEL + LANES
VMEM_LIMIT = 60 * 1024 * 1024


def _local_rows(tm):
    need = tm * TOP_K + N_EXPERTS * (CHUNK - 1)
    return -(-need // 512) * 512


def _t5_bucket_np(rel):
    n = np.maximum(rel, 0)
    max_exact = N_BUCKETS // 2
    nf = np.maximum(n, 1).astype(np.float64)
    large = max_exact + (np.log(nf / max_exact) / math.log(MAX_DISTANCE / max_exact)
                         * (N_BUCKETS - max_exact)).astype(np.int32)
    large = np.minimum(large, N_BUCKETS - 1)
    return np.where(n < max_exact, n, large).astype(np.int32)


def _rms(x, w):
    return x * lax.rsqrt(jnp.mean(x * x, axis=-1, keepdims=True) + EPS) * w


def _lane_iota(rows=1):
    return lax.broadcasted_iota(I32, (rows, LANES), 1)


def _lo_half():
    return _lane_iota() < HEAD_DIM


def _pair_norm(t, w128):
    lo = _lo_half()
    sq = t * t
    s_lo = jnp.sum(jnp.where(lo, sq, 0.0), axis=-1, keepdims=True)
    s_hi = jnp.sum(jnp.where(lo, 0.0, sq), axis=-1, keepdims=True)
    r = jnp.where(lo, lax.rsqrt(s_lo * (1.0 / HEAD_DIM) + EPS), lax.rsqrt(s_hi * (1.0 / HEAD_DIM) + EPS))
    return t * r * w128


def _top4_gates(logits_t):
    tokens = logits_t.shape[1]
    expert = lax.broadcasted_iota(I32, (N_EXPERTS, tokens), 0).astype(F32)
    vals, idxs = [], []
    l = logits_t
    for _ in range(TOP_K):
        m = jnp.max(l, axis=0, keepdims=True)
        idx = jnp.min(jnp.where(l == m, expert, float(N_EXPERTS)), axis=0, keepdims=True)
        vals.append(m)
        idxs.append(idx)
        l = jnp.where(expert == idx, -jnp.inf, l)
    es = [jnp.exp(v - vals[0]) for v in vals]
    den = es[0] + es[1] + es[2] + es[3]
    return idxs, [e / den for e in es]


def _rows_to_sublanes(rows, n_rows, tokens, fill=0.0):
    sub = lax.broadcasted_iota(I32, (n_rows, tokens), 0)
    out = jnp.full((n_rows, tokens), fill, F32)
    for k, r in enumerate(rows):
        out = jnp.where(sub == k, r, out)
    return out


def _onehot_row_builder(lp_t, tm):
    lp_group = jnp.floor(lp_t * (1.0 / LANES))
    lp_off = lp_t - lp_group * LANES
    row_off = lax.broadcasted_iota(I32, (LANES, tm), 0).astype(F32).astype(BF16)
    one_bf = jnp.ones((LANES, tm), BF16)
    zero_bf = jnp.zeros((LANES, tm), BF16)

    def build(r0, rows):
        parts = []
        for s in range(rows // LANES):
            group = float(r0 // LANES + s)
            hit = None
            for k in range(TOP_K):
                off_k = jnp.where(lp_group[k:k + 1, :] == group, lp_off[k:k + 1, :], -1.0).astype(BF16)
                hit_k = row_off == off_k
                hit = hit_k if hit is None else jnp.logical_or(hit, hit_k)
            parts.append(jnp.where(hit, one_bf, zero_bf))
        return jnp.concatenate(parts, axis=0)

    return build


def _dispatch(h2, idxs, gates, xs_ref, lp_ref, stat_ref):
    tm = h2.shape[0]
    lrows = xs_ref.shape[0]
    expert = lax.broadcasted_iota(I32, (N_EXPERTS, tm), 0).astype(F32)
    member = jnp.zeros((N_EXPERTS, tm), F32)
    for k in range(TOP_K):
        member = member + jnp.where(expert == idxs[k], 1.0, 0.0)
    cnt = jnp.broadcast_to(jnp.sum(member, axis=1, keepdims=True), (N_EXPERTS, LANES))
    seg = jnp.floor((cnt + (CHUNK - 1)) * (1.0 / CHUNK)) * CHUNK
    sub32 = lax.broadcasted_iota(I32, (N_EXPERTS, LANES), 0)
    incl = seg
    for s in (1, 2, 4, 8, 16):
        incl = incl + jnp.where(sub32 >= s, pltpu.roll(incl, s, axis=0), 0.0)
    starts = incl - seg
    lane32 = _lane_iota(N_EXPERTS)
    stat_ref[0] = jnp.where(lane32 == 0, seg, jnp.where(lane32 == 1, starts, 0.0))

    earlier = (lax.broadcasted_iota(I32, (tm, tm), 0) < lax.broadcasted_iota(I32, (tm, tm), 1)).astype(BF16)
    rank = jnp.dot(member.astype(BF16), earlier, preferred_element_type=F32)
    pos = starts[:, 0:1] + rank
    lps = [jnp.sum(jnp.where(expert == idxs[k], pos, 0.0), axis=0, keepdims=True) for k in range(TOP_K)]
    lp_t = _rows_to_sublanes(lps, 8, tm, fill=-1.0)
    lp_ref[0] = lp_t

    g_hi = [g.astype(BF16).astype(F32) for g in gates]
    g_lo = [g - h for g, h in zip(gates, g_hi)]
    meta_in = _rows_to_sublanes(list(idxs) + g_hi + g_lo, LANES, tm).T
    rhs = jnp.concatenate([h2.astype(BF16), meta_in.astype(BF16)], axis=1)

    starts32 = jnp.concatenate([starts, jnp.full((LANES - N_EXPERTS, LANES), 1e9, F32)], axis=0).T[0:1, :]
    onehot_rows = _onehot_row_builder(lp_t, tm)

    bounds = list(range(0, lrows - 512, 512)) + [lrows - 512, lrows - 256]
    for r0, r1 in zip(bounds, bounds[1:] + [lrows]):
        sub = r1 - r0
        rid = (lax.broadcasted_iota(I32, (sub, 1), 0) + r0).astype(F32)
        full = jnp.dot(onehot_rows(r0, sub), rhs, preferred_element_type=F32)
        got = full[:, D_MODEL:]
        e_row = jnp.sum(jnp.where(rid >= starts32, 1.0, 0.0), axis=-1, keepdims=True) - 1.0
        g_sum = pltpu.roll(got, LANES - TOP_K, axis=1) + pltpu.roll(got, LANES - 2 * TOP_K, axis=1)
        lane_s = _lane_iota(sub)
        pick = jnp.logical_and(lane_s < TOP_K, got == e_row)
        gate_row = jnp.sum(jnp.where(pick, g_sum, 0.0), axis=-1, keepdims=True)
        gr_hi = gate_row.astype(BF16).astype(F32)
        meta = jnp.where(lane_s == 0, gr_hi, jnp.where(lane_s == 1, gate_row - gr_hi, 0.0))
        xs_ref[r0:r0 + sub, 0:D_MODEL] = full[:, :D_MODEL].astype(BF16)
        xs_ref[r0:r0 + sub, D_MODEL:XS_WIDTH] = meta.astype(BF16)


def _epilogue(x, merged, wout_ref, fnw_ref, wr_ref, br_ref, x1_ref):
    x1 = x + jnp.dot(merged.astype(BF16), wout_ref[...], preferred_element_type=F32)
    x1_ref[...] = x1
    h2 = _rms(x1, fnw_ref[...]).astype(BF16)
    logits_t = lax.dot_general(wr_ref[...].astype(BF16), h2, (((1,), (1,)), ((), ())),
                               preferred_element_type=F32) + br_ref[...]
    idxs, gates = _top4_gates(logits_t)
    return h2, idxs, gates


def _mixer_prompt_kernel(nj, sink_ref, x_ref, anw_ref, win_ref, convw_ref, qnw_ref, knw_ref, bias_ref, wout_ref,
                         fnw_ref, wr_ref, br_ref, h2s_ref, routes_ref,
                         x1_ref, xs_ref, lp_ref, lps_ref, stat_ref, kout_ref, vout_ref, cout_ref,
                         ubuf, q_s, kd_s, vt_s, ya_s, st_s, pt_s):
    i = pl.program_id(0)
    n_tiles = pl.num_programs(0) - 1

    @pl.when(i < n_tiles)
    def _():
        _prompt_tile(lax.rem(i, nj), nj, sink_ref, x_ref, anw_ref, win_ref, convw_ref, qnw_ref, knw_ref, bias_ref,
                     wout_ref, fnw_ref, wr_ref, br_ref, x1_ref, xs_ref, lp_ref, stat_ref, kout_ref, vout_ref,
                     cout_ref, ubuf, q_s, kd_s, vt_s, ya_s, st_s, pt_s)

    @pl.when(i == n_tiles)
    def _():
        route = routes_ref[...]
        _dispatch(h2s_ref[...], [route[k:k + 1, :] for k in range(TOP_K)],
                  [route[TOP_K + k:TOP_K + k + 1, :] for k in range(TOP_K)], xs_ref, lps_ref, stat_ref)


def _prompt_tile(j, nj, sink_ref, x_ref, anw_ref, win_ref, convw_ref, qnw_ref, knw_ref, bias_ref, wout_ref,
                 fnw_ref, wr_ref, br_ref, x1_ref, xs_ref, lp_ref, stat_ref, kout_ref, vout_ref, cout_ref,
                 ubuf, q_s, kd_s, vt_s, ya_s, st_s, pt_s):
    tm = x_ref.shape[0]
    nblk = tm // ATTN_BLOCK
    first_tile = j == 0
    lo = _lo_half()

    x = x_ref[...]
    h = _rms(x, anw_ref[...]).astype(BF16)

    def proj(off, n):
        return jnp.dot(h, win_ref[:, off:off + n], preferred_element_type=F32)

    @pl.when(first_tile)
    def _():
        ubuf[0:8, :] = jnp.zeros((8, D_MODEL), F32)
        kd_s[0:ATTN_BLOCK, :] = jnp.zeros((ATTN_BLOCK, 2 * LANES), BF16)
        vt_s[0] = jnp.zeros((N_KV, LANES, ATTN_BLOCK), BF16)

    q = proj(OFF_Q, D_MODEL)
    kv = proj(OFF_K, 2 * LANES)
    k = _pair_norm(kv[:, :LANES], knw_ref[...])
    v = kv[:, LANES:]
    qnw = qnw_ref[...]
    for p in range(N_HEADS // 2):
        sl = slice(p * LANES, (p + 1) * LANES)
        q_s[:, sl] = (_pair_norm(q[:, sl], qnw) * ATTN_SCALE).astype(BF16)
    k_sw = pltpu.roll(k, HEAD_DIM, axis=1)
    v_sw = pltpu.roll(v, HEAD_DIM, axis=1)
    kd_s[ATTN_BLOCK:tm + ATTN_BLOCK, 0:LANES] = jnp.where(lo, k, k_sw).astype(BF16)
    kd_s[ATTN_BLOCK:tm + ATTN_BLOCK, LANES:2 * LANES] = jnp.where(lo, k_sw, k).astype(BF16)
    v_dup = (jnp.where(lo, v, v_sw), jnp.where(lo, v_sw, v))
    for b in range(nblk):
        for g in range(N_KV):
            vt_s[b + 1, g] = v_dup[g][b * ATTN_BLOCK:(b + 1) * ATTN_BLOCK, :].T.astype(BF16)

    u = proj(OFF_CG, D_MODEL) * proj(OFF_XIN, D_MODEL)
    ubuf[8:tm + 8, :] = u
    cw = convw_ref[...]
    conv = ubuf[6:tm + 6, :] * cw[0:1, :] + ubuf[7:tm + 7, :] * cw[1:2, :] + u * cw[2:3, :]
    merged = jax.nn.sigmoid(proj(OFF_GC, D_MODEL)) * (proj(OFF_BG, D_MODEL) * conv)
    tail = ubuf[tm + 6:tm + 8, :]
    cout_ref[0] = tail
    ubuf[6:8, :] = tail

    @pl.when(j == nj - 1)
    def _():
        kout_ref[0] = k[tm - ATTN_BLOCK:, :]
        vout_ref[0] = v[tm - ATTN_BLOCK:, :]

    prev_rows = lax.broadcasted_iota(I32, (2 * ATTN_BLOCK, 1), 0) < ATTN_BLOCK
    feat_lo = lax.broadcasted_iota(I32, (LANES, 1), 0) < HEAD_DIM

    units = [(blk, g) for blk in range(nblk) for g in range(N_KV)]

    def scores(n):
        blk, g = units[n]
        r0 = blk * ATTN_BLOCK
        qb = q_s[r0:r0 + ATTN_BLOCK, :]
        parts = []
        for t in range(GROUP):
            hd = g * GROUP + t
            slab = qb[:, (hd // 2) * LANES:(hd // 2 + 1) * LANES]
            keep = lo if hd % 2 == 0 else jnp.logical_not(lo)
            parts.append(jnp.where(keep, slab, jnp.zeros_like(slab)))
        lhs = jnp.concatenate(parts, axis=0)
        st = lax.dot_general(kd_s[r0:r0 + 2 * ATTN_BLOCK, g * LANES:(g + 1) * LANES], lhs,
                             (((1,), (1,)), ((), ())), preferred_element_type=F32)
        st = st + bias_ref[g]
        if blk == 0:
            st = jnp.where(jnp.logical_and(prev_rows, first_tile), NEG_INF, st)
        st_s[n % 2] = st

    def softmax_values(n):
        blk, g = units[n]
        r0 = blk * ATTN_BLOCK
        sink = sink_ref[g:g + 1, :]
        rdens = []
        for t in range(GROUP):
            cols = slice(t * LANES, (t + 1) * LANES)
            s = st_s[n % 2, :, cols]
            m = jnp.maximum(jnp.max(s, axis=0, keepdims=True), sink[:, cols])
            pr = jnp.exp(s - m)
            den = jnp.sum(pr, axis=0, keepdims=True) + jnp.exp(sink[:, cols] - m)
            pt_s[:, cols] = pr.astype(BF16)
            rdens.append(1.0 / den)
        vt = jnp.concatenate([vt_s[blk, g], vt_s[blk + 1, g]], axis=1)
        ot = jnp.dot(vt, pt_s[...], preferred_element_type=F32)
        for i in range(GROUP // 2):
            pair = g * (GROUP // 2) + i
            even = ot[:, (2 * i) * LANES:(2 * i + 1) * LANES] * rdens[2 * i]
            odd = ot[:, (2 * i + 1) * LANES:(2 * i + 2) * LANES] * rdens[2 * i + 1]
            ya_s[r0:r0 + ATTN_BLOCK, pair * LANES:(pair + 1) * LANES] = jnp.where(feat_lo, even, odd).T

    ga_parts = []
    ga_cols = D_MODEL // (len(units) // 2)
    scores(0)
    for n in range(len(units)):
        if n + 1 < len(units):
            scores(n + 1)
        if n % 2 == 0:
            ga_parts.append(jax.nn.sigmoid(proj(OFF_GA + (n // 2) * ga_cols, ga_cols)))
        softmax_values(n)

    kd_s[0:ATTN_BLOCK, :] = kd_s[tm:tm + ATTN_BLOCK, :]
    vt_s[0] = vt_s[nblk]

    merged = merged + jnp.concatenate(ga_parts, axis=1) * ya_s[...]
    h2, idxs, gates = _epilogue(x_ref[...], merged, wout_ref, fnw_ref, wr_ref, br_ref, x1_ref)
    _dispatch(h2, idxs, gates, xs_ref, lp_ref, stat_ref)


def _const_spec(shape):
    nd = len(shape)
    return pl.BlockSpec(shape, lambda *_: (0,) * nd, pipeline_mode=pl.Buffered(1))


def _mixer_prompt(x, sink_rows, anw, win_bf, convw, qnw, knw, bias_tab, wout_bf, fnw, wr_t, br_col, h2s, routes):
    batch, seq, _ = x.shape
    tm = MIXER_TILE
    lrows = _local_rows(tm)
    nj = seq // tm
    tokens = batch * seq
    n_tiles = batch * nj
    nb = h2s.shape[0]
    x2 = x.reshape(tokens, D_MODEL)
    tile = lambda i: jnp.minimum(i, n_tiles - 1)
    tok_spec = lambda width: pl.BlockSpec((tm, width), lambda i: (tile(i), 0))
    per_batch = lambda rows, width: pl.BlockSpec((1, rows, width), lambda i: (tile(i) // nj, 0, 0))
    in_specs = [
        _const_spec((N_KV, GROUP * ATTN_BLOCK)),
        tok_spec(D_MODEL),
        _const_spec((1, D_MODEL)),
        _const_spec((D_MODEL, IN_DIM)),
        _const_spec((3, D_MODEL)),
        _const_spec((1, LANES)),
        _const_spec((1, LANES)),
        _const_spec((N_KV, 2 * ATTN_BLOCK, GROUP * ATTN_BLOCK)),
        _const_spec((D_MODEL, D_MODEL)),
        _const_spec((1, D_MODEL)),
        _const_spec((N_EXPERTS, D_MODEL)),
        _const_spec((N_EXPERTS, 1)),
        _const_spec((nb, D_MODEL)),
        _const_spec((2 * TOP_K, nb)),
    ]
    out_shape = (
        jax.ShapeDtypeStruct((tokens, D_MODEL), F32),
        jax.ShapeDtypeStruct(((n_tiles + 1) * lrows, XS_WIDTH), BF16),
        jax.ShapeDtypeStruct((n_tiles, 8, tm), F32),
        jax.ShapeDtypeStruct((1, 8, nb), F32),
        jax.ShapeDtypeStruct((n_tiles + 1, N_EXPERTS, LANES), F32),
        jax.ShapeDtypeStruct((batch, ATTN_BLOCK, LANES), F32),
        jax.ShapeDtypeStruct((batch, ATTN_BLOCK, LANES), F32),
        jax.ShapeDtypeStruct((batch, 2, D_MODEL), F32),
    )
    out_specs = (
        tok_spec(D_MODEL),
        pl.BlockSpec((lrows, XS_WIDTH), lambda i: (i, 0)),
        pl.BlockSpec((1, 8, tm), lambda i: (tile(i), 0, 0)),
        pl.BlockSpec((1, 8, nb), lambda i: (0, 0, 0)),
        pl.BlockSpec((1, N_EXPERTS, LANES), lambda i: (i, 0, 0)),
        per_batch(ATTN_BLOCK, LANES), per_batch(ATTN_BLOCK, LANES), per_batch(2, D_MODEL),
    )
    scratch = [
        pltpu.VMEM((tm + 8, D_MODEL), F32),
        pltpu.VMEM((tm, D_MODEL), BF16),
        pltpu.VMEM((tm + ATTN_BLOCK, 2 * LANES), BF16),
        pltpu.VMEM((tm // ATTN_BLOCK + 1, N_KV, LANES, ATTN_BLOCK), BF16),
        pltpu.VMEM((tm, D_MODEL), F32),
        pltpu.VMEM((2, 2 * ATTN_BLOCK, GROUP * ATTN_BLOCK), F32),
        pltpu.VMEM((2 * ATTN_BLOCK, GROUP * ATTN_BLOCK), BF16),
    ]
    return pl.pallas_call(
        functools.partial(_mixer_prompt_kernel, nj),
        grid=(n_tiles + 1,),
        in_specs=in_specs,
        out_specs=out_specs,
        out_shape=out_shape,
        scratch_shapes=scratch,
        compiler_params=pltpu.CompilerParams(dimension_semantics=("arbitrary",), vmem_limit_bytes=VMEM_LIMIT),
        name="mixer_prompt",
    )(sink_rows, x2, anw, win_bf, convw, qnw, knw, bias_tab, wout_bf, fnw, wr_t, br_col, h2s, routes)


SAMPLE_CHUNK = 32
SAMPLE_GROUP = 8


def _mixer_sample_kernel(x_ref, p0_ref, p1_ref, kp_ref, vp_ref, sink_ref, anw_ref, win_ref, convw_ref,
                         qnw_ref, knw_ref, bias_ref, wout_ref, fnw_ref, wr_ref, br_ref,
                         x1_ref, h2_ref, route_ref, unew_ref, kc_ref, vc_ref,
                         qh_s, o_s, kn_s, vn_s, conv_s, ga_s):
    c = pl.program_id(0)
    nb = x_ref.shape[0]
    tc = kp_ref.shape[0]
    lo = _lo_half()

    @pl.when(c == 0)
    def _():
        h = _rms(x_ref[...], anw_ref[...]).astype(BF16)

        def proj(off, n):
            return jnp.dot(h, win_ref[:, off:off + n], preferred_element_type=F32)

        u = proj(OFF_CG, D_MODEL) * proj(OFF_XIN, D_MODEL)
        unew_ref[...] = u
        cw = convw_ref[...]
        conv = p0_ref[...] * cw[0:1, :] + p1_ref[...] * cw[1:2, :] + u * cw[2:3, :]
        conv_s[...] = jax.nn.sigmoid(proj(OFF_GC, D_MODEL)) * (proj(OFF_BG, D_MODEL) * conv)
        ga_s[...] = jax.nn.sigmoid(proj(OFF_GA, D_MODEL))

        q = proj(OFF_Q, D_MODEL)
        kv = proj(OFF_K, 2 * LANES)
        kn_s[...] = _pair_norm(kv[:, :LANES], knw_ref[...])
        vn_s[...] = kv[:, LANES:]

        qnw = qnw_ref[...]
        for hd in range(N_HEADS):
            pair, half, grp = hd // 2, hd % 2, hd // GROUP
            slab = _pair_norm(q[:, pair * LANES:(pair + 1) * LANES], qnw) * ATTN_SCALE
            slab = jnp.where(lo if half == 0 else jnp.logical_not(lo), slab, 0.0)
            if half != grp:
                slab = pltpu.roll(slab, HEAD_DIM, axis=1)
            qh_s[hd * nb:(hd + 1) * nb, :] = slab

    sink = sink_ref[...]
    bias = bias_ref[...]
    rows = lax.broadcasted_iota(I32, (WINDOW, 1), 0)
    row0 = rows == 0
    row_last = rows == WINDOW - 1

    grp = SAMPLE_GROUP
    for ci in range(tc // grp):
        t0 = ci * grp
        qs, ks, vs = [], [], []
        for t in range(t0, t0 + grp):
            b = c * tc + t
            qs.append(qh_s[pl.ds(b, N_HEADS, stride=nb), :])
            k_new = kn_s[pl.ds(b, 1), :]
            v_new = vn_s[pl.ds(b, 1), :]
            k_old = kp_ref[t]
            v_old = vp_ref[t]
            kc_ref[t] = jnp.where(row_last, k_new, pltpu.roll(k_old, WINDOW - 1, axis=0))
            vc_ref[t] = jnp.where(row_last, v_new, pltpu.roll(v_old, WINDOW - 1, axis=0))
            ks.append(jnp.where(row0, k_new, k_old).astype(BF16))
            vs.append(jnp.where(row0, v_new, v_old).astype(BF16))
        q_all = jnp.concatenate(qs, axis=0).astype(BF16)
        s_all = lax.dot_general(q_all, jnp.concatenate(ks, axis=0), (((1,), (1,)), ((), ())),
                                preferred_element_type=F32)
        p_rows, rdens = [], []
        for g in range(grp):
            s = s_all[g * N_HEADS:(g + 1) * N_HEADS, g * WINDOW:(g + 1) * WINDOW] + bias
            m = jnp.maximum(jnp.max(s, axis=-1, keepdims=True), sink)
            pr = jnp.exp(s - m)
            rdens.append(1.0 / (jnp.sum(pr, axis=-1, keepdims=True) + jnp.exp(sink - m)))
            zero = jnp.zeros((N_HEADS, WINDOW), BF16)
            p_rows.append(jnp.concatenate([pr.astype(BF16) if j == g else zero for j in range(grp)], axis=1))
        o_all = jnp.dot(jnp.concatenate(p_rows, axis=0), jnp.concatenate(vs, axis=0),
                        preferred_element_type=F32)
        for g in range(grp):
            b = c * tc + t0 + g
            o_s[pl.ds(b, N_HEADS, stride=nb), :] = o_all[g * N_HEADS:(g + 1) * N_HEADS, :] * rdens[g]

    @pl.when(c == pl.num_programs(0) - 1)
    def _():
        cols = []
        for pair in range(N_HEADS // 2):
            halves = []
            for half in range(2):
                hd = 2 * pair + half
                slab = o_s[hd * nb:(hd + 1) * nb, :]
                if half != hd // GROUP:
                    slab = pltpu.roll(slab, HEAD_DIM, axis=1)
                halves.append(slab)
            cols.append(jnp.where(lo, halves[0], halves[1]))
        merged = conv_s[...] + ga_s[...] * jnp.concatenate(cols, axis=1)
        h2, idxs, gates = _epilogue(x_ref[...], merged, wout_ref, fnw_ref, wr_ref, br_ref, x1_ref)
        h2_ref[...] = h2
        route_ref[...] = _rows_to_sublanes(list(idxs) + list(gates), 2 * TOP_K, nb)


def _mixer_sample(x, p0, p1, k_past, v_past, sink_col, anw, win_bf, convw, qnw, knw, bias_s, wout_bf, fnw, wr, br):
    nb = x.shape[0]
    tc = SAMPLE_CHUNK
    consts = (x, p0, p1)
    params = (sink_col, anw, win_bf, convw, qnw, knw, bias_s, wout_bf, fnw, wr, br)
    cache_spec = pl.BlockSpec((tc, WINDOW, LANES), lambda c: (c, 0, 0))
    full = lambda shape: pl.BlockSpec(shape, lambda c: (0,) * len(shape))
    out_shape = (
        jax.ShapeDtypeStruct((nb, D_MODEL), F32),
        jax.ShapeDtypeStruct((nb, D_MODEL), BF16),
        jax.ShapeDtypeStruct((2 * TOP_K, nb), F32),
        jax.ShapeDtypeStruct((nb, D_MODEL), F32),
        jax.ShapeDtypeStruct((nb, WINDOW, LANES), F32),
        jax.ShapeDtypeStruct((nb, WINDOW, LANES), F32),
    )
    out_specs = tuple(full(s.shape) for s in out_shape[:4]) + (cache_spec, cache_spec)
    scratch = [
        pltpu.VMEM((N_HEADS * nb, LANES), F32),
        pltpu.VMEM((N_HEADS * nb, LANES), F32),
        pltpu.VMEM((nb, LANES), F32),
        pltpu.VMEM((nb, LANES), F32),
        pltpu.VMEM((nb, D_MODEL), F32),
        pltpu.VMEM((nb, D_MODEL), F32),
    ]
    return pl.pallas_call(
        _mixer_sample_kernel,
        grid=(nb // tc,),
        in_specs=[_const_spec(a.shape) for a in consts] + [cache_spec, cache_spec]
        + [_const_spec(a.shape) for a in params],
        out_specs=out_specs,
        out_shape=out_shape,
        scratch_shapes=scratch,
        compiler_params=pltpu.CompilerParams(dimension_semantics=("arbitrary",), vmem_limit_bytes=VMEM_LIMIT),
        name="mixer_sample",
    )(*consts, k_past, v_past, *params)


def _expert_kernel(be_ref, nxt_ref, nact_ref, src_ref, dst_ref, xs_hbm, wup_hbm, bup_ref, wdn_hbm, bdn_ref, ys_hbm,
                   wup_s, wdn_s, perm_s, wup_stage, wdn_stage, xbuf, ybuf, in_sem, out_sem, w_sem):
    i = pl.program_id(0)
    n_real = nact_ref[0]
    n_active = nact_ref[1]
    e = be_ref[i]
    e_prev = be_ref[jnp.maximum(i - 1, 0)]
    slot = lax.rem(i, 2)

    def weight_copies(expert):
        return (pltpu.make_async_copy(wup_hbm.at[expert], wup_stage, w_sem.at[0]),
                pltpu.make_async_copy(wdn_hbm.at[expert], wdn_stage, w_sem.at[1]))

    def gather_copy(blk, slt, c):
        row = pl.multiple_of(src_ref[blk * CHUNKS_PER_BLOCK + c] * CHUNK, CHUNK)
        return pltpu.make_async_copy(xs_hbm.at[pl.ds(row, CHUNK), :],
                                     xbuf.at[slt, pl.ds(c * CHUNK, CHUNK), :], in_sem.at[slt])

    def scatter_chunks(blk, slt, wait):
        base = blk * CHUNKS_PER_BLOCK

        def one(c, dst):
            row = pl.multiple_of(dst * CHUNK, CHUNK)
            cp = pltpu.make_async_copy(ybuf.at[slt, pl.ds(c * CHUNK, CHUNK), :],
                                       ys_hbm.at[pl.ds(row, CHUNK), :], out_sem.at[slt])
            if wait:
                cp.wait()
            else:
                cp.start()

        full = dst_ref[base + CHUNKS_PER_BLOCK - 1] >= 0

        @pl.when(full)
        def _():
            for c in range(CHUNKS_PER_BLOCK):
                one(c, dst_ref[base + c])

        @pl.when(jnp.logical_not(full))
        def _():
            for c in range(CHUNKS_PER_BLOCK - 1):
                dst = dst_ref[base + c]

                @pl.when(dst >= 0)
                def _():
                    one(c, dst)

    @pl.when(jnp.logical_and(i == 0, n_real > 0))
    def _():
        for c in range(CHUNKS_PER_BLOCK):
            gather_copy(0, 0, c).start()

    @pl.when(jnp.logical_and(i == 0, n_real > 0))
    def _():
        for cp in weight_copies(e):
            cp.start()

    @pl.when(jnp.logical_and(i < n_real, jnp.logical_or(i == 0, e != e_prev)))
    def _():
        for cp in weight_copies(e):
            cp.wait()
        wup_s[...] = wup_stage[...].astype(BF16)
        half = LANES // 2
        for cs in range(D_MODEL // LANES):
            cols = slice(cs * LANES, (cs + 1) * LANES)
            for c in range(D_MODEL // LANES):
                for par in range(2):
                    s0 = c * LANES + par * half
                    perm_s[cs, pl.ds(c * LANES + par, half, stride=2), :] = wdn_stage[s0:s0 + half, cols]
            wdn_s[:, cols] = perm_s[cs].astype(BF16)
        nxt = nxt_ref[i]

        @pl.when(nxt < N_EXPERTS)
        def _():
            for cp in weight_copies(nxt):
                cp.start()

    @pl.when(i + 1 < n_real)
    def _():
        for c in range(CHUNKS_PER_BLOCK):
            gather_copy(i + 1, 1 - slot, c).start()

    @pl.when(jnp.logical_and(i >= 2, i < n_active))
    def _():
        scatter_chunks(i - 2, slot, wait=True)

    @pl.when(jnp.logical_and(i >= n_real, i < n_active))
    def _():
        ybuf[slot] = jnp.zeros((EXPERT_BLOCK, D_MODEL), BF16)
        scatter_chunks(i, slot, wait=False)

    def ffn(rows):
        for c in range(CHUNKS_PER_BLOCK):
            gather_copy(i, slot, c).wait()

        xb = xbuf[slot, 0:rows, :]
        meta = xb[:, D_MODEL:].astype(F32)
        gate = meta[:, 0:1] + meta[:, 1:2]
        u = jnp.dot(xb[:, :D_MODEL], wup_s[...], preferred_element_type=F32) + bup_ref[0]
        even = (_lane_iota() & 1) == 0
        cols = []
        for c in range(D_MODEL // LANES):
            c0 = u[:, (2 * c) * LANES:(2 * c + 1) * LANES]
            c1 = u[:, (2 * c + 1) * LANES:(2 * c + 2) * LANES]
            glu = jnp.where(even, c0, pltpu.roll(c1, 1, axis=1))
            lin = jnp.where(even, pltpu.roll(c0, LANES - 1, axis=1), c1)
            glu = jnp.minimum(glu, SWIGLU_LIMIT)
            lin = jnp.clip(lin, -SWIGLU_LIMIT, SWIGLU_LIMIT)
            cols.append(glu * jax.nn.sigmoid(SWIGLU_ALPHA * glu) * (lin + 1.0))
        a = jnp.concatenate(cols, axis=1).astype(BF16)
        y = jnp.dot(a, wdn_s[...], preferred_element_type=F32) + bdn_ref[0]
        ybuf[slot, 0:rows, :] = (y * gate).astype(BF16)
        scatter_chunks(i, slot, wait=False)

    part = FFN_ROWS // CHUNK
    n_parts = EXPERT_BLOCK // FFN_ROWS
    base = i * CHUNKS_PER_BLOCK
    parts_used = 1 + sum((dst_ref[base + q * part] >= 0).astype(I32) for q in range(1, n_parts))
    for q in range(1, n_parts + 1):
        @pl.when(jnp.logical_and(i < n_real, parts_used == q))
        def _(q=q):
            ffn(q * FFN_ROWS)

    @pl.when(i == n_active - 1)
    def _():
        @pl.when(i >= 1)
        def _():
            scatter_chunks(i - 1, 1 - slot, wait=True)

        scatter_chunks(i, slot, wait=True)


def _experts(block_expert, next_expert, n_active, chunk_src, chunk_dst, xs_big, ys_rows, w_up, b_up, w_down, b_down):
    n_blocks = block_expert.shape[0]
    bspec = lambda shape: pl.BlockSpec(
        shape, lambda i, be, nx, na, cs, cd: (jnp.minimum(be[i], N_EXPERTS - 1), 0, 0))
    grid_spec = pltpu.PrefetchScalarGridSpec(
        num_scalar_prefetch=5,
        grid=(n_blocks,),
        in_specs=[
            pl.BlockSpec(memory_space=pl.ANY),
            pl.BlockSpec(memory_space=pl.ANY),
            bspec((1, 1, 2 * D_MODEL)),
            pl.BlockSpec(memory_space=pl.ANY),
            bspec((1, 1, D_MODEL)),
        ],
        out_specs=pl.BlockSpec(memory_space=pl.ANY),
        scratch_shapes=[
            pltpu.VMEM((D_MODEL, 2 * D_MODEL), BF16),
            pltpu.VMEM((D_MODEL, D_MODEL), BF16),
            pltpu.VMEM((D_MODEL // LANES, D_MODEL, LANES), F32),
            pltpu.VMEM((D_MODEL, 2 * D_MODEL), F32),
            pltpu.VMEM((D_MODEL, D_MODEL), F32),
            pltpu.VMEM((2, EXPERT_BLOCK, XS_WIDTH), BF16),
            pltpu.VMEM((2, EXPERT_BLOCK, D_MODEL), BF16),
            pltpu.SemaphoreType.DMA((2,)),
            pltpu.SemaphoreType.DMA((2,)),
            pltpu.SemaphoreType.DMA((2,)),
        ],
    )
    return pl.pallas_call(
        _expert_kernel,
        grid_spec=grid_spec,
        out_shape=jax.ShapeDtypeStruct((ys_rows, D_MODEL), BF16),
        compiler_params=pltpu.CompilerParams(dimension_semantics=("arbitrary",), vmem_limit_bytes=VMEM_LIMIT),
        name="experts",
    )(block_expert, next_expert, n_active, chunk_src, chunk_dst, xs_big, w_up,
      b_up.reshape(N_EXPERTS, 1, 2 * D_MODEL), w_down, b_down.reshape(N_EXPERTS, 1, D_MODEL))


def _combine_kernel(lp_ref, ys_ref, x1_ref, out_ref):
    tm = x1_ref.shape[0]
    lrows = ys_ref.shape[0]
    onehot = _onehot_row_builder(lp_ref[0], tm)(0, lrows)
    y = lax.dot_general(onehot, ys_ref[...], (((0,), (0,)), ((), ())), preferred_element_type=F32)
    out_ref[...] = x1_ref[...] + y


def _combine(lp, ys_big, x1, tm, lrows, first_block):
    tokens = x1.shape[0]
    return pl.pallas_call(
        _combine_kernel,
        grid=(tokens // tm,),
        in_specs=[
            pl.BlockSpec((1, 8, tm), lambda j: (j, 0, 0)),
            pl.BlockSpec((lrows, D_MODEL), lambda j: (first_block + j, 0)),
            pl.BlockSpec((tm, D_MODEL), lambda j: (j, 0)),
        ],
        out_specs=pl.BlockSpec((tm, D_MODEL), lambda j: (j, 0)),
        out_shape=jax.ShapeDtypeStruct((tokens, D_MODEL), F32),
        compiler_params=pltpu.CompilerParams(dimension_semantics=("arbitrary",), vmem_limit_bytes=VMEM_LIMIT),
        name="combine",
    )(lp, ys_big, x1)


def _bias_tables(rel_bias):
    qi = np.arange(ATTN_BLOCK)[:, None]
    kj = np.arange(2 * ATTN_BLOCK)[None, :]
    rel = qi + ATTN_BLOCK - kj
    valid = (rel >= 0) & (rel < WINDOW)
    bucket = np.where(valid, _t5_bucket_np(rel), -1)
    slot_rel = np.where(np.arange(WINDOW) == 0, 0, WINDOW - np.arange(WINDOW))
    slot_bucket = _t5_bucket_np(slot_rel)
    buckets = np.arange(N_BUCKETS)
    onehot = bucket[None, :, :] == buckets[:, None, None]
    tab = jnp.sum(jnp.where(onehot[None], rel_bias.T[:, :, None, None], 0.0), axis=1)
    tab = jnp.where(valid[None], tab, NEG_INF)
    slot_onehot = slot_bucket[None, :] == buckets[:, None]
    sample_tab = jnp.sum(jnp.where(slot_onehot[None], rel_bias.T[:, :, None], 0.0), axis=1)
    prompt_tab = tab.reshape(N_KV, GROUP, ATTN_BLOCK, 2 * ATTN_BLOCK).transpose(0, 3, 1, 2)
    return prompt_tab.reshape(N_KV, 2 * ATTN_BLOCK, GROUP * ATTN_BLOCK), sample_tab


def _chunk_tables(seg, starts, tile_base, tile_rows, n_blocks):
    n_tiles = seg.shape[0]
    n_seg = (N_EXPERTS + 1) * n_tiles
    used = jnp.sum(seg, axis=1)
    seg_e = jnp.concatenate([seg.T, (tile_rows - used)[None, :]], axis=0)
    src0 = jnp.concatenate([tile_base[None, :] + starts.T, (tile_base + used)[None, :]], axis=0).reshape(-1)
    total = jnp.sum(seg_e, axis=1)
    region = (total + EXPERT_BLOCK - 1) // EXPERT_BLOCK * EXPERT_BLOCK
    pad_end = jnp.cumsum(region)
    pad_start = pad_end - region
    g_start = (pad_start[:, None] + jnp.cumsum(seg_e, axis=1) - seg_e).reshape(-1)
    g_end = g_start + seg_e.reshape(-1)
    rows = jnp.arange(n_blocks * CHUNKS_PER_BLOCK, dtype=I32) * CHUNK
    passed = g_end[None, :] <= rows[:, None]

    def at_segment(table, sentinel):
        ext = jnp.concatenate([table, jnp.array([sentinel], I32)])
        return ext[0] + jnp.sum(jnp.where(passed, (ext[1:] - ext[:-1])[None, :], 0), axis=1)

    seg_start = at_segment(g_start, 1 << 30)
    valid = rows >= seg_start
    src_row = at_segment(src0, 0) + rows - seg_start
    chunk_src = jnp.where(valid, src_row // CHUNK, 0).astype(I32)
    chunk_dst = jnp.where(valid, src_row // CHUNK, -1).astype(I32)
    blk_rows = jnp.arange(n_blocks, dtype=I32) * EXPERT_BLOCK
    block_region = jnp.minimum(jnp.sum((pad_end[None, :] <= blk_rows[:, None]).astype(I32), axis=1),
                               N_EXPERTS).astype(I32)
    counts = jnp.stack([pad_end[N_EXPERTS - 1], pad_end[N_EXPERTS]]).astype(I32) // EXPERT_BLOCK
    ridx = jnp.arange(N_EXPERTS + 1, dtype=I32)
    later = jnp.logical_and(ridx[None, :] > ridx[:, None], (region > 0)[None, :])
    next_region = jnp.min(jnp.where(later, ridx[None, :], N_EXPERTS + 1), axis=1)
    next_of_block = jnp.sum(jnp.where(block_region[:, None] == ridx[None, :], next_region[None, :], 0), axis=1)
    return block_region, next_of_block.astype(I32), counts, chunk_src, chunk_dst


def kernel(x_prompt, x_sample, state_conv, cache_k_win, cache_v_win, rel_bias, attn_norm_w, w_in, conv_w,
           q_norm_w, k_norm_w, sinks, w_out, ffn_norm_w, w_router, b_router, w_up, b_up, w_down, b_down):
    batch, seq, _ = x_prompt.shape
    nb = x_sample.shape[0]
    anw = attn_norm_w[0].reshape(1, D_MODEL)
    fnw = ffn_norm_w[0].reshape(1, D_MODEL)
    win_bf = w_in[0].astype(BF16)
    wout_bf = w_out[0].astype(BF16)
    qnw = jnp.tile(q_norm_w[0], 2).reshape(1, LANES)
    knw = jnp.tile(k_norm_w[0], 2).reshape(1, LANES)
    wr_t = w_router[0].T
    br = b_router[0].reshape(N_EXPERTS, 1)
    prompt_tab, sample_tab = _bias_tables(rel_bias)

    k_past = cache_k_win[0].reshape(nb, WINDOW, LANES)
    v_past = cache_v_win[0].reshape(nb, WINDOW, LANES)
    x1s, h2s, routes, unew, new_k_sample, new_v_sample = _mixer_sample(
        x_sample.reshape(nb, D_MODEL), state_conv[0, :, 0, :], state_conv[0, :, 1, :], k_past, v_past,
        sinks[0].reshape(N_HEADS, 1), anw, win_bf, conv_w[0], qnw, knw, sample_tab, wout_bf, fnw, wr_t, br)

    sink_rows = jnp.repeat(sinks[0].reshape(N_KV, GROUP), ATTN_BLOCK, axis=1)
    x1p, xs_big, lpp, lps, stat, kp, vp, cp = _mixer_prompt(
        x_prompt, sink_rows, anw, win_bf, conv_w[0], qnw, knw, prompt_tab, wout_bf, fnw, wr_t, br, h2s, routes)

    n_tiles = stat.shape[0]
    lrows = _local_rows(MIXER_TILE)
    xs_rows = n_tiles * lrows
    seg = stat[:, :, 0].astype(I32)
    starts = stat[:, :, 1].astype(I32)
    tile_base = jnp.arange(n_tiles, dtype=I32) * lrows
    tile_rows = jnp.full((n_tiles,), lrows, I32)
    n_blocks = -(-(xs_rows + (N_EXPERTS + 1) * (EXPERT_BLOCK - 1)) // EXPERT_BLOCK)
    block_expert, next_expert, n_active, chunk_src, chunk_dst = _chunk_tables(seg, starts, tile_base, tile_rows,
                                                                              n_blocks)

    ys_big = _experts(block_expert, next_expert, n_active, chunk_src, chunk_dst, xs_big, xs_rows, w_up[0], b_up[0],
                      w_down[0], b_down[0])

    y_prompt = _combine(lpp, ys_big, x1p, MIXER_TILE, lrows, 0).reshape(batch, seq, D_MODEL)
    y_sample = _combine(lps, ys_big, x1s, nb, lrows, n_tiles - 1).reshape(nb, 1, D_MODEL)

    new_conv_sample = jnp.stack([state_conv[0, :, 1, :], unew], axis=1)
    return (
        y_prompt,
        y_sample,
        cp[None],
        kp.reshape(1, batch, ATTN_BLOCK, N_KV, HEAD_DIM),
        vp.reshape(1, batch, ATTN_BLOCK, N_KV, HEAD_DIM),
        new_conv_sample[None],
        new_k_sample.reshape(1, nb, WINDOW, N_KV, HEAD_DIM),
        new_v_sample.reshape(1, nb, WINDOW, N_KV, HEAD_DIM),
    )
```

```python
import functools
import math

import numpy as np
import jax
import jax.numpy as jnp
from jax import lax
from jax.experimental import pallas as pl
from jax.experimental.pallas import tpu as pltpu

F32 = jnp.float32
BF16 = jnp.bfloat16
I32 = jnp.int32

D_MODEL = 1024
HEAD_DIM = 64
N_HEADS = 16
N_KV = 2
GROUP = N_HEADS // N_KV
WINDOW = 128
ATTN_BLOCK = 128
N_BUCKETS = 32
MAX_DISTANCE = 128
NEG_INF = -1e30
N_EXPERTS = 32
TOP_K = 4
SWIGLU_ALPHA = 1.702
SWIGLU_LIMIT = 7.0
EPS = 1e-5
ATTN_SCALE = HEAD_DIM ** -0.5

OFF_XIN, OFF_BG, OFF_CG, OFF_Q = 0, 1024, 2048, 3072
OFF_K, OFF_V, OFF_GC, OFF_GA = 4096, 4224, 4352, 5376
IN_DIM = 6400

LANES = 128
MIXER_TILE = 512
EXPERT_BLOCK = 512
FFN_ROWS = 128
CHUNK = 16
CHUNKS_PER_BLOCK = EXPERT_BLOCK // CHUNK
XS_WIDTH = D_MODEL + LANES
VMEM_LIMIT = 60 * 1024 * 1024


def _local_rows(tm):
    need = tm * TOP_K + N_EXPERTS * (CHUNK - 1)
    return -(-need // 512) * 512


def _t5_bucket_np(rel):
    n = np.maximum(rel, 0)
    max_exact = N_BUCKETS // 2
    nf = np.maximum(n, 1).astype(np.float64)
    large = max_exact + (np.log(nf / max_exact) / math.log(MAX_DISTANCE / max_exact)
                         * (N_BUCKETS - max_exact)).astype(np.int32)
    large = np.minimum(large, N_BUCKETS - 1)
    return np.where(n < max_exact, n, large).astype(np.int32)


def _rms(x, w):
    return x * lax.rsqrt(jnp.mean(x * x, axis=-1, keepdims=True) + EPS) * w


def _lane_iota(rows=1):
    return lax.broadcasted_iota(I32, (rows, LANES), 1)


def _lo_half():
    return _lane_iota() < HEAD_DIM


def _pair_norm(t, w128):
    lo = _lo_half()
    sq = t * t
    s_lo = jnp.sum(jnp.where(lo, sq, 0.0), axis=-1, keepdims=True)
    s_hi = jnp.sum(jnp.where(lo, 0.0, sq), axis=-1, keepdims=True)
    r = jnp.where(lo, lax.rsqrt(s_lo * (1.0 / HEAD_DIM) + EPS), lax.rsqrt(s_hi * (1.0 / HEAD_DIM) + EPS))
    return t * r * w128


def _top4_gates(logits_t):
    tokens = logits_t.shape[1]
    expert = lax.broadcasted_iota(I32, (N_EXPERTS, tokens), 0).astype(F32)
    vals, idxs = [], []
    l = logits_t
    for _ in range(TOP_K):
        m = jnp.max(l, axis=0, keepdims=True)
        idx = jnp.min(jnp.where(l == m, expert, float(N_EXPERTS)), axis=0, keepdims=True)
        vals.append(m)
        idxs.append(idx)
        l = jnp.where(expert == idx, -jnp.inf, l)
    es = [jnp.exp(v - vals[0]) for v in vals]
    den = es[0] + es[1] + es[2] + es[3]
    return idxs, [e / den for e in es]


def _rows_to_sublanes(rows, n_rows, tokens, fill=0.0):
    sub = lax.broadcasted_iota(I32, (n_rows, tokens), 0)
    out = jnp.full((n_rows, tokens), fill, F32)
    for k, r in enumerate(rows):
        out = jnp.where(sub == k, r, out)
    return out


def _onehot_row_builder(lp_t, tm):
    lp_group = jnp.floor(lp_t * (1.0 / LANES))
    lp_off = lp_t - lp_group * LANES
    row_off = lax.broadcasted_iota(I32, (LANES, tm), 0).astype(F32).astype(BF16)
    one_bf = jnp.ones((LANES, tm), BF16)
    zero_bf = jnp.zeros((LANES, tm), BF16)

    def build(r0, rows):
        parts = []
        for s in range(rows // LANES):
            group = float(r0 // LANES + s)
            hit = None
            for k in range(TOP_K):
                off_k = jnp.where(lp_group[k:k + 1, :] == group, lp_off[k:k + 1, :], -1.0).astype(BF16)
                hit_k = row_off == off_k
                hit = hit_k if hit is None else jnp.logical_or(hit, hit_k)
            parts.append(jnp.where(hit, one_bf, zero_bf))
        return jnp.concatenate(parts, axis=0)

    return build


def _dispatch(h2, idxs, gates, xs_ref, lp_ref, stat_ref):
    tm = h2.shape[0]
    lrows = xs_ref.shape[0]
    expert = lax.broadcasted_iota(I32, (N_EXPERTS, tm), 0).astype(F32)
    member = jnp.zeros((N_EXPERTS, tm), F32)
    for k in range(TOP_K):
        member = member + jnp.where(expert == idxs[k], 1.0, 0.0)
    cnt = jnp.broadcast_to(jnp.sum(member, axis=1, keepdims=True), (N_EXPERTS, LANES))
    seg = jnp.floor((cnt + (CHUNK - 1)) * (1.0 / CHUNK)) * CHUNK
    sub32 = lax.broadcasted_iota(I32, (N_EXPERTS, LANES), 0)
    incl = seg
    for s in (1, 2, 4, 8, 16):
        incl = incl + jnp.where(sub32 >= s, pltpu.roll(incl, s, axis=0), 0.0)
    starts = incl - seg
    lane32 = _lane_iota(N_EXPERTS)
    stat_ref[0] = jnp.where(lane32 == 0, seg, jnp.where(lane32 == 1, starts, 0.0))

    earlier = (lax.broadcasted_iota(I32, (tm, tm), 0) < lax.broadcasted_iota(I32, (tm, tm), 1)).astype(BF16)
    rank = jnp.dot(member.astype(BF16), earlier, preferred_element_type=F32)
    pos = starts[:, 0:1] + rank
    lps = [jnp.sum(jnp.where(expert == idxs[k], pos, 0.0), axis=0, keepdims=True) for k in range(TOP_K)]
    lp_t = _rows_to_sublanes(lps, 8, tm, fill=-1.0)
    lp_ref[0] = lp_t

    g_hi = [g.astype(BF16).astype(F32) for g in gates]
    g_lo = [g - h for g, h in zip(gates, g_hi)]
    meta_in = _rows_to_sublanes(list(idxs) + g_hi + g_lo, LANES, tm).T
    rhs = jnp.concatenate([h2.astype(BF16), meta_in.astype(BF16)], axis=1)

    starts32 = jnp.concatenate([starts, jnp.full((LANES - N_EXPERTS, LANES), 1e9, F32)], axis=0).T[0:1, :]
    onehot_rows = _onehot_row_builder(lp_t, tm)

    bounds = list(range(0, lrows - 512, 512)) + [lrows - 512, lrows - 256]
    for r0, r1 in zip(bounds, bounds[1:] + [lrows]):
        sub = r1 - r0
        rid = (lax.broadcasted_iota(I32, (sub, 1), 0) + r0).astype(F32)
        full = jnp.dot(onehot_rows(r0, sub), rhs, preferred_element_type=F32)
        got = full[:, D_MODEL:]
        e_row = jnp.sum(jnp.where(rid >= starts32, 1.0, 0.0), axis=-1, keepdims=True) - 1.0
        g_sum = pltpu.roll(got, LANES - TOP_K, axis=1) + pltpu.roll(got, LANES - 2 * TOP_K, axis=1)
        lane_s = _lane_iota(sub)
        pick = jnp.logical_and(lane_s < TOP_K, got == e_row)
        gate_row = jnp.sum(jnp.where(pick, g_sum, 0.0), axis=-1, keepdims=True)
        gr_hi = gate_row.astype(BF16).astype(F32)
        meta = jnp.where(lane_s == 0, gr_hi, jnp.where(lane_s == 1, gate_row - gr_hi, 0.0))
        xs_ref[r0:r0 + sub, 0:D_MODEL] = full[:, :D_MODEL].astype(BF16)
        xs_ref[r0:r0 + sub, D_MODEL:XS_WIDTH] = meta.astype(BF16)


def _epilogue(x, merged, wout_ref, fnw_ref, wr_ref, br_ref, x1_ref):
    x1 = x + jnp.dot(merged.astype(BF16), wout_ref[...], preferred_element_type=F32)
    x1_ref[...] = x1
    h2 = _rms(x1, fnw_ref[...]).astype(BF16)
    logits_t = lax.dot_general(wr_ref[...].astype(BF16), h2, (((1,), (1,)), ((), ())),
                               preferred_element_type=F32) + br_ref[...]
    idxs, gates = _top4_gates(logits_t)
    return h2, idxs, gates


def _mixer_prompt_kernel(nj, sink_ref, x_ref, anw_ref, win_ref, convw_ref, qnw_ref, knw_ref, bias_ref, wout_ref,
                         fnw_ref, wr_ref, br_ref, h2s_ref, routes_ref,
                         x1_ref, xs_ref, lp_ref, lps_ref, stat_ref, kout_ref, vout_ref, cout_ref,
                         ubuf, q_s, kd_s, vt_s, ya_s, st_s, pt_s):
    i = pl.program_id(0)
    n_tiles = pl.num_programs(0) - 1

    @pl.when(i < n_tiles)
    def _():
        _prompt_tile(lax.rem(i, nj), nj, sink_ref, x_ref, anw_ref, win_ref, convw_ref, qnw_ref, knw_ref, bias_ref,
                     wout_ref, fnw_ref, wr_ref, br_ref, x1_ref, xs_ref, lp_ref, stat_ref, kout_ref, vout_ref,
                     cout_ref, ubuf, q_s, kd_s, vt_s, ya_s, st_s, pt_s)

    @pl.when(i == n_tiles)
    def _():
        route = routes_ref[...]
        _dispatch(h2s_ref[...], [route[k:k + 1, :] for k in range(TOP_K)],
                  [route[TOP_K + k:TOP_K + k + 1, :] for k in range(TOP_K)], xs_ref, lps_ref, stat_ref)


def _prompt_tile(j, nj, sink_ref, x_ref, anw_ref, win_ref, convw_ref, qnw_ref, knw_ref, bias_ref, wout_ref,
                 fnw_ref, wr_ref, br_ref, x1_ref, xs_ref, lp_ref, stat_ref, kout_ref, vout_ref, cout_ref,
                 ubuf, q_s, kd_s, vt_s, ya_s, st_s, pt_s):
    tm = x_ref.shape[0]
    nblk = tm // ATTN_BLOCK
    first_tile = j == 0
    lo = _lo_half()

    x = x_ref[...]
    h = _rms(x, anw_ref[...]).astype(BF16)

    def proj(off, n):
        return jnp.dot(h, win_ref[:, off:off + n], preferred_element_type=F32)

    @pl.when(first_tile)
    def _():
        ubuf[0:8, :] = jnp.zeros((8, D_MODEL), F32)
        kd_s[0:ATTN_BLOCK, :] = jnp.zeros((ATTN_BLOCK, 2 * LANES), BF16)
        vt_s[0] = jnp.zeros((N_KV, LANES, ATTN_BLOCK), BF16)

    q = proj(OFF_Q, D_MODEL)
    kv = proj(OFF_K, 2 * LANES)
    k = _pair_norm(kv[:, :LANES], knw_ref[...])
    v = kv[:, LANES:]
    qnw = qnw_ref[...]
    for p in range(N_HEADS // 2):
        sl = slice(p * LANES, (p + 1) * LANES)
        q_s[:, sl] = (_pair_norm(q[:, sl], qnw) * ATTN_SCALE).astype(BF16)
    k_sw = pltpu.roll(k, HEAD_DIM, axis=1)
    v_sw = pltpu.roll(v, HEAD_DIM, axis=1)
    kd_s[ATTN_BLOCK:tm + ATTN_BLOCK, 0:LANES] = jnp.where(lo, k, k_sw).astype(BF16)
    kd_s[ATTN_BLOCK:tm + ATTN_BLOCK, LANES:2 * LANES] = jnp.where(lo, k_sw, k).astype(BF16)
    v_dup = (jnp.where(lo, v, v_sw), jnp.where(lo, v_sw, v))
    for b in range(nblk):
        for g in range(N_KV):
            vt_s[b + 1, g] = v_dup[g][b * ATTN_BLOCK:(b + 1) * ATTN_BLOCK, :].T.astype(BF16)

    u = proj(OFF_CG, D_MODEL) * proj(OFF_XIN, D_MODEL)
    ubuf[8:tm + 8, :] = u
    cw = convw_ref[...]
    conv = ubuf[6:tm + 6, :] * cw[0:1, :] + ubuf[7:tm + 7, :] * cw[1:2, :] + u * cw[2:3, :]
    merged = jax.nn.sigmoid(proj(OFF_GC, D_MODEL)) * (proj(OFF_BG, D_MODEL) * conv)
    tail = ubuf[tm + 6:tm + 8, :]
    cout_ref[0] = tail
    ubuf[6:8, :] = tail

    @pl.when(j == nj - 1)
    def _():
        kout_ref[0] = k[tm - ATTN_BLOCK:, :]
        vout_ref[0] = v[tm - ATTN_BLOCK:, :]

    prev_rows = lax.broadcasted_iota(I32, (2 * ATTN_BLOCK, 1), 0) < ATTN_BLOCK
    feat_lo = lax.broadcasted_iota(I32, (LANES, 1), 0) < HEAD_DIM

    units = [(blk, g) for blk in range(nblk) for g in range(N_KV)]

    def scores(n):
        blk, g = units[n]
        r0 = blk * ATTN_BLOCK
        qb = q_s[r0:r0 + ATTN_BLOCK, :]
        parts = []
        for t in range(GROUP):
            hd = g * GROUP + t
            slab = qb[:, (hd // 2) * LANES:(hd // 2 + 1) * LANES]
            keep = lo if hd % 2 == 0 else jnp.logical_not(lo)
            parts.append(jnp.where(keep, slab, jnp.zeros_like(slab)))
        lhs = jnp.concatenate(parts, axis=0)
        st = lax.dot_general(kd_s[r0:r0 + 2 * ATTN_BLOCK, g * LANES:(g + 1) * LANES], lhs,
                             (((1,), (1,)), ((), ())), preferred_element_type=F32)
        st = st + bias_ref[g]
        if blk == 0:
            st = jnp.where(jnp.logical_and(prev_rows, first_tile), NEG_INF, st)
        st_s[n % 2] = st

    def softmax_values(n):
        blk, g = units[n]
        r0 = blk * ATTN_BLOCK
        sink = sink_ref[g:g + 1, :]
        rdens = []
        for t in range(GROUP):
            cols = slice(t * LANES, (t + 1) * LANES)
            s = st_s[n % 2, :, cols]
            m = jnp.maximum(jnp.max(s, axis=0, keepdims=True), sink[:, cols])
            pr = jnp.exp(s - m)
            den = jnp.sum(pr, axis=0, keepdims=True) + jnp.exp(sink[:, cols] - m)
            pt_s[:, cols] = pr.astype(BF16)
            rdens.append(1.0 / den)
        vt = jnp.concatenate([vt_s[blk, g], vt_s[blk + 1, g]], axis=1)
        ot = jnp.dot(vt, pt_s[...], preferred_element_type=F32)
        for i in range(GROUP // 2):
            pair = g * (GROUP // 2) + i
            even = ot[:, (2 * i) * LANES:(2 * i + 1) * LANES] * rdens[2 * i]
            odd = ot[:, (2 * i + 1) * LANES:(2 * i + 2) * LANES] * rdens[2 * i + 1]
            ya_s[r0:r0 + ATTN_BLOCK, pair * LANES:(pair + 1) * LANES] = jnp.where(feat_lo, even, odd).T

    ga_parts = []
    ga_cols = D_MODEL // (len(units) // 2)
    scores(0)
    for n in range(len(units)):
        if n + 1 < len(units):
            scores(n + 1)
        if n % 2 == 0:
            ga_parts.append(jax.nn.sigmoid(proj(OFF_GA + (n // 2) * ga_cols, ga_cols)))
        softmax_values(n)

    kd_s[0:ATTN_BLOCK, :] = kd_s[tm:tm + ATTN_BLOCK, :]
    vt_s[0] = vt_s[nblk]

    merged = merged + jnp.concatenate(ga_parts, axis=1) * ya_s[...]
    h2, idxs, gates = _epilogue(x_ref[...], merged, wout_ref, fnw_ref, wr_ref, br_ref, x1_ref)
    _dispatch(h2, idxs, gates, xs_ref, lp_ref, stat_ref)


def _const_spec(shape):
    nd = len(shape)
    return pl.BlockSpec(shape, lambda *_: (0,) * nd, pipeline_mode=pl.Buffered(1))


def _mixer_prompt(x, sink_rows, anw, win_bf, convw, qnw, knw, bias_tab, wout_bf, fnw, wr_t, br_col, h2s, routes):
    batch, seq, _ = x.shape
    tm = MIXER_TILE
    lrows = _local_rows(tm)
    nj = seq // tm
    tokens = batch * seq
    n_tiles = batch * nj
    nb = h2s.shape[0]
    x2 = x.reshape(tokens, D_MODEL)
    tile = lambda i: jnp.minimum(i, n_tiles - 1)
    tok_spec = lambda width: pl.BlockSpec((tm, width), lambda i: (tile(i), 0))
    per_batch = lambda rows, width: pl.BlockSpec((1, rows, width), lambda i: (tile(i) // nj, 0, 0))
    in_specs = [
        _const_spec((N_KV, GROUP * ATTN_BLOCK)),
        tok_spec(D_MODEL),
        _const_spec((1, D_MODEL)),
        _const_spec((D_MODEL, IN_DIM)),
        _const_spec((3, D_MODEL)),
        _const_spec((1, LANES)),
        _const_spec((1, LANES)),
        _const_spec((N_KV, 2 * ATTN_BLOCK, GROUP * ATTN_BLOCK)),
        _const_spec((D_MODEL, D_MODEL)),
        _const_spec((1, D_MODEL)),
        _const_spec((N_EXPERTS, D_MODEL)),
        _const_spec((N_EXPERTS, 1)),
        _const_spec((nb, D_MODEL)),
        _const_spec((2 * TOP_K, nb)),
    ]
    out_shape = (
        jax.ShapeDtypeStruct((tokens, D_MODEL), F32),
        jax.ShapeDtypeStruct(((n_tiles + 1) * lrows, XS_WIDTH), BF16),
        jax.ShapeDtypeStruct((n_tiles, 8, tm), F32),
        jax.ShapeDtypeStruct((1, 8, nb), F32),
        jax.ShapeDtypeStruct((n_tiles + 1, N_EXPERTS, LANES), F32),
        jax.ShapeDtypeStruct((batch, ATTN_BLOCK, LANES), F32),
        jax.ShapeDtypeStruct((batch, ATTN_BLOCK, LANES), F32),
        jax.ShapeDtypeStruct((batch, 2, D_MODEL), F32),
    )
    out_specs = (
        tok_spec(D_MODEL),
        pl.BlockSpec((lrows, XS_WIDTH), lambda i: (i, 0)),
        pl.BlockSpec((1, 8, tm), lambda i: (tile(i), 0, 0)),
        pl.BlockSpec((1, 8, nb), lambda i: (0, 0, 0)),
        pl.BlockSpec((1, N_EXPERTS, LANES), lambda i: (i, 0, 0)),
        per_batch(ATTN_BLOCK, LANES), per_batch(ATTN_BLOCK, LANES), per_batch(2, D_MODEL),
    )
    scratch = [
        pltpu.VMEM((tm + 8, D_MODEL), F32),
        pltpu.VMEM((tm, D_MODEL), BF16),
        pltpu.VMEM((tm + ATTN_BLOCK, 2 * LANES), BF16),
        pltpu.VMEM((tm // ATTN_BLOCK + 1, N_KV, LANES, ATTN_BLOCK), BF16),
        pltpu.VMEM((tm, D_MODEL), F32),
        pltpu.VMEM((2, 2 * ATTN_BLOCK, GROUP * ATTN_BLOCK), F32),
        pltpu.VMEM((2 * ATTN_BLOCK, GROUP * ATTN_BLOCK), BF16),
    ]
    return pl.pallas_call(
        functools.partial(_mixer_prompt_kernel, nj),
        grid=(n_tiles + 1,),
        in_specs=in_specs,
        out_specs=out_specs,
        out_shape=out_shape,
        scratch_shapes=scratch,
        compiler_params=pltpu.CompilerParams(dimension_semantics=("arbitrary",), vmem_limit_bytes=VMEM_LIMIT),
        name="mixer_prompt",
    )(sink_rows, x2, anw, win_bf, convw, qnw, knw, bias_tab, wout_bf, fnw, wr_t, br_col, h2s, routes)


SAMPLE_CHUNK = 32
SAMPLE_GROUP = 8


def _mixer_sample_kernel(x_ref, p0_ref, p1_ref, kp_ref, vp_ref, sink_ref, anw_ref, win_ref, convw_ref,
                         qnw_ref, knw_ref, bias_ref, wout_ref, fnw_ref, wr_ref, br_ref,
                         x1_ref, h2_ref, route_ref, unew_ref, kc_ref, vc_ref,
                         qh_s, o_s, kn_s, vn_s, conv_s, ga_s):
    c = pl.program_id(0)
    nb = x_ref.shape[0]
    tc = kp_ref.shape[0]
    lo = _lo_half()

    @pl.when(c == 0)
    def _():
        h = _rms(x_ref[...], anw_ref[...]).astype(BF16)

        def proj(off, n):
            return jnp.dot(h, win_ref[:, off:off + n], preferred_element_type=F32)

        u = proj(OFF_CG, D_MODEL) * proj(OFF_XIN, D_MODEL)
        unew_ref[...] = u
        cw = convw_ref[...]
        conv = p0_ref[...] * cw[0:1, :] + p1_ref[...] * cw[1:2, :] + u * cw[2:3, :]
        conv_s[...] = jax.nn.sigmoid(proj(OFF_GC, D_MODEL)) * (proj(OFF_BG, D_MODEL) * conv)
        ga_s[...] = jax.nn.sigmoid(proj(OFF_GA, D_MODEL))

        q = proj(OFF_Q, D_MODEL)
        kv = proj(OFF_K, 2 * LANES)
        kn_s[...] = _pair_norm(kv[:, :LANES], knw_ref[...])
        vn_s[...] = kv[:, LANES:]

        qnw = qnw_ref[...]
        for hd in range(N_HEADS):
            pair, half, grp = hd // 2, hd % 2, hd // GROUP
            slab = _pair_norm(q[:, pair * LANES:(pair + 1) * LANES], qnw) * ATTN_SCALE
            slab = jnp.where(lo if half == 0 else jnp.logical_not(lo), slab, 0.0)
            if half != grp:
                slab = pltpu.roll(slab, HEAD_DIM, axis=1)
            qh_s[hd * nb:(hd + 1) * nb, :] = slab

    sink = sink_ref[...]
    bias = bias_ref[...]
    rows = lax.broadcasted_iota(I32, (WINDOW, 1), 0)
    row0 = rows == 0
    row_last = rows == WINDOW - 1

    grp = SAMPLE_GROUP
    for ci in range(tc // grp):
        t0 = ci * grp
        qs, ks, vs = [], [], []
        for t in range(t0, t0 + grp):
            b = c * tc + t
            qs.append(qh_s[pl.ds(b, N_HEADS, stride=nb), :])
            k_new = kn_s[pl.ds(b, 1), :]
            v_new = vn_s[pl.ds(b, 1), :]
            k_old = kp_ref[t]
            v_old = vp_ref[t]
            kc_ref[t] = jnp.where(row_last, k_new, pltpu.roll(k_old, WINDOW - 1, axis=0))
            vc_ref[t] = jnp.where(row_last, v_new, pltpu.roll(v_old, WINDOW - 1, axis=0))
            ks.append(jnp.where(row0, k_new, k_old).astype(BF16))
            vs.append(jnp.where(row0, v_new, v_old).astype(BF16))
        q_all = jnp.concatenate(qs, axis=0).astype(BF16)
        s_all = lax.dot_general(q_all, jnp.concatenate(ks, axis=0), (((1,), (1,)), ((), ())),
                                preferred_element_type=F32)
        p_rows, rdens = [], []
        for g in range(grp):
            s = s_all[g * N_HEADS:(g + 1) * N_HEADS, g * WINDOW:(g + 1) * WINDOW] + bias
            m = jnp.maximum(jnp.max(s, axis=-1, keepdims=True), sink)
            pr = jnp.exp(s - m)
            rdens.append(1.0 / (jnp.sum(pr, axis=-1, keepdims=True) + jnp.exp(sink - m)))
            zero = jnp.zeros((N_HEADS, WINDOW), BF16)
            p_rows.append(jnp.concatenate([pr.astype(BF16) if j == g else zero for j in range(grp)], axis=1))
        o_all = jnp.dot(jnp.concatenate(p_rows, axis=0), jnp.concatenate(vs, axis=0),
                        preferred_element_type=F32)
        for g in range(grp):
            b = c * tc + t0 + g
            o_s[pl.ds(b, N_HEADS, stride=nb), :] = o_all[g * N_HEADS:(g + 1) * N_HEADS, :] * rdens[g]

    @pl.when(c == pl.num_programs(0) - 1)
    def _():
        cols = []
        for pair in range(N_HEADS // 2):
            halves = []
            for half in range(2):
                hd = 2 * pair + half
                slab = o_s[hd * nb:(hd + 1) * nb, :]
                if half != hd // GROUP:
                    slab = pltpu.roll(slab, HEAD_DIM, axis=1)
                halves.append(slab)
            cols.append(jnp.where(lo, halves[0], halves[1]))
        merged = conv_s[...] + ga_s[...] * jnp.concatenate(cols, axis=1)
        h2, idxs, gates = _epilogue(x_ref[...], merged, wout_ref, fnw_ref, wr_ref, br_ref, x1_ref)
        h2_ref[...] = h2
        route_ref[...] = _rows_to_sublanes(list(idxs) + list(gates), 2 * TOP_K, nb)


def _mixer_sample(x, p0, p1, k_past, v_past, sink_col, anw, win_bf, convw, qnw, knw, bias_s, wout_bf, fnw, wr, br):
    nb = x.shape[0]
    tc = SAMPLE_CHUNK
    consts = (x, p0, p1)
    params = (sink_col, anw, win_bf, convw, qnw, knw, bias_s, wout_bf, fnw, wr, br)
    cache_spec = pl.BlockSpec((tc, WINDOW, LANES), lambda c: (c, 0, 0))
    full = lambda shape: pl.BlockSpec(shape, lambda c: (0,) * len(shape))
    out_shape = (
        jax.ShapeDtypeStruct((nb, D_MODEL), F32),
        jax.ShapeDtypeStruct((nb, D_MODEL), BF16),
        jax.ShapeDtypeStruct((2 * TOP_K, nb), F32),
        jax.ShapeDtypeStruct((nb, D_MODEL), F32),
        jax.ShapeDtypeStruct((nb, WINDOW, LANES), F32),
        jax.ShapeDtypeStruct((nb, WINDOW, LANES), F32),
    )
    out_specs = tuple(full(s.shape) for s in out_shape[:4]) + (cache_spec, cache_spec)
    scratch = [
        pltpu.VMEM((N_HEADS * nb, LANES), F32),
        pltpu.VMEM((N_HEADS * nb, LANES), F32),
        pltpu.VMEM((nb, LANES), F32),
        pltpu.VMEM((nb, LANES), F32),
        pltpu.VMEM((nb, D_MODEL), F32),
        pltpu.VMEM((nb, D_MODEL), F32),
    ]
    return pl.pallas_call(
        _mixer_sample_kernel,
        grid=(nb // tc,),
        in_specs=[_const_spec(a.shape) for a in consts] + [cache_spec, cache_spec]
        + [_const_spec(a.shape) for a in params],
        out_specs=out_specs,
        out_shape=out_shape,
        scratch_shapes=scratch,
        compiler_params=pltpu.CompilerParams(dimension_semantics=("arbitrary",), vmem_limit_bytes=VMEM_LIMIT),
        name="mixer_sample",
    )(*consts, k_past, v_past, *params)


def _expert_kernel(be_ref, nxt_ref, nact_ref, src_ref, dst_ref, xs_hbm, wup_hbm, bup_ref, wdn_hbm, bdn_ref, ys_hbm,
                   wup_s, wdn_s, perm_s, wup_stage, wdn_stage, xbuf, ybuf, in_sem, out_sem, w_sem):
    i = pl.program_id(0)
    n_real = nact_ref[0]
    n_active = nact_ref[1]
    e = be_ref[i]
    e_prev = be_ref[jnp.maximum(i - 1, 0)]
    slot = lax.rem(i, 2)

    def weight_copies(expert):
        return (pltpu.make_async_copy(wup_hbm.at[expert], wup_stage, w_sem.at[0]),
                pltpu.make_async_copy(wdn_hbm.at[expert], wdn_stage, w_sem.at[1]))

    def gather_copy(blk, slt, c):
        row = pl.multiple_of(src_ref[blk * CHUNKS_PER_BLOCK + c] * CHUNK, CHUNK)
        return pltpu.make_async_copy(xs_hbm.at[pl.ds(row, CHUNK), :],
                                     xbuf.at[slt, pl.ds(c * CHUNK, CHUNK), :], in_sem.at[slt])

    def scatter_chunks(blk, slt, wait):
        base = blk * CHUNKS_PER_BLOCK

        def one(c, dst):
            row = pl.multiple_of(dst * CHUNK, CHUNK)
            cp = pltpu.make_async_copy(ybuf.at[slt, pl.ds(c * CHUNK, CHUNK), :],
                                       ys_hbm.at[pl.ds(row, CHUNK), :], out_sem.at[slt])
            if wait:
                cp.wait()
            else:
                cp.start()

        full = dst_ref[base + CHUNKS_PER_BLOCK - 1] >= 0

        @pl.when(full)
        def _():
            for c in range(CHUNKS_PER_BLOCK):
                one(c, dst_ref[base + c])

        @pl.when(jnp.logical_not(full))
        def _():
            for c in range(CHUNKS_PER_BLOCK - 1):
                dst = dst_ref[base + c]

                @pl.when(dst >= 0)
                def _():
                    one(c, dst)

    @pl.when(jnp.logical_and(i == 0, n_real > 0))
    def _():
        for c in range(CHUNKS_PER_BLOCK):
            gather_copy(0, 0, c).start()

    @pl.when(jnp.logical_and(i == 0, n_real > 0))
    def _():
        for cp in weight_copies(e):
            cp.start()

    @pl.when(jnp.logical_and(i < n_real, jnp.logical_or(i == 0, e != e_prev)))
    def _():
        for cp in weight_copies(e):
            cp.wait()
        wup_s[...] = wup_stage[...].astype(BF16)
        half = LANES // 2
        for cs in range(D_MODEL // LANES):
            cols = slice(cs * LANES, (cs + 1) * LANES)
            for c in range(D_MODEL // LANES):
                for par in range(2):
                    s0 = c * LANES + par * half
                    perm_s[cs, pl.ds(c * LANES + par, half, stride=2), :] = wdn_stage[s0:s0 + half, cols]
            wdn_s[:, cols] = perm_s[cs].astype(BF16)
        nxt = nxt_ref[i]

        @pl.when(nxt < N_EXPERTS)
        def _():
            for cp in weight_copies(nxt):
                cp.start()

    @pl.when(i + 1 < n_real)
    def _():
        for c in range(CHUNKS_PER_BLOCK):
            gather_copy(i + 1, 1 - slot, c).start()

    @pl.when(jnp.logical_and(i >= 2, i < n_active))
    def _():
        scatter_chunks(i - 2, slot, wait=True)

    @pl.when(jnp.logical_and(i >= n_real, i < n_active))
    def _():
        ybuf[slot] = jnp.zeros((EXPERT_BLOCK, D_MODEL), BF16)
        scatter_chunks(i, slot, wait=False)

    def ffn(rows):
        for c in range(CHUNKS_PER_BLOCK):
            gather_copy(i, slot, c).wait()

        xb = xbuf[slot, 0:rows, :]
        meta = xb[:, D_MODEL:].astype(F32)
        gate = meta[:, 0:1] + meta[:, 1:2]
        u = jnp.dot(xb[:, :D_MODEL], wup_s[...], preferred_element_type=F32) + bup_ref[0]
        even = (_lane_iota() & 1) == 0
        cols = []
        for c in range(D_MODEL // LANES):
            c0 = u[:, (2 * c) * LANES:(2 * c + 1) * LANES]
            c1 = u[:, (2 * c + 1) * LANES:(2 * c + 2) * LANES]
            glu = jnp.where(even, c0, pltpu.roll(c1, 1, axis=1))
            lin = jnp.where(even, pltpu.roll(c0, LANES - 1, axis=1), c1)
            glu = jnp.minimum(glu, SWIGLU_LIMIT)
            lin = jnp.clip(lin, -SWIGLU_LIMIT, SWIGLU_LIMIT)
            cols.append(glu * jax.nn.sigmoid(SWIGLU_ALPHA * glu) * (lin + 1.0))
        a = jnp.concatenate(cols, axis=1).astype(BF16)
        y = jnp.dot(a, wdn_s[...], preferred_element_type=F32) + bdn_ref[0]
        ybuf[slot, 0:rows, :] = (y * gate).astype(BF16)
        scatter_chunks(i, slot, wait=False)

    part = FFN_ROWS // CHUNK
    n_parts = EXPERT_BLOCK // FFN_ROWS
    base = i * CHUNKS_PER_BLOCK
    parts_used = 1 + sum((dst_ref[base + q * part] >= 0).astype(I32) for q in range(1, n_parts))
    for q in range(1, n_parts + 1):
        @pl.when(jnp.logical_and(i < n_real, parts_used == q))
        def _(q=q):
            ffn(q * FFN_ROWS)

    @pl.when(i == n_active - 1)
    def _():
        @pl.when(i >= 1)
        def _():
            scatter_chunks(i - 1, 1 - slot, wait=True)

        scatter_chunks(i, slot, wait=True)


def _experts(block_expert, next_expert, n_active, chunk_src, chunk_dst, xs_big, ys_rows, w_up, b_up, w_down, b_down):
    n_blocks = block_expert.shape[0]
    bspec = lambda shape: pl.BlockSpec(
        shape, lambda i, be, nx, na, cs, cd: (jnp.minimum(be[i], N_EXPERTS - 1), 0, 0))
    grid_spec = pltpu.PrefetchScalarGridSpec(
        num_scalar_prefetch=5,
        grid=(n_blocks,),
        in_specs=[
            pl.BlockSpec(memory_space=pl.ANY),
            pl.BlockSpec(memory_space=pl.ANY),
            bspec((1, 1, 2 * D_MODEL)),
            pl.BlockSpec(memory_space=pl.ANY),
            bspec((1, 1, D_MODEL)),
        ],
        out_specs=pl.BlockSpec(memory_space=pl.ANY),
        scratch_shapes=[
            pltpu.VMEM((D_MODEL, 2 * D_MODEL), BF16),
            pltpu.VMEM((D_MODEL, D_MODEL), BF16),
            pltpu.VMEM((D_MODEL // LANES, D_MODEL, LANES), F32),
            pltpu.VMEM((D_MODEL, 2 * D_MODEL), F32),
            pltpu.VMEM((D_MODEL, D_MODEL), F32),
            pltpu.VMEM((2, EXPERT_BLOCK, XS_WIDTH), BF16),
            pltpu.VMEM((2, EXPERT_BLOCK, D_MODEL), BF16),
            pltpu.SemaphoreType.DMA((2,)),
            pltpu.SemaphoreType.DMA((2,)),
            pltpu.SemaphoreType.DMA((2,)),
        ],
    )
    return pl.pallas_call(
        _expert_kernel,
        grid_spec=grid_spec,
        out_shape=jax.ShapeDtypeStruct((ys_rows, D_MODEL), BF16),
        compiler_params=pltpu.CompilerParams(dimension_semantics=("arbitrary",), vmem_limit_bytes=VMEM_LIMIT),
        name="experts",
    )(block_expert, next_expert, n_active, chunk_src, chunk_dst, xs_big, w_up,
      b_up.reshape(N_EXPERTS, 1, 2 * D_MODEL), w_down, b_down.reshape(N_EXPERTS, 1, D_MODEL))


def _combine_kernel(lp_ref, ys_ref, x1_ref, out_ref):
    tm = x1_ref.shape[0]
    lrows = ys_ref.shape[0]
    onehot_rows = _onehot_row_builder(lp_ref[0], tm)
    y = x1_ref[...]
    step = 512
    for r0 in range(0, lrows, step):
        y = y + lax.dot_general(onehot_rows(r0, step), ys_ref[r0:r0 + step, :], (((0,), (0,)), ((), ())),
                                preferred_element_type=F32)
    out_ref[...] = y


def _combine(lp, ys_big, x1, tm, lrows, first_block):
    tokens = x1.shape[0]
    return pl.pallas_call(
        _combine_kernel,
        grid=(tokens // tm,),
        in_specs=[
            pl.BlockSpec((1, 8, tm), lambda j: (j, 0, 0)),
            pl.BlockSpec((lrows, D_MODEL), lambda j: (first_block + j, 0)),
            pl.BlockSpec((tm, D_MODEL), lambda j: (j, 0)),
        ],
        out_specs=pl.BlockSpec((tm, D_MODEL), lambda j: (j, 0)),
        out_shape=jax.ShapeDtypeStruct((tokens, D_MODEL), F32),
        compiler_params=pltpu.CompilerParams(dimension_semantics=("arbitrary",), vmem_limit_bytes=VMEM_LIMIT),
        name="combine",
    )(lp, ys_big, x1)


def _bias_tables(rel_bias):
    qi = np.arange(ATTN_BLOCK)[:, None]
    kj = np.arange(2 * ATTN_BLOCK)[None, :]
    rel = qi + ATTN_BLOCK - kj
    valid = (rel >= 0) & (rel < WINDOW)
    bucket = np.where(valid, _t5_bucket_np(rel), -1)
    slot_rel = np.where(np.arange(WINDOW) == 0, 0, WINDOW - np.arange(WINDOW))
    slot_bucket = _t5_bucket_np(slot_rel)
    buckets = np.arange(N_BUCKETS)
    onehot = bucket[None, :, :] == buckets[:, None, None]
    tab = jnp.sum(jnp.where(onehot[None], rel_bias.T[:, :, None, None], 0.0), axis=1)
    tab = jnp.where(valid[None], tab, NEG_INF)
    slot_onehot = slot_bucket[None, :] == buckets[:, None]
    sample_tab = jnp.sum(jnp.where(slot_onehot[None], rel_bias.T[:, :, None], 0.0), axis=1)
    prompt_tab = tab.reshape(N_KV, GROUP, ATTN_BLOCK, 2 * ATTN_BLOCK).transpose(0, 3, 1, 2)
    return prompt_tab.reshape(N_KV, 2 * ATTN_BLOCK, GROUP * ATTN_BLOCK), sample_tab


def _chunk_tables(seg, starts, tile_base, tile_rows, n_blocks):
    n_tiles = seg.shape[0]
    n_seg = (N_EXPERTS + 1) * n_tiles
    used = jnp.sum(seg, axis=1)
    seg_e = jnp.concatenate([seg.T, (tile_rows - used)[None, :]], axis=0)
    src0 = jnp.concatenate([tile_base[None, :] + starts.T, (tile_base + used)[None, :]], axis=0).reshape(-1)
    total = jnp.sum(seg_e, axis=1)
    region = (total + EXPERT_BLOCK - 1) // EXPERT_BLOCK * EXPERT_BLOCK
    pad_end = jnp.cumsum(region)
    pad_start = pad_end - region
    g_start = (pad_start[:, None] + jnp.cumsum(seg_e, axis=1) - seg_e).reshape(-1)
    g_end = g_start + seg_e.reshape(-1)
    rows = jnp.arange(n_blocks * CHUNKS_PER_BLOCK, dtype=I32) * CHUNK
    passed = g_end[None, :] <= rows[:, None]

    def at_segment(table, sentinel):
        ext = jnp.concatenate([table, jnp.array([sentinel], I32)])
        return ext[0] + jnp.sum(jnp.where(passed, (ext[1:] - ext[:-1])[None, :], 0), axis=1)

    seg_start = at_segment(g_start, 1 << 30)
    valid = rows >= seg_start
    src_row = at_segment(src0, 0) + rows - seg_start
    chunk_src = jnp.where(valid, src_row // CHUNK, 0).astype(I32)
    chunk_dst = jnp.where(valid, src_row // CHUNK, -1).astype(I32)
    blk_rows = jnp.arange(n_blocks, dtype=I32) * EXPERT_BLOCK
    block_region = jnp.minimum(jnp.sum((pad_end[None, :] <= blk_rows[:, None]).astype(I32), axis=1),
                               N_EXPERTS).astype(I32)
    counts = jnp.stack([pad_end[N_EXPERTS - 1], pad_end[N_EXPERTS]]).astype(I32) // EXPERT_BLOCK
    ridx = jnp.arange(N_EXPERTS + 1, dtype=I32)
    later = jnp.logical_and(ridx[None, :] > ridx[:, None], (region > 0)[None, :])
    next_region = jnp.min(jnp.where(later, ridx[None, :], N_EXPERTS + 1), axis=1)
    next_of_block = jnp.sum(jnp.where(block_region[:, None] == ridx[None, :], next_region[None, :], 0), axis=1)
    return block_region, next_of_block.astype(I32), counts, chunk_src, chunk_dst


def kernel(x_prompt, x_sample, state_conv, cache_k_win, cache_v_win, rel_bias, attn_norm_w, w_in, conv_w,
           q_norm_w, k_norm_w, sinks, w_out, ffn_norm_w, w_router, b_router, w_up, b_up, w_down, b_down):
    batch, seq, _ = x_prompt.shape
    nb = x_sample.shape[0]
    anw = attn_norm_w[0].reshape(1, D_MODEL)
    fnw = ffn_norm_w[0].reshape(1, D_MODEL)
    win_bf = w_in[0].astype(BF16)
    wout_bf = w_out[0].astype(BF16)
    qnw = jnp.tile(q_norm_w[0], 2).reshape(1, LANES)
    knw = jnp.tile(k_norm_w[0], 2).reshape(1, LANES)
    wr_t = w_router[0].T
    br = b_router[0].reshape(N_EXPERTS, 1)
    prompt_tab, sample_tab = _bias_tables(rel_bias)

    k_past = cache_k_win[0].reshape(nb, WINDOW, LANES)
    v_past = cache_v_win[0].reshape(nb, WINDOW, LANES)
    x1s, h2s, routes, unew, new_k_sample, new_v_sample = _mixer_sample(
        x_sample.reshape(nb, D_MODEL), state_conv[0, :, 0, :], state_conv[0, :, 1, :], k_past, v_past,
        sinks[0].reshape(N_HEADS, 1), anw, win_bf, conv_w[0], qnw, knw, sample_tab, wout_bf, fnw, wr_t, br)

    sink_rows = jnp.repeat(sinks[0].reshape(N_KV, GROUP), ATTN_BLOCK, axis=1)
    x1p, xs_big, lpp, lps, stat, kp, vp, cp = _mixer_prompt(
        x_prompt, sink_rows, anw, win_bf, conv_w[0], qnw, knw, prompt_tab, wout_bf, fnw, wr_t, br, h2s, routes)

    n_tiles = stat.shape[0]
    lrows = _local_rows(MIXER_TILE)
    xs_rows = n_tiles * lrows
    seg = stat[:, :, 0].astype(I32)
    starts = stat[:, :, 1].astype(I32)
    tile_base = jnp.arange(n_tiles, dtype=I32) * lrows
    tile_rows = jnp.full((n_tiles,), lrows, I32)
    n_blocks = -(-(xs_rows + (N_EXPERTS + 1) * (EXPERT_BLOCK - 1)) // EXPERT_BLOCK)
    block_expert, next_expert, n_active, chunk_src, chunk_dst = _chunk_tables(seg, starts, tile_base, tile_rows,
                                                                              n_blocks)

    ys_big = _experts(block_expert, next_expert, n_active, chunk_src, chunk_dst, xs_big, xs_rows, w_up[0], b_up[0],
                      w_down[0], b_down[0])

    y_prompt = _combine(lpp, ys_big, x1p, MIXER_TILE, lrows, 0).reshape(batch, seq, D_MODEL)
    y_sample = _combine(lps, ys_big, x1s, nb, lrows, n_tiles - 1).reshape(nb, 1, D_MODEL)

    new_conv_sample = jnp.stack([state_conv[0, :, 1, :], unew], axis=1)
    return (
        y_prompt,
        y_sample,
        cp[None],
        kp.reshape(1, batch, ATTN_BLOCK, N_KV, HEAD_DIM),
        vp.reshape(1, batch, ATTN_BLOCK, N_KV, HEAD_DIM),
        new_conv_sample[None],
        new_k_sample.reshape(1, nb, WINDOW, N_KV, HEAD_DIM),
        new_v_sample.reshape(1, nb, WINDOW, N_KV, HEAD_DIM),
    )
```

```python
import functools
import math

import numpy as np
import jax
import jax.numpy as jnp
from jax import lax
from jax.experimental import pallas as pl
from jax.experimental.pallas import tpu as pltpu

F32 = jnp.float32
BF16 = jnp.bfloat16
I32 = jnp.int32

D_MODEL = 1024
HEAD_DIM = 64
N_HEADS = 16
N_KV = 2
GROUP = N_HEADS // N_KV
WINDOW = 128
ATTN_BLOCK = 128
N_BUCKETS = 32
MAX_DISTANCE = 128
NEG_INF = -1e30
N_EXPERTS = 32
TOP_K = 4
SWIGLU_ALPHA = 1.702
SWIGLU_LIMIT = 7.0
EPS = 1e-5
ATTN_SCALE = HEAD_DIM ** -0.5

OFF_XIN, OFF_BG, OFF_CG, OFF_Q = 0, 1024, 2048, 3072
OFF_K, OFF_V, OFF_GC, OFF_GA = 4096, 4224, 4352, 5376
IN_DIM = 6400

LANES = 128
MIXER_TILE = 512
EXPERT_BLOCK = 512
FFN_ROWS = 128
CHUNK = 16
CHUNKS_PER_BLOCK = EXPERT_BLOCK // CHUNK
XS_WIDTH = D_MODEL + LANES
VMEM_LIMIT = 60 * 1024 * 1024


def _local_rows(tm):
    need = tm * TOP_K + N_EXPERTS * (CHUNK - 1)
    return -(-need // 512) * 512


def _t5_bucket_np(rel):
    n = np.maximum(rel, 0)
    max_exact = N_BUCKETS // 2
    nf = np.maximum(n, 1).astype(np.float64)
    large = max_exact + (np.log(nf / max_exact) / math.log(MAX_DISTANCE / max_exact)
                         * (N_BUCKETS - max_exact)).astype(np.int32)
    large = np.minimum(large, N_BUCKETS - 1)
    return np.where(n < max_exact, n, large).astype(np.int32)


def _rms(x, w):
    return x * lax.rsqrt(jnp.mean(x * x, axis=-1, keepdims=True) + EPS) * w


def _lane_iota(rows=1):
    return lax.broadcasted_iota(I32, (rows, LANES), 1)


def _lo_half():
    return _lane_iota() < HEAD_DIM


def _pair_norm(t, w128):
    lo = _lo_half()
    sq = t * t
    s_lo = jnp.sum(jnp.where(lo, sq, 0.0), axis=-1, keepdims=True)
    s_hi = jnp.sum(jnp.where(lo, 0.0, sq), axis=-1, keepdims=True)
    r = jnp.where(lo, lax.rsqrt(s_lo * (1.0 / HEAD_DIM) + EPS), lax.rsqrt(s_hi * (1.0 / HEAD_DIM) + EPS))
    return t * r * w128


def _top4_gates(logits_t):
    tokens = logits_t.shape[1]
    expert = lax.broadcasted_iota(I32, (N_EXPERTS, tokens), 0).astype(F32)
    vals, idxs = [], []
    l = logits_t
    for _ in range(TOP_K):
        m = jnp.max(l, axis=0, keepdims=True)
        idx = jnp.min(jnp.where(l == m, expert, float(N_EXPERTS)), axis=0, keepdims=True)
        vals.append(m)
        idxs.append(idx)
        l = jnp.where(expert == idx, -jnp.inf, l)
    es = [jnp.exp(v - vals[0]) for v in vals]
    den = es[0] + es[1] + es[2] + es[3]
    return idxs, [e / den for e in es]


def _rows_to_sublanes(rows, n_rows, tokens, fill=0.0):
    sub = lax.broadcasted_iota(I32, (n_rows, tokens), 0)
    out = jnp.full((n_rows, tokens), fill, F32)
    for k, r in enumerate(rows):
        out = jnp.where(sub == k, r, out)
    return out


def _onehot_row_builder(lp_t, tm):
    lp_group = jnp.floor(lp_t * (1.0 / LANES))
    lp_off = lp_t - lp_group * LANES
    row_off = lax.broadcasted_iota(I32, (LANES, tm), 0).astype(F32).astype(BF16)
    one_bf = jnp.ones((LANES, tm), BF16)
    zero_bf = jnp.zeros((LANES, tm), BF16)

    def build(r0, rows):
        parts = []
        for s in range(rows // LANES):
            group = float(r0 // LANES + s)
            hit = None
            for k in range(TOP_K):
                off_k = jnp.where(lp_group[k:k + 1, :] == group, lp_off[k:k + 1, :], -1.0).astype(BF16)
                hit_k = row_off == off_k
                hit = hit_k if hit is None else jnp.logical_or(hit, hit_k)
            parts.append(jnp.where(hit, one_bf, zero_bf))
        return jnp.concatenate(parts, axis=0)

    return build


def _dispatch(h2, idxs, gates, xs_ref, lp_ref, stat_ref):
    tm = h2.shape[0]
    lrows = xs_ref.shape[0]
    expert = lax.broadcasted_iota(I32, (N_EXPERTS, tm), 0).astype(F32)
    member = jnp.zeros((N_EXPERTS, tm), F32)
    for k in range(TOP_K):
        member = member + jnp.where(expert == idxs[k], 1.0, 0.0)
    cnt = jnp.broadcast_to(jnp.sum(member, axis=1, keepdims=True), (N_EXPERTS, LANES))
    seg = jnp.floor((cnt + (CHUNK - 1)) * (1.0 / CHUNK)) * CHUNK
    sub32 = lax.broadcasted_iota(I32, (N_EXPERTS, LANES), 0)
    incl = seg
    for s in (1, 2, 4, 8, 16):
        incl = incl + jnp.where(sub32 >= s, pltpu.roll(incl, s, axis=0), 0.0)
    starts = incl - seg
    lane32 = _lane_iota(N_EXPERTS)
    stat_ref[0] = jnp.where(lane32 == 0, seg, jnp.where(lane32 == 1, starts, 0.0))

    earlier = (lax.broadcasted_iota(I32, (tm, tm), 0) < lax.broadcasted_iota(I32, (tm, tm), 1)).astype(BF16)
    rank = jnp.dot(member.astype(BF16), earlier, preferred_element_type=F32)
    pos = starts[:, 0:1] + rank
    lps = [jnp.sum(jnp.where(expert == idxs[k], pos, 0.0), axis=0, keepdims=True) for k in range(TOP_K)]
    lp_t = _rows_to_sublanes(lps, 8, tm, fill=-1.0)
    lp_ref[0] = lp_t

    g_hi = [g.astype(BF16).astype(F32) for g in gates]
    g_lo = [g - h for g, h in zip(gates, g_hi)]
    meta_in = _rows_to_sublanes(list(idxs) + g_hi + g_lo, LANES, tm).T
    rhs = jnp.concatenate([h2.astype(BF16), meta_in.astype(BF16)], axis=1)

    starts32 = jnp.concatenate([starts, jnp.full((LANES - N_EXPERTS, LANES), 1e9, F32)], axis=0).T[0:1, :]
    onehot_rows = _onehot_row_builder(lp_t, tm)

    bounds = list(range(0, lrows - 512, 512)) + [lrows - 512, lrows - 256]
    for r0, r1 in zip(bounds, bounds[1:] + [lrows]):
        sub = r1 - r0
        rid = (lax.broadcasted_iota(I32, (sub, 1), 0) + r0).astype(F32)
        full = jnp.dot(onehot_rows(r0, sub), rhs, preferred_element_type=F32)
        got = full[:, D_MODEL:]
        e_row = jnp.sum(jnp.where(rid >= starts32, 1.0, 0.0), axis=-1, keepdims=True) - 1.0
        g_sum = pltpu.roll(got, LANES - TOP_K, axis=1) + pltpu.roll(got, LANES - 2 * TOP_K, axis=1)
        lane_s = _lane_iota(sub)
        pick = jnp.logical_and(lane_s < TOP_K, got == e_row)
        gate_row = jnp.sum(jnp.where(pick, g_sum, 0.0), axis=-1, keepdims=True)
        gr_hi = gate_row.astype(BF16).astype(F32)
        meta = jnp.where(lane_s == 0, gr_hi, jnp.where(lane_s == 1, gate_row - gr_hi, 0.0))
        xs_ref[r0:r0 + sub, 0:D_MODEL] = full[:, :D_MODEL].astype(BF16)
        xs_ref[r0:r0 + sub, D_MODEL:XS_WIDTH] = meta.astype(BF16)


def _epilogue(x, merged, wout_ref, fnw_ref, wr_ref, br_ref, x1_ref):
    x1 = x + jnp.dot(merged.astype(BF16), wout_ref[...], preferred_element_type=F32)
    x1_ref[...] = x1
    h2 = _rms(x1, fnw_ref[...]).astype(BF16)
    logits_t = lax.dot_general(wr_ref[...].astype(BF16), h2, (((1,), (1,)), ((), ())),
                               preferred_element_type=F32) + br_ref[...]
    idxs, gates = _top4_gates(logits_t)
    return h2, idxs, gates


def _mixer_prompt_kernel(nj, relb_ref, bucket_ref, sink_ref, x_ref, anw_ref, win_ref, convw_ref, qnw_ref, knw_ref,
                         wout_ref, fnw_ref, wr_ref, br_ref, h2s_ref, routes_ref,
                         x1_ref, xs_ref, lp_ref, lps_ref, stat_ref, kout_ref, vout_ref, cout_ref,
                         ubuf, q_s, kd_s, vt_s, ya_s, st_s, pt_s, bias_s):
    i = pl.program_id(0)
    n_tiles = pl.num_programs(0) - 1

    @pl.when(i == 0)
    def _():
        bucket = bucket_ref[...]
        for hd in range(N_HEADS):
            acc = jnp.full(bucket.shape, NEG_INF, F32)
            for b in range(N_BUCKETS):
                acc = jnp.where(bucket == b, relb_ref[b, hd], acc)
            bias_s[hd // GROUP, :, (hd % GROUP) * LANES:(hd % GROUP + 1) * LANES] = acc

    @pl.when(i < n_tiles)
    def _():
        _prompt_tile(lax.rem(i, nj), nj, sink_ref, x_ref, anw_ref, win_ref, convw_ref, qnw_ref, knw_ref, bias_s,
                     wout_ref, fnw_ref, wr_ref, br_ref, x1_ref, xs_ref, lp_ref, stat_ref, kout_ref, vout_ref,
                     cout_ref, ubuf, q_s, kd_s, vt_s, ya_s, st_s, pt_s)

    @pl.when(i == n_tiles)
    def _():
        route = routes_ref[...]
        _dispatch(h2s_ref[...], [route[k:k + 1, :] for k in range(TOP_K)],
                  [route[TOP_K + k:TOP_K + k + 1, :] for k in range(TOP_K)], xs_ref, lps_ref, stat_ref)


def _prompt_tile(j, nj, sink_ref, x_ref, anw_ref, win_ref, convw_ref, qnw_ref, knw_ref, bias_ref, wout_ref,
                 fnw_ref, wr_ref, br_ref, x1_ref, xs_ref, lp_ref, stat_ref, kout_ref, vout_ref, cout_ref,
                 ubuf, q_s, kd_s, vt_s, ya_s, st_s, pt_s):
    tm = x_ref.shape[0]
    nblk = tm // ATTN_BLOCK
    first_tile = j == 0
    lo = _lo_half()

    x = x_ref[...]
    h = _rms(x, anw_ref[...]).astype(BF16)

    def proj(off, n):
        return jnp.dot(h, win_ref[:, off:off + n], preferred_element_type=F32)

    @pl.when(first_tile)
    def _():
        ubuf[0:8, :] = jnp.zeros((8, D_MODEL), F32)
        kd_s[0:ATTN_BLOCK, :] = jnp.zeros((ATTN_BLOCK, 2 * LANES), BF16)
        vt_s[0] = jnp.zeros((N_KV, LANES, ATTN_BLOCK), BF16)

    q = proj(OFF_Q, D_MODEL)
    kv = proj(OFF_K, 2 * LANES)
    k = _pair_norm(kv[:, :LANES], knw_ref[...])
    v = kv[:, LANES:]
    qnw = qnw_ref[...]
    for p in range(N_HEADS // 2):
        sl = slice(p * LANES, (p + 1) * LANES)
        q_s[:, sl] = (_pair_norm(q[:, sl], qnw) * ATTN_SCALE).astype(BF16)
    k_sw = pltpu.roll(k, HEAD_DIM, axis=1)
    v_sw = pltpu.roll(v, HEAD_DIM, axis=1)
    kd_s[ATTN_BLOCK:tm + ATTN_BLOCK, 0:LANES] = jnp.where(lo, k, k_sw).astype(BF16)
    kd_s[ATTN_BLOCK:tm + ATTN_BLOCK, LANES:2 * LANES] = jnp.where(lo, k_sw, k).astype(BF16)
    v_dup = (jnp.where(lo, v, v_sw), jnp.where(lo, v_sw, v))
    for b in range(nblk):
        for g in range(N_KV):
            vt_s[b + 1, g] = v_dup[g][b * ATTN_BLOCK:(b + 1) * ATTN_BLOCK, :].T.astype(BF16)

    u = proj(OFF_CG, D_MODEL) * proj(OFF_XIN, D_MODEL)
    ubuf[8:tm + 8, :] = u
    cw = convw_ref[...]
    conv = ubuf[6:tm + 6, :] * cw[0:1, :] + ubuf[7:tm + 7, :] * cw[1:2, :] + u * cw[2:3, :]
    merged = jax.nn.sigmoid(proj(OFF_GC, D_MODEL)) * (proj(OFF_BG, D_MODEL) * conv)
    tail = ubuf[tm + 6:tm + 8, :]
    cout_ref[0] = tail
    ubuf[6:8, :] = tail

    @pl.when(j == nj - 1)
    def _():
        kout_ref[0] = k[tm - ATTN_BLOCK:, :]
        vout_ref[0] = v[tm - ATTN_BLOCK:, :]

    prev_rows = lax.broadcasted_iota(I32, (2 * ATTN_BLOCK, 1), 0) < ATTN_BLOCK
    feat_lo = lax.broadcasted_iota(I32, (LANES, 1), 0) < HEAD_DIM

    units = [(blk, g) for blk in range(nblk) for g in range(N_KV)]

    def scores(n):
        blk, g = units[n]
        r0 = blk * ATTN_BLOCK
        qb = q_s[r0:r0 + ATTN_BLOCK, :]
        parts = []
        for t in range(GROUP):
            hd = g * GROUP + t
            slab = qb[:, (hd // 2) * LANES:(hd // 2 + 1) * LANES]
            keep = lo if hd % 2 == 0 else jnp.logical_not(lo)
            parts.append(jnp.where(keep, slab, jnp.zeros_like(slab)))
        lhs = jnp.concatenate(parts, axis=0)
        st = lax.dot_general(kd_s[r0:r0 + 2 * ATTN_BLOCK, g * LANES:(g + 1) * LANES], lhs,
                             (((1,), (1,)), ((), ())), preferred_element_type=F32)
        st = st + bias_ref[g]
        if blk == 0:
            st = jnp.where(jnp.logical_and(prev_rows, first_tile), NEG_INF, st)
        st_s[n % 2] = st

    def softmax_values(n):
        blk, g = units[n]
        r0 = blk * ATTN_BLOCK
        sink = sink_ref[g:g + 1, :]
        rdens = []
        for t in range(GROUP):
            cols = slice(t * LANES, (t + 1) * LANES)
            s = st_s[n % 2, :, cols]
            m = jnp.maximum(jnp.max(s, axis=0, keepdims=True), sink[:, cols])
            pr = jnp.exp(s - m)
            den = jnp.sum(pr, axis=0, keepdims=True) + jnp.exp(sink[:, cols] - m)
            pt_s[:, cols] = pr.astype(BF16)
            rdens.append(1.0 / den)
        vt = jnp.concatenate([vt_s[blk, g], vt_s[blk + 1, g]], axis=1)
        ot = jnp.dot(vt, pt_s[...], preferred_element_type=F32)
        for i in range(GROUP // 2):
            pair = g * (GROUP // 2) + i
            even = ot[:, (2 * i) * LANES:(2 * i + 1) * LANES] * rdens[2 * i]
            odd = ot[:, (2 * i + 1) * LANES:(2 * i + 2) * LANES] * rdens[2 * i + 1]
            ya_s[r0:r0 + ATTN_BLOCK, pair * LANES:(pair + 1) * LANES] = jnp.where(feat_lo, even, odd).T

    ga_parts = []
    ga_cols = D_MODEL // (len(units) // 2)
    scores(0)
    for n in range(len(units)):
        if n + 1 < len(units):
            scores(n + 1)
        if n % 2 == 0:
            ga_parts.append(jax.nn.sigmoid(proj(OFF_GA + (n // 2) * ga_cols, ga_cols)))
        softmax_values(n)

    kd_s[0:ATTN_BLOCK, :] = kd_s[tm:tm + ATTN_BLOCK, :]
    vt_s[0] = vt_s[nblk]

    merged = merged + jnp.concatenate(ga_parts, axis=1) * ya_s[...]
    h2, idxs, gates = _epilogue(x_ref[...], merged, wout_ref, fnw_ref, wr_ref, br_ref, x1_ref)
    _dispatch(h2, idxs, gates, xs_ref, lp_ref, stat_ref)


def _const_spec(shape):
    nd = len(shape)
    return pl.BlockSpec(shape, lambda *_: (0,) * nd, pipeline_mode=pl.Buffered(1))


def _mixer_prompt(x, rel_bias, bucket_t, sink_rows, anw, win_bf, convw, qnw, knw, wout_bf, fnw, wr_t, br_col, h2s,
                  routes):
    batch, seq, _ = x.shape
    tm = MIXER_TILE
    lrows = _local_rows(tm)
    nj = seq // tm
    tokens = batch * seq
    n_tiles = batch * nj
    nb = h2s.shape[0]
    x2 = x.reshape(tokens, D_MODEL)
    tile = lambda i: jnp.minimum(i, n_tiles - 1)
    tok_spec = lambda width: pl.BlockSpec((tm, width), lambda i: (tile(i), 0))
    per_batch = lambda rows, width: pl.BlockSpec((1, rows, width), lambda i: (tile(i) // nj, 0, 0))
    in_specs = [
        pl.BlockSpec(memory_space=pltpu.SMEM),
        _const_spec((2 * ATTN_BLOCK, ATTN_BLOCK)),
        _const_spec((N_KV, GROUP * ATTN_BLOCK)),
        tok_spec(D_MODEL),
        _const_spec((1, D_MODEL)),
        _const_spec((D_MODEL, IN_DIM)),
        _const_spec((3, D_MODEL)),
        _const_spec((1, LANES)),
        _const_spec((1, LANES)),
        _const_spec((D_MODEL, D_MODEL)),
        _const_spec((1, D_MODEL)),
        _const_spec((N_EXPERTS, D_MODEL)),
        _const_spec((N_EXPERTS, 1)),
        _const_spec((nb, D_MODEL)),
        _const_spec((2 * TOP_K, nb)),
    ]
    out_shape = (
        jax.ShapeDtypeStruct((tokens, D_MODEL), F32),
        jax.ShapeDtypeStruct(((n_tiles + 1) * lrows, XS_WIDTH), BF16),
        jax.ShapeDtypeStruct((n_tiles, 8, tm), F32),
        jax.ShapeDtypeStruct((1, 8, nb), F32),
        jax.ShapeDtypeStruct((n_tiles + 1, N_EXPERTS, LANES), F32),
        jax.ShapeDtypeStruct((batch, ATTN_BLOCK, LANES), F32),
        jax.ShapeDtypeStruct((batch, ATTN_BLOCK, LANES), F32),
        jax.ShapeDtypeStruct((batch, 2, D_MODEL), F32),
    )
    out_specs = (
        tok_spec(D_MODEL),
        pl.BlockSpec((lrows, XS_WIDTH), lambda i: (i, 0)),
        pl.BlockSpec((1, 8, tm), lambda i: (tile(i), 0, 0)),
        pl.BlockSpec((1, 8, nb), lambda i: (0, 0, 0)),
        pl.BlockSpec((1, N_EXPERTS, LANES), lambda i: (i, 0, 0)),
        per_batch(ATTN_BLOCK, LANES), per_batch(ATTN_BLOCK, LANES), per_batch(2, D_MODEL),
    )
    scratch = [
        pltpu.VMEM((tm + 8, D_MODEL), F32),
        pltpu.VMEM((tm, D_MODEL), BF16),
        pltpu.VMEM((tm + ATTN_BLOCK, 2 * LANES), BF16),
        pltpu.VMEM((tm // ATTN_BLOCK + 1, N_KV, LANES, ATTN_BLOCK), BF16),
        pltpu.VMEM((tm, D_MODEL), F32),
        pltpu.VMEM((2, 2 * ATTN_BLOCK, GROUP * ATTN_BLOCK), F32),
        pltpu.VMEM((2 * ATTN_BLOCK, GROUP * ATTN_BLOCK), BF16),
        pltpu.VMEM((N_KV, 2 * ATTN_BLOCK, GROUP * ATTN_BLOCK), F32),
    ]
    return pl.pallas_call(
        functools.partial(_mixer_prompt_kernel, nj),
        grid=(n_tiles + 1,),
        in_specs=in_specs,
        out_specs=out_specs,
        out_shape=out_shape,
        scratch_shapes=scratch,
        compiler_params=pltpu.CompilerParams(dimension_semantics=("arbitrary",), vmem_limit_bytes=VMEM_LIMIT),
        name="mixer_prompt",
    )(rel_bias, bucket_t, sink_rows, x2, anw, win_bf, convw, qnw, knw, wout_bf, fnw, wr_t, br_col, h2s, routes)


SAMPLE_CHUNK = 32
SAMPLE_GROUP = 8


def _mixer_sample_kernel(x_ref, p0_ref, p1_ref, kp_ref, vp_ref, sink_ref, anw_ref, win_ref, convw_ref,
                         qnw_ref, knw_ref, bias_ref, wout_ref, fnw_ref, wr_ref, br_ref,
                         x1_ref, h2_ref, route_ref, unew_ref, kc_ref, vc_ref,
                         qh_s, o_s, kn_s, vn_s, conv_s, ga_s):
    c = pl.program_id(0)
    nb = x_ref.shape[0]
    tc = kp_ref.shape[0]
    lo = _lo_half()

    @pl.when(c == 0)
    def _():
        h = _rms(x_ref[...], anw_ref[...]).astype(BF16)

        def proj(off, n):
            return jnp.dot(h, win_ref[:, off:off + n], preferred_element_type=F32)

        u = proj(OFF_CG, D_MODEL) * proj(OFF_XIN, D_MODEL)
        unew_ref[...] = u
        cw = convw_ref[...]
        conv = p0_ref[...] * cw[0:1, :] + p1_ref[...] * cw[1:2, :] + u * cw[2:3, :]
        conv_s[...] = jax.nn.sigmoid(proj(OFF_GC, D_MODEL)) * (proj(OFF_BG, D_MODEL) * conv)
        ga_s[...] = jax.nn.sigmoid(proj(OFF_GA, D_MODEL))

        q = proj(OFF_Q, D_MODEL)
        kv = proj(OFF_K, 2 * LANES)
        kn_s[...] = _pair_norm(kv[:, :LANES], knw_ref[...])
        vn_s[...] = kv[:, LANES:]

        qnw = qnw_ref[...]
        for hd in range(N_HEADS):
            pair, half, grp = hd // 2, hd % 2, hd // GROUP
            slab = _pair_norm(q[:, pair * LANES:(pair + 1) * LANES], qnw) * ATTN_SCALE
            slab = jnp.where(lo if half == 0 else jnp.logical_not(lo), slab, 0.0)
            if half != grp:
                slab = pltpu.roll(slab, HEAD_DIM, axis=1)
            qh_s[hd * nb:(hd + 1) * nb, :] = slab

    sink = sink_ref[...]
    bias = bias_ref[...]
    rows = lax.broadcasted_iota(I32, (WINDOW, 1), 0)
    row0 = rows == 0
    row_last = rows == WINDOW - 1

    grp = SAMPLE_GROUP
    for ci in range(tc // grp):
        t0 = ci * grp
        qs, ks, vs = [], [], []
        for t in range(t0, t0 + grp):
            b = c * tc + t
            qs.append(qh_s[pl.ds(b, N_HEADS, stride=nb), :])
            k_new = kn_s[pl.ds(b, 1), :]
            v_new = vn_s[pl.ds(b, 1), :]
            k_old = kp_ref[t]
            v_old = vp_ref[t]
            kc_ref[t] = jnp.where(row_last, k_new, pltpu.roll(k_old, WINDOW - 1, axis=0))
            vc_ref[t] = jnp.where(row_last, v_new, pltpu.roll(v_old, WINDOW - 1, axis=0))
            ks.append(jnp.where(row0, k_new, k_old).astype(BF16))
            vs.append(jnp.where(row0, v_new, v_old).astype(BF16))
        q_all = jnp.concatenate(qs, axis=0).astype(BF16)
        s_all = lax.dot_general(q_all, jnp.concatenate(ks, axis=0), (((1,), (1,)), ((), ())),
                                preferred_element_type=F32)
        p_rows, rdens = [], []
        for g in range(grp):
            s = s_all[g * N_HEADS:(g + 1) * N_HEADS, g * WINDOW:(g + 1) * WINDOW] + bias
            m = jnp.maximum(jnp.max(s, axis=-1, keepdims=True), sink)
            pr = jnp.exp(s - m)
            rdens.append(1.0 / (jnp.sum(pr, axis=-1, keepdims=True) + jnp.exp(sink - m)))
            zero = jnp.zeros((N_HEADS, WINDOW), BF16)
            p_rows.append(jnp.concatenate([pr.astype(BF16) if j == g else zero for j in range(grp)], axis=1))
        o_all = jnp.dot(jnp.concatenate(p_rows, axis=0), jnp.concatenate(vs, axis=0),
                        preferred_element_type=F32)
        for g in range(grp):
            b = c * tc + t0 + g
            o_s[pl.ds(b, N_HEADS, stride=nb), :] = o_all[g * N_HEADS:(g + 1) * N_HEADS, :] * rdens[g]

    @pl.when(c == pl.num_programs(0) - 1)
    def _():
        cols = []
        for pair in range(N_HEADS // 2):
            halves = []
            for half in range(2):
                hd = 2 * pair + half
                slab = o_s[hd * nb:(hd + 1) * nb, :]
                if half != hd // GROUP:
                    slab = pltpu.roll(slab, HEAD_DIM, axis=1)
                halves.append(slab)
            cols.append(jnp.where(lo, halves[0], halves[1]))
        merged = conv_s[...] + ga_s[...] * jnp.concatenate(cols, axis=1)
        h2, idxs, gates = _epilogue(x_ref[...], merged, wout_ref, fnw_ref, wr_ref, br_ref, x1_ref)
        h2_ref[...] = h2
        route_ref[...] = _rows_to_sublanes(list(idxs) + list(gates), 2 * TOP_K, nb)


def _mixer_sample(x, p0, p1, k_past, v_past, sink_col, anw, win_bf, convw, qnw, knw, bias_s, wout_bf, fnw, wr, br):
    nb = x.shape[0]
    tc = SAMPLE_CHUNK
    consts = (x, p0, p1)
    params = (sink_col, anw, win_bf, convw, qnw, knw, bias_s, wout_bf, fnw, wr, br)
    cache_spec = pl.BlockSpec((tc, WINDOW, LANES), lambda c: (c, 0, 0))
    full = lambda shape: pl.BlockSpec(shape, lambda c: (0,) * len(shape))
    out_shape = (
        jax.ShapeDtypeStruct((nb, D_MODEL), F32),
        jax.ShapeDtypeStruct((nb, D_MODEL), BF16),
        jax.ShapeDtypeStruct((2 * TOP_K, nb), F32),
        jax.ShapeDtypeStruct((nb, D_MODEL), F32),
        jax.ShapeDtypeStruct((nb, WINDOW, LANES), F32),
        jax.ShapeDtypeStruct((nb, WINDOW, LANES), F32),
    )
    out_specs = tuple(full(s.shape) for s in out_shape[:4]) + (cache_spec, cache_spec)
    scratch = [
        pltpu.VMEM((N_HEADS * nb, LANES), F32),
        pltpu.VMEM((N_HEADS * nb, LANES), F32),
        pltpu.VMEM((nb, LANES), F32),
        pltpu.VMEM((nb, LANES), F32),
        pltpu.VMEM((nb, D_MODEL), F32),
        pltpu.VMEM((nb, D_MODEL), F32),
    ]
    return pl.pallas_call(
        _mixer_sample_kernel,
        grid=(nb // tc,),
        in_specs=[_const_spec(a.shape) for a in consts] + [cache_spec, cache_spec]
        + [_const_spec(a.shape) for a in params],
        out_specs=out_specs,
        out_shape=out_shape,
        scratch_shapes=scratch,
        compiler_params=pltpu.CompilerParams(dimension_semantics=("arbitrary",), vmem_limit_bytes=VMEM_LIMIT),
        name="mixer_sample",
    )(*consts, k_past, v_past, *params)


def _expert_kernel(be_ref, nxt_ref, nact_ref, src_ref, dst_ref, xs_hbm, wup_hbm, bup_ref, wdn_hbm, bdn_ref, ys_hbm,
                   wup_s, wdn_s, perm_s, wup_stage, wdn_stage, xbuf, ybuf, in_sem, out_sem, w_sem):
    i = pl.program_id(0)
    n_real = nact_ref[0]
    n_active = nact_ref[1]
    e = be_ref[i]
    e_prev = be_ref[jnp.maximum(i - 1, 0)]
    slot = lax.rem(i, 2)

    def weight_copies(expert):
        return (pltpu.make_async_copy(wup_hbm.at[expert], wup_stage, w_sem.at[0]),
                pltpu.make_async_copy(wdn_hbm.at[expert], wdn_stage, w_sem.at[1]))

    def gather_copy(blk, slt, c):
        row = pl.multiple_of(src_ref[blk * CHUNKS_PER_BLOCK + c] * CHUNK, CHUNK)
        return pltpu.make_async_copy(xs_hbm.at[pl.ds(row, CHUNK), :],
                                     xbuf.at[slt, pl.ds(c * CHUNK, CHUNK), :], in_sem.at[slt])

    def scatter_chunks(blk, slt, wait):
        base = blk * CHUNKS_PER_BLOCK

        def one(c, dst):
            row = pl.multiple_of(dst * CHUNK, CHUNK)
            cp = pltpu.make_async_copy(ybuf.at[slt, pl.ds(c * CHUNK, CHUNK), :],
                                       ys_hbm.at[pl.ds(row, CHUNK), :], out_sem.at[slt])
            if wait:
                cp.wait()
            else:
                cp.start()

        full = dst_ref[base + CHUNKS_PER_BLOCK - 1] >= 0

        @pl.when(full)
        def _():
            for c in range(CHUNKS_PER_BLOCK):
                one(c, dst_ref[base + c])

        @pl.when(jnp.logical_not(full))
        def _():
            for c in range(CHUNKS_PER_BLOCK - 1):
                dst = dst_ref[base + c]

                @pl.when(dst >= 0)
                def _():
                    one(c, dst)

    @pl.when(jnp.logical_and(i == 0, n_real > 0))
    def _():
        for c in range(CHUNKS_PER_BLOCK):
            gather_copy(0, 0, c).start()

    @pl.when(jnp.logical_and(i == 0, n_real > 0))
    def _():
        for cp in weight_copies(e):
            cp.start()

    @pl.when(jnp.logical_and(i < n_real, jnp.logical_or(i == 0, e != e_prev)))
    def _():
        for cp in weight_copies(e):
            cp.wait()
        wup_s[...] = wup_stage[...].astype(BF16)
        half = LANES // 2
        for cs in range(D_MODEL // LANES):
            cols = slice(cs * LANES, (cs + 1) * LANES)
            for c in range(D_MODEL // LANES):
                for par in range(2):
                    s0 = c * LANES + par * half
                    perm_s[cs, pl.ds(c * LANES + par, half, stride=2), :] = wdn_stage[s0:s0 + half, cols]
            wdn_s[:, cols] = perm_s[cs].astype(BF16)
        nxt = nxt_ref[i]

        @pl.when(nxt < N_EXPERTS)
        def _():
            for cp in weight_copies(nxt):
                cp.start()

    @pl.when(i + 1 < n_real)
    def _():
        for c in range(CHUNKS_PER_BLOCK):
            gather_copy(i + 1, 1 - slot, c).start()

    @pl.when(jnp.logical_and(i >= 2, i < n_active))
    def _():
        scatter_chunks(i - 2, slot, wait=True)

    @pl.when(jnp.logical_and(i >= n_real, i < n_active))
    def _():
        ybuf[slot] = jnp.zeros((EXPERT_BLOCK, D_MODEL), BF16)
        scatter_chunks(i, slot, wait=False)

    def ffn(rows):
        for c in range(CHUNKS_PER_BLOCK):
            gather_copy(i, slot, c).wait()

        xb = xbuf[slot, 0:rows, :]
        meta = xb[:, D_MODEL:].astype(F32)
        gate = meta[:, 0:1] + meta[:, 1:2]
        u = jnp.dot(xb[:, :D_MODEL], wup_s[...], preferred_element_type=F32) + bup_ref[0]
        even = (_lane_iota() & 1) == 0
        cols = []
        for c in range(D_MODEL // LANES):
            c0 = u[:, (2 * c) * LANES:(2 * c + 1) * LANES]
            c1 = u[:, (2 * c + 1) * LANES:(2 * c + 2) * LANES]
            glu = jnp.where(even, c0, pltpu.roll(c1, 1, axis=1))
            lin = jnp.where(even, pltpu.roll(c0, LANES - 1, axis=1), c1)
            glu = jnp.minimum(glu, SWIGLU_LIMIT)
            lin = jnp.clip(lin, -SWIGLU_LIMIT, SWIGLU_LIMIT)
            cols.append(glu * jax.nn.sigmoid(SWIGLU_ALPHA * glu) * (lin + 1.0))
        a = jnp.concatenate(cols, axis=1).astype(BF16)
        y = jnp.dot(a, wdn_s[...], preferred_element_type=F32) + bdn_ref[0]
        ybuf[slot, 0:rows, :] = (y * gate).astype(BF16)
        scatter_chunks(i, slot, wait=False)

    part = FFN_ROWS // CHUNK
    n_parts = EXPERT_BLOCK // FFN_ROWS
    base = i * CHUNKS_PER_BLOCK
    parts_used = 1 + sum((dst_ref[base + q * part] >= 0).astype(I32) for q in range(1, n_parts))
    for q in range(1, n_parts + 1):
        @pl.when(jnp.logical_and(i < n_real, parts_used == q))
        def _(q=q):
            ffn(q * FFN_ROWS)

    @pl.when(i == n_active - 1)
    def _():
        @pl.when(i >= 1)
        def _():
            scatter_chunks(i - 1, 1 - slot, wait=True)

        scatter_chunks(i, slot, wait=True)


def _experts(block_expert, next_expert, n_active, chunk_src, chunk_dst, xs_big, ys_rows, w_up, b_up, w_down, b_down):
    n_blocks = block_expert.shape[0]
    bspec = lambda shape: pl.BlockSpec(
        shape, lambda i, be, nx, na, cs, cd: (jnp.minimum(be[i], N_EXPERTS - 1), 0, 0))
    grid_spec = pltpu.PrefetchScalarGridSpec(
        num_scalar_prefetch=5,
        grid=(n_blocks,),
        in_specs=[
            pl.BlockSpec(memory_space=pl.ANY),
            pl.BlockSpec(memory_space=pl.ANY),
            bspec((1, 1, 2 * D_MODEL)),
            pl.BlockSpec(memory_space=pl.ANY),
            bspec((1, 1, D_MODEL)),
        ],
        out_specs=pl.BlockSpec(memory_space=pl.ANY),
        scratch_shapes=[
            pltpu.VMEM((D_MODEL, 2 * D_MODEL), BF16),
            pltpu.VMEM((D_MODEL, D_MODEL), BF16),
            pltpu.VMEM((D_MODEL // LANES, D_MODEL, LANES), F32),
            pltpu.VMEM((D_MODEL, 2 * D_MODEL), F32),
            pltpu.VMEM((D_MODEL, D_MODEL), F32),
            pltpu.VMEM((2, EXPERT_BLOCK, XS_WIDTH), BF16),
            pltpu.VMEM((2, EXPERT_BLOCK, D_MODEL), BF16),
            pltpu.SemaphoreType.DMA((2,)),
            pltpu.SemaphoreType.DMA((2,)),
            pltpu.SemaphoreType.DMA((2,)),
        ],
    )
    return pl.pallas_call(
        _expert_kernel,
        grid_spec=grid_spec,
        out_shape=jax.ShapeDtypeStruct((ys_rows, D_MODEL), BF16),
        compiler_params=pltpu.CompilerParams(dimension_semantics=("arbitrary",), vmem_limit_bytes=VMEM_LIMIT),
        name="experts",
    )(block_expert, next_expert, n_active, chunk_src, chunk_dst, xs_big, w_up,
      b_up.reshape(N_EXPERTS, 1, 2 * D_MODEL), w_down, b_down.reshape(N_EXPERTS, 1, D_MODEL))


def _combine_kernel(lp_ref, ys_ref, x1_ref, out_ref):
    tm = x1_ref.shape[0]
    lrows = ys_ref.shape[0]
    onehot_rows = _onehot_row_builder(lp_ref[0], tm)
    y = x1_ref[...]
    step = 512
    for r0 in range(0, lrows, step):
        y = y + lax.dot_general(onehot_rows(r0, step), ys_ref[r0:r0 + step, :], (((0,), (0,)), ((), ())),
                                preferred_element_type=F32)
    out_ref[...] = y


def _combine(lp, ys_big, x1, tm, lrows, first_block):
    tokens = x1.shape[0]
    return pl.pallas_call(
        _combine_kernel,
        grid=(tokens // tm,),
        in_specs=[
            pl.BlockSpec((1, 8, tm), lambda j: (j, 0, 0)),
            pl.BlockSpec((lrows, D_MODEL), lambda j: (first_block + j, 0)),
            pl.BlockSpec((tm, D_MODEL), lambda j: (j, 0)),
        ],
        out_specs=pl.BlockSpec((tm, D_MODEL), lambda j: (j, 0)),
        out_shape=jax.ShapeDtypeStruct((tokens, D_MODEL), F32),
        compiler_params=pltpu.CompilerParams(dimension_semantics=("arbitrary",), vmem_limit_bytes=VMEM_LIMIT),
        name="combine",
    )(lp, ys_big, x1)


def _bias_tables(rel_bias):
    qi = np.arange(ATTN_BLOCK)[:, None]
    kj = np.arange(2 * ATTN_BLOCK)[None, :]
    rel = qi + ATTN_BLOCK - kj
    valid = (rel >= 0) & (rel < WINDOW)
    bucket = np.where(valid, _t5_bucket_np(rel), -1)
    slot_rel = np.where(np.arange(WINDOW) == 0, 0, WINDOW - np.arange(WINDOW))
    slot_bucket = _t5_bucket_np(slot_rel)
    buckets = np.arange(N_BUCKETS)
    slot_onehot = slot_bucket[None, :] == buckets[:, None]
    sample_tab = jnp.sum(jnp.where(slot_onehot[None], rel_bias.T[:, :, None], 0.0), axis=1)
    return jnp.asarray(bucket.T.astype(np.int32)), sample_tab


def _chunk_tables(seg, starts, tile_base, tile_rows, n_blocks):
    n_tiles = seg.shape[0]
    n_seg = (N_EXPERTS + 1) * n_tiles
    used = jnp.sum(seg, axis=1)
    seg_e = jnp.concatenate([seg.T, (tile_rows - used)[None, :]], axis=0)
    src0 = jnp.concatenate([tile_base[None, :] + starts.T, (tile_base + used)[None, :]], axis=0).reshape(-1)
    total = jnp.sum(seg_e, axis=1)
    region = (total + EXPERT_BLOCK - 1) // EXPERT_BLOCK * EXPERT_BLOCK
    pad_end = jnp.cumsum(region)
    pad_start = pad_end - region
    g_start = (pad_start[:, None] + jnp.cumsum(seg_e, axis=1) - seg_e).reshape(-1)
    g_end = g_start + seg_e.reshape(-1)
    rows = jnp.arange(n_blocks * CHUNKS_PER_BLOCK, dtype=I32) * CHUNK
    passed = g_end[None, :] <= rows[:, None]

    def at_segment(table, sentinel):
        ext = jnp.concatenate([table, jnp.array([sentinel], I32)])
        return ext[0] + jnp.sum(jnp.where(passed, (ext[1:] - ext[:-1])[None, :], 0), axis=1)

    seg_start = at_segment(g_start, 1 << 30)
    valid = rows >= seg_start
    src_row = at_segment(src0, 0) + rows - seg_start
    chunk_src = jnp.where(valid, src_row // CHUNK, 0).astype(I32)
    chunk_dst = jnp.where(valid, src_row // CHUNK, -1).astype(I32)
    blk_rows = jnp.arange(n_blocks, dtype=I32) * EXPERT_BLOCK
    block_region = jnp.minimum(jnp.sum((pad_end[None, :] <= blk_rows[:, None]).astype(I32), axis=1),
                               N_EXPERTS).astype(I32)
    counts = jnp.stack([pad_end[N_EXPERTS - 1], pad_end[N_EXPERTS]]).astype(I32) // EXPERT_BLOCK
    ridx = jnp.arange(N_EXPERTS + 1, dtype=I32)
    later = jnp.logical_and(ridx[None, :] > ridx[:, None], (region > 0)[None, :])
    next_region = jnp.min(jnp.where(later, ridx[None, :], N_EXPERTS + 1), axis=1)
    next_of_block = jnp.sum(jnp.where(block_region[:, None] == ridx[None, :], next_region[None, :], 0), axis=1)
    return block_region, next_of_block.astype(I32), counts, chunk_src, chunk_dst


def kernel(x_prompt, x_sample, state_conv, cache_k_win, cache_v_win, rel_bias, attn_norm_w, w_in, conv_w,
           q_norm_w, k_norm_w, sinks, w_out, ffn_norm_w, w_router, b_router, w_up, b_up, w_down, b_down):
    batch, seq, _ = x_prompt.shape
    nb = x_sample.shape[0]
    anw = attn_norm_w[0].reshape(1, D_MODEL)
    fnw = ffn_norm_w[0].reshape(1, D_MODEL)
    win_bf = w_in[0].astype(BF16)
    wout_bf = w_out[0].astype(BF16)
    qnw = jnp.tile(q_norm_w[0], 2).reshape(1, LANES)
    knw = jnp.tile(k_norm_w[0], 2).reshape(1, LANES)
    wr_t = w_router[0].T
    br = b_router[0].reshape(N_EXPERTS, 1)
    bucket_t, sample_tab = _bias_tables(rel_bias)

    k_past = cache_k_win[0].reshape(nb, WINDOW, LANES)
    v_past = cache_v_win[0].reshape(nb, WINDOW, LANES)
    x1s, h2s, routes, unew, new_k_sample, new_v_sample = _mixer_sample(
        x_sample.reshape(nb, D_MODEL), state_conv[0, :, 0, :], state_conv[0, :, 1, :], k_past, v_past,
        sinks[0].reshape(N_HEADS, 1), anw, win_bf, conv_w[0], qnw, knw, sample_tab, wout_bf, fnw, wr_t, br)

    sink_rows = jnp.repeat(sinks[0].reshape(N_KV, GROUP), ATTN_BLOCK, axis=1)
    x1p, xs_big, lpp, lps, stat, kp, vp, cp = _mixer_prompt(
        x_prompt, rel_bias, bucket_t, sink_rows, anw, win_bf, conv_w[0], qnw, knw, wout_bf, fnw, wr_t, br, h2s, routes)

    n_tiles = stat.shape[0]
    lrows = _local_rows(MIXER_TILE)
    xs_rows = n_tiles * lrows
    seg = stat[:, :, 0].astype(I32)
    starts = stat[:, :, 1].astype(I32)
    tile_base = jnp.arange(n_tiles, dtype=I32) * lrows
    tile_rows = jnp.full((n_tiles,), lrows, I32)
    n_blocks = -(-(xs_rows + (N_EXPERTS + 1) * (EXPERT_BLOCK - 1)) // EXPERT_BLOCK)
    block_expert, next_expert, n_active, chunk_src, chunk_dst = _chunk_tables(seg, starts, tile_base, tile_rows,
                                                                              n_blocks)

    ys_big = _experts(block_expert, next_expert, n_active, chunk_src, chunk_dst, xs_big, xs_rows, w_up[0], b_up[0],
                      w_down[0], b_down[0])

    y_prompt = _combine(lpp, ys_big, x1p, MIXER_TILE, lrows, 0).reshape(batch, seq, D_MODEL)
    y_sample = _combine(lps, ys_big, x1s, nb, lrows, n_tiles - 1).reshape(nb, 1, D_MODEL)

    new_conv_sample = jnp.stack([state_conv[0, :, 1, :], unew], axis=1)
    return (
        y_prompt,
        y_sample,
        cp[None],
        kp.reshape(1, batch, ATTN_BLOCK, N_KV, HEAD_DIM),
        vp.reshape(1, batch, ATTN_BLOCK, N_KV, HEAD_DIM),
        new_conv_sample[None],
        new_k_sample.reshape(1, nb, WINDOW, N_KV, HEAD_DIM),
        new_v_sample.reshape(1, nb, WINDOW, N_KV, HEAD_DIM),
    )
```
